```python
import math
import jax
import jax.numpy as jnp
from jax import lax
import numpy as np

D_MODEL = 2048
BATCH = 4
SEQ = 4096
DEPTH = 2

CTX_LEN = 256
GRID_W = 64
HEAD_DIM = 128
N_BRANCHES = 4
BRANCH_WIDTH = 4 * HEAD_DIM
GQA_HEADS = 4
GQA_KV_HEADS = 2
DIFF_HEADS = 4
DIFF_QK_DIM = 64
NA_HEADS = 4
NA_ROWS = 8
NA_COLS = 16
SWA_HEADS = 4
SWA_KV_HEADS = 2
SWA_WINDOW = 128
Q_BLOCK = 128
D_FF = 5632
N_EXPERTS = 8
TOP_K = 2
N_DENSE_LAYERS = (DEPTH + 1) // 2
N_MOE_LAYERS = DEPTH // 2
ADA_CHUNKS = 6
NORM_EPS = 1e-6
ROPE_THETA = 10000.0
NEG_INF = -1e30

IN_SPLITS = (
    GQA_HEADS * HEAD_DIM, GQA_KV_HEADS * HEAD_DIM, GQA_KV_HEADS * HEAD_DIM,
    DIFF_HEADS * 2 * DIFF_QK_DIM, DIFF_HEADS * 2 * DIFF_QK_DIM, DIFF_HEADS * HEAD_DIM,
    NA_HEADS * HEAD_DIM, NA_HEADS * HEAD_DIM, NA_HEADS * HEAD_DIM,
    SWA_HEADS * HEAD_DIM, SWA_KV_HEADS * HEAD_DIM, SWA_KV_HEADS * HEAD_DIM,
    N_BRANCHES * D_MODEL,
)
IN_COLS = sum(IN_SPLITS)

kernel_name = "hybrid_dit_parallel_mixers_moe"


def rms_norm(x, g):
    xf = x.astype(jnp.float32)
    y = xf * lax.rsqrt(jnp.mean(xf * xf, axis=-1, keepdims=True) + NORM_EPS)
    return (y * g.astype(jnp.float32)).astype(x.dtype)


def modulate(h, shift, scale):
    return h * (1 + scale) + shift


def rope_1d(x, pos):
    half = x.shape[-1] // 2
    freqs = ROPE_THETA ** (-jnp.arange(half, dtype=jnp.float32) / half)
    ang = pos[:, None] * freqs[None, :]
    cos = jnp.cos(ang).astype(x.dtype)
    sin = jnp.sin(ang).astype(x.dtype)
    x1, x2 = x[..., :half], x[..., half:]
    return jnp.concatenate([x1 * cos - x2 * sin, x1 * sin + x2 * cos], axis=-1)


def axial_rope(x, rows, cols):
    h = x.shape[-1] // 2
    return jnp.concatenate([rope_1d(x[..., :h], rows), rope_1d(x[..., h:], cols)], axis=-1)


def to_heads(t, n_heads):
    b, n, _ = t.shape
    return t.reshape(b, n, n_heads, -1).transpose(0, 2, 1, 3)


def from_heads(t):
    b, h, n, d = t.shape
    return t.transpose(0, 2, 1, 3).reshape(b, n, h * d)


def softmax_f32(s):
    return jax.nn.softmax(s.astype(jnp.float32), axis=-1)


def sink_softmax(s, sink):
    m = jnp.maximum(jnp.max(s, axis=-1, keepdims=True), sink)
    e = jnp.exp(s - m)
    return e / (jnp.sum(e, axis=-1, keepdims=True) + jnp.exp(sink - m))


def sweep_query_blocks(fn, q):
    n, d = q.shape[-2], q.shape[-1]
    nb = n // Q_BLOCK
    qb = jnp.moveaxis(q.reshape(q.shape[:-2] + (nb, Q_BLOCK, d)), -3, 0)
    out = jnp.moveaxis(lax.map(fn, qb), 0, -3)
    return out.reshape(out.shape[:-3] + (n, out.shape[-1]))


def gqa_attend(q, k, v):
    scale = q.shape[-1] ** -0.5

    def block(qb):
        s = jnp.einsum('bkgqd,bksd->bkgqs', qb, k).astype(jnp.float32) * scale
        return jnp.einsum('bkgqs,bksd->bkgqd', softmax_f32(s).astype(v.dtype), v)

    return sweep_query_blocks(block, q)


def mixer_gqa(q_l, k_l, v_l, q_c, k_c, v_c, qk_g, rows, cols, need_ctx):
    g = GQA_HEADS // GQA_KV_HEADS

    def prep(q, k, v):
        return (rms_norm(to_heads(q, GQA_HEADS), qk_g[0]),
                rms_norm(to_heads(k, GQA_KV_HEADS), qk_g[1]),
                to_heads(v, GQA_KV_HEADS))

    ql, kl, vl = prep(q_l, k_l, v_l)
    qc, kc, vc = prep(q_c, k_c, v_c)
    ql = axial_rope(ql, rows, cols)
    kl = axial_rope(kl, rows, cols)
    b, _, n, d = ql.shape
    kk = jnp.concatenate([kc, kl], axis=2)
    vv = jnp.concatenate([vc, vl], axis=2)
    o_l = gqa_attend(ql.reshape(b, GQA_KV_HEADS, g, n, d), kk, vv).reshape(b, GQA_HEADS, n, d)
    o_c = None
    if need_ctx:
        l = qc.shape[2]
        o_c = from_heads(gqa_attend(qc.reshape(b, GQA_KV_HEADS, g, l, d), kc, vc).reshape(b, GQA_HEADS, l, d))
    return from_heads(o_l), o_c


def mixer_diff(q_l, k_l, v_l, q_c, k_c, v_c, lam_p, subln_g, lambda_init, rows, cols, need_ctx):
    def maps(t):
        b, n, _ = t.shape
        return t.reshape(b, n, DIFF_HEADS, 2, DIFF_QK_DIM).transpose(0, 2, 3, 1, 4)

    lp = lam_p.astype(jnp.float32)
    lam = jnp.exp(jnp.sum(lp[0] * lp[1])) - jnp.exp(jnp.sum(lp[2] * lp[3])) + lambda_init
    scale = DIFF_QK_DIM ** -0.5
    ql = axial_rope(maps(q_l), rows, cols)
    kl = axial_rope(maps(k_l), rows, cols)
    vl = to_heads(v_l, DIFF_HEADS)
    qc, kc, vc = maps(q_c), maps(k_c), to_heads(v_c, DIFF_HEADS)

    def attend(q, k, v):
        def block(qb):
            p = softmax_f32(jnp.einsum('bhmqd,bhmsd->bhmqs', qb, k).astype(jnp.float32) * scale)
            a = p[:, :, 0] - lam * p[:, :, 1]
            return jnp.einsum('bhqs,bhsd->bhqd', a.astype(v.dtype), v)
        o = sweep_query_blocks(block, q)
        return from_heads(rms_norm(o, subln_g) * (1.0 - lambda_init))

    o_l = attend(ql, jnp.concatenate([kc, kl], axis=3), jnp.concatenate([vc, vl], axis=2))
    o_c = attend(qc, kc, vc) if need_ctx else None
    return o_l, o_c


def mixer_neighborhood(q_l, k_l, v_l, q_c, k_c, v_c, rpb, need_ctx):
    ql, kl, vl = to_heads(q_l, NA_HEADS), to_heads(k_l, NA_HEADS), to_heads(v_l, NA_HEADS)
    qc, kc, vc = to_heads(q_c, NA_HEADS), to_heads(k_c, NA_HEADS), to_heads(v_c, NA_HEADS)
    b, h, n, d = ql.shape
    l = kc.shape[2]
    rows_n = n // GRID_W
    kr = min(NA_ROWS, rows_n)
    scale = d ** -0.5
    kg = kl.reshape(b, h, rows_n, GRID_W, d)
    vg = vl.reshape(b, h, rows_n, GRID_W, d)
    col = jnp.arange(GRID_W)
    cstart = jnp.clip(col - NA_COLS // 2, 0, GRID_W - NA_COLS)
    in_win = (col[None, :] >= cstart[:, None]) & (col[None, :] < cstart[:, None] + NA_COLS)
    dj = jnp.clip(col[None, :] - col[:, None] + NA_COLS - 1, 0, 2 * NA_COLS - 2)
    mask = jnp.broadcast_to(in_win[:, None, :], (GRID_W, kr, GRID_W)).reshape(GRID_W, kr * GRID_W)

    def row_block(args):
        r, q_row = args
        rs = jnp.clip(r - kr // 2, 0, rows_n - kr)
        kb = lax.dynamic_slice_in_dim(kg, rs, kr, axis=2).reshape(b, h, kr * GRID_W, d)
        vb = lax.dynamic_slice_in_dim(vg, rs, kr, axis=2).reshape(b, h, kr * GRID_W, d)
        di = rs + jnp.arange(kr) - r + NA_ROWS - 1
        bias = rpb[:, di][:, :, dj]
        bias = bias.transpose(0, 2, 1, 3).reshape(h, GRID_W, kr * GRID_W).astype(jnp.float32)
        s_band = jnp.einsum('bhqd,bhsd->bhqs', q_row, kb).astype(jnp.float32) * scale + bias
        s_band = jnp.where(mask, s_band, NEG_INF)
        s_ctx = jnp.einsum('bhqd,bhsd->bhqs', q_row, kc).astype(jnp.float32) * scale
        p = softmax_f32(jnp.concatenate([s_ctx, s_band], axis=-1)).astype(vb.dtype)
        return (jnp.einsum('bhqs,bhsd->bhqd', p[..., :l], vc)
                + jnp.einsum('bhqs,bhsd->bhqd', p[..., l:], vb))

    q_rows = jnp.moveaxis(ql.reshape(b, h, rows_n, GRID_W, d), 2, 0)
    o = lax.map(row_block, (jnp.arange(rows_n), q_rows))
    o_l = jnp.moveaxis(o, 0, 2).reshape(b, h, n, d)
    o_c = from_heads(gqa_attend(qc[:, :, None], kc, vc)[:, :, 0]) if need_ctx else None
    return from_heads(o_l), o_c


def mixer_swa_sink(q_l, k_l, v_l, q_c, k_c, v_c, sink, rows, cols, need_ctx):
    g = SWA_HEADS // SWA_KV_HEADS
    w = SWA_WINDOW
    ql = axial_rope(to_heads(q_l, SWA_HEADS), rows, cols)
    kl = axial_rope(to_heads(k_l, SWA_KV_HEADS), rows, cols)
    vl = to_heads(v_l, SWA_KV_HEADS)
    qc, kc, vc = to_heads(q_c, SWA_HEADS), to_heads(k_c, SWA_KV_HEADS), to_heads(v_c, SWA_KV_HEADS)
    b, _, n, d = ql.shape
    l = kc.shape[2]
    nb = n // w
    scale = d ** -0.5
    sink_kg = sink.astype(jnp.float32).reshape(SWA_KV_HEADS, g)
    qb = ql.reshape(b, SWA_KV_HEADS, g, nb, w, d)

    def bands(t):
        tp = jnp.pad(t, ((0, 0), (0, 0), (w, w), (0, 0))).reshape(b, SWA_KV_HEADS, nb + 2, w, d)
        return jnp.concatenate([tp[:, :, :-2], tp[:, :, 1:-1], tp[:, :, 2:]], axis=-2)

    kw, vw = bands(kl), bands(vl)
    qpos = jnp.arange(n).reshape(nb, w)
    kpos = jnp.arange(nb)[:, None] * w - w + jnp.arange(3 * w)[None, :]
    valid = ((jnp.abs(qpos[:, :, None] - kpos[:, None, :]) <= w)
             & (kpos >= 0)[:, None, :] & (kpos < n)[:, None, :])
    s_band = jnp.einsum('bkgnqd,bknsd->bkgnqs', qb, kw).astype(jnp.float32) * scale
    s_band = jnp.where(valid, s_band, NEG_INF)
    s_ctx = jnp.einsum('bkgnqd,bksd->bkgnqs', qb, kc).astype(jnp.float32) * scale
    p = sink_softmax(jnp.concatenate([s_ctx, s_band], axis=-1),
                     sink_kg[None, :, :, None, None, None]).astype(vl.dtype)
    o = (jnp.einsum('bkgnqs,bksd->bkgnqd', p[..., :l], vc)
         + jnp.einsum('bkgnqs,bknsd->bkgnqd', p[..., l:], vw))
    o_l = from_heads(o.reshape(b, SWA_HEADS, n, d))
    o_c = None
    if need_ctx:
        s = jnp.einsum('bkgqd,bksd->bkgqs', qc.reshape(b, SWA_KV_HEADS, g, l, d), kc).astype(jnp.float32) * scale
        pc = sink_softmax(s, sink_kg[None, :, :, None, None]).astype(vc.dtype)
        o_c = from_heads(jnp.einsum('bkgqs,bksd->bkgqd', pc, vc).reshape(b, SWA_HEADS, l, d))
    return o_l, o_c


def merge_branches(outs, gate_logits, w_branch, w_out):
    o = jnp.stack(outs, axis=2)
    y = jnp.einsum('bnic,icm->bnim', o, w_branch)
    g = jax.nn.sigmoid(gate_logits.reshape(gate_logits.shape[:2] + (N_BRANCHES, D_MODEL)))
    return jnp.sum(g * y, axis=2) @ w_out


def token_mixers(h_lat, h_ctx, w_in, qk_g, lam_p, subln_g, rpb, sink, w_branch, w_out, lambda_init, need_ctx):
    n = h_lat.shape[1]
    pos = jnp.arange(n)
    rows = (pos // GRID_W).astype(jnp.float32)
    cols = (pos % GRID_W).astype(jnp.float32)
    points = [int(p) for p in np.cumsum(IN_SPLITS)[:-1]]
    pl = jnp.split(h_lat @ w_in, points, axis=-1)
    pc = jnp.split(h_ctx @ w_in, points, axis=-1)
    a_l, a_c = mixer_gqa(*pl[0:3], *pc[0:3], qk_g, rows, cols, need_ctx)
    b_l, b_c = mixer_diff(*pl[3:6], *pc[3:6], lam_p, subln_g, lambda_init, rows, cols, need_ctx)
    c_l, c_c = mixer_neighborhood(*pl[6:9], *pc[6:9], rpb, need_ctx)
    d_l, d_c = mixer_swa_sink(*pl[9:12], *pc[9:12], sink, rows, cols, need_ctx)
    y_lat = merge_branches([a_l, b_l, c_l, d_l], pl[12], w_branch, w_out)
    y_ctx = merge_branches([a_c, b_c, c_c, d_c], pc[12], w_branch, w_out) if need_ctx else None
    return y_lat, y_ctx


def swiglu(h, wg, wu, wd):
    return (jax.nn.silu(h @ wg) * (h @ wu)) @ wd


def moe_swiglu(h, router, wg, wu, wd):
    probs = jax.nn.softmax((h @ router).astype(jnp.float32), axis=-1)
    top_p, top_i = lax.top_k(probs, TOP_K)
    top_p = top_p / jnp.sum(top_p, axis=-1, keepdims=True)
    combine = jnp.sum(jax.nn.one_hot(top_i, N_EXPERTS, dtype=jnp.float32) * top_p[..., None], axis=-2).astype(h.dtype)
    y = jnp.zeros_like(h)
    for e in range(N_EXPERTS):
        y = y + combine[..., e:e + 1] * swiglu(h, wg[e], wu[e], wd[e])
    return y


def setup_inputs(seed: int = 0) -> dict:
    key = jax.random.key(seed)
    ks = jax.random.split(key, 24)
    f32 = jnp.float32

    def w(k, shape, fan_in, gain=1.0):
        return jax.random.normal(k, shape, f32) * (gain * fan_in ** -0.5)

    def near_one(k, shape):
        return 1.0 + 0.02 * jax.random.normal(k, shape, f32)

    return {
        "x": jax.random.normal(ks[0], (BATCH, SEQ, D_MODEL), f32),
        "c": jax.random.normal(ks[1], (BATCH, D_MODEL), f32),
        "ctx": jax.random.normal(ks[2], (BATCH, CTX_LEN, D_MODEL), f32),
        "c_ctx": jax.random.normal(ks[3], (D_MODEL,), f32),
        "attn_norm_g": near_one(ks[4], (DEPTH, D_MODEL)),
        "ffn_norm_g": near_one(ks[5], (DEPTH, D_MODEL)),
        "ada_w": w(ks[6], (DEPTH, D_MODEL, ADA_CHUNKS * D_MODEL), D_MODEL, 0.5),
        "ada_b": 0.02 * jax.random.normal(ks[7], (DEPTH, ADA_CHUNKS * D_MODEL), f32),
        "w_in": w(ks[8], (DEPTH, D_MODEL, IN_COLS), D_MODEL),
        "qk_norm_g": near_one(ks[9], (DEPTH, 2, HEAD_DIM)),
        "diff_lambda": 0.1 * jax.random.normal(ks[10], (DEPTH, 4, DIFF_QK_DIM), f32),
        "diff_subln_g": near_one(ks[11], (DEPTH, HEAD_DIM)),
        "na_rpb": 0.1 * jax.random.normal(ks[12], (DEPTH, NA_HEADS, 2 * NA_ROWS - 1, 2 * NA_COLS - 1), f32),
        "swa_sink": 0.5 * jax.random.normal(ks[13], (DEPTH, SWA_HEADS), f32),
        "w_branch": w(ks[14], (DEPTH, N_BRANCHES, BRANCH_WIDTH, D_MODEL), BRANCH_WIDTH),
        "w_out": w(ks[15], (DEPTH, D_MODEL, D_MODEL), D_MODEL),
        "ffn_w_gate": w(ks[16], (N_DENSE_LAYERS, D_MODEL, D_FF), D_MODEL),
        "ffn_w_up": w(ks[17], (N_DENSE_LAYERS, D_MODEL, D_FF), D_MODEL),
        "ffn_w_down": w(ks[18], (N_DENSE_LAYERS, D_FF, D_MODEL), D_FF),
        "moe_router": w(ks[19], (N_MOE_LAYERS, D_MODEL, N_EXPERTS), D_MODEL),
        "moe_w_gate": w(ks[20], (N_MOE_LAYERS, N_EXPERTS, D_MODEL, D_FF), D_MODEL),
        "moe_w_up": w(ks[21], (N_MOE_LAYERS, N_EXPERTS, D_MODEL, D_FF), D_MODEL),
        "moe_w_down": w(ks[22], (N_MOE_LAYERS, N_EXPERTS, D_FF, D_MODEL), D_FF),
        "final_norm_g": near_one(ks[23], (D_MODEL,)),
    }


def reference(x, c, ctx, c_ctx, attn_norm_g, ffn_norm_g, ada_w, ada_b, w_in, qk_norm_g,
              diff_lambda, diff_subln_g, na_rpb, swa_sink, w_branch, w_out,
              ffn_w_gate, ffn_w_up, ffn_w_down, moe_router, moe_w_gate, moe_w_up, moe_w_down,
              final_norm_g):
    x_lat, x_ctx = x, ctx
    l = ctx.shape[1]
    for i in range(DEPTH):
        need_ctx = i < DEPTH - 1
        sh_a, sc_a, g_a, sh_f, sc_f, g_f = [m[:, None, :] for m in jnp.split(jax.nn.silu(c) @ ada_w[i] + ada_b[i], ADA_CHUNKS, axis=-1)]
        csh_a, csc_a, cg_a, csh_f, csc_f, cg_f = jnp.split(jax.nn.silu(c_ctx) @ ada_w[i] + ada_b[i], ADA_CHUNKS, axis=-1)
        lambda_init = 0.8 - 0.6 * math.exp(-0.3 * i)
        h_lat = modulate(rms_norm(x_lat, attn_norm_g[i]), sh_a, sc_a)
        h_ctx = modulate(rms_norm(x_ctx, attn_norm_g[i]), csh_a, csc_a)
        y_lat, y_ctx = token_mixers(h_lat, h_ctx, w_in[i], qk_norm_g[i], diff_lambda[i], diff_subln_g[i],
                                    na_rpb[i], swa_sink[i], w_branch[i], w_out[i], lambda_init, need_ctx)
        x_lat = x_lat + g_a * y_lat
        h_lat = modulate(rms_norm(x_lat, ffn_norm_g[i]), sh_f, sc_f)
        if need_ctx:
            x_ctx = x_ctx + cg_a * y_ctx
            h_ctx = modulate(rms_norm(x_ctx, ffn_norm_g[i]), csh_f, csc_f)
            h = jnp.concatenate([h_ctx, h_lat], axis=1)
        else:
            h = h_lat
        j = i // 2
        if i % 2 == 0:
            f = swiglu(h, ffn_w_gate[j], ffn_w_up[j], ffn_w_down[j])
        else:
            f = moe_swiglu(h, moe_router[j], moe_w_gate[j], moe_w_up[j], moe_w_down[j])
        if need_ctx:
            x_ctx = x_ctx + cg_f * f[:, :l]
            x_lat = x_lat + g_f * f[:, l:]
        else:
            x_lat = x_lat + g_f * f
    return rms_norm(x_lat, final_norm_g)
```

```python
import functools
import math

import numpy as np
import jax
import jax.numpy as jnp
from jax import lax
from jax.experimental import pallas as pl
from jax.experimental.pallas import tpu as pltpu

F32 = jnp.float32
BF16 = jnp.bfloat16

GRID_W = 64
HEAD_DIM = 128
N_BRANCHES = 4
BRANCH_WIDTH = 4 * HEAD_DIM
DIFF_QK_DIM = 64
NA_ROWS = 8
NA_COLS = 16
SWA_WINDOW = 128
N_EXPERTS = 8
TOP_K = 2
NORM_EPS = 1e-6
ROPE_THETA = 10000.0
NEG_INF = -1e30
LOG2E = math.log2(math.e)

QKV_COLS = 5120
COL_GQA_Q, COL_GQA_K, COL_GQA_V = 0, 512, 768
COL_DIFF_Q, COL_DIFF_K, COL_DIFF_V = 1024, 1536, 2048
COL_NA_Q, COL_NA_K, COL_NA_V = 2560, 3072, 3584
COL_SWA_Q, COL_SWA_K, COL_SWA_V = 4096, 4608, 4864

LANES = 128
VMEM_LIMIT = 56 * 1024 * 1024

QS128 = HEAD_DIM ** -0.5 * LOG2E
QS64 = DIFF_QK_DIM ** -0.5 * LOG2E


def _cparams(sem):
    return pltpu.CompilerParams(dimension_semantics=sem, vmem_limit_bytes=VMEM_LIMIT)


def _dot(a, b):
    return jnp.dot(a, b, preferred_element_type=F32)


def _dot_nt(a, b):
    return lax.dot_general(a, b, (((1,), (1,)), ((), ())), preferred_element_type=F32)


def _rms(x):
    return x * lax.rsqrt(jnp.mean(x * x, axis=-1, keepdims=True) + NORM_EPS)


def _norm_mod(x, g, mod_ref, shift_idx, scale_idx):
    y = _rms(x) * g
    return y * (1.0 + mod_ref[scale_idx:scale_idx + 1, :]) + mod_ref[shift_idx:shift_idx + 1, :]


def _adaln_kernel(c_ref, w_ref, b_ref, o_ref):
    c = c_ref[...]
    s = c * jax.nn.sigmoid(c)
    o_ref[...] = jnp.dot(s, w_ref[...], preferred_element_type=F32,
                         precision=lax.Precision.HIGHEST) + b_ref[...]


def _adaln(cvec, w, b):
    rows, d = cvec.shape
    cols = w.shape[1]
    tn = 1024 if cols % 1024 == 0 else cols
    return pl.pallas_call(
        _adaln_kernel,
        grid=(cols // tn,),
        in_specs=[pl.BlockSpec((rows, d), lambda j: (0, 0)),
                  pl.BlockSpec((d, tn), lambda j: (0, j)),
                  pl.BlockSpec((1, tn), lambda j: (0, j))],
        out_specs=pl.BlockSpec((rows, tn), lambda j: (0, j)),
        out_shape=jax.ShapeDtypeStruct((rows, cols), F32),
        compiler_params=_cparams(("arbitrary",)),
        name="adaln",
    )(cvec, w, b.reshape(1, cols))


def _norm_mod_kernel(x_ref, g_ref, mod_ref, h_ref, *, shift_idx, scale_idx):
    h_ref[...] = _norm_mod(x_ref[...], g_ref[...], mod_ref, shift_idx, scale_idx).astype(BF16)


def _norm_mod_call(x, g, mod, shift_idx, scale_idx):
    b, r, d = x.shape
    tm = min(r, 512)
    return pl.pallas_call(
        functools.partial(_norm_mod_kernel, shift_idx=shift_idx, scale_idx=scale_idx),
        grid=(b, r // tm),
        in_specs=[pl.BlockSpec((None, tm, d), lambda i, t: (i, t, 0)),
                  pl.BlockSpec((1, d), lambda i, t: (0, 0)),
                  pl.BlockSpec((None, 6, d), lambda i, t: (i, 0, 0))],
        out_specs=pl.BlockSpec((None, tm, d), lambda i, t: (i, t, 0)),
        out_shape=jax.ShapeDtypeStruct((b, r, d), BF16),
        compiler_params=_cparams(("parallel", "parallel")),
        name="norm_mod",
    )(x, g.reshape(1, d), mod)


_PLAIN = (None, None, 1.0)
_QKV_TILE_OPS = {
    0: [(0, 32, QS128)] * 4,
    1: [(1, 32, 1.0)] * 2 + [_PLAIN] * 2,
    2: [(None, 16, QS64)] * 4,
    3: [(None, 16, 1.0)] * 4,
    5: [(None, None, QS128)] * 4,
    8: [(None, 32, QS128)] * 4,
    9: [(None, 32, 1.0)] * 2 + [_PLAIN] * 2,
}


def _rope(y, cos, sin, hw):
    lane = lax.broadcasted_iota(jnp.int32, y.shape, 1)
    first = (lane % (2 * hw)) < hw
    partner = jnp.where(first, pltpu.roll(y, LANES - hw, 1), pltpu.roll(y, hw, 1))
    return y * cos + partner * sin


def _qkv_kernel(h_ref, w_ref, cos32_ref, sin32_ref, cos16_ref, sin16_ref, g_ref, o_ref):
    j = pl.program_id(2)
    acc = _dot(h_ref[...], w_ref[...])

    def emit(ops):
        for c, (norm_row, hw, scale) in enumerate(ops):
            y = acc[:, c * LANES:(c + 1) * LANES]
            if norm_row is not None:
                y = _rms(y) * g_ref[norm_row:norm_row + 1, :]
            if hw == 32:
                y = _rope(y, cos32_ref[...], sin32_ref[...], 32)
            elif hw == 16:
                y = _rope(y, cos16_ref[...], sin16_ref[...], 16)
            if scale != 1.0:
                y = y * scale
            o_ref[:, c * LANES:(c + 1) * LANES] = y.astype(o_ref.dtype)

    special = sorted(_QKV_TILE_OPS)
    for jj in special:
        pl.when(j == jj)(functools.partial(emit, _QKV_TILE_OPS[jj]))
    is_plain = functools.reduce(jnp.logical_and, [j != jj for jj in special])
    pl.when(is_plain)(functools.partial(emit, [_PLAIN] * 4))


def _qkv_call(h, w_qkv, tables, qk_g):
    b, r, d = h.shape
    cols = w_qkv.shape[1]
    tn = 512
    tm = 1024 if r % 1024 == 0 else r
    tab_spec = pl.BlockSpec((tm, LANES), lambda i, t, j: (t, 0))
    return pl.pallas_call(
        _qkv_kernel,
        grid=(b, r // tm, cols // tn),
        in_specs=[pl.BlockSpec((None, tm, d), lambda i, t, j: (i, t, 0)),
                  pl.BlockSpec((d, tn), lambda i, t, j: (0, j)),
                  tab_spec, tab_spec, tab_spec, tab_spec,
                  pl.BlockSpec((2, LANES), lambda i, t, j: (0, 0))],
        out_specs=pl.BlockSpec((None, tm, tn), lambda i, t, j: (i, t, j)),
        out_shape=jax.ShapeDtypeStruct((b, r, cols), BF16),
        compiler_params=_cparams(("parallel", "parallel", "arbitrary")),
        name="qkv_proj",
    )(h, w_qkv, *tables, qk_g)


def _matmul_kernel(h_ref, w_ref, o_ref):
    o_ref[...] = _dot(h_ref[...], w_ref[...]).astype(o_ref.dtype)


def _matmul_call(h, w):
    b, r, d = h.shape
    cols = w.shape[1]
    tn = 512
    tm = 1024 if r % 1024 == 0 else r
    return pl.pallas_call(
        _matmul_kernel,
        grid=(b, r // tm, cols // tn),
        in_specs=[pl.BlockSpec((None, tm, d), lambda i, t, j: (i, t, 0)),
                  pl.BlockSpec((d, tn), lambda i, t, j: (0, j))],
        out_specs=pl.BlockSpec((None, tm, tn), lambda i, t, j: (i, t, j)),
        out_shape=jax.ShapeDtypeStruct((b, r, cols), BF16),
        compiler_params=_cparams(("parallel", "parallel", "arbitrary")),
        name="gate_proj",
    )(h, w)


def _diff_lambda(lam_ref, lambda_init):
    a = jnp.sum(lam_ref[0:1, :] * lam_ref[1:2, :], axis=-1, keepdims=True)
    b = jnp.sum(lam_ref[2:3, :] * lam_ref[3:4, :], axis=-1, keepdims=True)
    return jnp.exp(a) - jnp.exp(b) + lambda_init


def _split_maps(q):
    lane = lax.broadcasted_iota(jnp.int32, q.shape, 1)
    zero = jnp.zeros_like(q)
    return jnp.concatenate([jnp.where(lane < DIFF_QK_DIM, q, zero),
                            jnp.where(lane >= DIFF_QK_DIM, q, zero)], axis=0)


def _flash_pair(q2, kc_ref, vc_ref, kl_ref, vl_ref, tk):
    m_rows = q2.shape[0]
    n_lat = kl_ref.shape[0]

    def step(k, v, carry):
        m, l, acc = carry
        s = _dot_nt(q2, k)
        m_new = jnp.maximum(m, jnp.max(s, axis=-1, keepdims=True))
        p = jnp.exp2(s - m_new)
        alpha = jnp.exp2(m - m_new)
        l = alpha * l + jnp.sum(p, axis=-1, keepdims=True)
        acc = alpha * acc + _dot(p.astype(BF16), v)
        return m_new, l, acc

    carry = (jnp.full((m_rows, 1), NEG_INF, F32), jnp.zeros((m_rows, 1), F32),
             jnp.zeros((m_rows, HEAD_DIM), F32))
    carry = step(kc_ref[...], vc_ref[...], carry)

    def body(c, cr):
        off = pl.multiple_of(c * tk, tk)
        return step(kl_ref[pl.ds(off, tk), :], vl_ref[pl.ds(off, tk), :], cr)

    _, l, acc = lax.fori_loop(0, n_lat // tk, body, carry)
    return acc / l


def _gqa_kernel(q_ref, kc_ref, vc_ref, kl_ref, vl_ref, o_ref, *, tk):
    tq = q_ref.shape[0]
    q = q_ref[...]
    q2 = jnp.concatenate([q[:, :HEAD_DIM], q[:, HEAD_DIM:]], axis=0)
    o = _flash_pair(q2, kc_ref, vc_ref, kl_ref, vl_ref, tk)
    o_ref[:, :HEAD_DIM] = o[:tq].astype(o_ref.dtype)
    o_ref[:, HEAD_DIM:] = o[tq:].astype(o_ref.dtype)


def _diff_kernel(q_ref, kc_ref, vc_ref, kl_ref, vl_ref, lam_ref, subg_ref, o_ref, *, tk, lambda_init):
    tq = q_ref.shape[0]
    o = _flash_pair(_split_maps(q_ref[...]), kc_ref, vc_ref, kl_ref, vl_ref, tk)
    lam = _diff_lambda(lam_ref, lambda_init)
    d = o[:tq] - lam * o[tq:]
    o_ref[...] = (_rms(d) * subg_ref[...] * (1.0 - lambda_init)).astype(o_ref.dtype)


def _gqa_call(p_lat, p_ctx):
    b, n, _ = p_lat.shape
    l = p_ctx.shape[1]
    tq = 512
    tk = 512
    hd = HEAD_DIM
    kcol, vcol = COL_GQA_K // hd, COL_GQA_V // hd
    return pl.pallas_call(
        functools.partial(_gqa_kernel, tk=tk),
        grid=(b, 2, n // tq),
        in_specs=[pl.BlockSpec((None, tq, 2 * hd), lambda i, k, t: (i, t, k)),
                  pl.BlockSpec((None, l, hd), lambda i, k, t: (i, 0, kcol + k)),
                  pl.BlockSpec((None, l, hd), lambda i, k, t: (i, 0, vcol + k)),
                  pl.BlockSpec((None, n, hd), lambda i, k, t: (i, 0, kcol + k)),
                  pl.BlockSpec((None, n, hd), lambda i, k, t: (i, 0, vcol + k))],
        out_specs=pl.BlockSpec((None, tq, 2 * hd), lambda i, k, t: (i, t, k)),
        out_shape=jax.ShapeDtypeStruct((b, n, BRANCH_WIDTH), BF16),
        compiler_params=_cparams(("parallel", "parallel", "arbitrary")),
        name="attn_gqa",
    )(p_lat, p_ctx, p_ctx, p_lat, p_lat)


def _diff_call(p_lat, p_ctx, lam_p, subg, lambda_init):
    b, n, _ = p_lat.shape
    l = p_ctx.shape[1]
    tq = 512
    tk = 512
    hd = HEAD_DIM
    qcol, kcol, vcol = COL_DIFF_Q // hd, COL_DIFF_K // hd, COL_DIFF_V // hd
    return pl.pallas_call(
        functools.partial(_diff_kernel, tk=tk, lambda_init=lambda_init),
        grid=(b, 4, n // tq),
        in_specs=[pl.BlockSpec((None, tq, hd), lambda i, h, t: (i, t, qcol + h)),
                  pl.BlockSpec((None, l, hd), lambda i, h, t: (i, 0, kcol + h)),
                  pl.BlockSpec((None, l, hd), lambda i, h, t: (i, 0, vcol + h)),
                  pl.BlockSpec((None, n, hd), lambda i, h, t: (i, 0, kcol + h)),
                  pl.BlockSpec((None, n, hd), lambda i, h, t: (i, 0, vcol + h)),
                  pl.BlockSpec((4, DIFF_QK_DIM), lambda i, h, t: (0, 0)),
                  pl.BlockSpec((1, hd), lambda i, h, t: (0, 0))],
        out_specs=pl.BlockSpec((None, tq, hd), lambda i, h, t: (i, t, h)),
        out_shape=jax.ShapeDtypeStruct((b, n, BRANCH_WIDTH), BF16),
        compiler_params=_cparams(("parallel", "parallel", "arbitrary")),
        name="attn_diff",
    )(p_lat, p_ctx, p_ctx, p_lat, p_lat, lam_p, subg.reshape(1, hd))


NA_GROUP_ROWS = 8
NA_WIN_ROWS = 2 * NA_GROUP_ROWS


def _na_window_start(g, rows_n):
    return jnp.clip(g * NA_GROUP_ROWS - NA_ROWS // 2, 0, rows_n - NA_WIN_ROWS)


def _na_bias_tables(rpb, rows_n):
    n_groups = rows_n // NA_GROUP_ROWS
    variants = [0, min(1, n_groups - 1), n_groups - 1]
    a = np.arange(NA_GROUP_ROWS)[:, None, None, None]
    qc = np.arange(GRID_W)[None, :, None, None]
    u = np.arange(NA_WIN_ROWS)[None, None, :, None]
    kc = np.arange(GRID_W)[None, None, None, :]
    di_all, dj_all, ok_all = [], [], []
    for g in variants:
        ws = int(np.clip(g * NA_GROUP_ROWS - NA_ROWS // 2, 0, rows_n - NA_WIN_ROWS))
        r = g * NA_GROUP_ROWS + a
        rs = np.clip(r - NA_ROWS // 2, 0, rows_n - NA_ROWS)
        krow = ws + u
        ok_row = (krow >= rs) & (krow < rs + NA_ROWS)
        di = krow - r + NA_ROWS - 1
        cstart = np.clip(qc - NA_COLS // 2, 0, GRID_W - NA_COLS)
        ok_col = (kc >= cstart) & (kc < cstart + NA_COLS)
        dj = np.clip(kc - qc + NA_COLS - 1, 0, 2 * NA_COLS - 2)
        shape = (NA_GROUP_ROWS, GRID_W, NA_WIN_ROWS, GRID_W)
        di_all.append(np.broadcast_to(np.clip(di, 0, 2 * NA_ROWS - 2), shape))
        dj_all.append(np.broadcast_to(dj, shape))
        ok_all.append(np.broadcast_to(ok_row & ok_col, shape))
    di = np.stack(di_all).reshape(3, NA_GROUP_ROWS * GRID_W, NA_WIN_ROWS * GRID_W)
    dj = np.stack(dj_all).reshape(di.shape)
    ok = np.stack(ok_all).reshape(di.shape)
    bias = rpb.astype(F32)[:, di, dj] * LOG2E
    return jnp.where(ok[None], bias, NEG_INF)


def _na_kernel(q_ref, kc_ref, vc_ref, kl_ref, vl_ref, bias_ref, o_ref, *, rows_n):
    g = pl.program_id(2)
    win = NA_WIN_ROWS * GRID_W
    off = pl.multiple_of(_na_window_start(g, rows_n) * GRID_W, 256)
    q = q_ref[...]
    s_b = _dot_nt(q, kl_ref[pl.ds(off, win), :]) + bias_ref[...]
    s_c = _dot_nt(q, kc_ref[...])
    m = jnp.maximum(jnp.max(s_b, axis=-1, keepdims=True), jnp.max(s_c, axis=-1, keepdims=True))
    p_b = jnp.exp2(s_b - m)
    p_c = jnp.exp2(s_c - m)
    l = jnp.sum(p_b, axis=-1, keepdims=True) + jnp.sum(p_c, axis=-1, keepdims=True)
    o = _dot(p_c.astype(BF16), vc_ref[...]) + _dot(p_b.astype(BF16), vl_ref[pl.ds(off, win), :])
    o_ref[...] = (o / l).astype(o_ref.dtype)


def _na_call(p_lat, p_ctx, bias):
    b, n, _ = p_lat.shape
    l = p_ctx.shape[1]
    hd = HEAD_DIM
    rows_n = n // GRID_W
    n_groups = rows_n // NA_GROUP_ROWS
    tq = NA_GROUP_ROWS * GRID_W
    win = NA_WIN_ROWS * GRID_W
    qcol, kcol, vcol = COL_NA_Q // hd, COL_NA_K // hd, COL_NA_V // hd

    def bias_map(i, h, g):
        return (h, jnp.where(g == 0, 0, jnp.where(g == n_groups - 1, 2, 1)), 0, 0)

    return pl.pallas_call(
        functools.partial(_na_kernel, rows_n=rows_n),
        grid=(b, 4, n_groups),
        in_specs=[pl.BlockSpec((None, tq, hd), lambda i, h, g: (i, g, qcol + h)),
                  pl.BlockSpec((None, l, hd), lambda i, h, g: (i, 0, kcol + h)),
                  pl.BlockSpec((None, l, hd), lambda i, h, g: (i, 0, vcol + h)),
                  pl.BlockSpec((None, n, hd), lambda i, h, g: (i, 0, kcol + h)),
                  pl.BlockSpec((None, n, hd), lambda i, h, g: (i, 0, vcol + h)),
                  pl.BlockSpec((None, None, tq, win), bias_map)],
        out_specs=pl.BlockSpec((None, tq, hd), lambda i, h, g: (i, g, h)),
        out_shape=jax.ShapeDtypeStruct((b, n, BRANCH_WIDTH), BF16),
        compiler_params=_cparams(("parallel", "parallel", "arbitrary")),
        name="attn_na",
    )(p_lat, p_ctx, p_ctx, p_lat, p_lat, bias)


def _swa_kernel(q_ref, kc_ref, vc_ref, kl_ref, vl_ref, sink_ref, o_ref, *, win):
    t = pl.program_id(2)
    tq = q_ref.shape[0]
    n = kl_ref.shape[0]
    q0 = t * tq
    ws = pl.multiple_of(jnp.clip(q0 - SWA_WINDOW, 0, n - win), SWA_WINDOW)
    kb = kl_ref[pl.ds(ws, win), :]
    vb = vl_ref[pl.ds(ws, win), :]
    row = lax.broadcasted_iota(jnp.int32, (tq, win), 0)
    col = lax.broadcasted_iota(jnp.int32, (tq, win), 1)
    valid = jnp.abs(row + (q0 - ws) - col) <= SWA_WINDOW
    for gi in range(2):
        q = q_ref[:, gi * HEAD_DIM:(gi + 1) * HEAD_DIM]
        sink = sink_ref[gi:gi + 1, 0:1]
        s_b = jnp.where(valid, _dot_nt(q, kb), NEG_INF)
        s_c = _dot_nt(q, kc_ref[...])
        m = jnp.maximum(jnp.maximum(jnp.max(s_b, axis=-1, keepdims=True),
                                    jnp.max(s_c, axis=-1, keepdims=True)), sink)
        e_b = jnp.exp2(s_b - m)
        e_c = jnp.exp2(s_c - m)
        den = (jnp.sum(e_b, axis=-1, keepdims=True) + jnp.sum(e_c, axis=-1, keepdims=True)
               + jnp.exp2(sink - m))
        o = _dot(e_c.astype(BF16), vc_ref[...]) + _dot(e_b.astype(BF16), vb)
        o_ref[:, gi * HEAD_DIM:(gi + 1) * HEAD_DIM] = (o / den).astype(o_ref.dtype)


def _swa_call(p_lat, p_ctx, sink_tab):
    b, n, _ = p_lat.shape
    l = p_ctx.shape[1]
    hd = HEAD_DIM
    tq = 512
    win = tq + 2 * SWA_WINDOW
    qcol, kcol, vcol = COL_SWA_Q // (2 * hd), COL_SWA_K // hd, COL_SWA_V // hd
    return pl.pallas_call(
        functools.partial(_swa_kernel, win=win),
        grid=(b, 2, n // tq),
        in_specs=[pl.BlockSpec((None, tq, 2 * hd), lambda i, k, t: (i, t, qcol + k)),
                  pl.BlockSpec((None, l, hd), lambda i, k, t: (i, 0, kcol + k)),
                  pl.BlockSpec((None, l, hd), lambda i, k, t: (i, 0, vcol + k)),
                  pl.BlockSpec((None, n, hd), lambda i, k, t: (i, 0, kcol + k)),
                  pl.BlockSpec((None, n, hd), lambda i, k, t: (i, 0, vcol + k)),
                  pl.BlockSpec((None, 2, LANES), lambda i, k, t: (k, 0, 0))],
        out_specs=pl.BlockSpec((None, tq, 2 * hd), lambda i, k, t: (i, t, k)),
        out_shape=jax.ShapeDtypeStruct((b, n, BRANCH_WIDTH), BF16),
        compiler_params=_cparams(("parallel", "parallel", "arbitrary")),
        name="attn_swa",
    )(p_lat, p_ctx, p_ctx, p_lat, p_lat, sink_tab)


def _softmax_attend(q, k, v, sink=None):
    s = _dot_nt(q, k)
    m = jnp.max(s, axis=-1, keepdims=True)
    if sink is not None:
        m = jnp.maximum(m, sink)
    e = jnp.exp2(s - m)
    den = jnp.sum(e, axis=-1, keepdims=True)
    if sink is not None:
        den = den + jnp.exp2(sink - m)
    return _dot(e.astype(BF16), v) / den


def _ctx_attn_kernel(p_ref, lam_ref, subg_ref, sink_ref, o_ref, *, lambda_init):
    hd = HEAD_DIM

    def col(c0, h):
        return p_ref[:, c0 + h * hd:c0 + (h + 1) * hd]

    lam = _diff_lambda(lam_ref, lambda_init)
    for h in range(4):
        o = _softmax_attend(col(COL_GQA_Q, h), col(COL_GQA_K, h // 2), col(COL_GQA_V, h // 2))
        o_ref[:, h * hd:(h + 1) * hd] = o.astype(o_ref.dtype)
    for h in range(4):
        q = col(COL_DIFF_Q, h)
        tq = q.shape[0]
        o2 = _softmax_attend(_split_maps(q), col(COL_DIFF_K, h), col(COL_DIFF_V, h))
        d = o2[:tq] - lam * o2[tq:]
        d = _rms(d) * subg_ref[...] * (1.0 - lambda_init)
        o_ref[:, BRANCH_WIDTH + h * hd:BRANCH_WIDTH + (h + 1) * hd] = d.astype(o_ref.dtype)
    for h in range(4):
        o = _softmax_attend(col(COL_NA_Q, h), col(COL_NA_K, h), col(COL_NA_V, h))
        o_ref[:, 2 * BRANCH_WIDTH + h * hd:2 * BRANCH_WIDTH + (h + 1) * hd] = o.astype(o_ref.dtype)
    for h in range(4):
        o = _softmax_attend(col(COL_SWA_Q, h), col(COL_SWA_K, h // 2), col(COL_SWA_V, h // 2),
                            sink=sink_ref[h:h + 1, 0:1])
        o_ref[:, 3 * BRANCH_WIDTH + h * hd:3 * BRANCH_WIDTH + (h + 1) * hd] = o.astype(o_ref.dtype)


def _ctx_attn_call(p_ctx, lam_p, subg, sink_tab, lambda_init):
    b, l, cols = p_ctx.shape
    return pl.pallas_call(
        functools.partial(_ctx_attn_kernel, lambda_init=lambda_init),
        grid=(b,),
        in_specs=[pl.BlockSpec((None, l, cols), lambda i: (i, 0, 0)),
                  pl.BlockSpec((4, DIFF_QK_DIM), lambda i: (0, 0)),
                  pl.BlockSpec((1, HEAD_DIM), lambda i: (0, 0)),
                  pl.BlockSpec((4, LANES), lambda i: (0, 0))],
        out_specs=pl.BlockSpec((None, l, N_BRANCHES * BRANCH_WIDTH), lambda i: (i, 0, 0)),
        out_shape=jax.ShapeDtypeStruct((b, l, N_BRANCHES * BRANCH_WIDTH), BF16),
        compiler_params=_cparams(("parallel",)),
        name="attn_ctx",
    )(p_ctx, lam_p, subg.reshape(1, HEAD_DIM), sink_tab.reshape(4, LANES))


def _merge_kernel(o0, o1, o2, o3, g0, g1, g2, g3, wb_ref, wo_ref, x_ref, mod_ref, ng_ref, *rest,
                  with_router):
    if with_router:
        router_ref, xo_ref, h_ref, lg_ref, acc_ref = rest
    else:
        xo_ref, h_ref, acc_ref = rest
    nc = pl.program_id(2)
    s = None
    for i, (o_ref, g_ref) in enumerate(((o0, g0), (o1, g1), (o2, g2), (o3, g3))):
        y = _dot(o_ref[...], wb_ref[i])
        term = jax.nn.sigmoid(g_ref[...].astype(F32)) * y
        s = term if s is None else s + term
    z = _dot(s.astype(BF16), wo_ref[...])

    @pl.when(nc == 0)
    def _():
        acc_ref[...] = z

    @pl.when(nc != 0)
    def _():
        acc_ref[...] += z

    @pl.when(nc == pl.num_programs(2) - 1)
    def _():
        xn = x_ref[...] + mod_ref[2:3, :] * acc_ref[...]
        xo_ref[...] = xn
        hn = _norm_mod(xn, ng_ref[...], mod_ref, 3, 4)
        h_ref[...] = hn.astype(h_ref.dtype)
        if with_router:
            lg_ref[...] = jnp.dot(hn, router_ref[...], preferred_element_type=F32,
                                  precision=lax.Precision.HIGHEST)


def _merge_call(o_parts, gates, w_branch, w_out, x, mod, next_g, router=None):
    b, r, d = x.shape
    nch = 4
    dc = d // nch
    tm = min(r, 512)
    with_router = router is not None

    def o_spec(cb):
        return pl.BlockSpec((None, tm, BRANCH_WIDTH), lambda i, t, c: (i, t, cb))

    def g_spec(br):
        return pl.BlockSpec((None, tm, dc), lambda i, t, c: (i, t, br * nch + c))

    in_specs = ([o_spec(cb) for _, cb in o_parts] + [g_spec(br) for br in range(N_BRANCHES)] + [
        pl.BlockSpec((N_BRANCHES, BRANCH_WIDTH, dc), lambda i, t, c: (0, 0, c)),
        pl.BlockSpec((dc, d), lambda i, t, c: (c, 0)),
        pl.BlockSpec((None, tm, d), lambda i, t, c: (i, t, 0)),
        pl.BlockSpec((None, 6, d), lambda i, t, c: (i, 0, 0)),
        pl.BlockSpec((1, d), lambda i, t, c: (0, 0))])
    args = [a for a, _ in o_parts] + [gates] * N_BRANCHES + [w_branch, w_out, x, mod, next_g.reshape(1, d)]
    row_spec = pl.BlockSpec((None, tm, d), lambda i, t, c: (i, t, 0))
    out_specs = [row_spec, row_spec]
    out_shape = [jax.ShapeDtypeStruct((b, r, d), F32), jax.ShapeDtypeStruct((b, r, d), BF16)]
    if with_router:
        in_specs.append(pl.BlockSpec((d, LANES), lambda i, t, c: (0, 0)))
        args.append(router)
        out_specs.append(pl.BlockSpec((None, tm, LANES), lambda i, t, c: (i, t, 0)))
        out_shape.append(jax.ShapeDtypeStruct((b, r, LANES), F32))
    return pl.pallas_call(
        functools.partial(_merge_kernel, with_router=with_router),
        grid=(b, r // tm, nch),
        in_specs=in_specs,
        out_specs=out_specs,
        out_shape=out_shape,
        scratch_shapes=[pltpu.VMEM((tm, d), F32)],
        compiler_params=_cparams(("parallel", "parallel", "arbitrary")),
        name="merge",
    )(*args)


def _swiglu_partial(h, wg, wu, wd):
    gate = _dot(h, wg)
    up = _dot(h, wu)
    act = gate * jax.nn.sigmoid(gate) * up
    return _dot(act.astype(BF16), wd)


def _ffn_kernel(h_ref, wg_ref, wu_ref, wd_ref, x_ref, mod_ref, ng_ref, nmod_ref, xo_ref, hn_ref,
                acc_ref):
    f = pl.program_id(2)
    z = _swiglu_partial(h_ref[...], wg_ref[...], wu_ref[...], wd_ref[...])

    @pl.when(f == 0)
    def _():
        acc_ref[...] = z

    @pl.when(f != 0)
    def _():
        acc_ref[...] += z

    @pl.when(f == pl.num_programs(2) - 1)
    def _():
        xn = x_ref[...] + mod_ref[5:6, :] * acc_ref[...]
        xo_ref[...] = xn
        hn_ref[...] = _norm_mod(xn, ng_ref[...], nmod_ref, 0, 1).astype(hn_ref.dtype)


def _ffn_call(h, wg, wu, wd, x, mod, next_g, next_mod):
    b, r, d = x.shape
    dff = wg.shape[1]
    tf = 512
    tm = min(r, 512)
    row_spec = pl.BlockSpec((None, tm, d), lambda i, t, f: (i, t, 0))
    mod_spec = pl.BlockSpec((None, 6, d), lambda i, t, f: (i, 0, 0))
    return pl.pallas_call(
        _ffn_kernel,
        grid=(b, r // tm, dff // tf),
        in_specs=[row_spec,
                  pl.BlockSpec((d, tf), lambda i, t, f: (0, f)),
                  pl.BlockSpec((d, tf), lambda i, t, f: (0, f)),
                  pl.BlockSpec((tf, d), lambda i, t, f: (f, 0)),
                  row_spec, mod_spec,
                  pl.BlockSpec((1, d), lambda i, t, f: (0, 0)),
                  mod_spec],
        out_specs=[row_spec, row_spec],
        out_shape=[jax.ShapeDtypeStruct((b, r, d), F32), jax.ShapeDtypeStruct((b, r, d), BF16)],
        scratch_shapes=[pltpu.VMEM((tm, d), F32)],
        compiler_params=_cparams(("parallel", "parallel", "arbitrary")),
        name="ffn_dense",
    )(h, wg, wu, wd, x, mod, next_g.reshape(1, d), next_mod)


def _route_kernel(lg_ref, idx_ref, w_ref):
    lg = lg_ref[...]
    lane = lax.broadcasted_iota(jnp.int32, lg.shape, 1)
    valid = lane < N_EXPERTS
    mx = jnp.max(jnp.where(valid, lg, -jnp.inf), axis=-1, keepdims=True)
    e = jnp.where(valid, jnp.exp(lg - mx), 0.0)
    p = e / jnp.sum(e, axis=-1, keepdims=True)
    p1 = jnp.max(p, axis=-1, keepdims=True)
    i1 = jnp.min(jnp.where(p == p1, lane, LANES), axis=-1, keepdims=True)
    rest = jnp.where(jnp.logical_or(lane == i1, jnp.logical_not(valid)), -1.0, p)
    p2 = jnp.max(rest, axis=-1, keepdims=True)
    i2 = jnp.min(jnp.where(rest == p2, lane, LANES), axis=-1, keepdims=True)
    tot = p1 + p2
    idx_ref[...] = jnp.where(lane == 0, i1, jnp.where(lane == 1, i2, 0))
    w_ref[...] = jnp.where(lane == 0, p1 / tot, jnp.where(lane == 1, p2 / tot, 0.0))


def _route_call(logits):
    r = logits.shape[0]
    tm = min(r, 1024)
    spec = pl.BlockSpec((tm, LANES), lambda i: (i, 0))
    return pl.pallas_call(
        _route_kernel,
        grid=(r // tm,),
        in_specs=[spec],
        out_specs=[spec, spec],
        out_shape=[jax.ShapeDtypeStruct((r, LANES), jnp.int32), jax.ShapeDtypeStruct((r, LANES), F32)],
        compiler_params=_cparams(("parallel",)),
        name="route_top2",
    )(logits)


def _moe_kernel(te_ref, nv_ref, x_ref, wg_ref, wu_ref, wd_ref, cw_ref, o_ref, acc_ref):
    i = pl.program_id(0)
    f = pl.program_id(1)

    @pl.when(i < nv_ref[0])
    def _():
        z = _swiglu_partial(x_ref[...], wg_ref[...], wu_ref[...], wd_ref[...])

        @pl.when(f == 0)
        def _():
            acc_ref[...] = z

        @pl.when(f != 0)
        def _():
            acc_ref[...] += z

    @pl.when(f == pl.num_programs(1) - 1)
    def _():
        valid = i < nv_ref[0]
        o_ref[...] = jnp.where(valid, cw_ref[...] * acc_ref[...], 0.0).astype(o_ref.dtype)


def _moe_call(xs, wg, wu, wd, cw, tile_expert, n_valid, tm):
    r, d = xs.shape
    dff = wg.shape[2]
    tf = 512
    grid_spec = pltpu.PrefetchScalarGridSpec(
        num_scalar_prefetch=2,
        grid=(r // tm, dff // tf),
        in_specs=[pl.BlockSpec((tm, d), lambda i, f, te, nv: (i, 0)),
                  pl.BlockSpec((None, d, tf), lambda i, f, te, nv: (te[i], 0, f)),
                  pl.BlockSpec((None, d, tf), lambda i, f, te, nv: (te[i], 0, f)),
                  pl.BlockSpec((None, tf, d), lambda i, f, te, nv: (te[i], f, 0)),
                  pl.BlockSpec((tm, 1), lambda i, f, te, nv: (i, 0))],
        out_specs=pl.BlockSpec((tm, d), lambda i, f, te, nv: (i, 0)),
        scratch_shapes=[pltpu.VMEM((tm, d), F32)])
    return pl.pallas_call(
        _moe_kernel,
        grid_spec=grid_spec,
        out_shape=jax.ShapeDtypeStruct((r, d), BF16),
        compiler_params=_cparams(("arbitrary", "arbitrary")),
        name="moe_experts",
    )(tile_expert, n_valid, xs, wg, wu, wd, cw)


def _moe_plan(idx, wts, tm):
    t = idx.shape[0]
    a = t * TOP_K
    flat_e = idx.reshape(a)
    onehot = (flat_e[:, None] == jnp.arange(N_EXPERTS, dtype=jnp.int32)[None, :]).astype(jnp.int32)
    csum = jnp.cumsum(onehot, axis=0)
    rank = jnp.sum((csum - onehot) * onehot, axis=1)
    counts = csum[-1]
    padded = ((counts + tm - 1) // tm) * tm
    ends = jnp.cumsum(padded)
    offs = ends - padded
    dest = offs[flat_e] + rank
    rows = a + N_EXPERTS * tm
    src_tok = jnp.zeros((rows,), jnp.int32).at[dest].set(jnp.arange(a, dtype=jnp.int32) // TOP_K)
    cw = jnp.zeros((rows,), F32).at[dest].set(wts.reshape(a))
    tile_start = jnp.arange(rows // tm, dtype=jnp.int32) * tm
    tile_expert = jnp.minimum(jnp.searchsorted(ends, tile_start, side="right"),
                              N_EXPERTS - 1).astype(jnp.int32)
    n_valid = (ends[-1] // tm).astype(jnp.int32).reshape(1)
    return dest.reshape(t, TOP_K), src_tok, cw.reshape(rows, 1), tile_expert, n_valid


def _final_kernel(x_ref, y_ref, mod_ref, g_ref, o_ref):
    xn = x_ref[...] + mod_ref[5:6, :] * y_ref[...].astype(F32)
    o_ref[...] = _rms(xn) * g_ref[...]


def _final_call(x, y, mod, g):
    b, r, d = x.shape
    tm = min(r, 512)
    row_spec = pl.BlockSpec((None, tm, d), lambda i, t: (i, t, 0))
    return pl.pallas_call(
        _final_kernel,
        grid=(b, r // tm),
        in_specs=[row_spec, row_spec,
                  pl.BlockSpec((None, 6, d), lambda i, t: (i, 0, 0)),
                  pl.BlockSpec((1, d), lambda i, t: (0, 0))],
        out_specs=row_spec,
        out_shape=jax.ShapeDtypeStruct((b, r, d), F32),
        compiler_params=_cparams(("parallel", "parallel")),
        name="final_norm",
    )(x, y, mod, g.reshape(1, d))


def _rope_tables(n):
    pos = jnp.arange(n)
    rows = (pos // GRID_W).astype(F32)
    cols = (pos % GRID_W).astype(F32)
    lane = np.arange(LANES)
    out = []
    for hw in (32, 16):
        period = 4 * hw
        u = lane % period
        use_cols = (u // (2 * hw)) == 1
        w = u % (2 * hw)
        freqs = jnp.asarray(ROPE_THETA ** (-(w % hw).astype(np.float32) / hw), F32)
        p = jnp.where(jnp.asarray(use_cols)[None, :], cols[:, None], rows[:, None])
        ang = p * freqs[None, :]
        sign = jnp.asarray(np.where(w < hw, -1.0, 1.0), F32)
        out += [jnp.cos(ang), jnp.sin(ang) * sign[None, :]]
    return out


def kernel(x, c, ctx, c_ctx, attn_norm_g, ffn_norm_g, ada_w, ada_b, w_in, qk_norm_g, diff_lambda,
           diff_subln_g, na_rpb, swa_sink, w_branch, w_out, ffn_w_gate, ffn_w_up, ffn_w_down,
           moe_router, moe_w_gate, moe_w_up, moe_w_down, final_norm_g):
    b, n, d = x.shape
    l = ctx.shape[1]
    depth = w_in.shape[0]
    assert depth == 2, "laid out for one dense layer followed by one routed last layer"
    rows_n = n // GRID_W

    lat_tables = _rope_tables(n)
    ones = jnp.ones((l, LANES), F32)
    zeros = jnp.zeros((l, LANES), F32)
    ctx_tables = [ones, zeros, ones, zeros]

    cvec = jnp.zeros((8, d), F32).at[:b].set(c).at[b].set(c_ctx)
    mods = []
    for i in range(depth):
        m = _adaln(cvec, ada_w[i], ada_b[i]).reshape(8, 6, d)
        mods.append((m[:b], jnp.broadcast_to(m[b:b + 1], (b, 6, d))))

    def mixers(i, h_lat, h_ctx):
        lambda_init = 0.8 - 0.6 * math.exp(-0.3 * i)
        w_qkv = w_in[i, :, :QKV_COLS].astype(BF16)
        sink_tab = jnp.broadcast_to((swa_sink[i].astype(F32) * LOG2E)[:, None], (4, LANES))
        p_lat = _qkv_call(h_lat, w_qkv, lat_tables, qk_norm_g[i])
        p_ctx = _qkv_call(h_ctx, w_qkv, ctx_tables, qk_norm_g[i])
        o_lat = [_gqa_call(p_lat, p_ctx),
                 _diff_call(p_lat, p_ctx, diff_lambda[i], diff_subln_g[i], lambda_init),
                 _na_call(p_lat, p_ctx, _na_bias_tables(na_rpb[i], rows_n)),
                 _swa_call(p_lat, p_ctx, sink_tab.reshape(2, 2, LANES))]
        return p_ctx, o_lat, sink_tab, lambda_init

    mod_lat, mod_ctx = mods[0]
    h_lat = _norm_mod_call(x, attn_norm_g[0], mod_lat, 0, 1)
    h_ctx = _norm_mod_call(ctx, attn_norm_g[0], mod_ctx, 0, 1)
    p_ctx, o_lat, sink_tab, lambda_init = mixers(0, h_lat, h_ctx)
    w_gate = w_in[0, :, QKV_COLS:].astype(BF16)
    wb = w_branch[0].astype(BF16)
    wo = w_out[0].astype(BF16)
    x_lat, h_lat = _merge_call([(o, 0) for o in o_lat], _matmul_call(h_lat, w_gate), wb, wo, x,
                               mod_lat, ffn_norm_g[0])
    o_ctx = _ctx_attn_call(p_ctx, diff_lambda[0], diff_subln_g[0], sink_tab, lambda_init)
    x_ctx, h_ctx = _merge_call([(o_ctx, k) for k in range(N_BRANCHES)], _matmul_call(h_ctx, w_gate),
                               wb, wo, ctx, mod_ctx, ffn_norm_g[0])
    wg = ffn_w_gate[0].astype(BF16)
    wu = ffn_w_up[0].astype(BF16)
    wd = ffn_w_down[0].astype(BF16)
    _, h_ctx = _ffn_call(h_ctx, wg, wu, wd, x_ctx, mod_ctx, attn_norm_g[1], mods[1][1])
    x_lat, h_lat = _ffn_call(h_lat, wg, wu, wd, x_lat, mod_lat, attn_norm_g[1], mods[1][0])

    mod_lat, _ = mods[1]
    _, o_lat, _, _ = mixers(1, h_lat, h_ctx)
    router = jnp.zeros((d, LANES), F32).at[:, :N_EXPERTS].set(moe_router[0])
    x_lat, h_lat, logits = _merge_call(
        [(o, 0) for o in o_lat], _matmul_call(h_lat, w_in[1, :, QKV_COLS:].astype(BF16)),
        w_branch[1].astype(BF16), w_out[1].astype(BF16), x_lat, mod_lat, ffn_norm_g[1], router)

    tm_moe = 512
    idx_pad, wts_pad = _route_call(logits.reshape(b * n, LANES))
    dest, src_tok, cw, tile_expert, n_valid = _moe_plan(idx_pad[:, :TOP_K], wts_pad[:, :TOP_K], tm_moe)
    xs = jnp.take(h_lat.reshape(b * n, d), src_tok, axis=0)
    ys = _moe_call(xs, moe_w_gate[0].astype(BF16), moe_w_up[0].astype(BF16),
                   moe_w_down[0].astype(BF16), cw, tile_expert, n_valid, tm_moe)
    y = (jnp.take(ys, dest[:, 0], axis=0).astype(F32)
         + jnp.take(ys, dest[:, 1], axis=0).astype(F32)).reshape(b, n, d)
    return _final_call(x_lat, y, mod_lat, final_norm_g)
```

```python
import functools
import math

import numpy as np
import jax
import jax.numpy as jnp
from jax import lax
from jax.experimental import pallas as pl
from jax.experimental.pallas import tpu as pltpu

F32 = jnp.float32
BF16 = jnp.bfloat16

GRID_W = 64
HEAD_DIM = 128
N_BRANCHES = 4
BRANCH_WIDTH = 4 * HEAD_DIM
DIFF_QK_DIM = 64
NA_ROWS = 8
NA_COLS = 16
SWA_WINDOW = 128
N_EXPERTS = 8
TOP_K = 2
NORM_EPS = 1e-6
ROPE_THETA = 10000.0
NEG_INF = -1e30
LOG2E = math.log2(math.e)

QKV_COLS = 5120
COL_GQA_Q, COL_GQA_K, COL_GQA_V = 0, 512, 768
COL_DIFF_Q, COL_DIFF_K, COL_DIFF_V = 1024, 1536, 2048
COL_NA_Q, COL_NA_K, COL_NA_V = 2560, 3072, 3584
COL_SWA_Q, COL_SWA_K, COL_SWA_V = 4096, 4608, 4864

LANES = 128
VMEM_LIMIT = 56 * 1024 * 1024

QS128 = HEAD_DIM ** -0.5 * LOG2E
QS64 = DIFF_QK_DIM ** -0.5 * LOG2E


def _cparams(sem):
    return pltpu.CompilerParams(dimension_semantics=sem, vmem_limit_bytes=VMEM_LIMIT)


def _dot(a, b):
    return jnp.dot(a, b, preferred_element_type=F32)


def _dot_nt(a, b):
    return lax.dot_general(a, b, (((1,), (1,)), ((), ())), preferred_element_type=F32)


def _rms(x):
    return x * lax.rsqrt(jnp.mean(x * x, axis=-1, keepdims=True) + NORM_EPS)


def _norm_mod(x, g, mod_ref, shift_idx, scale_idx):
    y = _rms(x) * g
    return y * (1.0 + mod_ref[scale_idx:scale_idx + 1, :]) + mod_ref[shift_idx:shift_idx + 1, :]


def _adaln_kernel(c_ref, w_ref, b_ref, o_ref):
    c = c_ref[...]
    s = c * jax.nn.sigmoid(c)
    o_ref[...] = jnp.dot(s, w_ref[...], preferred_element_type=F32,
                         precision=lax.Precision.HIGHEST) + b_ref[...]


def _adaln(cvec, w, b):
    rows, d = cvec.shape
    cols = w.shape[1]
    tn = 1024 if cols % 1024 == 0 else cols
    return pl.pallas_call(
        _adaln_kernel,
        grid=(cols // tn,),
        in_specs=[pl.BlockSpec((rows, d), lambda j: (0, 0)),
                  pl.BlockSpec((d, tn), lambda j: (0, j)),
                  pl.BlockSpec((1, tn), lambda j: (0, j))],
        out_specs=pl.BlockSpec((rows, tn), lambda j: (0, j)),
        out_shape=jax.ShapeDtypeStruct((rows, cols), F32),
        compiler_params=_cparams(("arbitrary",)),
        name="adaln",
    )(cvec, w, b.reshape(1, cols))


def _norm_mod_kernel(x_ref, g_ref, mod_ref, h_ref, *, shift_idx, scale_idx):
    h_ref[...] = _norm_mod(x_ref[...], g_ref[...], mod_ref, shift_idx, scale_idx).astype(BF16)


def _norm_mod_call(x, g, mod, shift_idx, scale_idx):
    b, r, d = x.shape
    tm = min(r, 512)
    return pl.pallas_call(
        functools.partial(_norm_mod_kernel, shift_idx=shift_idx, scale_idx=scale_idx),
        grid=(b, r // tm),
        in_specs=[pl.BlockSpec((None, tm, d), lambda i, t: (i, t, 0)),
                  pl.BlockSpec((1, d), lambda i, t: (0, 0)),
                  pl.BlockSpec((None, 6, d), lambda i, t: (i, 0, 0))],
        out_specs=pl.BlockSpec((None, tm, d), lambda i, t: (i, t, 0)),
        out_shape=jax.ShapeDtypeStruct((b, r, d), BF16),
        compiler_params=_cparams(("parallel", "parallel")),
        name="norm_mod",
    )(x, g.reshape(1, d), mod)


_PLAIN = (None, None, 1.0)
_QKV_TILE_OPS = {
    0: [(0, 32, QS128)] * 4,
    1: [(1, 32, 1.0)] * 2 + [_PLAIN] * 2,
    2: [(None, 16, QS64)] * 4,
    3: [(None, 16, 1.0)] * 4,
    5: [(None, None, QS128)] * 4,
    8: [(None, 32, QS128)] * 4,
    9: [(None, 32, 1.0)] * 2 + [_PLAIN] * 2,
}


def _rope(y, cos, sin, hw):
    lane = lax.broadcasted_iota(jnp.int32, y.shape, 1)
    first = (lane % (2 * hw)) < hw
    partner = jnp.where(first, pltpu.roll(y, LANES - hw, 1), pltpu.roll(y, hw, 1))
    return y * cos + partner * sin


def _qkv_kernel(h_ref, w_ref, cos32_ref, sin32_ref, cos16_ref, sin16_ref, g_ref, o_ref):
    j = pl.program_id(2)
    acc = _dot(h_ref[...], w_ref[...])

    def emit(ops):
        for c, (norm_row, hw, scale) in enumerate(ops):
            y = acc[:, c * LANES:(c + 1) * LANES]
            if norm_row is not None:
                y = _rms(y) * g_ref[norm_row:norm_row + 1, :]
            if hw == 32:
                y = _rope(y, cos32_ref[...], sin32_ref[...], 32)
            elif hw == 16:
                y = _rope(y, cos16_ref[...], sin16_ref[...], 16)
            if scale != 1.0:
                y = y * scale
            o_ref[:, c * LANES:(c + 1) * LANES] = y.astype(o_ref.dtype)

    special = sorted(_QKV_TILE_OPS)
    for jj in special:
        pl.when(j == jj)(functools.partial(emit, _QKV_TILE_OPS[jj]))
    is_plain = functools.reduce(jnp.logical_and, [j != jj for jj in special])
    pl.when(is_plain)(functools.partial(emit, [_PLAIN] * 4))


def _qkv_call(h, w_qkv, tables, qk_g):
    b, r, d = h.shape
    cols = w_qkv.shape[1]
    tn = 512
    tm = 1024 if r % 1024 == 0 else r
    tab_spec = pl.BlockSpec((tm, LANES), lambda i, t, j: (t, 0))
    return pl.pallas_call(
        _qkv_kernel,
        grid=(b, r // tm, cols // tn),
        in_specs=[pl.BlockSpec((None, tm, d), lambda i, t, j: (i, t, 0)),
                  pl.BlockSpec((d, tn), lambda i, t, j: (0, j)),
                  tab_spec, tab_spec, tab_spec, tab_spec,
                  pl.BlockSpec((2, LANES), lambda i, t, j: (0, 0))],
        out_specs=pl.BlockSpec((None, tm, tn), lambda i, t, j: (i, t, j)),
        out_shape=jax.ShapeDtypeStruct((b, r, cols), BF16),
        compiler_params=_cparams(("parallel", "parallel", "arbitrary")),
        name="qkv_proj",
    )(h, w_qkv, *tables, qk_g)


def _matmul_kernel(h_ref, w_ref, o_ref):
    o_ref[...] = _dot(h_ref[...], w_ref[...]).astype(o_ref.dtype)


def _matmul_call(h, w):
    b, r, d = h.shape
    cols = w.shape[1]
    tn = 512
    tm = 1024 if r % 1024 == 0 else r
    return pl.pallas_call(
        _matmul_kernel,
        grid=(b, r // tm, cols // tn),
        in_specs=[pl.BlockSpec((None, tm, d), lambda i, t, j: (i, t, 0)),
                  pl.BlockSpec((d, tn), lambda i, t, j: (0, j))],
        out_specs=pl.BlockSpec((None, tm, tn), lambda i, t, j: (i, t, j)),
        out_shape=jax.ShapeDtypeStruct((b, r, cols), BF16),
        compiler_params=_cparams(("parallel", "parallel", "arbitrary")),
        name="gate_proj",
    )(h, w)


def _diff_lambda(lam_ref, lambda_init):
    a = jnp.sum(lam_ref[0:1, :] * lam_ref[1:2, :], axis=-1, keepdims=True)
    b = jnp.sum(lam_ref[2:3, :] * lam_ref[3:4, :], axis=-1, keepdims=True)
    return jnp.exp(a) - jnp.exp(b) + lambda_init


def _split_maps(q):
    lane = lax.broadcasted_iota(jnp.int32, q.shape, 1)
    zero = jnp.zeros_like(q)
    return jnp.concatenate([jnp.where(lane < DIFF_QK_DIM, q, zero),
                            jnp.where(lane >= DIFF_QK_DIM, q, zero)], axis=0)


def _flash_pair(q2, kc_ref, vc_ref, kl_ref, vl_ref, tk):
    m_rows = q2.shape[0]
    n_lat = kl_ref.shape[0]

    def step(k, v, carry):
        m, l, acc = carry
        s = _dot_nt(q2, k)
        m_new = jnp.maximum(m, jnp.max(s, axis=-1, keepdims=True))
        p = jnp.exp2(s - m_new)
        alpha = jnp.exp2(m - m_new)
        l = alpha * l + jnp.sum(p, axis=-1, keepdims=True)
        acc = alpha * acc + _dot(p.astype(BF16), v)
        return m_new, l, acc

    carry = (jnp.full((m_rows, 1), NEG_INF, F32), jnp.zeros((m_rows, 1), F32),
             jnp.zeros((m_rows, HEAD_DIM), F32))
    carry = step(kc_ref[...], vc_ref[...], carry)

    def body(c, cr):
        off = pl.multiple_of(c * tk, tk)
        return step(kl_ref[pl.ds(off, tk), :], vl_ref[pl.ds(off, tk), :], cr)

    _, l, acc = lax.fori_loop(0, n_lat // tk, body, carry)
    return acc / l


def _gqa_kernel(q_ref, kc_ref, vc_ref, kl_ref, vl_ref, o_ref, *, tk):
    tq = q_ref.shape[0]
    q = q_ref[...]
    q2 = jnp.concatenate([q[:, :HEAD_DIM], q[:, HEAD_DIM:]], axis=0)
    o = _flash_pair(q2, kc_ref, vc_ref, kl_ref, vl_ref, tk)
    o_ref[:, :HEAD_DIM] = o[:tq].astype(o_ref.dtype)
    o_ref[:, HEAD_DIM:] = o[tq:].astype(o_ref.dtype)


def _diff_kernel(q_ref, kc_ref, vc_ref, kl_ref, vl_ref, lam_ref, subg_ref, o_ref, *, tk, lambda_init):
    tq = q_ref.shape[0]
    o = _flash_pair(_split_maps(q_ref[...]), kc_ref, vc_ref, kl_ref, vl_ref, tk)
    lam = _diff_lambda(lam_ref, lambda_init)
    d = o[:tq] - lam * o[tq:]
    o_ref[...] = (_rms(d) * subg_ref[...] * (1.0 - lambda_init)).astype(o_ref.dtype)


def _gqa_call(p_lat, p_ctx):
    b, n, _ = p_lat.shape
    l = p_ctx.shape[1]
    tq = 512
    tk = 512
    hd = HEAD_DIM
    kcol, vcol = COL_GQA_K // hd, COL_GQA_V // hd
    return pl.pallas_call(
        functools.partial(_gqa_kernel, tk=tk),
        grid=(b, 2, n // tq),
        in_specs=[pl.BlockSpec((None, tq, 2 * hd), lambda i, k, t: (i, t, k)),
                  pl.BlockSpec((None, l, hd), lambda i, k, t: (i, 0, kcol + k)),
                  pl.BlockSpec((None, l, hd), lambda i, k, t: (i, 0, vcol + k)),
                  pl.BlockSpec((None, n, hd), lambda i, k, t: (i, 0, kcol + k)),
                  pl.BlockSpec((None, n, hd), lambda i, k, t: (i, 0, vcol + k))],
        out_specs=pl.BlockSpec((None, tq, 2 * hd), lambda i, k, t: (i, t, k)),
        out_shape=jax.ShapeDtypeStruct((b, n, BRANCH_WIDTH), BF16),
        compiler_params=_cparams(("parallel", "parallel", "arbitrary")),
        name="attn_gqa",
    )(p_lat, p_ctx, p_ctx, p_lat, p_lat)


def _diff_call(p_lat, p_ctx, lam_p, subg, lambda_init):
    b, n, _ = p_lat.shape
    l = p_ctx.shape[1]
    tq = 512
    tk = 512
    hd = HEAD_DIM
    qcol, kcol, vcol = COL_DIFF_Q // hd, COL_DIFF_K // hd, COL_DIFF_V // hd
    return pl.pallas_call(
        functools.partial(_diff_kernel, tk=tk, lambda_init=lambda_init),
        grid=(b, 4, n // tq),
        in_specs=[pl.BlockSpec((None, tq, hd), lambda i, h, t: (i, t, qcol + h)),
                  pl.BlockSpec((None, l, hd), lambda i, h, t: (i, 0, kcol + h)),
                  pl.BlockSpec((None, l, hd), lambda i, h, t: (i, 0, vcol + h)),
                  pl.BlockSpec((None, n, hd), lambda i, h, t: (i, 0, kcol + h)),
                  pl.BlockSpec((None, n, hd), lambda i, h, t: (i, 0, vcol + h)),
                  pl.BlockSpec((4, DIFF_QK_DIM), lambda i, h, t: (0, 0)),
                  pl.BlockSpec((1, hd), lambda i, h, t: (0, 0))],
        out_specs=pl.BlockSpec((None, tq, hd), lambda i, h, t: (i, t, h)),
        out_shape=jax.ShapeDtypeStruct((b, n, BRANCH_WIDTH), BF16),
        compiler_params=_cparams(("parallel", "parallel", "arbitrary")),
        name="attn_diff",
    )(p_lat, p_ctx, p_ctx, p_lat, p_lat, lam_p, subg.reshape(1, hd))


NA_GROUP_ROWS = 8
NA_WIN_ROWS = 2 * NA_GROUP_ROWS


def _na_window_start(g, rows_n):
    return jnp.clip(g * NA_GROUP_ROWS - NA_ROWS // 2, 0, rows_n - NA_WIN_ROWS)


def _na_bias_tables(rpb, rows_n):
    n_groups = rows_n // NA_GROUP_ROWS
    variants = [0, min(1, n_groups - 1), n_groups - 1]
    qc = np.arange(GRID_W)[:, None]
    kc = np.arange(GRID_W)[None, :]
    cstart = np.clip(qc - NA_COLS // 2, 0, GRID_W - NA_COLS)
    ok_col = (kc >= cstart) & (kc < cstart + NA_COLS)
    dj = np.clip(kc - qc + NA_COLS - 1, 0, 2 * NA_COLS - 2)
    sel_j = (dj[..., None] == np.arange(2 * NA_COLS - 1)) & ok_col[..., None]
    a = np.arange(NA_GROUP_ROWS)[:, None]
    u = np.arange(NA_WIN_ROWS)[None, :]
    sel_i, ok_row = [], []
    for g in variants:
        ws = int(np.clip(g * NA_GROUP_ROWS - NA_ROWS // 2, 0, rows_n - NA_WIN_ROWS))
        r = g * NA_GROUP_ROWS + a
        rs = np.clip(r - NA_ROWS // 2, 0, rows_n - NA_ROWS)
        krow = ws + u
        ok = (krow >= rs) & (krow < rs + NA_ROWS)
        di = krow - r + NA_ROWS - 1
        sel_i.append((di[..., None] == np.arange(2 * NA_ROWS - 1)) & ok[..., None])
        ok_row.append(ok)
    sel_i = np.stack(sel_i)
    ok_row = np.stack(ok_row)
    hi = lax.Precision.HIGHEST
    by_col = jnp.einsum("hij,qkj->hiqk", rpb.astype(F32), jnp.asarray(sel_j, F32), precision=hi)
    bias = jnp.einsum("vaui,hiqk->hvaquk", jnp.asarray(sel_i, F32), by_col, precision=hi)
    valid = ok_row[:, :, None, :, None] & ok_col[None, None, :, None, :]
    out = jnp.where(valid[None], bias * LOG2E, NEG_INF)
    return out.reshape(rpb.shape[0], 3, NA_GROUP_ROWS * GRID_W, NA_WIN_ROWS * GRID_W)


def _na_kernel(q_ref, kc_ref, vc_ref, kl_ref, vl_ref, bias_ref, o_ref, *, rows_n):
    g = pl.program_id(2)
    win = NA_WIN_ROWS * GRID_W
    off = pl.multiple_of(_na_window_start(g, rows_n) * GRID_W, 256)
    q = q_ref[...]
    s_b = _dot_nt(q, kl_ref[pl.ds(off, win), :]) + bias_ref[...]
    s_c = _dot_nt(q, kc_ref[...])
    m = jnp.maximum(jnp.max(s_b, axis=-1, keepdims=True), jnp.max(s_c, axis=-1, keepdims=True))
    p_b = jnp.exp2(s_b - m)
    p_c = jnp.exp2(s_c - m)
    l = jnp.sum(p_b, axis=-1, keepdims=True) + jnp.sum(p_c, axis=-1, keepdims=True)
    o = _dot(p_c.astype(BF16), vc_ref[...]) + _dot(p_b.astype(BF16), vl_ref[pl.ds(off, win), :])
    o_ref[...] = (o / l).astype(o_ref.dtype)


def _na_call(p_lat, p_ctx, bias):
    b, n, _ = p_lat.shape
    l = p_ctx.shape[1]
    hd = HEAD_DIM
    rows_n = n // GRID_W
    n_groups = rows_n // NA_GROUP_ROWS
    tq = NA_GROUP_ROWS * GRID_W
    win = NA_WIN_ROWS * GRID_W
    qcol, kcol, vcol = COL_NA_Q // hd, COL_NA_K // hd, COL_NA_V // hd

    def bias_map(i, h, g):
        return (h, jnp.where(g == 0, 0, jnp.where(g == n_groups - 1, 2, 1)), 0, 0)

    return pl.pallas_call(
        functools.partial(_na_kernel, rows_n=rows_n),
        grid=(b, 4, n_groups),
        in_specs=[pl.BlockSpec((None, tq, hd), lambda i, h, g: (i, g, qcol + h)),
                  pl.BlockSpec((None, l, hd), lambda i, h, g: (i, 0, kcol + h)),
                  pl.BlockSpec((None, l, hd), lambda i, h, g: (i, 0, vcol + h)),
                  pl.BlockSpec((None, n, hd), lambda i, h, g: (i, 0, kcol + h)),
                  pl.BlockSpec((None, n, hd), lambda i, h, g: (i, 0, vcol + h)),
                  pl.BlockSpec((None, None, tq, win), bias_map)],
        out_specs=pl.BlockSpec((None, tq, hd), lambda i, h, g: (i, g, h)),
        out_shape=jax.ShapeDtypeStruct((b, n, BRANCH_WIDTH), BF16),
        compiler_params=_cparams(("parallel", "parallel", "arbitrary")),
        name="attn_na",
    )(p_lat, p_ctx, p_ctx, p_lat, p_lat, bias)


def _swa_kernel(q_ref, kc_ref, vc_ref, kl_ref, vl_ref, sink_ref, o_ref, *, win):
    t = pl.program_id(2)
    tq = q_ref.shape[0]
    n = kl_ref.shape[0]
    q0 = t * tq
    ws = pl.multiple_of(jnp.clip(q0 - SWA_WINDOW, 0, n - win), SWA_WINDOW)
    kb = kl_ref[pl.ds(ws, win), :]
    vb = vl_ref[pl.ds(ws, win), :]
    row = lax.broadcasted_iota(jnp.int32, (tq, win), 0)
    col = lax.broadcasted_iota(jnp.int32, (tq, win), 1)
    valid = jnp.abs(row + (q0 - ws) - col) <= SWA_WINDOW
    for gi in range(2):
        q = q_ref[:, gi * HEAD_DIM:(gi + 1) * HEAD_DIM]
        sink = sink_ref[gi:gi + 1, 0:1]
        s_b = jnp.where(valid, _dot_nt(q, kb), NEG_INF)
        s_c = _dot_nt(q, kc_ref[...])
        m = jnp.maximum(jnp.maximum(jnp.max(s_b, axis=-1, keepdims=True),
                                    jnp.max(s_c, axis=-1, keepdims=True)), sink)
        e_b = jnp.exp2(s_b - m)
        e_c = jnp.exp2(s_c - m)
        den = (jnp.sum(e_b, axis=-1, keepdims=True) + jnp.sum(e_c, axis=-1, keepdims=True)
               + jnp.exp2(sink - m))
        o = _dot(e_c.astype(BF16), vc_ref[...]) + _dot(e_b.astype(BF16), vb)
        o_ref[:, gi * HEAD_DIM:(gi + 1) * HEAD_DIM] = (o / den).astype(o_ref.dtype)


def _swa_call(p_lat, p_ctx, sink_tab):
    b, n, _ = p_lat.shape
    l = p_ctx.shape[1]
    hd = HEAD_DIM
    tq = 512
    win = tq + 2 * SWA_WINDOW
    qcol, kcol, vcol = COL_SWA_Q // (2 * hd), COL_SWA_K // hd, COL_SWA_V // hd
    return pl.pallas_call(
        functools.partial(_swa_kernel, win=win),
        grid=(b, 2, n // tq),
        in_specs=[pl.BlockSpec((None, tq, 2 * hd), lambda i, k, t: (i, t, qcol + k)),
                  pl.BlockSpec((None, l, hd), lambda i, k, t: (i, 0, kcol + k)),
                  pl.BlockSpec((None, l, hd), lambda i, k, t: (i, 0, vcol + k)),
                  pl.BlockSpec((None, n, hd), lambda i, k, t: (i, 0, kcol + k)),
                  pl.BlockSpec((None, n, hd), lambda i, k, t: (i, 0, vcol + k)),
                  pl.BlockSpec((None, 2, LANES), lambda i, k, t: (k, 0, 0))],
        out_specs=pl.BlockSpec((None, tq, 2 * hd), lambda i, k, t: (i, t, k)),
        out_shape=jax.ShapeDtypeStruct((b, n, BRANCH_WIDTH), BF16),
        compiler_params=_cparams(("parallel", "parallel", "arbitrary")),
        name="attn_swa",
    )(p_lat, p_ctx, p_ctx, p_lat, p_lat, sink_tab)


def _softmax_attend(q, k, v, sink=None):
    s = _dot_nt(q, k)
    m = jnp.max(s, axis=-1, keepdims=True)
    if sink is not None:
        m = jnp.maximum(m, sink)
    e = jnp.exp2(s - m)
    den = jnp.sum(e, axis=-1, keepdims=True)
    if sink is not None:
        den = den + jnp.exp2(sink - m)
    return _dot(e.astype(BF16), v) / den


def _ctx_attn_kernel(p_ref, lam_ref, subg_ref, sink_ref, o_ref, *, lambda_init):
    hd = HEAD_DIM

    def col(c0, h):
        return p_ref[:, c0 + h * hd:c0 + (h + 1) * hd]

    lam = _diff_lambda(lam_ref, lambda_init)
    for h in range(4):
        o = _softmax_attend(col(COL_GQA_Q, h), col(COL_GQA_K, h // 2), col(COL_GQA_V, h // 2))
        o_ref[:, h * hd:(h + 1) * hd] = o.astype(o_ref.dtype)
    for h in range(4):
        q = col(COL_DIFF_Q, h)
        tq = q.shape[0]
        o2 = _softmax_attend(_split_maps(q), col(COL_DIFF_K, h), col(COL_DIFF_V, h))
        d = o2[:tq] - lam * o2[tq:]
        d = _rms(d) * subg_ref[...] * (1.0 - lambda_init)
        o_ref[:, BRANCH_WIDTH + h * hd:BRANCH_WIDTH + (h + 1) * hd] = d.astype(o_ref.dtype)
    for h in range(4):
        o = _softmax_attend(col(COL_NA_Q, h), col(COL_NA_K, h), col(COL_NA_V, h))
        o_ref[:, 2 * BRANCH_WIDTH + h * hd:2 * BRANCH_WIDTH + (h + 1) * hd] = o.astype(o_ref.dtype)
    for h in range(4):
        o = _softmax_attend(col(COL_SWA_Q, h), col(COL_SWA_K, h // 2), col(COL_SWA_V, h // 2),
                            sink=sink_ref[h:h + 1, 0:1])
        o_ref[:, 3 * BRANCH_WIDTH + h * hd:3 * BRANCH_WIDTH + (h + 1) * hd] = o.astype(o_ref.dtype)


def _ctx_attn_call(p_ctx, lam_p, subg, sink_tab, lambda_init):
    b, l, cols = p_ctx.shape
    return pl.pallas_call(
        functools.partial(_ctx_attn_kernel, lambda_init=lambda_init),
        grid=(b,),
        in_specs=[pl.BlockSpec((None, l, cols), lambda i: (i, 0, 0)),
                  pl.BlockSpec((4, DIFF_QK_DIM), lambda i: (0, 0)),
                  pl.BlockSpec((1, HEAD_DIM), lambda i: (0, 0)),
                  pl.BlockSpec((4, LANES), lambda i: (0, 0))],
        out_specs=pl.BlockSpec((None, l, N_BRANCHES * BRANCH_WIDTH), lambda i: (i, 0, 0)),
        out_shape=jax.ShapeDtypeStruct((b, l, N_BRANCHES * BRANCH_WIDTH), BF16),
        compiler_params=_cparams(("parallel",)),
        name="attn_ctx",
    )(p_ctx, lam_p, subg.reshape(1, HEAD_DIM), sink_tab.reshape(4, LANES))


def _merge_kernel(o0, o1, o2, o3, g0, g1, g2, g3, wb_ref, wo_ref, x_ref, mod_ref, ng_ref, *rest,
                  with_router):
    if with_router:
        router_ref, xo_ref, h_ref, lg_ref, acc_ref = rest
    else:
        xo_ref, h_ref, acc_ref = rest
    nc = pl.program_id(2)
    s = None
    for i, (o_ref, g_ref) in enumerate(((o0, g0), (o1, g1), (o2, g2), (o3, g3))):
        y = _dot(o_ref[...], wb_ref[i])
        term = jax.nn.sigmoid(g_ref[...].astype(F32)) * y
        s = term if s is None else s + term
    z = _dot(s.astype(BF16), wo_ref[...])

    @pl.when(nc == 0)
    def _():
        acc_ref[...] = z

    @pl.when(nc != 0)
    def _():
        acc_ref[...] += z

    @pl.when(nc == pl.num_programs(2) - 1)
    def _():
        xn = x_ref[...] + mod_ref[2:3, :] * acc_ref[...]
        xo_ref[...] = xn
        hn = _norm_mod(xn, ng_ref[...], mod_ref, 3, 4)
        h_ref[...] = hn.astype(h_ref.dtype)
        if with_router:
            lg_ref[...] = jnp.dot(hn, router_ref[...], preferred_element_type=F32,
                                  precision=lax.Precision.HIGHEST)


def _merge_call(o_parts, gates, w_branch, w_out, x, mod, next_g, router=None):
    b, r, d = x.shape
    nch = 4
    dc = d // nch
    tm = min(r, 512)
    with_router = router is not None

    def o_spec(cb):
        return pl.BlockSpec((None, tm, BRANCH_WIDTH), lambda i, t, c: (i, t, cb))

    def g_spec(br):
        return pl.BlockSpec((None, tm, dc), lambda i, t, c: (i, t, br * nch + c))

    in_specs = ([o_spec(cb) for _, cb in o_parts] + [g_spec(br) for br in range(N_BRANCHES)] + [
        pl.BlockSpec((N_BRANCHES, BRANCH_WIDTH, dc), lambda i, t, c: (0, 0, c)),
        pl.BlockSpec((dc, d), lambda i, t, c: (c, 0)),
        pl.BlockSpec((None, tm, d), lambda i, t, c: (i, t, 0)),
        pl.BlockSpec((None, 6, d), lambda i, t, c: (i, 0, 0)),
        pl.BlockSpec((1, d), lambda i, t, c: (0, 0))])
    args = [a for a, _ in o_parts] + [gates] * N_BRANCHES + [w_branch, w_out, x, mod, next_g.reshape(1, d)]
    row_spec = pl.BlockSpec((None, tm, d), lambda i, t, c: (i, t, 0))
    out_specs = [row_spec, row_spec]
    out_shape = [jax.ShapeDtypeStruct((b, r, d), F32), jax.ShapeDtypeStruct((b, r, d), BF16)]
    if with_router:
        in_specs.append(pl.BlockSpec((d, LANES), lambda i, t, c: (0, 0)))
        args.append(router)
        out_specs.append(pl.BlockSpec((None, tm, LANES), lambda i, t, c: (i, t, 0)))
        out_shape.append(jax.ShapeDtypeStruct((b, r, LANES), F32))
    return pl.pallas_call(
        functools.partial(_merge_kernel, with_router=with_router),
        grid=(b, r // tm, nch),
        in_specs=in_specs,
        out_specs=out_specs,
        out_shape=out_shape,
        scratch_shapes=[pltpu.VMEM((tm, d), F32)],
        compiler_params=_cparams(("parallel", "parallel", "arbitrary")),
        name="merge",
    )(*args)


def _swiglu_partial(h, wg, wu, wd):
    gate = _dot(h, wg)
    up = _dot(h, wu)
    act = gate * jax.nn.sigmoid(gate) * up
    return _dot(act.astype(BF16), wd)


def _ffn_kernel(h_ref, wg_ref, wu_ref, wd_ref, x_ref, mod_ref, ng_ref, nmod_ref, xo_ref, hn_ref,
                acc_ref):
    f = pl.program_id(2)
    z = _swiglu_partial(h_ref[...], wg_ref[...], wu_ref[...], wd_ref[...])

    @pl.when(f == 0)
    def _():
        acc_ref[...] = z

    @pl.when(f != 0)
    def _():
        acc_ref[...] += z

    @pl.when(f == pl.num_programs(2) - 1)
    def _():
        xn = x_ref[...] + mod_ref[5:6, :] * acc_ref[...]
        xo_ref[...] = xn
        hn_ref[...] = _norm_mod(xn, ng_ref[...], nmod_ref, 0, 1).astype(hn_ref.dtype)


def _ffn_call(h, wg, wu, wd, x, mod, next_g, next_mod):
    b, r, d = x.shape
    dff = wg.shape[1]
    tf = 512
    tm = min(r, 512)
    row_spec = pl.BlockSpec((None, tm, d), lambda i, t, f: (i, t, 0))
    mod_spec = pl.BlockSpec((None, 6, d), lambda i, t, f: (i, 0, 0))
    return pl.pallas_call(
        _ffn_kernel,
        grid=(b, r // tm, dff // tf),
        in_specs=[row_spec,
                  pl.BlockSpec((d, tf), lambda i, t, f: (0, f)),
                  pl.BlockSpec((d, tf), lambda i, t, f: (0, f)),
                  pl.BlockSpec((tf, d), lambda i, t, f: (f, 0)),
                  row_spec, mod_spec,
                  pl.BlockSpec((1, d), lambda i, t, f: (0, 0)),
                  mod_spec],
        out_specs=[row_spec, row_spec],
        out_shape=[jax.ShapeDtypeStruct((b, r, d), F32), jax.ShapeDtypeStruct((b, r, d), BF16)],
        scratch_shapes=[pltpu.VMEM((tm, d), F32)],
        compiler_params=_cparams(("parallel", "parallel", "arbitrary")),
        name="ffn_dense",
    )(h, wg, wu, wd, x, mod, next_g.reshape(1, d), next_mod)


def _route_kernel(lg_ref, idx_ref, w_ref):
    lg = lg_ref[...]
    lane = lax.broadcasted_iota(jnp.int32, lg.shape, 1)
    valid = lane < N_EXPERTS
    mx = jnp.max(jnp.where(valid, lg, -jnp.inf), axis=-1, keepdims=True)
    e = jnp.where(valid, jnp.exp(lg - mx), 0.0)
    p = e / jnp.sum(e, axis=-1, keepdims=True)
    p1 = jnp.max(p, axis=-1, keepdims=True)
    i1 = jnp.min(jnp.where(p == p1, lane, LANES), axis=-1, keepdims=True)
    rest = jnp.where(jnp.logical_or(lane == i1, jnp.logical_not(valid)), -1.0, p)
    p2 = jnp.max(rest, axis=-1, keepdims=True)
    i2 = jnp.min(jnp.where(rest == p2, lane, LANES), axis=-1, keepdims=True)
    tot = p1 + p2
    idx_ref[...] = jnp.where(lane == 0, i1, jnp.where(lane == 1, i2, 0))
    w_ref[...] = jnp.where(lane == 0, p1 / tot, jnp.where(lane == 1, p2 / tot, 0.0))


def _route_call(logits):
    r = logits.shape[0]
    tm = min(r, 1024)
    spec = pl.BlockSpec((tm, LANES), lambda i: (i, 0))
    return pl.pallas_call(
        _route_kernel,
        grid=(r // tm,),
        in_specs=[spec],
        out_specs=[spec, spec],
        out_shape=[jax.ShapeDtypeStruct((r, LANES), jnp.int32), jax.ShapeDtypeStruct((r, LANES), F32)],
        compiler_params=_cparams(("parallel",)),
        name="route_top2",
    )(logits)


def _moe_kernel(te_ref, nv_ref, x_ref, wg_ref, wu_ref, wd_ref, cw_ref, o_ref, acc_ref):
    i = pl.program_id(0)
    f = pl.program_id(1)

    @pl.when(i < nv_ref[0])
    def _():
        z = _swiglu_partial(x_ref[...], wg_ref[...], wu_ref[...], wd_ref[...])

        @pl.when(f == 0)
        def _():
            acc_ref[...] = z

        @pl.when(f != 0)
        def _():
            acc_ref[...] += z

    @pl.when(f == pl.num_programs(1) - 1)
    def _():
        valid = i < nv_ref[0]
        o_ref[...] = jnp.where(valid, cw_ref[...] * acc_ref[...], 0.0).astype(o_ref.dtype)


def _moe_call(xs, wg, wu, wd, cw, tile_expert, n_valid, tm):
    r, d = xs.shape
    dff = wg.shape[2]
    tf = 512
    grid_spec = pltpu.PrefetchScalarGridSpec(
        num_scalar_prefetch=2,
        grid=(r // tm, dff // tf),
        in_specs=[pl.BlockSpec((tm, d), lambda i, f, te, nv: (i, 0)),
                  pl.BlockSpec((None, d, tf), lambda i, f, te, nv: (te[i], 0, f)),
                  pl.BlockSpec((None, d, tf), lambda i, f, te, nv: (te[i], 0, f)),
                  pl.BlockSpec((None, tf, d), lambda i, f, te, nv: (te[i], f, 0)),
                  pl.BlockSpec((tm, 1), lambda i, f, te, nv: (i, 0))],
        out_specs=pl.BlockSpec((tm, d), lambda i, f, te, nv: (i, 0)),
        scratch_shapes=[pltpu.VMEM((tm, d), F32)])
    return pl.pallas_call(
        _moe_kernel,
        grid_spec=grid_spec,
        out_shape=jax.ShapeDtypeStruct((r, d), BF16),
        compiler_params=_cparams(("arbitrary", "arbitrary")),
        name="moe_experts",
    )(tile_expert, n_valid, xs, wg, wu, wd, cw)


def _moe_plan(idx, wts, tm):
    t = idx.shape[0]
    a = t * TOP_K
    flat_e = idx.reshape(a)
    onehot = (flat_e[:, None] == jnp.arange(N_EXPERTS, dtype=jnp.int32)[None, :]).astype(jnp.int32)
    csum = jnp.cumsum(onehot, axis=0)
    rank = jnp.sum((csum - onehot) * onehot, axis=1)
    counts = csum[-1]
    padded = ((counts + tm - 1) // tm) * tm
    ends = jnp.cumsum(padded)
    offs = ends - padded
    dest = offs[flat_e] + rank
    rows = a + N_EXPERTS * tm
    src_tok = jnp.zeros((rows,), jnp.int32).at[dest].set(jnp.arange(a, dtype=jnp.int32) // TOP_K)
    cw = jnp.zeros((rows,), F32).at[dest].set(wts.reshape(a))
    tile_start = jnp.arange(rows // tm, dtype=jnp.int32) * tm
    tile_expert = jnp.minimum(jnp.sum((tile_start[:, None] >= ends[None, :]).astype(jnp.int32), axis=1),
                              N_EXPERTS - 1)
    n_valid = (ends[-1] // tm).astype(jnp.int32).reshape(1)
    return dest.reshape(t, TOP_K), src_tok, cw.reshape(rows, 1), tile_expert, n_valid


def _final_kernel(x_ref, y_ref, mod_ref, g_ref, o_ref):
    xn = x_ref[...] + mod_ref[5:6, :] * y_ref[...].astype(F32)
    o_ref[...] = _rms(xn) * g_ref[...]


def _final_call(x, y, mod, g):
    b, r, d = x.shape
    tm = min(r, 512)
    row_spec = pl.BlockSpec((None, tm, d), lambda i, t: (i, t, 0))
    return pl.pallas_call(
        _final_kernel,
        grid=(b, r // tm),
        in_specs=[row_spec, row_spec,
                  pl.BlockSpec((None, 6, d), lambda i, t: (i, 0, 0)),
                  pl.BlockSpec((1, d), lambda i, t: (0, 0))],
        out_specs=row_spec,
        out_shape=jax.ShapeDtypeStruct((b, r, d), F32),
        compiler_params=_cparams(("parallel", "parallel")),
        name="final_norm",
    )(x, y, mod, g.reshape(1, d))


def _rope_tables(n):
    pos = jnp.arange(n)
    rows = (pos // GRID_W).astype(F32)
    cols = (pos % GRID_W).astype(F32)
    lane = np.arange(LANES)
    out = []
    for hw in (32, 16):
        period = 4 * hw
        u = lane % period
        use_cols = (u // (2 * hw)) == 1
        w = u % (2 * hw)
        freqs = jnp.asarray(ROPE_THETA ** (-(w % hw).astype(np.float32) / hw), F32)
        p = jnp.where(jnp.asarray(use_cols)[None, :], cols[:, None], rows[:, None])
        ang = p * freqs[None, :]
        sign = jnp.asarray(np.where(w < hw, -1.0, 1.0), F32)
        out += [jnp.cos(ang), jnp.sin(ang) * sign[None, :]]
    return out


def kernel(x, c, ctx, c_ctx, attn_norm_g, ffn_norm_g, ada_w, ada_b, w_in, qk_norm_g, diff_lambda,
           diff_subln_g, na_rpb, swa_sink, w_branch, w_out, ffn_w_gate, ffn_w_up, ffn_w_down,
           moe_router, moe_w_gate, moe_w_up, moe_w_down, final_norm_g):
    b, n, d = x.shape
    l = ctx.shape[1]
    depth = w_in.shape[0]
    assert depth == 2, "laid out for one dense layer followed by one routed last layer"
    rows_n = n // GRID_W

    lat_tables = _rope_tables(n)
    ones = jnp.ones((l, LANES), F32)
    zeros = jnp.zeros((l, LANES), F32)
    ctx_tables = [ones, zeros, ones, zeros]

    cvec = jnp.zeros((8, d), F32).at[:b].set(c).at[b].set(c_ctx)
    mods = []
    for i in range(depth):
        m = _adaln(cvec, ada_w[i], ada_b[i]).reshape(8, 6, d)
        mods.append((m[:b], jnp.broadcast_to(m[b:b + 1], (b, 6, d))))

    def mixers(i, h_lat, h_ctx):
        lambda_init = 0.8 - 0.6 * math.exp(-0.3 * i)
        w_qkv = w_in[i, :, :QKV_COLS].astype(BF16)
        sink_tab = jnp.broadcast_to((swa_sink[i].astype(F32) * LOG2E)[:, None], (4, LANES))
        p_lat = _qkv_call(h_lat, w_qkv, lat_tables, qk_norm_g[i])
        p_ctx = _qkv_call(h_ctx, w_qkv, ctx_tables, qk_norm_g[i])
        o_lat = [_gqa_call(p_lat, p_ctx),
                 _diff_call(p_lat, p_ctx, diff_lambda[i], diff_subln_g[i], lambda_init),
                 _na_call(p_lat, p_ctx, _na_bias_tables(na_rpb[i], rows_n)),
                 _swa_call(p_lat, p_ctx, sink_tab.reshape(2, 2, LANES))]
        return p_ctx, o_lat, sink_tab, lambda_init

    mod_lat, mod_ctx = mods[0]
    h_lat = _norm_mod_call(x, attn_norm_g[0], mod_lat, 0, 1)
    h_ctx = _norm_mod_call(ctx, attn_norm_g[0], mod_ctx, 0, 1)
    p_ctx, o_lat, sink_tab, lambda_init = mixers(0, h_lat, h_ctx)
    w_gate = w_in[0, :, QKV_COLS:].astype(BF16)
    wb = w_branch[0].astype(BF16)
    wo = w_out[0].astype(BF16)
    x_lat, h_lat = _merge_call([(o, 0) for o in o_lat], _matmul_call(h_lat, w_gate), wb, wo, x,
                               mod_lat, ffn_norm_g[0])
    o_ctx = _ctx_attn_call(p_ctx, diff_lambda[0], diff_subln_g[0], sink_tab, lambda_init)
    x_ctx, h_ctx = _merge_call([(o_ctx, k) for k in range(N_BRANCHES)], _matmul_call(h_ctx, w_gate),
                               wb, wo, ctx, mod_ctx, ffn_norm_g[0])
    wg = ffn_w_gate[0].astype(BF16)
    wu = ffn_w_up[0].astype(BF16)
    wd = ffn_w_down[0].astype(BF16)
    _, h_ctx = _ffn_call(h_ctx, wg, wu, wd, x_ctx, mod_ctx, attn_norm_g[1], mods[1][1])
    x_lat, h_lat = _ffn_call(h_lat, wg, wu, wd, x_lat, mod_lat, attn_norm_g[1], mods[1][0])

    mod_lat, _ = mods[1]
    _, o_lat, _, _ = mixers(1, h_lat, h_ctx)
    router = jnp.zeros((d, LANES), F32).at[:, :N_EXPERTS].set(moe_router[0])
    x_lat, h_lat, logits = _merge_call(
        [(o, 0) for o in o_lat], _matmul_call(h_lat, w_in[1, :, QKV_COLS:].astype(BF16)),
        w_branch[1].astype(BF16), w_out[1].astype(BF16), x_lat, mod_lat, ffn_norm_g[1], router)

    tm_moe = 512
    idx_pad, wts_pad = _route_call(logits.reshape(b * n, LANES))
    dest, src_tok, cw, tile_expert, n_valid = _moe_plan(idx_pad[:, :TOP_K], wts_pad[:, :TOP_K], tm_moe)
    xs = jnp.take(h_lat.reshape(b * n, d), src_tok, axis=0)
    ys = _moe_call(xs, moe_w_gate[0].astype(BF16), moe_w_up[0].astype(BF16),
                   moe_w_down[0].astype(BF16), cw, tile_expert, n_valid, tm_moe)
    y = (jnp.take(ys, dest[:, 0], axis=0).astype(F32)
         + jnp.take(ys, dest[:, 1], axis=0).astype(F32)).reshape(b, n, d)
    return _final_call(x_lat, y, mod_lat, final_norm_g)
```

```python
import functools
import math

import numpy as np
import jax
import jax.numpy as jnp
from jax import lax
from jax.experimental import pallas as pl
from jax.experimental.pallas import tpu as pltpu

F32 = jnp.float32
BF16 = jnp.bfloat16

GRID_W = 64
HEAD_DIM = 128
N_BRANCHES = 4
BRANCH_WIDTH = 4 * HEAD_DIM
DIFF_QK_DIM = 64
NA_ROWS = 8
NA_COLS = 16
SWA_WINDOW = 128
N_EXPERTS = 8
TOP_K = 2
NORM_EPS = 1e-6
ROPE_THETA = 10000.0
NEG_INF = -1e30
LOG2E = math.log2(math.e)

QKV_COLS = 5120
COL_GQA_Q, COL_GQA_K, COL_GQA_V = 0, 512, 768
COL_DIFF_Q, COL_DIFF_K, COL_DIFF_V = 1024, 1536, 2048
COL_NA_Q, COL_NA_K, COL_NA_V = 2560, 3072, 3584
COL_SWA_Q, COL_SWA_K, COL_SWA_V = 4096, 4608, 4864

LANES = 128
VMEM_LIMIT = 56 * 1024 * 1024

QS128 = HEAD_DIM ** -0.5 * LOG2E
QS64 = DIFF_QK_DIM ** -0.5 * LOG2E


def _cparams(sem):
    return pltpu.CompilerParams(dimension_semantics=sem, vmem_limit_bytes=VMEM_LIMIT)


def _dot(a, b):
    return jnp.dot(a, b, preferred_element_type=F32)


def _dot_nt(a, b):
    return lax.dot_general(a, b, (((1,), (1,)), ((), ())), preferred_element_type=F32)


def _rms(x):
    return x * lax.rsqrt(jnp.mean(x * x, axis=-1, keepdims=True) + NORM_EPS)


def _norm_mod(x, g, mod_ref, shift_idx, scale_idx):
    y = _rms(x) * g
    return y * (1.0 + mod_ref[scale_idx:scale_idx + 1, :]) + mod_ref[shift_idx:shift_idx + 1, :]


def _adaln_kernel(c_ref, w_ref, b_ref, o_ref):
    c = c_ref[...]
    s = c * jax.nn.sigmoid(c)
    o_ref[...] = jnp.dot(s, w_ref[...], preferred_element_type=F32,
                         precision=lax.Precision.HIGHEST) + b_ref[...]


def _adaln(cvec, w, b):
    rows, d = cvec.shape
    cols = w.shape[1]
    tn = 1024 if cols % 1024 == 0 else cols
    return pl.pallas_call(
        _adaln_kernel,
        grid=(cols // tn,),
        in_specs=[pl.BlockSpec((rows, d), lambda j: (0, 0)),
                  pl.BlockSpec((d, tn), lambda j: (0, j)),
                  pl.BlockSpec((1, tn), lambda j: (0, j))],
        out_specs=pl.BlockSpec((rows, tn), lambda j: (0, j)),
        out_shape=jax.ShapeDtypeStruct((rows, cols), F32),
        compiler_params=_cparams(("arbitrary",)),
        name="adaln",
    )(cvec, w, b.reshape(1, cols))


def _norm_mod_kernel(x_ref, g_ref, mod_ref, h_ref, *, shift_idx, scale_idx):
    h_ref[...] = _norm_mod(x_ref[...], g_ref[...], mod_ref, shift_idx, scale_idx).astype(BF16)


def _norm_mod_call(x, g, mod, shift_idx, scale_idx):
    b, r, d = x.shape
    tm = min(r, 512)
    return pl.pallas_call(
        functools.partial(_norm_mod_kernel, shift_idx=shift_idx, scale_idx=scale_idx),
        grid=(b, r // tm),
        in_specs=[pl.BlockSpec((None, tm, d), lambda i, t: (i, t, 0)),
                  pl.BlockSpec((1, d), lambda i, t: (0, 0)),
                  pl.BlockSpec((None, 6, d), lambda i, t: (i, 0, 0))],
        out_specs=pl.BlockSpec((None, tm, d), lambda i, t: (i, t, 0)),
        out_shape=jax.ShapeDtypeStruct((b, r, d), BF16),
        compiler_params=_cparams(("parallel", "parallel")),
        name="norm_mod",
    )(x, g.reshape(1, d), mod)


_PLAIN = (None, None, 1.0)
_QKV_TILE_OPS = {
    0: [(0, 32, QS128)] * 4,
    1: [(1, 32, 1.0)] * 2 + [_PLAIN] * 2,
    2: [(None, 16, QS64)] * 4,
    3: [(None, 16, 1.0)] * 4,
    5: [(None, None, QS128)] * 4,
    8: [(None, 32, QS128)] * 4,
    9: [(None, 32, 1.0)] * 2 + [_PLAIN] * 2,
}


def _rope(y, cos, sin, hw):
    lane = lax.broadcasted_iota(jnp.int32, y.shape, 1)
    first = (lane % (2 * hw)) < hw
    partner = jnp.where(first, pltpu.roll(y, LANES - hw, 1), pltpu.roll(y, hw, 1))
    return y * cos + partner * sin


def _qkv_kernel(h_ref, w_ref, cos32_ref, sin32_ref, cos16_ref, sin16_ref, g_ref, o_ref):
    j = pl.program_id(2)
    acc = _dot(h_ref[...], w_ref[...])

    def emit(ops):
        for c, (norm_row, hw, scale) in enumerate(ops):
            y = acc[:, c * LANES:(c + 1) * LANES]
            if norm_row is not None:
                y = _rms(y) * g_ref[norm_row:norm_row + 1, :]
            if hw == 32:
                y = _rope(y, cos32_ref[...], sin32_ref[...], 32)
            elif hw == 16:
                y = _rope(y, cos16_ref[...], sin16_ref[...], 16)
            if scale != 1.0:
                y = y * scale
            o_ref[:, c * LANES:(c + 1) * LANES] = y.astype(o_ref.dtype)

    special = sorted(_QKV_TILE_OPS)
    for jj in special:
        pl.when(j == jj)(functools.partial(emit, _QKV_TILE_OPS[jj]))
    is_plain = functools.reduce(jnp.logical_and, [j != jj for jj in special])
    pl.when(is_plain)(functools.partial(emit, [_PLAIN] * 4))


def _qkv_call(h, w_qkv, tables, qk_g):
    b, r, d = h.shape
    cols = w_qkv.shape[1]
    tn = 512
    tm = 1024 if r % 1024 == 0 else r
    tab_spec = pl.BlockSpec((tm, LANES), lambda i, t, j: (t, 0))
    return pl.pallas_call(
        _qkv_kernel,
        grid=(b, r // tm, cols // tn),
        in_specs=[pl.BlockSpec((None, tm, d), lambda i, t, j: (i, t, 0)),
                  pl.BlockSpec((d, tn), lambda i, t, j: (0, j)),
                  tab_spec, tab_spec, tab_spec, tab_spec,
                  pl.BlockSpec((2, LANES), lambda i, t, j: (0, 0))],
        out_specs=pl.BlockSpec((None, tm, tn), lambda i, t, j: (i, t, j)),
        out_shape=jax.ShapeDtypeStruct((b, r, cols), BF16),
        compiler_params=_cparams(("parallel", "parallel", "arbitrary")),
        name="qkv_proj",
    )(h, w_qkv, *tables, qk_g)


def _matmul_kernel(h_ref, w_ref, o_ref):
    o_ref[...] = _dot(h_ref[...], w_ref[...]).astype(o_ref.dtype)


def _matmul_call(h, w):
    b, r, d = h.shape
    cols = w.shape[1]
    tn = 512
    tm = 1024 if r % 1024 == 0 else r
    return pl.pallas_call(
        _matmul_kernel,
        grid=(b, r // tm, cols // tn),
        in_specs=[pl.BlockSpec((None, tm, d), lambda i, t, j: (i, t, 0)),
                  pl.BlockSpec((d, tn), lambda i, t, j: (0, j))],
        out_specs=pl.BlockSpec((None, tm, tn), lambda i, t, j: (i, t, j)),
        out_shape=jax.ShapeDtypeStruct((b, r, cols), BF16),
        compiler_params=_cparams(("parallel", "parallel", "arbitrary")),
        name="gate_proj",
    )(h, w)


def _diff_lambda(lam_ref, lambda_init):
    a = jnp.sum(lam_ref[0:1, :] * lam_ref[1:2, :], axis=-1, keepdims=True)
    b = jnp.sum(lam_ref[2:3, :] * lam_ref[3:4, :], axis=-1, keepdims=True)
    return jnp.exp(a) - jnp.exp(b) + lambda_init


def _split_maps(q):
    lane = lax.broadcasted_iota(jnp.int32, q.shape, 1)
    zero = jnp.zeros_like(q)
    return jnp.concatenate([jnp.where(lane < DIFF_QK_DIM, q, zero),
                            jnp.where(lane >= DIFF_QK_DIM, q, zero)], axis=0)


FLASH_ROW_BLOCK = 256


def _flash_pair(q2, kc_ref, vc_ref, kl_ref, vl_ref, tk):
    m_rows = q2.shape[0]
    n_lat = kl_ref.shape[0]
    nb = m_rows // FLASH_ROW_BLOCK
    qs = [q2[i * FLASH_ROW_BLOCK:(i + 1) * FLASH_ROW_BLOCK] for i in range(nb)]

    def step(k, v, carry):
        v1 = jnp.concatenate([v, jnp.ones_like(v)], axis=1)
        out = []
        for q, (m, acc) in zip(qs, carry):
            s = _dot_nt(q, k)
            m_new = jnp.maximum(m, jnp.max(s, axis=-1, keepdims=True))
            p = jnp.exp2(s - m_new)
            acc = jnp.exp2(m - m_new) * acc + _dot(p.astype(BF16), v1)
            out.append((m_new, acc))
        return tuple(out)

    carry = tuple((jnp.full((FLASH_ROW_BLOCK, 1), NEG_INF, F32),
                   jnp.zeros((FLASH_ROW_BLOCK, 2 * HEAD_DIM), F32)) for _ in range(nb))
    carry = step(kc_ref[...], vc_ref[...], carry)

    for c in range(n_lat // tk):
        carry = step(kl_ref[c * tk:(c + 1) * tk, :], vl_ref[c * tk:(c + 1) * tk, :], carry)
    return jnp.concatenate([acc[:, :HEAD_DIM] / acc[:, HEAD_DIM:] for _, acc in carry], axis=0)


def _gqa_kernel(q_ref, kc_ref, vc_ref, kl_ref, vl_ref, o_ref, *, tk):
    tq = q_ref.shape[0]
    q = q_ref[...]
    q2 = jnp.concatenate([q[:, :HEAD_DIM], q[:, HEAD_DIM:]], axis=0)
    o = _flash_pair(q2, kc_ref, vc_ref, kl_ref, vl_ref, tk)
    o_ref[:, :HEAD_DIM] = o[:tq].astype(o_ref.dtype)
    o_ref[:, HEAD_DIM:] = o[tq:].astype(o_ref.dtype)


def _diff_kernel(q_ref, kc_ref, vc_ref, kl_ref, vl_ref, lam_ref, subg_ref, o_ref, *, tk, lambda_init):
    tq = q_ref.shape[0]
    o = _flash_pair(_split_maps(q_ref[...]), kc_ref, vc_ref, kl_ref, vl_ref, tk)
    lam = _diff_lambda(lam_ref, lambda_init)
    d = o[:tq] - lam * o[tq:]
    o_ref[...] = (_rms(d) * subg_ref[...] * (1.0 - lambda_init)).astype(o_ref.dtype)


def _gqa_call(p_lat, p_ctx):
    b, n, _ = p_lat.shape
    l = p_ctx.shape[1]
    tq = 512
    tk = 512
    hd = HEAD_DIM
    kcol, vcol = COL_GQA_K // hd, COL_GQA_V // hd
    return pl.pallas_call(
        functools.partial(_gqa_kernel, tk=tk),
        grid=(b, 2, n // tq),
        in_specs=[pl.BlockSpec((None, tq, 2 * hd), lambda i, k, t: (i, t, k)),
                  pl.BlockSpec((None, l, hd), lambda i, k, t: (i, 0, kcol + k)),
                  pl.BlockSpec((None, l, hd), lambda i, k, t: (i, 0, vcol + k)),
                  pl.BlockSpec((None, n, hd), lambda i, k, t: (i, 0, kcol + k)),
                  pl.BlockSpec((None, n, hd), lambda i, k, t: (i, 0, vcol + k))],
        out_specs=pl.BlockSpec((None, tq, 2 * hd), lambda i, k, t: (i, t, k)),
        out_shape=jax.ShapeDtypeStruct((b, n, BRANCH_WIDTH), BF16),
        compiler_params=_cparams(("parallel", "parallel", "arbitrary")),
        name="attn_gqa",
    )(p_lat, p_ctx, p_ctx, p_lat, p_lat)


def _diff_call(p_lat, p_ctx, lam_p, subg, lambda_init):
    b, n, _ = p_lat.shape
    l = p_ctx.shape[1]
    tq = 512
    tk = 512
    hd = HEAD_DIM
    qcol, kcol, vcol = COL_DIFF_Q // hd, COL_DIFF_K // hd, COL_DIFF_V // hd
    return pl.pallas_call(
        functools.partial(_diff_kernel, tk=tk, lambda_init=lambda_init),
        grid=(b, 4, n // tq),
        in_specs=[pl.BlockSpec((None, tq, hd), lambda i, h, t: (i, t, qcol + h)),
                  pl.BlockSpec((None, l, hd), lambda i, h, t: (i, 0, kcol + h)),
                  pl.BlockSpec((None, l, hd), lambda i, h, t: (i, 0, vcol + h)),
                  pl.BlockSpec((None, n, hd), lambda i, h, t: (i, 0, kcol + h)),
                  pl.BlockSpec((None, n, hd), lambda i, h, t: (i, 0, vcol + h)),
                  pl.BlockSpec((4, DIFF_QK_DIM), lambda i, h, t: (0, 0)),
                  pl.BlockSpec((1, hd), lambda i, h, t: (0, 0))],
        out_specs=pl.BlockSpec((None, tq, hd), lambda i, h, t: (i, t, h)),
        out_shape=jax.ShapeDtypeStruct((b, n, BRANCH_WIDTH), BF16),
        compiler_params=_cparams(("parallel", "parallel", "arbitrary")),
        name="attn_diff",
    )(p_lat, p_ctx, p_ctx, p_lat, p_lat, lam_p, subg.reshape(1, hd))


NA_GROUP_ROWS = 8
NA_WIN_ROWS = 2 * NA_GROUP_ROWS


def _na_window_start(g, rows_n):
    return jnp.clip(g * NA_GROUP_ROWS - NA_ROWS // 2, 0, rows_n - NA_WIN_ROWS)


def _na_bias_tables(rpb, rows_n):
    n_groups = rows_n // NA_GROUP_ROWS
    variants = [0, min(1, n_groups - 1), n_groups - 1]
    qc = np.arange(GRID_W)[:, None]
    kc = np.arange(GRID_W)[None, :]
    cstart = np.clip(qc - NA_COLS // 2, 0, GRID_W - NA_COLS)
    ok_col = (kc >= cstart) & (kc < cstart + NA_COLS)
    dj = np.clip(kc - qc + NA_COLS - 1, 0, 2 * NA_COLS - 2)
    sel_j = (dj[..., None] == np.arange(2 * NA_COLS - 1)) & ok_col[..., None]
    a = np.arange(NA_GROUP_ROWS)[:, None]
    u = np.arange(NA_WIN_ROWS)[None, :]
    sel_i, ok_row = [], []
    for g in variants:
        ws = int(np.clip(g * NA_GROUP_ROWS - NA_ROWS // 2, 0, rows_n - NA_WIN_ROWS))
        r = g * NA_GROUP_ROWS + a
        rs = np.clip(r - NA_ROWS // 2, 0, rows_n - NA_ROWS)
        krow = ws + u
        ok = (krow >= rs) & (krow < rs + NA_ROWS)
        di = krow - r + NA_ROWS - 1
        sel_i.append((di[..., None] == np.arange(2 * NA_ROWS - 1)) & ok[..., None])
        ok_row.append(ok)
    sel_i = np.stack(sel_i)
    ok_row = np.stack(ok_row)
    hi = lax.Precision.HIGHEST
    by_col = jnp.einsum("hij,qkj->hiqk", rpb.astype(F32), jnp.asarray(sel_j, F32), precision=hi)
    bias = jnp.einsum("vaui,hiqk->hvaquk", jnp.asarray(sel_i, F32), by_col, precision=hi)
    valid = ok_row[:, :, None, :, None] & ok_col[None, None, :, None, :]
    out = jnp.where(valid[None], bias * LOG2E, NEG_INF)
    return out.reshape(rpb.shape[0], 3, NA_GROUP_ROWS * GRID_W, NA_WIN_ROWS * GRID_W)


def _na_kernel(q_ref, kc_ref, vc_ref, kl_ref, vl_ref, bias_ref, o_ref, *, rows_n):
    g = pl.program_id(2)
    win = NA_WIN_ROWS * GRID_W
    off = pl.multiple_of(_na_window_start(g, rows_n) * GRID_W, 256)
    q = q_ref[...]
    s_b = _dot_nt(q, kl_ref[pl.ds(off, win), :]) + bias_ref[...]
    s_c = _dot_nt(q, kc_ref[...])
    m = jnp.maximum(jnp.max(s_b, axis=-1, keepdims=True), jnp.max(s_c, axis=-1, keepdims=True))
    p_b = jnp.exp2(s_b - m)
    p_c = jnp.exp2(s_c - m)
    l = jnp.sum(p_b, axis=-1, keepdims=True) + jnp.sum(p_c, axis=-1, keepdims=True)
    o = _dot(p_c.astype(BF16), vc_ref[...]) + _dot(p_b.astype(BF16), vl_ref[pl.ds(off, win), :])
    o_ref[...] = (o / l).astype(o_ref.dtype)


def _na_call(p_lat, p_ctx, bias):
    b, n, _ = p_lat.shape
    l = p_ctx.shape[1]
    hd = HEAD_DIM
    rows_n = n // GRID_W
    n_groups = rows_n // NA_GROUP_ROWS
    tq = NA_GROUP_ROWS * GRID_W
    win = NA_WIN_ROWS * GRID_W
    qcol, kcol, vcol = COL_NA_Q // hd, COL_NA_K // hd, COL_NA_V // hd

    def bias_map(i, h, g):
        return (h, jnp.where(g == 0, 0, jnp.where(g == n_groups - 1, 2, 1)), 0, 0)

    return pl.pallas_call(
        functools.partial(_na_kernel, rows_n=rows_n),
        grid=(b, 4, n_groups),
        in_specs=[pl.BlockSpec((None, tq, hd), lambda i, h, g: (i, g, qcol + h)),
                  pl.BlockSpec((None, l, hd), lambda i, h, g: (i, 0, kcol + h)),
                  pl.BlockSpec((None, l, hd), lambda i, h, g: (i, 0, vcol + h)),
                  pl.BlockSpec((None, n, hd), lambda i, h, g: (i, 0, kcol + h)),
                  pl.BlockSpec((None, n, hd), lambda i, h, g: (i, 0, vcol + h)),
                  pl.BlockSpec((None, None, tq, win), bias_map)],
        out_specs=pl.BlockSpec((None, tq, hd), lambda i, h, g: (i, g, h)),
        out_shape=jax.ShapeDtypeStruct((b, n, BRANCH_WIDTH), BF16),
        compiler_params=_cparams(("parallel", "parallel", "arbitrary")),
        name="attn_na",
    )(p_lat, p_ctx, p_ctx, p_lat, p_lat, bias)


def _swa_kernel(q_ref, kc_ref, vc_ref, kl_ref, vl_ref, sink_ref, o_ref, *, win):
    t = pl.program_id(2)
    tq = q_ref.shape[0]
    n = kl_ref.shape[0]
    q0 = t * tq
    ws = pl.multiple_of(jnp.clip(q0 - SWA_WINDOW, 0, n - win), SWA_WINDOW)
    kb = kl_ref[pl.ds(ws, win), :]
    vb = vl_ref[pl.ds(ws, win), :]
    row = lax.broadcasted_iota(jnp.int32, (tq, win), 0)
    col = lax.broadcasted_iota(jnp.int32, (tq, win), 1)
    valid = jnp.abs(row + (q0 - ws) - col) <= SWA_WINDOW
    for gi in range(2):
        q = q_ref[:, gi * HEAD_DIM:(gi + 1) * HEAD_DIM]
        sink = sink_ref[gi:gi + 1, 0:1]
        s_b = jnp.where(valid, _dot_nt(q, kb), NEG_INF)
        s_c = _dot_nt(q, kc_ref[...])
        m = jnp.maximum(jnp.maximum(jnp.max(s_b, axis=-1, keepdims=True),
                                    jnp.max(s_c, axis=-1, keepdims=True)), sink)
        e_b = jnp.exp2(s_b - m)
        e_c = jnp.exp2(s_c - m)
        den = (jnp.sum(e_b, axis=-1, keepdims=True) + jnp.sum(e_c, axis=-1, keepdims=True)
               + jnp.exp2(sink - m))
        o = _dot(e_c.astype(BF16), vc_ref[...]) + _dot(e_b.astype(BF16), vb)
        o_ref[:, gi * HEAD_DIM:(gi + 1) * HEAD_DIM] = (o / den).astype(o_ref.dtype)


def _swa_call(p_lat, p_ctx, sink_tab):
    b, n, _ = p_lat.shape
    l = p_ctx.shape[1]
    hd = HEAD_DIM
    tq = 512
    win = tq + 2 * SWA_WINDOW
    qcol, kcol, vcol = COL_SWA_Q // (2 * hd), COL_SWA_K // hd, COL_SWA_V // hd
    return pl.pallas_call(
        functools.partial(_swa_kernel, win=win),
        grid=(b, 2, n // tq),
        in_specs=[pl.BlockSpec((None, tq, 2 * hd), lambda i, k, t: (i, t, qcol + k)),
                  pl.BlockSpec((None, l, hd), lambda i, k, t: (i, 0, kcol + k)),
                  pl.BlockSpec((None, l, hd), lambda i, k, t: (i, 0, vcol + k)),
                  pl.BlockSpec((None, n, hd), lambda i, k, t: (i, 0, kcol + k)),
                  pl.BlockSpec((None, n, hd), lambda i, k, t: (i, 0, vcol + k)),
                  pl.BlockSpec((None, 2, LANES), lambda i, k, t: (k, 0, 0))],
        out_specs=pl.BlockSpec((None, tq, 2 * hd), lambda i, k, t: (i, t, k)),
        out_shape=jax.ShapeDtypeStruct((b, n, BRANCH_WIDTH), BF16),
        compiler_params=_cparams(("parallel", "parallel", "arbitrary")),
        name="attn_swa",
    )(p_lat, p_ctx, p_ctx, p_lat, p_lat, sink_tab)


def _softmax_attend(q, k, v, sink=None):
    s = _dot_nt(q, k)
    m = jnp.max(s, axis=-1, keepdims=True)
    if sink is not None:
        m = jnp.maximum(m, sink)
    e = jnp.exp2(s - m)
    den = jnp.sum(e, axis=-1, keepdims=True)
    if sink is not None:
        den = den + jnp.exp2(sink - m)
    return _dot(e.astype(BF16), v) / den


def _ctx_attn_kernel(p_ref, lam_ref, subg_ref, sink_ref, o_ref, *, lambda_init):
    hd = HEAD_DIM

    def col(c0, h):
        return p_ref[:, c0 + h * hd:c0 + (h + 1) * hd]

    lam = _diff_lambda(lam_ref, lambda_init)
    for h in range(4):
        o = _softmax_attend(col(COL_GQA_Q, h), col(COL_GQA_K, h // 2), col(COL_GQA_V, h // 2))
        o_ref[:, h * hd:(h + 1) * hd] = o.astype(o_ref.dtype)
    for h in range(4):
        q = col(COL_DIFF_Q, h)
        tq = q.shape[0]
        o2 = _softmax_attend(_split_maps(q), col(COL_DIFF_K, h), col(COL_DIFF_V, h))
        d = o2[:tq] - lam * o2[tq:]
        d = _rms(d) * subg_ref[...] * (1.0 - lambda_init)
        o_ref[:, BRANCH_WIDTH + h * hd:BRANCH_WIDTH + (h + 1) * hd] = d.astype(o_ref.dtype)
    for h in range(4):
        o = _softmax_attend(col(COL_NA_Q, h), col(COL_NA_K, h), col(COL_NA_V, h))
        o_ref[:, 2 * BRANCH_WIDTH + h * hd:2 * BRANCH_WIDTH + (h + 1) * hd] = o.astype(o_ref.dtype)
    for h in range(4):
        o = _softmax_attend(col(COL_SWA_Q, h), col(COL_SWA_K, h // 2), col(COL_SWA_V, h // 2),
                            sink=sink_ref[h:h + 1, 0:1])
        o_ref[:, 3 * BRANCH_WIDTH + h * hd:3 * BRANCH_WIDTH + (h + 1) * hd] = o.astype(o_ref.dtype)


def _ctx_attn_call(p_ctx, lam_p, subg, sink_tab, lambda_init):
    b, l, cols = p_ctx.shape
    return pl.pallas_call(
        functools.partial(_ctx_attn_kernel, lambda_init=lambda_init),
        grid=(b,),
        in_specs=[pl.BlockSpec((None, l, cols), lambda i: (i, 0, 0)),
                  pl.BlockSpec((4, DIFF_QK_DIM), lambda i: (0, 0)),
                  pl.BlockSpec((1, HEAD_DIM), lambda i: (0, 0)),
                  pl.BlockSpec((4, LANES), lambda i: (0, 0))],
        out_specs=pl.BlockSpec((None, l, N_BRANCHES * BRANCH_WIDTH), lambda i: (i, 0, 0)),
        out_shape=jax.ShapeDtypeStruct((b, l, N_BRANCHES * BRANCH_WIDTH), BF16),
        compiler_params=_cparams(("parallel",)),
        name="attn_ctx",
    )(p_ctx, lam_p, subg.reshape(1, HEAD_DIM), sink_tab.reshape(4, LANES))


def _merge_kernel(o0, o1, o2, o3, g0, g1, g2, g3, wb_ref, wo_ref, x_ref, mod_ref, ng_ref, *rest,
                  with_router):
    if with_router:
        router_ref, xo_ref, h_ref, lg_ref, acc_ref = rest
    else:
        xo_ref, h_ref, acc_ref = rest
    nc = pl.program_id(2)
    s = None
    for i, (o_ref, g_ref) in enumerate(((o0, g0), (o1, g1), (o2, g2), (o3, g3))):
        y = _dot(o_ref[...], wb_ref[i])
        term = jax.nn.sigmoid(g_ref[...].astype(F32)) * y
        s = term if s is None else s + term
    z = _dot(s.astype(BF16), wo_ref[...])

    @pl.when(nc == 0)
    def _():
        acc_ref[...] = z

    @pl.when(nc != 0)
    def _():
        acc_ref[...] += z

    @pl.when(nc == pl.num_programs(2) - 1)
    def _():
        xn = x_ref[...] + mod_ref[2:3, :] * acc_ref[...]
        xo_ref[...] = xn
        hn = _norm_mod(xn, ng_ref[...], mod_ref, 3, 4)
        h_ref[...] = hn.astype(h_ref.dtype)
        if with_router:
            lg_ref[...] = jnp.dot(hn, router_ref[...], preferred_element_type=F32,
                                  precision=lax.Precision.HIGHEST)


def _merge_call(o_parts, gates, w_branch, w_out, x, mod, next_g, router=None):
    b, r, d = x.shape
    nch = 4
    dc = d // nch
    tm = min(r, 512)
    with_router = router is not None

    def o_spec(cb):
        return pl.BlockSpec((None, tm, BRANCH_WIDTH), lambda i, t, c: (i, t, cb))

    def g_spec(br):
        return pl.BlockSpec((None, tm, dc), lambda i, t, c: (i, t, br * nch + c))

    in_specs = ([o_spec(cb) for _, cb in o_parts] + [g_spec(br) for br in range(N_BRANCHES)] + [
        pl.BlockSpec((N_BRANCHES, BRANCH_WIDTH, dc), lambda i, t, c: (0, 0, c)),
        pl.BlockSpec((dc, d), lambda i, t, c: (c, 0)),
        pl.BlockSpec((None, tm, d), lambda i, t, c: (i, t, 0)),
        pl.BlockSpec((None, 6, d), lambda i, t, c: (i, 0, 0)),
        pl.BlockSpec((1, d), lambda i, t, c: (0, 0))])
    args = [a for a, _ in o_parts] + [gates] * N_BRANCHES + [w_branch, w_out, x, mod, next_g.reshape(1, d)]
    row_spec = pl.BlockSpec((None, tm, d), lambda i, t, c: (i, t, 0))
    out_specs = [row_spec, row_spec]
    out_shape = [jax.ShapeDtypeStruct((b, r, d), F32), jax.ShapeDtypeStruct((b, r, d), BF16)]
    if with_router:
        in_specs.append(pl.BlockSpec((d, LANES), lambda i, t, c: (0, 0)))
        args.append(router)
        out_specs.append(pl.BlockSpec((None, tm, LANES), lambda i, t, c: (i, t, 0)))
        out_shape.append(jax.ShapeDtypeStruct((b, r, LANES), F32))
    return pl.pallas_call(
        functools.partial(_merge_kernel, with_router=with_router),
        grid=(b, r // tm, nch),
        in_specs=in_specs,
        out_specs=out_specs,
        out_shape=out_shape,
        scratch_shapes=[pltpu.VMEM((tm, d), F32)],
        compiler_params=_cparams(("parallel", "parallel", "arbitrary")),
        name="merge",
    )(*args)


def _swiglu_partial(h, wg, wu, wd):
    gate = _dot(h, wg)
    up = _dot(h, wu)
    act = gate * jax.nn.sigmoid(gate) * up
    return _dot(act.astype(BF16), wd)


def _ffn_kernel(h_ref, wg_ref, wu_ref, wd_ref, x_ref, mod_ref, ng_ref, nmod_ref, xo_ref, hn_ref,
                acc_ref):
    f = pl.program_id(2)
    z = _swiglu_partial(h_ref[...], wg_ref[...], wu_ref[...], wd_ref[...])

    @pl.when(f == 0)
    def _():
        acc_ref[...] = z

    @pl.when(f != 0)
    def _():
        acc_ref[...] += z

    @pl.when(f == pl.num_programs(2) - 1)
    def _():
        xn = x_ref[...] + mod_ref[5:6, :] * acc_ref[...]
        xo_ref[...] = xn
        hn_ref[...] = _norm_mod(xn, ng_ref[...], nmod_ref, 0, 1).astype(hn_ref.dtype)


def _ffn_call(h, wg, wu, wd, x, mod, next_g, next_mod):
    b, r, d = x.shape
    dff = wg.shape[1]
    tf = 512
    tm = min(r, 512)
    row_spec = pl.BlockSpec((None, tm, d), lambda i, t, f: (i, t, 0))
    mod_spec = pl.BlockSpec((None, 6, d), lambda i, t, f: (i, 0, 0))
    return pl.pallas_call(
        _ffn_kernel,
        grid=(b, r // tm, dff // tf),
        in_specs=[row_spec,
                  pl.BlockSpec((d, tf), lambda i, t, f: (0, f)),
                  pl.BlockSpec((d, tf), lambda i, t, f: (0, f)),
                  pl.BlockSpec((tf, d), lambda i, t, f: (f, 0)),
                  row_spec, mod_spec,
                  pl.BlockSpec((1, d), lambda i, t, f: (0, 0)),
                  mod_spec],
        out_specs=[row_spec, row_spec],
        out_shape=[jax.ShapeDtypeStruct((b, r, d), F32), jax.ShapeDtypeStruct((b, r, d), BF16)],
        scratch_shapes=[pltpu.VMEM((tm, d), F32)],
        compiler_params=_cparams(("parallel", "parallel", "arbitrary")),
        name="ffn_dense",
    )(h, wg, wu, wd, x, mod, next_g.reshape(1, d), next_mod)


def _route_kernel(lg_ref, idx_ref, w_ref):
    lg = lg_ref[...]
    lane = lax.broadcasted_iota(jnp.int32, lg.shape, 1)
    valid = lane < N_EXPERTS
    mx = jnp.max(jnp.where(valid, lg, -jnp.inf), axis=-1, keepdims=True)
    e = jnp.where(valid, jnp.exp(lg - mx), 0.0)
    p = e / jnp.sum(e, axis=-1, keepdims=True)
    p1 = jnp.max(p, axis=-1, keepdims=True)
    i1 = jnp.min(jnp.where(p == p1, lane, LANES), axis=-1, keepdims=True)
    rest = jnp.where(jnp.logical_or(lane == i1, jnp.logical_not(valid)), -1.0, p)
    p2 = jnp.max(rest, axis=-1, keepdims=True)
    i2 = jnp.min(jnp.where(rest == p2, lane, LANES), axis=-1, keepdims=True)
    tot = p1 + p2
    idx_ref[...] = jnp.where(lane == 0, i1, jnp.where(lane == 1, i2, 0))
    w_ref[...] = jnp.where(lane == 0, p1 / tot, jnp.where(lane == 1, p2 / tot, 0.0))


def _route_call(logits):
    r = logits.shape[0]
    tm = min(r, 1024)
    spec = pl.BlockSpec((tm, LANES), lambda i: (i, 0))
    return pl.pallas_call(
        _route_kernel,
        grid=(r // tm,),
        in_specs=[spec],
        out_specs=[spec, spec],
        out_shape=[jax.ShapeDtypeStruct((r, LANES), jnp.int32), jax.ShapeDtypeStruct((r, LANES), F32)],
        compiler_params=_cparams(("parallel",)),
        name="route_top2",
    )(logits)


def _moe_kernel(te_ref, nv_ref, x_ref, wg_ref, wu_ref, wd_ref, cw_ref, o_ref, acc_ref):
    i = pl.program_id(0)
    f = pl.program_id(1)

    @pl.when(i < nv_ref[0])
    def _():
        z = _swiglu_partial(x_ref[...], wg_ref[...], wu_ref[...], wd_ref[...])

        @pl.when(f == 0)
        def _():
            acc_ref[...] = z

        @pl.when(f != 0)
        def _():
            acc_ref[...] += z

    @pl.when(f == pl.num_programs(1) - 1)
    def _():
        valid = i < nv_ref[0]
        o_ref[...] = jnp.where(valid, cw_ref[...] * acc_ref[...], 0.0).astype(o_ref.dtype)


def _moe_call(xs, wg, wu, wd, cw, tile_expert, n_valid, tm):
    r, d = xs.shape
    dff = wg.shape[2]
    tf = 512
    grid_spec = pltpu.PrefetchScalarGridSpec(
        num_scalar_prefetch=2,
        grid=(r // tm, dff // tf),
        in_specs=[pl.BlockSpec((tm, d), lambda i, f, te, nv: (i, 0)),
                  pl.BlockSpec((None, d, tf), lambda i, f, te, nv: (te[i], 0, f)),
                  pl.BlockSpec((None, d, tf), lambda i, f, te, nv: (te[i], 0, f)),
                  pl.BlockSpec((None, tf, d), lambda i, f, te, nv: (te[i], f, 0)),
                  pl.BlockSpec((tm, 1), lambda i, f, te, nv: (i, 0))],
        out_specs=pl.BlockSpec((tm, d), lambda i, f, te, nv: (i, 0)),
        scratch_shapes=[pltpu.VMEM((tm, d), F32)])
    return pl.pallas_call(
        _moe_kernel,
        grid_spec=grid_spec,
        out_shape=jax.ShapeDtypeStruct((r, d), BF16),
        compiler_params=_cparams(("arbitrary", "arbitrary")),
        name="moe_experts",
    )(tile_expert, n_valid, xs, wg, wu, wd, cw)


def _moe_plan(idx, wts, tm):
    t = idx.shape[0]
    a = t * TOP_K
    flat_e = idx.reshape(a)
    onehot = (flat_e[:, None] == jnp.arange(N_EXPERTS, dtype=jnp.int32)[None, :]).astype(jnp.int32)
    csum = jnp.cumsum(onehot, axis=0)
    rank = jnp.sum((csum - onehot) * onehot, axis=1)
    counts = csum[-1]
    padded = ((counts + tm - 1) // tm) * tm
    ends = jnp.cumsum(padded)
    offs = ends - padded
    dest = offs[flat_e] + rank
    rows = a + N_EXPERTS * tm
    src_tok = jnp.zeros((rows,), jnp.int32).at[dest].set(jnp.arange(a, dtype=jnp.int32) // TOP_K)
    cw = jnp.zeros((rows,), F32).at[dest].set(wts.reshape(a))
    tile_start = jnp.arange(rows // tm, dtype=jnp.int32) * tm
    tile_expert = jnp.minimum(jnp.sum((tile_start[:, None] >= ends[None, :]).astype(jnp.int32), axis=1),
                              N_EXPERTS - 1)
    n_valid = (ends[-1] // tm).astype(jnp.int32).reshape(1)
    return dest.reshape(t, TOP_K), src_tok, cw.reshape(rows, 1), tile_expert, n_valid


def _final_kernel(x_ref, y_ref, mod_ref, g_ref, o_ref):
    xn = x_ref[...] + mod_ref[5:6, :] * y_ref[...].astype(F32)
    o_ref[...] = _rms(xn) * g_ref[...]


def _final_call(x, y, mod, g):
    b, r, d = x.shape
    tm = min(r, 512)
    row_spec = pl.BlockSpec((None, tm, d), lambda i, t: (i, t, 0))
    return pl.pallas_call(
        _final_kernel,
        grid=(b, r // tm),
        in_specs=[row_spec, row_spec,
                  pl.BlockSpec((None, 6, d), lambda i, t: (i, 0, 0)),
                  pl.BlockSpec((1, d), lambda i, t: (0, 0))],
        out_specs=row_spec,
        out_shape=jax.ShapeDtypeStruct((b, r, d), F32),
        compiler_params=_cparams(("parallel", "parallel")),
        name="final_norm",
    )(x, y, mod, g.reshape(1, d))


def _rope_tables(n):
    pos = jnp.arange(n)
    rows = (pos // GRID_W).astype(F32)
    cols = (pos % GRID_W).astype(F32)
    lane = np.arange(LANES)
    out = []
    for hw in (32, 16):
        period = 4 * hw
        u = lane % period
        use_cols = (u // (2 * hw)) == 1
        w = u % (2 * hw)
        freqs = jnp.asarray(ROPE_THETA ** (-(w % hw).astype(np.float32) / hw), F32)
        p = jnp.where(jnp.asarray(use_cols)[None, :], cols[:, None], rows[:, None])
        ang = p * freqs[None, :]
        sign = jnp.asarray(np.where(w < hw, -1.0, 1.0), F32)
        out += [jnp.cos(ang), jnp.sin(ang) * sign[None, :]]
    return out


def kernel(x, c, ctx, c_ctx, attn_norm_g, ffn_norm_g, ada_w, ada_b, w_in, qk_norm_g, diff_lambda,
           diff_subln_g, na_rpb, swa_sink, w_branch, w_out, ffn_w_gate, ffn_w_up, ffn_w_down,
           moe_router, moe_w_gate, moe_w_up, moe_w_down, final_norm_g):
    b, n, d = x.shape
    l = ctx.shape[1]
    depth = w_in.shape[0]
    assert depth == 2, "laid out for one dense layer followed by one routed last layer"
    rows_n = n // GRID_W

    lat_tables = _rope_tables(n)
    ones = jnp.ones((l, LANES), F32)
    zeros = jnp.zeros((l, LANES), F32)
    ctx_tables = [ones, zeros, ones, zeros]

    cvec = jnp.zeros((8, d), F32).at[:b].set(c).at[b].set(c_ctx)
    mods = []
    for i in range(depth):
        m = _adaln(cvec, ada_w[i], ada_b[i]).reshape(8, 6, d)
        mods.append((m[:b], jnp.broadcast_to(m[b:b + 1], (b, 6, d))))

    def mixers(i, h_lat, h_ctx):
        lambda_init = 0.8 - 0.6 * math.exp(-0.3 * i)
        w_qkv = w_in[i, :, :QKV_COLS].astype(BF16)
        sink_tab = jnp.broadcast_to((swa_sink[i].astype(F32) * LOG2E)[:, None], (4, LANES))
        p_lat = _qkv_call(h_lat, w_qkv, lat_tables, qk_norm_g[i])
        p_ctx = _qkv_call(h_ctx, w_qkv, ctx_tables, qk_norm_g[i])
        o_lat = [_gqa_call(p_lat, p_ctx),
                 _diff_call(p_lat, p_ctx, diff_lambda[i], diff_subln_g[i], lambda_init),
                 _na_call(p_lat, p_ctx, _na_bias_tables(na_rpb[i], rows_n)),
                 _swa_call(p_lat, p_ctx, sink_tab.reshape(2, 2, LANES))]
        return p_ctx, o_lat, sink_tab, lambda_init

    mod_lat, mod_ctx = mods[0]
    h_lat = _norm_mod_call(x, attn_norm_g[0], mod_lat, 0, 1)
    h_ctx = _norm_mod_call(ctx, attn_norm_g[0], mod_ctx, 0, 1)
    p_ctx, o_lat, sink_tab, lambda_init = mixers(0, h_lat, h_ctx)
    w_gate = w_in[0, :, QKV_COLS:].astype(BF16)
    wb = w_branch[0].astype(BF16)
    wo = w_out[0].astype(BF16)
    x_lat, h_lat = _merge_call([(o, 0) for o in o_lat], _matmul_call(h_lat, w_gate), wb, wo, x,
                               mod_lat, ffn_norm_g[0])
    o_ctx = _ctx_attn_call(p_ctx, diff_lambda[0], diff_subln_g[0], sink_tab, lambda_init)
    x_ctx, h_ctx = _merge_call([(o_ctx, k) for k in range(N_BRANCHES)], _matmul_call(h_ctx, w_gate),
                               wb, wo, ctx, mod_ctx, ffn_norm_g[0])
    wg = ffn_w_gate[0].astype(BF16)
    wu = ffn_w_up[0].astype(BF16)
    wd = ffn_w_down[0].astype(BF16)
    _, h_ctx = _ffn_call(h_ctx, wg, wu, wd, x_ctx, mod_ctx, attn_norm_g[1], mods[1][1])
    x_lat, h_lat = _ffn_call(h_lat, wg, wu, wd, x_lat, mod_lat, attn_norm_g[1], mods[1][0])

    mod_lat, _ = mods[1]
    _, o_lat, _, _ = mixers(1, h_lat, h_ctx)
    router = jnp.zeros((d, LANES), F32).at[:, :N_EXPERTS].set(moe_router[0])
    x_lat, h_lat, logits = _merge_call(
        [(o, 0) for o in o_lat], _matmul_call(h_lat, w_in[1, :, QKV_COLS:].astype(BF16)),
        w_branch[1].astype(BF16), w_out[1].astype(BF16), x_lat, mod_lat, ffn_norm_g[1], router)

    tm_moe = 512
    idx_pad, wts_pad = _route_call(logits.reshape(b * n, LANES))
    dest, src_tok, cw, tile_expert, n_valid = _moe_plan(idx_pad[:, :TOP_K], wts_pad[:, :TOP_K], tm_moe)
    xs = jnp.take(h_lat.reshape(b * n, d), src_tok, axis=0)
    ys = _moe_call(xs, moe_w_gate[0].astype(BF16), moe_w_up[0].astype(BF16),
                   moe_w_down[0].astype(BF16), cw, tile_expert, n_valid, tm_moe)
    y = (jnp.take(ys, dest[:, 0], axis=0).astype(F32)
         + jnp.take(ys, dest[:, 1], axis=0).astype(F32)).reshape(b, n, d)
    return _final_call(x_lat, y, mod_lat, final_norm_g)
```

```python
import functools
import math

import numpy as np
import jax
import jax.numpy as jnp
from jax import lax
from jax.experimental import pallas as pl
from jax.experimental.pallas import tpu as pltpu

F32 = jnp.float32
BF16 = jnp.bfloat16

GRID_W = 64
HEAD_DIM = 128
N_BRANCHES = 4
BRANCH_WIDTH = 4 * HEAD_DIM
DIFF_QK_DIM = 64
NA_ROWS = 8
NA_COLS = 16
SWA_WINDOW = 128
N_EXPERTS = 8
TOP_K = 2
NORM_EPS = 1e-6
ROPE_THETA = 10000.0
NEG_INF = -1e30
LOG2E = math.log2(math.e)

QKV_COLS = 5120
COL_GQA_Q, COL_GQA_K, COL_GQA_V = 0, 512, 768
COL_DIFF_Q, COL_DIFF_K, COL_DIFF_V = 1024, 1536, 2048
COL_NA_Q, COL_NA_K, COL_NA_V = 2560, 3072, 3584
COL_SWA_Q, COL_SWA_K, COL_SWA_V = 4096, 4608, 4864

LANES = 128
VMEM_LIMIT = 56 * 1024 * 1024

QS128 = HEAD_DIM ** -0.5 * LOG2E
QS64 = DIFF_QK_DIM ** -0.5 * LOG2E


def _cparams(sem):
    return pltpu.CompilerParams(dimension_semantics=sem, vmem_limit_bytes=VMEM_LIMIT)


def _dot(a, b):
    return jnp.dot(a, b, preferred_element_type=F32)


def _dot_nt(a, b):
    return lax.dot_general(a, b, (((1,), (1,)), ((), ())), preferred_element_type=F32)


def _rms(x):
    return x * lax.rsqrt(jnp.mean(x * x, axis=-1, keepdims=True) + NORM_EPS)


def _norm_mod(x, g, mod_ref, shift_idx, scale_idx):
    y = _rms(x) * g
    return y * (1.0 + mod_ref[scale_idx:scale_idx + 1, :]) + mod_ref[shift_idx:shift_idx + 1, :]


def _adaln_kernel(c_ref, w_ref, b_ref, o_ref):
    c = c_ref[...]
    s = c * jax.nn.sigmoid(c)
    o_ref[...] = jnp.dot(s, w_ref[...], preferred_element_type=F32,
                         precision=lax.Precision.HIGHEST) + b_ref[...]


def _adaln(cvec, w, b):
    rows, d = cvec.shape
    cols = w.shape[1]
    tn = 1024 if cols % 1024 == 0 else cols
    return pl.pallas_call(
        _adaln_kernel,
        grid=(cols // tn,),
        in_specs=[pl.BlockSpec((rows, d), lambda j: (0, 0)),
                  pl.BlockSpec((d, tn), lambda j: (0, j)),
                  pl.BlockSpec((1, tn), lambda j: (0, j))],
        out_specs=pl.BlockSpec((rows, tn), lambda j: (0, j)),
        out_shape=jax.ShapeDtypeStruct((rows, cols), F32),
        compiler_params=_cparams(("arbitrary",)),
        name="adaln",
    )(cvec, w, b.reshape(1, cols))


def _norm_mod_kernel(x_ref, g_ref, mod_ref, h_ref, *, shift_idx, scale_idx):
    h_ref[...] = _norm_mod(x_ref[...], g_ref[...], mod_ref, shift_idx, scale_idx).astype(BF16)


def _norm_mod_call(x, g, mod, shift_idx, scale_idx):
    b, r, d = x.shape
    tm = min(r, 512)
    return pl.pallas_call(
        functools.partial(_norm_mod_kernel, shift_idx=shift_idx, scale_idx=scale_idx),
        grid=(b, r // tm),
        in_specs=[pl.BlockSpec((None, tm, d), lambda i, t: (i, t, 0)),
                  pl.BlockSpec((1, d), lambda i, t: (0, 0)),
                  pl.BlockSpec((None, 6, d), lambda i, t: (i, 0, 0))],
        out_specs=pl.BlockSpec((None, tm, d), lambda i, t: (i, t, 0)),
        out_shape=jax.ShapeDtypeStruct((b, r, d), BF16),
        compiler_params=_cparams(("parallel", "parallel")),
        name="norm_mod",
    )(x, g.reshape(1, d), mod)


QKV_TILE = 4 * LANES
_PLAIN = (None, None, 1.0)
_QKV_TILE_OPS = {
    0: [(0, 32, QS128)] * 4,
    1: [(1, 32, 1.0)] * 2 + [_PLAIN] * 2,
    2: [(None, 16, QS64)] * 4,
    3: [(None, 16, 1.0)] * 4,
    5: [(None, None, QS128)] * 4,
    8: [(None, 32, QS128)] * 4,
    9: [(None, 32, 1.0)] * 2 + [_PLAIN] * 2,
}


def _rope(y, cos, sin, hw):
    lane = lax.broadcasted_iota(jnp.int32, y.shape, 1)
    first = (lane % (2 * hw)) < hw
    partner = jnp.where(first, pltpu.roll(y, LANES - hw, 1), pltpu.roll(y, hw, 1))
    return y * cos + partner * sin


def _qkv_kernel(h_ref, w_ref, cos32_ref, sin32_ref, cos16_ref, sin16_ref, g_ref, o_ref):
    h = h_ref[...]
    for j in range(o_ref.shape[1] // QKV_TILE):
        acc = _dot(h, w_ref[:, j * QKV_TILE:(j + 1) * QKV_TILE])
        for c, (norm_row, hw, scale) in enumerate(_QKV_TILE_OPS.get(j, [_PLAIN] * 4)):
            y = acc[:, c * LANES:(c + 1) * LANES]
            if norm_row is not None:
                y = _rms(y) * g_ref[norm_row:norm_row + 1, :]
            if hw == 32:
                y = _rope(y, cos32_ref[...], sin32_ref[...], 32)
            elif hw == 16:
                y = _rope(y, cos16_ref[...], sin16_ref[...], 16)
            if scale != 1.0:
                y = y * scale
            col = j * QKV_TILE + c * LANES
            o_ref[:, col:col + LANES] = y.astype(o_ref.dtype)


def _qkv_call(h, w_qkv, tables, qk_g):
    b, r, d = h.shape
    cols = w_qkv.shape[1]
    tm = min(r, 512)
    tab_spec = pl.BlockSpec((tm, LANES), lambda i, t: (t, 0))
    return pl.pallas_call(
        _qkv_kernel,
        grid=(b, r // tm),
        in_specs=[pl.BlockSpec((None, tm, d), lambda i, t: (i, t, 0)),
                  pl.BlockSpec((d, cols), lambda i, t: (0, 0), pipeline_mode=pl.Buffered(1)),
                  tab_spec, tab_spec, tab_spec, tab_spec,
                  pl.BlockSpec((2, LANES), lambda i, t: (0, 0))],
        out_specs=pl.BlockSpec((None, tm, cols), lambda i, t: (i, t, 0)),
        out_shape=jax.ShapeDtypeStruct((b, r, cols), BF16),
        compiler_params=_cparams(("parallel", "parallel")),
        name="qkv_proj",
    )(h, w_qkv, *tables, qk_g)


def _matmul_kernel(h_ref, w_ref, o_ref):
    o_ref[...] = _dot(h_ref[...], w_ref[...]).astype(o_ref.dtype)


def _matmul_call(h, w):
    b, r, d = h.shape
    cols = w.shape[1]
    tn = 512
    tm = 1024 if r % 1024 == 0 else r
    return pl.pallas_call(
        _matmul_kernel,
        grid=(b, r // tm, cols // tn),
        in_specs=[pl.BlockSpec((None, tm, d), lambda i, t, j: (i, t, 0)),
                  pl.BlockSpec((d, tn), lambda i, t, j: (0, j))],
        out_specs=pl.BlockSpec((None, tm, tn), lambda i, t, j: (i, t, j)),
        out_shape=jax.ShapeDtypeStruct((b, r, cols), BF16),
        compiler_params=_cparams(("parallel", "parallel", "arbitrary")),
        name="gate_proj",
    )(h, w)


def _diff_lambda(lam_ref, lambda_init):
    a = jnp.sum(lam_ref[0:1, :] * lam_ref[1:2, :], axis=-1, keepdims=True)
    b = jnp.sum(lam_ref[2:3, :] * lam_ref[3:4, :], axis=-1, keepdims=True)
    return jnp.exp(a) - jnp.exp(b) + lambda_init


def _split_maps(q):
    lane = lax.broadcasted_iota(jnp.int32, q.shape, 1)
    zero = jnp.zeros_like(q)
    return jnp.concatenate([jnp.where(lane < DIFF_QK_DIM, q, zero),
                            jnp.where(lane >= DIFF_QK_DIM, q, zero)], axis=0)


FLASH_ROW_BLOCK = 256


def _flash_pair(q2, kc_ref, vc_ref, kl_ref, vl_ref, tk):
    m_rows = q2.shape[0]
    n_lat = kl_ref.shape[0]
    nb = m_rows // FLASH_ROW_BLOCK
    qs = [q2[i * FLASH_ROW_BLOCK:(i + 1) * FLASH_ROW_BLOCK] for i in range(nb)]

    def step(k, v, carry):
        v1 = jnp.concatenate([v, jnp.ones_like(v)], axis=1)
        out = []
        for q, (m, acc) in zip(qs, carry):
            s = _dot_nt(q, k)
            m_new = jnp.maximum(m, jnp.max(s, axis=-1, keepdims=True))
            p = jnp.exp2(s - m_new)
            acc = jnp.exp2(m - m_new) * acc + _dot(p.astype(BF16), v1)
            out.append((m_new, acc))
        return tuple(out)

    carry = tuple((jnp.full((FLASH_ROW_BLOCK, 1), NEG_INF, F32),
                   jnp.zeros((FLASH_ROW_BLOCK, 2 * HEAD_DIM), F32)) for _ in range(nb))
    carry = step(kc_ref[...], vc_ref[...], carry)

    for c in range(n_lat // tk):
        carry = step(kl_ref[c * tk:(c + 1) * tk, :], vl_ref[c * tk:(c + 1) * tk, :], carry)
    return jnp.concatenate([acc[:, :HEAD_DIM] / acc[:, HEAD_DIM:] for _, acc in carry], axis=0)


def _gqa_kernel(q_ref, kc_ref, vc_ref, kl_ref, vl_ref, o_ref, *, tk):
    tq = q_ref.shape[0]
    q = q_ref[...]
    q2 = jnp.concatenate([q[:, :HEAD_DIM], q[:, HEAD_DIM:]], axis=0)
    o = _flash_pair(q2, kc_ref, vc_ref, kl_ref, vl_ref, tk)
    o_ref[:, :HEAD_DIM] = o[:tq].astype(o_ref.dtype)
    o_ref[:, HEAD_DIM:] = o[tq:].astype(o_ref.dtype)


def _diff_kernel(q_ref, kc_ref, vc_ref, kl_ref, vl_ref, lam_ref, subg_ref, o_ref, *, tk, lambda_init):
    tq = q_ref.shape[0]
    o = _flash_pair(_split_maps(q_ref[...]), kc_ref, vc_ref, kl_ref, vl_ref, tk)
    lam = _diff_lambda(lam_ref, lambda_init)
    d = o[:tq] - lam * o[tq:]
    o_ref[...] = (_rms(d) * subg_ref[...] * (1.0 - lambda_init)).astype(o_ref.dtype)


def _gqa_call(p_lat, p_ctx):
    b, n, _ = p_lat.shape
    l = p_ctx.shape[1]
    tq = 512
    tk = 512
    hd = HEAD_DIM
    kcol, vcol = COL_GQA_K // hd, COL_GQA_V // hd
    return pl.pallas_call(
        functools.partial(_gqa_kernel, tk=tk),
        grid=(b, 2, n // tq),
        in_specs=[pl.BlockSpec((None, tq, 2 * hd), lambda i, k, t: (i, t, k)),
                  pl.BlockSpec((None, l, hd), lambda i, k, t: (i, 0, kcol + k)),
                  pl.BlockSpec((None, l, hd), lambda i, k, t: (i, 0, vcol + k)),
                  pl.BlockSpec((None, n, hd), lambda i, k, t: (i, 0, kcol + k)),
                  pl.BlockSpec((None, n, hd), lambda i, k, t: (i, 0, vcol + k))],
        out_specs=pl.BlockSpec((None, tq, 2 * hd), lambda i, k, t: (i, t, k)),
        out_shape=jax.ShapeDtypeStruct((b, n, BRANCH_WIDTH), BF16),
        compiler_params=_cparams(("parallel", "parallel", "arbitrary")),
        name="attn_gqa",
    )(p_lat, p_ctx, p_ctx, p_lat, p_lat)


def _diff_call(p_lat, p_ctx, lam_p, subg, lambda_init):
    b, n, _ = p_lat.shape
    l = p_ctx.shape[1]
    tq = 512
    tk = 512
    hd = HEAD_DIM
    qcol, kcol, vcol = COL_DIFF_Q // hd, COL_DIFF_K // hd, COL_DIFF_V // hd
    return pl.pallas_call(
        functools.partial(_diff_kernel, tk=tk, lambda_init=lambda_init),
        grid=(b, 4, n // tq),
        in_specs=[pl.BlockSpec((None, tq, hd), lambda i, h, t: (i, t, qcol + h)),
                  pl.BlockSpec((None, l, hd), lambda i, h, t: (i, 0, kcol + h)),
                  pl.BlockSpec((None, l, hd), lambda i, h, t: (i, 0, vcol + h)),
                  pl.BlockSpec((None, n, hd), lambda i, h, t: (i, 0, kcol + h)),
                  pl.BlockSpec((None, n, hd), lambda i, h, t: (i, 0, vcol + h)),
                  pl.BlockSpec((4, DIFF_QK_DIM), lambda i, h, t: (0, 0)),
                  pl.BlockSpec((1, hd), lambda i, h, t: (0, 0))],
        out_specs=pl.BlockSpec((None, tq, hd), lambda i, h, t: (i, t, h)),
        out_shape=jax.ShapeDtypeStruct((b, n, BRANCH_WIDTH), BF16),
        compiler_params=_cparams(("parallel", "parallel", "arbitrary")),
        name="attn_diff",
    )(p_lat, p_ctx, p_ctx, p_lat, p_lat, lam_p, subg.reshape(1, hd))


NA_GROUP_ROWS = 8
NA_WIN_ROWS = 2 * NA_GROUP_ROWS


def _na_window_start(g, rows_n):
    return jnp.clip(g * NA_GROUP_ROWS - NA_ROWS // 2, 0, rows_n - NA_WIN_ROWS)


def _na_bias_tables(rpb, rows_n):
    n_groups = rows_n // NA_GROUP_ROWS
    variants = [0, min(1, n_groups - 1), n_groups - 1]
    qc = np.arange(GRID_W)[:, None]
    kc = np.arange(GRID_W)[None, :]
    cstart = np.clip(qc - NA_COLS // 2, 0, GRID_W - NA_COLS)
    ok_col = (kc >= cstart) & (kc < cstart + NA_COLS)
    dj = np.clip(kc - qc + NA_COLS - 1, 0, 2 * NA_COLS - 2)
    sel_j = (dj[..., None] == np.arange(2 * NA_COLS - 1)) & ok_col[..., None]
    a = np.arange(NA_GROUP_ROWS)[:, None]
    u = np.arange(NA_WIN_ROWS)[None, :]
    sel_i, ok_row = [], []
    for g in variants:
        ws = int(np.clip(g * NA_GROUP_ROWS - NA_ROWS // 2, 0, rows_n - NA_WIN_ROWS))
        r = g * NA_GROUP_ROWS + a
        rs = np.clip(r - NA_ROWS // 2, 0, rows_n - NA_ROWS)
        krow = ws + u
        ok = (krow >= rs) & (krow < rs + NA_ROWS)
        di = krow - r + NA_ROWS - 1
        sel_i.append((di[..., None] == np.arange(2 * NA_ROWS - 1)) & ok[..., None])
        ok_row.append(ok)
    sel_i = np.stack(sel_i)
    ok_row = np.stack(ok_row)
    hi = lax.Precision.HIGHEST
    by_col = jnp.einsum("hij,qkj->hiqk", rpb.astype(F32), jnp.asarray(sel_j, F32), precision=hi)
    bias = jnp.einsum("vaui,hiqk->hvaquk", jnp.asarray(sel_i, F32), by_col, precision=hi)
    valid = ok_row[:, :, None, :, None] & ok_col[None, None, :, None, :]
    out = jnp.where(valid[None], bias * LOG2E, NEG_INF)
    return out.reshape(rpb.shape[0], 3, NA_GROUP_ROWS * GRID_W, NA_WIN_ROWS * GRID_W)


def _na_kernel(q_ref, kc_ref, vc_ref, kl_ref, vl_ref, bias_ref, o_ref, *, rows_n):
    g = pl.program_id(2)
    win = NA_WIN_ROWS * GRID_W
    off = pl.multiple_of(_na_window_start(g, rows_n) * GRID_W, 256)
    q = q_ref[...]
    s_b = _dot_nt(q, kl_ref[pl.ds(off, win), :]) + bias_ref[...]
    s_c = _dot_nt(q, kc_ref[...])
    m = jnp.maximum(jnp.max(s_b, axis=-1, keepdims=True), jnp.max(s_c, axis=-1, keepdims=True))
    p_b = jnp.exp2(s_b - m)
    p_c = jnp.exp2(s_c - m)
    l = jnp.sum(p_b, axis=-1, keepdims=True) + jnp.sum(p_c, axis=-1, keepdims=True)
    o = _dot(p_c.astype(BF16), vc_ref[...]) + _dot(p_b.astype(BF16), vl_ref[pl.ds(off, win), :])
    o_ref[...] = (o / l).astype(o_ref.dtype)


def _na_call(p_lat, p_ctx, bias):
    b, n, _ = p_lat.shape
    l = p_ctx.shape[1]
    hd = HEAD_DIM
    rows_n = n // GRID_W
    n_groups = rows_n // NA_GROUP_ROWS
    tq = NA_GROUP_ROWS * GRID_W
    win = NA_WIN_ROWS * GRID_W
    qcol, kcol, vcol = COL_NA_Q // hd, COL_NA_K // hd, COL_NA_V // hd

    def bias_map(i, h, g):
        return (h, jnp.where(g == 0, 0, jnp.where(g == n_groups - 1, 2, 1)), 0, 0)

    return pl.pallas_call(
        functools.partial(_na_kernel, rows_n=rows_n),
        grid=(b, 4, n_groups),
        in_specs=[pl.BlockSpec((None, tq, hd), lambda i, h, g: (i, g, qcol + h)),
                  pl.BlockSpec((None, l, hd), lambda i, h, g: (i, 0, kcol + h)),
                  pl.BlockSpec((None, l, hd), lambda i, h, g: (i, 0, vcol + h)),
                  pl.BlockSpec((None, n, hd), lambda i, h, g: (i, 0, kcol + h)),
                  pl.BlockSpec((None, n, hd), lambda i, h, g: (i, 0, vcol + h)),
                  pl.BlockSpec((None, None, tq, win), bias_map)],
        out_specs=pl.BlockSpec((None, tq, hd), lambda i, h, g: (i, g, h)),
        out_shape=jax.ShapeDtypeStruct((b, n, BRANCH_WIDTH), BF16),
        compiler_params=_cparams(("parallel", "parallel", "arbitrary")),
        name="attn_na",
    )(p_lat, p_ctx, p_ctx, p_lat, p_lat, bias)


def _swa_kernel(q_ref, kc_ref, vc_ref, kl_ref, vl_ref, sink_ref, o_ref, *, win):
    t = pl.program_id(2)
    tq = q_ref.shape[0]
    n = kl_ref.shape[0]
    q0 = t * tq
    ws = pl.multiple_of(jnp.clip(q0 - SWA_WINDOW, 0, n - win), SWA_WINDOW)
    kb = kl_ref[pl.ds(ws, win), :]
    vb = vl_ref[pl.ds(ws, win), :]
    row = lax.broadcasted_iota(jnp.int32, (tq, win), 0)
    col = lax.broadcasted_iota(jnp.int32, (tq, win), 1)
    valid = jnp.abs(row + (q0 - ws) - col) <= SWA_WINDOW
    for gi in range(2):
        q = q_ref[:, gi * HEAD_DIM:(gi + 1) * HEAD_DIM]
        sink = sink_ref[gi:gi + 1, 0:1]
        s_b = jnp.where(valid, _dot_nt(q, kb), NEG_INF)
        s_c = _dot_nt(q, kc_ref[...])
        m = jnp.maximum(jnp.maximum(jnp.max(s_b, axis=-1, keepdims=True),
                                    jnp.max(s_c, axis=-1, keepdims=True)), sink)
        e_b = jnp.exp2(s_b - m)
        e_c = jnp.exp2(s_c - m)
        den = (jnp.sum(e_b, axis=-1, keepdims=True) + jnp.sum(e_c, axis=-1, keepdims=True)
               + jnp.exp2(sink - m))
        o = _dot(e_c.astype(BF16), vc_ref[...]) + _dot(e_b.astype(BF16), vb)
        o_ref[:, gi * HEAD_DIM:(gi + 1) * HEAD_DIM] = (o / den).astype(o_ref.dtype)


def _swa_call(p_lat, p_ctx, sink_tab):
    b, n, _ = p_lat.shape
    l = p_ctx.shape[1]
    hd = HEAD_DIM
    tq = 512
    win = tq + 2 * SWA_WINDOW
    qcol, kcol, vcol = COL_SWA_Q // (2 * hd), COL_SWA_K // hd, COL_SWA_V // hd
    return pl.pallas_call(
        functools.partial(_swa_kernel, win=win),
        grid=(b, 2, n // tq),
        in_specs=[pl.BlockSpec((None, tq, 2 * hd), lambda i, k, t: (i, t, qcol + k)),
                  pl.BlockSpec((None, l, hd), lambda i, k, t: (i, 0, kcol + k)),
                  pl.BlockSpec((None, l, hd), lambda i, k, t: (i, 0, vcol + k)),
                  pl.BlockSpec((None, n, hd), lambda i, k, t: (i, 0, kcol + k)),
                  pl.BlockSpec((None, n, hd), lambda i, k, t: (i, 0, vcol + k)),
                  pl.BlockSpec((None, 2, LANES), lambda i, k, t: (k, 0, 0))],
        out_specs=pl.BlockSpec((None, tq, 2 * hd), lambda i, k, t: (i, t, k)),
        out_shape=jax.ShapeDtypeStruct((b, n, BRANCH_WIDTH), BF16),
        compiler_params=_cparams(("parallel", "parallel", "arbitrary")),
        name="attn_swa",
    )(p_lat, p_ctx, p_ctx, p_lat, p_lat, sink_tab)


def _softmax_attend(q, k, v, sink=None):
    s = _dot_nt(q, k)
    m = jnp.max(s, axis=-1, keepdims=True)
    if sink is not None:
        m = jnp.maximum(m, sink)
    e = jnp.exp2(s - m)
    den = jnp.sum(e, axis=-1, keepdims=True)
    if sink is not None:
        den = den + jnp.exp2(sink - m)
    return _dot(e.astype(BF16), v) / den


def _ctx_attn_kernel(p_ref, lam_ref, subg_ref, sink_ref, o_ref, *, lambda_init):
    hd = HEAD_DIM

    def col(c0, h):
        return p_ref[:, c0 + h * hd:c0 + (h + 1) * hd]

    lam = _diff_lambda(lam_ref, lambda_init)
    for h in range(4):
        o = _softmax_attend(col(COL_GQA_Q, h), col(COL_GQA_K, h // 2), col(COL_GQA_V, h // 2))
        o_ref[:, h * hd:(h + 1) * hd] = o.astype(o_ref.dtype)
    for h in range(4):
        q = col(COL_DIFF_Q, h)
        tq = q.shape[0]
        o2 = _softmax_attend(_split_maps(q), col(COL_DIFF_K, h), col(COL_DIFF_V, h))
        d = o2[:tq] - lam * o2[tq:]
        d = _rms(d) * subg_ref[...] * (1.0 - lambda_init)
        o_ref[:, BRANCH_WIDTH + h * hd:BRANCH_WIDTH + (h + 1) * hd] = d.astype(o_ref.dtype)
    for h in range(4):
        o = _softmax_attend(col(COL_NA_Q, h), col(COL_NA_K, h), col(COL_NA_V, h))
        o_ref[:, 2 * BRANCH_WIDTH + h * hd:2 * BRANCH_WIDTH + (h + 1) * hd] = o.astype(o_ref.dtype)
    for h in range(4):
        o = _softmax_attend(col(COL_SWA_Q, h), col(COL_SWA_K, h // 2), col(COL_SWA_V, h // 2),
                            sink=sink_ref[h:h + 1, 0:1])
        o_ref[:, 3 * BRANCH_WIDTH + h * hd:3 * BRANCH_WIDTH + (h + 1) * hd] = o.astype(o_ref.dtype)


def _ctx_attn_call(p_ctx, lam_p, subg, sink_tab, lambda_init):
    b, l, cols = p_ctx.shape
    return pl.pallas_call(
        functools.partial(_ctx_attn_kernel, lambda_init=lambda_init),
        grid=(b,),
        in_specs=[pl.BlockSpec((None, l, cols), lambda i: (i, 0, 0)),
                  pl.BlockSpec((4, DIFF_QK_DIM), lambda i: (0, 0)),
                  pl.BlockSpec((1, HEAD_DIM), lambda i: (0, 0)),
                  pl.BlockSpec((4, LANES), lambda i: (0, 0))],
        out_specs=pl.BlockSpec((None, l, N_BRANCHES * BRANCH_WIDTH), lambda i: (i, 0, 0)),
        out_shape=jax.ShapeDtypeStruct((b, l, N_BRANCHES * BRANCH_WIDTH), BF16),
        compiler_params=_cparams(("parallel",)),
        name="attn_ctx",
    )(p_ctx, lam_p, subg.reshape(1, HEAD_DIM), sink_tab.reshape(4, LANES))


def _merge_kernel(o0, o1, o2, o3, g0, g1, g2, g3, wb_ref, wo_ref, x_ref, mod_ref, ng_ref, *rest,
                  with_router):
    if with_router:
        router_ref, xo_ref, h_ref, lg_ref, acc_ref = rest
    else:
        xo_ref, h_ref, acc_ref = rest
    nc = pl.program_id(2)

    @pl.when(nc == 0)
    def _():
        acc_ref[...] = jnp.zeros_like(acc_ref)

    s = None
    for i, (o_ref, g_ref) in enumerate(((o0, g0), (o1, g1), (o2, g2), (o3, g3))):
        y = _dot(o_ref[...], wb_ref[i])
        term = jax.nn.sigmoid(g_ref[...].astype(F32)) * y
        s = term if s is None else s + term
    acc_ref[...] += _dot(s.astype(BF16), wo_ref[...])

    @pl.when(nc == pl.num_programs(2) - 1)
    def _():
        xn = x_ref[...] + mod_ref[2:3, :] * acc_ref[...]
        xo_ref[...] = xn
        hn = _norm_mod(xn, ng_ref[...], mod_ref, 3, 4)
        h_ref[...] = hn.astype(h_ref.dtype)
        if with_router:
            lg_ref[...] = jnp.dot(hn, router_ref[...], preferred_element_type=F32,
                                  precision=lax.Precision.HIGHEST)


def _merge_call(o_parts, gates, w_branch, w_out, x, mod, next_g, router=None):
    b, r, d = x.shape
    nch = 4
    dc = d // nch
    tm = min(r, 512)
    with_router = router is not None

    def o_spec(cb):
        return pl.BlockSpec((None, tm, BRANCH_WIDTH), lambda i, t, c: (i, t, cb))

    def g_spec(br):
        return pl.BlockSpec((None, tm, dc), lambda i, t, c: (i, t, br * nch + c))

    in_specs = ([o_spec(cb) for _, cb in o_parts] + [g_spec(br) for br in range(N_BRANCHES)] + [
        pl.BlockSpec((N_BRANCHES, BRANCH_WIDTH, dc), lambda i, t, c: (0, 0, c)),
        pl.BlockSpec((dc, d), lambda i, t, c: (c, 0)),
        pl.BlockSpec((None, tm, d), lambda i, t, c: (i, t, 0)),
        pl.BlockSpec((None, 6, d), lambda i, t, c: (i, 0, 0)),
        pl.BlockSpec((1, d), lambda i, t, c: (0, 0))])
    args = [a for a, _ in o_parts] + [gates] * N_BRANCHES + [w_branch, w_out, x, mod, next_g.reshape(1, d)]
    row_spec = pl.BlockSpec((None, tm, d), lambda i, t, c: (i, t, 0))
    out_specs = [row_spec, row_spec]
    out_shape = [jax.ShapeDtypeStruct((b, r, d), F32), jax.ShapeDtypeStruct((b, r, d), BF16)]
    if with_router:
        in_specs.append(pl.BlockSpec((d, LANES), lambda i, t, c: (0, 0)))
        args.append(router)
        out_specs.append(pl.BlockSpec((None, tm, LANES), lambda i, t, c: (i, t, 0)))
        out_shape.append(jax.ShapeDtypeStruct((b, r, LANES), F32))
    return pl.pallas_call(
        functools.partial(_merge_kernel, with_router=with_router),
        grid=(b, r // tm, nch),
        in_specs=in_specs,
        out_specs=out_specs,
        out_shape=out_shape,
        scratch_shapes=[pltpu.VMEM((tm, d), F32)],
        compiler_params=_cparams(("parallel", "parallel", "arbitrary")),
        name="merge",
    )(*args)


def _swiglu_partial(h, wg, wu, wd):
    gate = _dot(h, wg)
    up = _dot(h, wu)
    act = gate * jax.nn.sigmoid(gate) * up
    return _dot(act.astype(BF16), wd)


def _ffn_kernel(h_ref, wg_ref, wu_ref, wd_ref, x_ref, mod_ref, ng_ref, nmod_ref, xo_ref, hn_ref,
                acc_ref):
    f = pl.program_id(2)

    @pl.when(f == 0)
    def _():
        acc_ref[...] = jnp.zeros_like(acc_ref)

    acc_ref[...] += _swiglu_partial(h_ref[...], wg_ref[...], wu_ref[...], wd_ref[...])

    @pl.when(f == pl.num_programs(2) - 1)
    def _():
        xn = x_ref[...] + mod_ref[5:6, :] * acc_ref[...]
        xo_ref[...] = xn
        hn_ref[...] = _norm_mod(xn, ng_ref[...], nmod_ref, 0, 1).astype(hn_ref.dtype)


def _ffn_call(h, wg, wu, wd, x, mod, next_g, next_mod):
    b, r, d = x.shape
    dff = wg.shape[1]
    tf = 512
    tm = min(r, 512)
    row_spec = pl.BlockSpec((None, tm, d), lambda i, t, f: (i, t, 0))
    mod_spec = pl.BlockSpec((None, 6, d), lambda i, t, f: (i, 0, 0))
    return pl.pallas_call(
        _ffn_kernel,
        grid=(b, r // tm, dff // tf),
        in_specs=[row_spec,
                  pl.BlockSpec((d, tf), lambda i, t, f: (0, f)),
                  pl.BlockSpec((d, tf), lambda i, t, f: (0, f)),
                  pl.BlockSpec((tf, d), lambda i, t, f: (f, 0)),
                  row_spec, mod_spec,
                  pl.BlockSpec((1, d), lambda i, t, f: (0, 0)),
                  mod_spec],
        out_specs=[row_spec, row_spec],
        out_shape=[jax.ShapeDtypeStruct((b, r, d), F32), jax.ShapeDtypeStruct((b, r, d), BF16)],
        scratch_shapes=[pltpu.VMEM((tm, d), F32)],
        compiler_params=_cparams(("parallel", "parallel", "arbitrary")),
        name="ffn_dense",
    )(h, wg, wu, wd, x, mod, next_g.reshape(1, d), next_mod)


def _route_kernel(lg_ref, idx_ref, w_ref):
    lg = lg_ref[...]
    lane = lax.broadcasted_iota(jnp.int32, lg.shape, 1)
    valid = lane < N_EXPERTS
    mx = jnp.max(jnp.where(valid, lg, -jnp.inf), axis=-1, keepdims=True)
    e = jnp.where(valid, jnp.exp(lg - mx), 0.0)
    p = e / jnp.sum(e, axis=-1, keepdims=True)
    p1 = jnp.max(p, axis=-1, keepdims=True)
    i1 = jnp.min(jnp.where(p == p1, lane, LANES), axis=-1, keepdims=True)
    rest = jnp.where(jnp.logical_or(lane == i1, jnp.logical_not(valid)), -1.0, p)
    p2 = jnp.max(rest, axis=-1, keepdims=True)
    i2 = jnp.min(jnp.where(rest == p2, lane, LANES), axis=-1, keepdims=True)
    tot = p1 + p2
    idx_ref[...] = jnp.where(lane == 0, i1, jnp.where(lane == 1, i2, 0))
    w_ref[...] = jnp.where(lane == 0, p1 / tot, jnp.where(lane == 1, p2 / tot, 0.0))


def _route_call(logits):
    r = logits.shape[0]
    tm = min(r, 1024)
    spec = pl.BlockSpec((tm, LANES), lambda i: (i, 0))
    return pl.pallas_call(
        _route_kernel,
        grid=(r // tm,),
        in_specs=[spec],
        out_specs=[spec, spec],
        out_shape=[jax.ShapeDtypeStruct((r, LANES), jnp.int32), jax.ShapeDtypeStruct((r, LANES), F32)],
        compiler_params=_cparams(("parallel",)),
        name="route_top2",
    )(logits)


def _moe_kernel(te_ref, nv_ref, x_ref, wg_ref, wu_ref, wd_ref, cw_ref, o_ref, acc_ref):
    i = pl.program_id(0)
    f = pl.program_id(1)

    @pl.when(f == 0)
    def _():
        acc_ref[...] = jnp.zeros_like(acc_ref)

    @pl.when(i < nv_ref[0])
    def _():
        acc_ref[...] += _swiglu_partial(x_ref[...], wg_ref[...], wu_ref[...], wd_ref[...])

    @pl.when(f == pl.num_programs(1) - 1)
    def _():
        o_ref[...] = (cw_ref[...] * acc_ref[...]).astype(o_ref.dtype)


def _moe_call(xs, wg, wu, wd, cw, tile_expert, n_valid, tm):
    r, d = xs.shape
    dff = wg.shape[2]
    tf = 512
    grid_spec = pltpu.PrefetchScalarGridSpec(
        num_scalar_prefetch=2,
        grid=(r // tm, dff // tf),
        in_specs=[pl.BlockSpec((tm, d), lambda i, f, te, nv: (i, 0)),
                  pl.BlockSpec((None, d, tf), lambda i, f, te, nv: (te[i], 0, f)),
                  pl.BlockSpec((None, d, tf), lambda i, f, te, nv: (te[i], 0, f)),
                  pl.BlockSpec((None, tf, d), lambda i, f, te, nv: (te[i], f, 0)),
                  pl.BlockSpec((tm, 1), lambda i, f, te, nv: (i, 0))],
        out_specs=pl.BlockSpec((tm, d), lambda i, f, te, nv: (i, 0)),
        scratch_shapes=[pltpu.VMEM((tm, d), F32)])
    return pl.pallas_call(
        _moe_kernel,
        grid_spec=grid_spec,
        out_shape=jax.ShapeDtypeStruct((r, d), BF16),
        compiler_params=_cparams(("arbitrary", "arbitrary")),
        name="moe_experts",
    )(tile_expert, n_valid, xs, wg, wu, wd, cw)


def _moe_plan(idx, wts, tm):
    t = idx.shape[0]
    a = t * TOP_K
    flat_e = idx.reshape(a)
    onehot = (flat_e[:, None] == jnp.arange(N_EXPERTS, dtype=jnp.int32)[None, :]).astype(jnp.int32)
    csum = jnp.cumsum(onehot, axis=0)
    rank = jnp.sum((csum - onehot) * onehot, axis=1)
    counts = csum[-1]
    padded = ((counts + tm - 1) // tm) * tm
    ends = jnp.cumsum(padded)
    offs = ends - padded
    dest = offs[flat_e] + rank
    rows = a + N_EXPERTS * tm
    src_tok = jnp.zeros((rows,), jnp.int32).at[dest].set(jnp.arange(a, dtype=jnp.int32) // TOP_K)
    cw = jnp.zeros((rows,), F32).at[dest].set(wts.reshape(a))
    tile_start = jnp.arange(rows // tm, dtype=jnp.int32) * tm
    tile_expert = jnp.minimum(jnp.sum((tile_start[:, None] >= ends[None, :]).astype(jnp.int32), axis=1),
                              N_EXPERTS - 1)
    n_valid = (ends[-1] // tm).astype(jnp.int32).reshape(1)
    return dest.reshape(t, TOP_K), src_tok, cw.reshape(rows, 1), tile_expert, n_valid


def _final_kernel(x_ref, y_ref, mod_ref, g_ref, o_ref):
    xn = x_ref[...] + mod_ref[5:6, :] * y_ref[...].astype(F32)
    o_ref[...] = _rms(xn) * g_ref[...]


def _final_call(x, y, mod, g):
    b, r, d = x.shape
    tm = min(r, 512)
    row_spec = pl.BlockSpec((None, tm, d), lambda i, t: (i, t, 0))
    return pl.pallas_call(
        _final_kernel,
        grid=(b, r // tm),
        in_specs=[row_spec, row_spec,
                  pl.BlockSpec((None, 6, d), lambda i, t: (i, 0, 0)),
                  pl.BlockSpec((1, d), lambda i, t: (0, 0))],
        out_specs=row_spec,
        out_shape=jax.ShapeDtypeStruct((b, r, d), F32),
        compiler_params=_cparams(("parallel", "parallel")),
        name="final_norm",
    )(x, y, mod, g.reshape(1, d))


def _rope_tables(n):
    pos = jnp.arange(n)
    rows = (pos // GRID_W).astype(F32)
    cols = (pos % GRID_W).astype(F32)
    lane = np.arange(LANES)
    out = []
    for hw in (32, 16):
        period = 4 * hw
        u = lane % period
        use_cols = (u // (2 * hw)) == 1
        w = u % (2 * hw)
        freqs = jnp.asarray(ROPE_THETA ** (-(w % hw).astype(np.float32) / hw), F32)
        p = jnp.where(jnp.asarray(use_cols)[None, :], cols[:, None], rows[:, None])
        ang = p * freqs[None, :]
        sign = jnp.asarray(np.where(w < hw, -1.0, 1.0), F32)
        out += [jnp.cos(ang), jnp.sin(ang) * sign[None, :]]
    return out


def kernel(x, c, ctx, c_ctx, attn_norm_g, ffn_norm_g, ada_w, ada_b, w_in, qk_norm_g, diff_lambda,
           diff_subln_g, na_rpb, swa_sink, w_branch, w_out, ffn_w_gate, ffn_w_up, ffn_w_down,
           moe_router, moe_w_gate, moe_w_up, moe_w_down, final_norm_g):
    b, n, d = x.shape
    l = ctx.shape[1]
    depth = w_in.shape[0]
    assert depth == 2, "laid out for one dense layer followed by one routed last layer"
    rows_n = n // GRID_W

    lat_tables = _rope_tables(n)
    ones = jnp.ones((l, LANES), F32)
    zeros = jnp.zeros((l, LANES), F32)
    ctx_tables = [ones, zeros, ones, zeros]

    cvec = jnp.zeros((8, d), F32).at[:b].set(c).at[b].set(c_ctx)
    mods = []
    for i in range(depth):
        m = _adaln(cvec, ada_w[i], ada_b[i]).reshape(8, 6, d)
        mods.append((m[:b], jnp.broadcast_to(m[b:b + 1], (b, 6, d))))

    def mixers(i, h_lat, h_ctx):
        lambda_init = 0.8 - 0.6 * math.exp(-0.3 * i)
        w_qkv = w_in[i, :, :QKV_COLS].astype(BF16)
        sink_tab = jnp.broadcast_to((swa_sink[i].astype(F32) * LOG2E)[:, None], (4, LANES))
        p_lat = _qkv_call(h_lat, w_qkv, lat_tables, qk_norm_g[i])
        p_ctx = _qkv_call(h_ctx, w_qkv, ctx_tables, qk_norm_g[i])
        o_lat = [_gqa_call(p_lat, p_ctx),
                 _diff_call(p_lat, p_ctx, diff_lambda[i], diff_subln_g[i], lambda_init),
                 _na_call(p_lat, p_ctx, _na_bias_tables(na_rpb[i], rows_n)),
                 _swa_call(p_lat, p_ctx, sink_tab.reshape(2, 2, LANES))]
        return p_ctx, o_lat, sink_tab, lambda_init

    mod_lat, mod_ctx = mods[0]
    h_lat = _norm_mod_call(x, attn_norm_g[0], mod_lat, 0, 1)
    h_ctx = _norm_mod_call(ctx, attn_norm_g[0], mod_ctx, 0, 1)
    p_ctx, o_lat, sink_tab, lambda_init = mixers(0, h_lat, h_ctx)
    w_gate = w_in[0, :, QKV_COLS:].astype(BF16)
    wb = w_branch[0].astype(BF16)
    wo = w_out[0].astype(BF16)
    x_lat, h_lat = _merge_call([(o, 0) for o in o_lat], _matmul_call(h_lat, w_gate), wb, wo, x,
                               mod_lat, ffn_norm_g[0])
    o_ctx = _ctx_attn_call(p_ctx, diff_lambda[0], diff_subln_g[0], sink_tab, lambda_init)
    x_ctx, h_ctx = _merge_call([(o_ctx, k) for k in range(N_BRANCHES)], _matmul_call(h_ctx, w_gate),
                               wb, wo, ctx, mod_ctx, ffn_norm_g[0])
    wg = ffn_w_gate[0].astype(BF16)
    wu = ffn_w_up[0].astype(BF16)
    wd = ffn_w_down[0].astype(BF16)
    _, h_ctx = _ffn_call(h_ctx, wg, wu, wd, x_ctx, mod_ctx, attn_norm_g[1], mods[1][1])
    x_lat, h_lat = _ffn_call(h_lat, wg, wu, wd, x_lat, mod_lat, attn_norm_g[1], mods[1][0])

    mod_lat, _ = mods[1]
    _, o_lat, _, _ = mixers(1, h_lat, h_ctx)
    router = jnp.zeros((d, LANES), F32).at[:, :N_EXPERTS].set(moe_router[0])
    x_lat, h_lat, logits = _merge_call(
        [(o, 0) for o in o_lat], _matmul_call(h_lat, w_in[1, :, QKV_COLS:].astype(BF16)),
        w_branch[1].astype(BF16), w_out[1].astype(BF16), x_lat, mod_lat, ffn_norm_g[1], router)

    tm_moe = 512
    idx_pad, wts_pad = _route_call(logits.reshape(b * n, LANES))
    dest, src_tok, cw, tile_expert, n_valid = _moe_plan(idx_pad[:, :TOP_K], wts_pad[:, :TOP_K], tm_moe)
    xs = jnp.take(h_lat.reshape(b * n, d), src_tok, axis=0)
    ys = _moe_call(xs, moe_w_gate[0].astype(BF16), moe_w_up[0].astype(BF16),
                   moe_w_down[0].astype(BF16), cw, tile_expert, n_valid, tm_moe)
    y = (jnp.take(ys, dest[:, 0], axis=0).astype(F32)
         + jnp.take(ys, dest[:, 1], axis=0).astype(F32)).reshape(b, n, d)
    return _final_call(x_lat, y, mod_lat, final_norm_g)
```

```python
import functools
import math

import numpy as np
import jax
import jax.numpy as jnp
from jax import lax
from jax.experimental import pallas as pl
from jax.experimental.pallas import tpu as pltpu

F32 = jnp.float32
BF16 = jnp.bfloat16

GRID_W = 64
HEAD_DIM = 128
N_BRANCHES = 4
BRANCH_WIDTH = 4 * HEAD_DIM
DIFF_QK_DIM = 64
NA_ROWS = 8
NA_COLS = 16
SWA_WINDOW = 128
N_EXPERTS = 8
TOP_K = 2
NORM_EPS = 1e-6
ROPE_THETA = 10000.0
NEG_INF = -1e30
LOG2E = math.log2(math.e)

QKV_COLS = 5120
COL_GQA_Q, COL_GQA_K, COL_GQA_V = 0, 512, 768
COL_DIFF_Q, COL_DIFF_K, COL_DIFF_V = 1024, 1536, 2048
COL_NA_Q, COL_NA_K, COL_NA_V = 2560, 3072, 3584
COL_SWA_Q, COL_SWA_K, COL_SWA_V = 4096, 4608, 4864

LANES = 128
VMEM_LIMIT = 56 * 1024 * 1024

QS128 = HEAD_DIM ** -0.5 * LOG2E
QS64 = DIFF_QK_DIM ** -0.5 * LOG2E


def _cparams(sem):
    return pltpu.CompilerParams(dimension_semantics=sem, vmem_limit_bytes=VMEM_LIMIT)


def _dot(a, b):
    return jnp.dot(a, b, preferred_element_type=F32)


def _dot_nt(a, b):
    return lax.dot_general(a, b, (((1,), (1,)), ((), ())), preferred_element_type=F32)


def _rms(x):
    return x * lax.rsqrt(jnp.mean(x * x, axis=-1, keepdims=True) + NORM_EPS)


def _norm_mod(x, g, mod_ref, shift_idx, scale_idx):
    y = _rms(x) * g
    return y * (1.0 + mod_ref[scale_idx:scale_idx + 1, :]) + mod_ref[shift_idx:shift_idx + 1, :]


def _adaln_kernel(c_ref, w_ref, b_ref, o_ref):
    c = c_ref[...]
    s = c * jax.nn.sigmoid(c)
    o_ref[...] = jnp.dot(s, w_ref[...], preferred_element_type=F32,
                         precision=lax.Precision.HIGHEST) + b_ref[...]


def _adaln(cvec, w, b):
    rows, d = cvec.shape
    cols = w.shape[1]
    tn = 1024 if cols % 1024 == 0 else cols
    return pl.pallas_call(
        _adaln_kernel,
        grid=(cols // tn,),
        in_specs=[pl.BlockSpec((rows, d), lambda j: (0, 0)),
                  pl.BlockSpec((d, tn), lambda j: (0, j)),
                  pl.BlockSpec((1, tn), lambda j: (0, j))],
        out_specs=pl.BlockSpec((rows, tn), lambda j: (0, j)),
        out_shape=jax.ShapeDtypeStruct((rows, cols), F32),
        compiler_params=_cparams(("arbitrary",)),
        name="adaln",
    )(cvec, w, b.reshape(1, cols))


def _norm_mod_kernel(x_ref, g_ref, mod_ref, h_ref, *, shift_idx, scale_idx):
    h_ref[...] = _norm_mod(x_ref[...], g_ref[...], mod_ref, shift_idx, scale_idx).astype(BF16)


def _norm_mod_call(x, g, mod, shift_idx, scale_idx):
    b, r, d = x.shape
    tm = min(r, 512)
    return pl.pallas_call(
        functools.partial(_norm_mod_kernel, shift_idx=shift_idx, scale_idx=scale_idx),
        grid=(b, r // tm),
        in_specs=[pl.BlockSpec((None, tm, d), lambda i, t: (i, t, 0)),
                  pl.BlockSpec((1, d), lambda i, t: (0, 0)),
                  pl.BlockSpec((None, 6, d), lambda i, t: (i, 0, 0))],
        out_specs=pl.BlockSpec((None, tm, d), lambda i, t: (i, t, 0)),
        out_shape=jax.ShapeDtypeStruct((b, r, d), BF16),
        compiler_params=_cparams(("parallel", "parallel")),
        name="norm_mod",
    )(x, g.reshape(1, d), mod)


QKV_TILE = 4 * LANES
_PLAIN = (None, None, 1.0)
_QKV_TILE_OPS = {
    0: [(0, 32, QS128)] * 4,
    1: [(1, 32, 1.0)] * 2 + [_PLAIN] * 2,
    2: [(None, 16, QS64)] * 4,
    3: [(None, 16, 1.0)] * 4,
    5: [(None, None, QS128)] * 4,
    8: [(None, 32, QS128)] * 4,
    9: [(None, 32, 1.0)] * 2 + [_PLAIN] * 2,
}


def _rope(y, cos, sin, hw):
    lane = lax.broadcasted_iota(jnp.int32, y.shape, 1)
    first = (lane % (2 * hw)) < hw
    partner = jnp.where(first, pltpu.roll(y, LANES - hw, 1), pltpu.roll(y, hw, 1))
    return y * cos + partner * sin


def _qkv_kernel(h_ref, w_ref, cos32_ref, sin32_ref, cos16_ref, sin16_ref, g_ref, o_ref):
    h = h_ref[...]
    for j in range(o_ref.shape[1] // QKV_TILE):
        acc = _dot(h, w_ref[:, j * QKV_TILE:(j + 1) * QKV_TILE])
        for c, (norm_row, hw, scale) in enumerate(_QKV_TILE_OPS.get(j, [_PLAIN] * 4)):
            y = acc[:, c * LANES:(c + 1) * LANES]
            if norm_row is not None:
                y = _rms(y) * g_ref[norm_row:norm_row + 1, :]
            if hw == 32:
                y = _rope(y, cos32_ref[...], sin32_ref[...], 32)
            elif hw == 16:
                y = _rope(y, cos16_ref[...], sin16_ref[...], 16)
            if scale != 1.0:
                y = y * scale
            col = j * QKV_TILE + c * LANES
            o_ref[:, col:col + LANES] = y.astype(o_ref.dtype)


def _qkv_call(h, w_qkv, tables, qk_g):
    b, r, d = h.shape
    cols = w_qkv.shape[1]
    tm = min(r, 512)
    tab_spec = pl.BlockSpec((tm, LANES), lambda i, t: (t, 0))
    return pl.pallas_call(
        _qkv_kernel,
        grid=(b, r // tm),
        in_specs=[pl.BlockSpec((None, tm, d), lambda i, t: (i, t, 0)),
                  pl.BlockSpec((d, cols), lambda i, t: (0, 0), pipeline_mode=pl.Buffered(1)),
                  tab_spec, tab_spec, tab_spec, tab_spec,
                  pl.BlockSpec((2, LANES), lambda i, t: (0, 0))],
        out_specs=pl.BlockSpec((None, tm, cols), lambda i, t: (i, t, 0)),
        out_shape=jax.ShapeDtypeStruct((b, r, cols), BF16),
        compiler_params=_cparams(("parallel", "parallel")),
        name="qkv_proj",
    )(h, w_qkv, *tables, qk_g)


def _matmul_kernel(h_ref, w_ref, o_ref):
    o_ref[...] = _dot(h_ref[...], w_ref[...]).astype(o_ref.dtype)


def _matmul_call(h, w):
    b, r, d = h.shape
    cols = w.shape[1]
    tn = 512
    tm = 1024 if r % 1024 == 0 else r
    return pl.pallas_call(
        _matmul_kernel,
        grid=(b, r // tm, cols // tn),
        in_specs=[pl.BlockSpec((None, tm, d), lambda i, t, j: (i, t, 0)),
                  pl.BlockSpec((d, tn), lambda i, t, j: (0, j))],
        out_specs=pl.BlockSpec((None, tm, tn), lambda i, t, j: (i, t, j)),
        out_shape=jax.ShapeDtypeStruct((b, r, cols), BF16),
        compiler_params=_cparams(("parallel", "parallel", "arbitrary")),
        name="gate_proj",
    )(h, w)


def _diff_lambda(lam_ref, lambda_init):
    a = jnp.sum(lam_ref[0:1, :] * lam_ref[1:2, :], axis=-1, keepdims=True)
    b = jnp.sum(lam_ref[2:3, :] * lam_ref[3:4, :], axis=-1, keepdims=True)
    return jnp.exp(a) - jnp.exp(b) + lambda_init


def _split_maps(q):
    lane = lax.broadcasted_iota(jnp.int32, q.shape, 1)
    zero = jnp.zeros_like(q)
    return jnp.concatenate([jnp.where(lane < DIFF_QK_DIM, q, zero),
                            jnp.where(lane >= DIFF_QK_DIM, q, zero)], axis=0)


FLASH_ROW_BLOCK = 256


def _flash_pair(q2, kc_ref, vc_ref, kl_ref, vl_ref, tk):
    m_rows = q2.shape[0]
    n_lat = kl_ref.shape[0]
    nb = m_rows // FLASH_ROW_BLOCK
    qs = [q2[i * FLASH_ROW_BLOCK:(i + 1) * FLASH_ROW_BLOCK] for i in range(nb)]

    def step(k, v, carry):
        v1 = jnp.concatenate([v, jnp.ones_like(v)], axis=1)
        out = []
        for q, (m, acc) in zip(qs, carry):
            s = _dot_nt(q, k)
            m_new = jnp.maximum(m, jnp.max(s, axis=-1, keepdims=True))
            p = jnp.exp2(s - m_new)
            acc = jnp.exp2(m - m_new) * acc + _dot(p.astype(BF16), v1)
            out.append((m_new, acc))
        return tuple(out)

    carry = tuple((jnp.full((FLASH_ROW_BLOCK, 1), NEG_INF, F32),
                   jnp.zeros((FLASH_ROW_BLOCK, 2 * HEAD_DIM), F32)) for _ in range(nb))
    carry = step(kc_ref[...], vc_ref[...], carry)

    for c in range(n_lat // tk):
        carry = step(kl_ref[c * tk:(c + 1) * tk, :], vl_ref[c * tk:(c + 1) * tk, :], carry)
    return jnp.concatenate([acc[:, :HEAD_DIM] / acc[:, HEAD_DIM:] for _, acc in carry], axis=0)


def _gqa_kernel(q_ref, kc_ref, vc_ref, kl_ref, vl_ref, o_ref, *, tk):
    tq = q_ref.shape[0]
    q = q_ref[...]
    q2 = jnp.concatenate([q[:, :HEAD_DIM], q[:, HEAD_DIM:]], axis=0)
    o = _flash_pair(q2, kc_ref, vc_ref, kl_ref, vl_ref, tk)
    o_ref[:, :HEAD_DIM] = o[:tq].astype(o_ref.dtype)
    o_ref[:, HEAD_DIM:] = o[tq:].astype(o_ref.dtype)


def _diff_kernel(q_ref, kc_ref, vc_ref, kl_ref, vl_ref, lam_ref, subg_ref, o_ref, *, tk, lambda_init):
    tq = q_ref.shape[0]
    o = _flash_pair(_split_maps(q_ref[...]), kc_ref, vc_ref, kl_ref, vl_ref, tk)
    lam = _diff_lambda(lam_ref, lambda_init)
    d = o[:tq] - lam * o[tq:]
    o_ref[...] = (_rms(d) * subg_ref[...] * (1.0 - lambda_init)).astype(o_ref.dtype)


def _gqa_call(p_lat, p_ctx):
    b, n, _ = p_lat.shape
    l = p_ctx.shape[1]
    tq = 512
    tk = 512
    hd = HEAD_DIM
    kcol, vcol = COL_GQA_K // hd, COL_GQA_V // hd
    return pl.pallas_call(
        functools.partial(_gqa_kernel, tk=tk),
        grid=(b, 2, n // tq),
        in_specs=[pl.BlockSpec((None, tq, 2 * hd), lambda i, k, t: (i, t, k)),
                  pl.BlockSpec((None, l, hd), lambda i, k, t: (i, 0, kcol + k)),
                  pl.BlockSpec((None, l, hd), lambda i, k, t: (i, 0, vcol + k)),
                  pl.BlockSpec((None, n, hd), lambda i, k, t: (i, 0, kcol + k)),
                  pl.BlockSpec((None, n, hd), lambda i, k, t: (i, 0, vcol + k))],
        out_specs=pl.BlockSpec((None, tq, 2 * hd), lambda i, k, t: (i, t, k)),
        out_shape=jax.ShapeDtypeStruct((b, n, BRANCH_WIDTH), BF16),
        compiler_params=_cparams(("parallel", "parallel", "arbitrary")),
        name="attn_gqa",
    )(p_lat, p_ctx, p_ctx, p_lat, p_lat)


def _diff_call(p_lat, p_ctx, lam_p, subg, lambda_init):
    b, n, _ = p_lat.shape
    l = p_ctx.shape[1]
    tq = 512
    tk = 512
    hd = HEAD_DIM
    qcol, kcol, vcol = COL_DIFF_Q // hd, COL_DIFF_K // hd, COL_DIFF_V // hd
    return pl.pallas_call(
        functools.partial(_diff_kernel, tk=tk, lambda_init=lambda_init),
        grid=(b, 4, n // tq),
        in_specs=[pl.BlockSpec((None, tq, hd), lambda i, h, t: (i, t, qcol + h)),
                  pl.BlockSpec((None, l, hd), lambda i, h, t: (i, 0, kcol + h)),
                  pl.BlockSpec((None, l, hd), lambda i, h, t: (i, 0, vcol + h)),
                  pl.BlockSpec((None, n, hd), lambda i, h, t: (i, 0, kcol + h)),
                  pl.BlockSpec((None, n, hd), lambda i, h, t: (i, 0, vcol + h)),
                  pl.BlockSpec((4, DIFF_QK_DIM), lambda i, h, t: (0, 0)),
                  pl.BlockSpec((1, hd), lambda i, h, t: (0, 0))],
        out_specs=pl.BlockSpec((None, tq, hd), lambda i, h, t: (i, t, h)),
        out_shape=jax.ShapeDtypeStruct((b, n, BRANCH_WIDTH), BF16),
        compiler_params=_cparams(("parallel", "parallel", "arbitrary")),
        name="attn_diff",
    )(p_lat, p_ctx, p_ctx, p_lat, p_lat, lam_p, subg.reshape(1, hd))


NA_GROUP_ROWS = 8
NA_WIN_ROWS = 2 * NA_GROUP_ROWS


def _na_window_start(g, rows_n):
    return jnp.clip(g * NA_GROUP_ROWS - NA_ROWS // 2, 0, rows_n - NA_WIN_ROWS)


def _na_bias_tables(rpb, rows_n):
    n_groups = rows_n // NA_GROUP_ROWS
    variants = [0, min(1, n_groups - 1), n_groups - 1]
    qc = np.arange(GRID_W)[:, None]
    kc = np.arange(GRID_W)[None, :]
    cstart = np.clip(qc - NA_COLS // 2, 0, GRID_W - NA_COLS)
    ok_col = (kc >= cstart) & (kc < cstart + NA_COLS)
    dj = np.clip(kc - qc + NA_COLS - 1, 0, 2 * NA_COLS - 2)
    sel_j = (dj[..., None] == np.arange(2 * NA_COLS - 1)) & ok_col[..., None]
    a = np.arange(NA_GROUP_ROWS)[:, None]
    u = np.arange(NA_WIN_ROWS)[None, :]
    sel_i, ok_row = [], []
    for g in variants:
        ws = int(np.clip(g * NA_GROUP_ROWS - NA_ROWS // 2, 0, rows_n - NA_WIN_ROWS))
        r = g * NA_GROUP_ROWS + a
        rs = np.clip(r - NA_ROWS // 2, 0, rows_n - NA_ROWS)
        krow = ws + u
        ok = (krow >= rs) & (krow < rs + NA_ROWS)
        di = krow - r + NA_ROWS - 1
        sel_i.append((di[..., None] == np.arange(2 * NA_ROWS - 1)) & ok[..., None])
        ok_row.append(ok)
    sel_i = np.stack(sel_i)
    ok_row = np.stack(ok_row)
    hi = lax.Precision.HIGHEST
    by_col = jnp.einsum("hij,qkj->hiqk", rpb.astype(F32), jnp.asarray(sel_j, F32), precision=hi)
    bias = jnp.einsum("vaui,hiqk->hvaquk", jnp.asarray(sel_i, F32), by_col, precision=hi)
    valid = ok_row[:, :, None, :, None] & ok_col[None, None, :, None, :]
    out = jnp.where(valid[None], bias * LOG2E, NEG_INF)
    return out.reshape(rpb.shape[0], 3, NA_GROUP_ROWS * GRID_W, NA_WIN_ROWS * GRID_W)


def _na_kernel(q_ref, kc_ref, vc_ref, kl_ref, vl_ref, bias_ref, o_ref, *, rows_n):
    g = pl.program_id(2)
    win = NA_WIN_ROWS * GRID_W
    off = pl.multiple_of(_na_window_start(g, rows_n) * GRID_W, 256)
    q = q_ref[...]
    s_b = _dot_nt(q, kl_ref[pl.ds(off, win), :]) + bias_ref[...]
    s_c = _dot_nt(q, kc_ref[...])
    m = jnp.maximum(jnp.max(s_b, axis=-1, keepdims=True), jnp.max(s_c, axis=-1, keepdims=True))
    p_b = jnp.exp2(s_b - m)
    p_c = jnp.exp2(s_c - m)
    l = jnp.sum(p_b, axis=-1, keepdims=True) + jnp.sum(p_c, axis=-1, keepdims=True)
    o = _dot(p_c.astype(BF16), vc_ref[...]) + _dot(p_b.astype(BF16), vl_ref[pl.ds(off, win), :])
    o_ref[...] = (o / l).astype(o_ref.dtype)


def _na_call(p_lat, p_ctx, bias):
    b, n, _ = p_lat.shape
    l = p_ctx.shape[1]
    hd = HEAD_DIM
    rows_n = n // GRID_W
    n_groups = rows_n // NA_GROUP_ROWS
    tq = NA_GROUP_ROWS * GRID_W
    win = NA_WIN_ROWS * GRID_W
    qcol, kcol, vcol = COL_NA_Q // hd, COL_NA_K // hd, COL_NA_V // hd

    def bias_map(i, h, g):
        return (h, jnp.where(g == 0, 0, jnp.where(g == n_groups - 1, 2, 1)), 0, 0)

    return pl.pallas_call(
        functools.partial(_na_kernel, rows_n=rows_n),
        grid=(b, 4, n_groups),
        in_specs=[pl.BlockSpec((None, tq, hd), lambda i, h, g: (i, g, qcol + h)),
                  pl.BlockSpec((None, l, hd), lambda i, h, g: (i, 0, kcol + h)),
                  pl.BlockSpec((None, l, hd), lambda i, h, g: (i, 0, vcol + h)),
                  pl.BlockSpec((None, n, hd), lambda i, h, g: (i, 0, kcol + h)),
                  pl.BlockSpec((None, n, hd), lambda i, h, g: (i, 0, vcol + h)),
                  pl.BlockSpec((None, None, tq, win), bias_map)],
        out_specs=pl.BlockSpec((None, tq, hd), lambda i, h, g: (i, g, h)),
        out_shape=jax.ShapeDtypeStruct((b, n, BRANCH_WIDTH), BF16),
        compiler_params=_cparams(("parallel", "parallel", "arbitrary")),
        name="attn_na",
    )(p_lat, p_ctx, p_ctx, p_lat, p_lat, bias)


def _swa_kernel(q_ref, kc_ref, vc_ref, kl_ref, vl_ref, sink_ref, o_ref, *, win):
    t = pl.program_id(2)
    tq = q_ref.shape[0]
    n = kl_ref.shape[0]
    q0 = t * tq
    ws = pl.multiple_of(jnp.clip(q0 - SWA_WINDOW, 0, n - win), SWA_WINDOW)
    kb = kl_ref[pl.ds(ws, win), :]
    vb = vl_ref[pl.ds(ws, win), :]
    row = lax.broadcasted_iota(jnp.int32, (tq, win), 0)
    col = lax.broadcasted_iota(jnp.int32, (tq, win), 1)
    valid = jnp.abs(row + (q0 - ws) - col) <= SWA_WINDOW
    for gi in range(2):
        q = q_ref[:, gi * HEAD_DIM:(gi + 1) * HEAD_DIM]
        sink = sink_ref[gi:gi + 1, 0:1]
        s_b = jnp.where(valid, _dot_nt(q, kb), NEG_INF)
        s_c = _dot_nt(q, kc_ref[...])
        m = jnp.maximum(jnp.maximum(jnp.max(s_b, axis=-1, keepdims=True),
                                    jnp.max(s_c, axis=-1, keepdims=True)), sink)
        e_b = jnp.exp2(s_b - m)
        e_c = jnp.exp2(s_c - m)
        den = (jnp.sum(e_b, axis=-1, keepdims=True) + jnp.sum(e_c, axis=-1, keepdims=True)
               + jnp.exp2(sink - m))
        o = _dot(e_c.astype(BF16), vc_ref[...]) + _dot(e_b.astype(BF16), vb)
        o_ref[:, gi * HEAD_DIM:(gi + 1) * HEAD_DIM] = (o / den).astype(o_ref.dtype)


def _swa_call(p_lat, p_ctx, sink_tab):
    b, n, _ = p_lat.shape
    l = p_ctx.shape[1]
    hd = HEAD_DIM
    tq = 512
    win = tq + 2 * SWA_WINDOW
    qcol, kcol, vcol = COL_SWA_Q // (2 * hd), COL_SWA_K // hd, COL_SWA_V // hd
    return pl.pallas_call(
        functools.partial(_swa_kernel, win=win),
        grid=(b, 2, n // tq),
        in_specs=[pl.BlockSpec((None, tq, 2 * hd), lambda i, k, t: (i, t, qcol + k)),
                  pl.BlockSpec((None, l, hd), lambda i, k, t: (i, 0, kcol + k)),
                  pl.BlockSpec((None, l, hd), lambda i, k, t: (i, 0, vcol + k)),
                  pl.BlockSpec((None, n, hd), lambda i, k, t: (i, 0, kcol + k)),
                  pl.BlockSpec((None, n, hd), lambda i, k, t: (i, 0, vcol + k)),
                  pl.BlockSpec((None, 2, LANES), lambda i, k, t: (k, 0, 0))],
        out_specs=pl.BlockSpec((None, tq, 2 * hd), lambda i, k, t: (i, t, k)),
        out_shape=jax.ShapeDtypeStruct((b, n, BRANCH_WIDTH), BF16),
        compiler_params=_cparams(("parallel", "parallel", "arbitrary")),
        name="attn_swa",
    )(p_lat, p_ctx, p_ctx, p_lat, p_lat, sink_tab)


def _softmax_attend(q, k, v, sink=None):
    s = _dot_nt(q, k)
    m = jnp.max(s, axis=-1, keepdims=True)
    if sink is not None:
        m = jnp.maximum(m, sink)
    e = jnp.exp2(s - m)
    den = jnp.sum(e, axis=-1, keepdims=True)
    if sink is not None:
        den = den + jnp.exp2(sink - m)
    return _dot(e.astype(BF16), v) / den


def _ctx_attn_kernel(p_ref, lam_ref, subg_ref, sink_ref, o_ref, *, lambda_init):
    hd = HEAD_DIM

    def col(c0, h):
        return p_ref[:, c0 + h * hd:c0 + (h + 1) * hd]

    lam = _diff_lambda(lam_ref, lambda_init)
    for h in range(4):
        o = _softmax_attend(col(COL_GQA_Q, h), col(COL_GQA_K, h // 2), col(COL_GQA_V, h // 2))
        o_ref[:, h * hd:(h + 1) * hd] = o.astype(o_ref.dtype)
    for h in range(4):
        q = col(COL_DIFF_Q, h)
        tq = q.shape[0]
        o2 = _softmax_attend(_split_maps(q), col(COL_DIFF_K, h), col(COL_DIFF_V, h))
        d = o2[:tq] - lam * o2[tq:]
        d = _rms(d) * subg_ref[...] * (1.0 - lambda_init)
        o_ref[:, BRANCH_WIDTH + h * hd:BRANCH_WIDTH + (h + 1) * hd] = d.astype(o_ref.dtype)
    for h in range(4):
        o = _softmax_attend(col(COL_NA_Q, h), col(COL_NA_K, h), col(COL_NA_V, h))
        o_ref[:, 2 * BRANCH_WIDTH + h * hd:2 * BRANCH_WIDTH + (h + 1) * hd] = o.astype(o_ref.dtype)
    for h in range(4):
        o = _softmax_attend(col(COL_SWA_Q, h), col(COL_SWA_K, h // 2), col(COL_SWA_V, h // 2),
                            sink=sink_ref[h:h + 1, 0:1])
        o_ref[:, 3 * BRANCH_WIDTH + h * hd:3 * BRANCH_WIDTH + (h + 1) * hd] = o.astype(o_ref.dtype)


def _ctx_attn_call(p_ctx, lam_p, subg, sink_tab, lambda_init):
    b, l, cols = p_ctx.shape
    return pl.pallas_call(
        functools.partial(_ctx_attn_kernel, lambda_init=lambda_init),
        grid=(b,),
        in_specs=[pl.BlockSpec((None, l, cols), lambda i: (i, 0, 0)),
                  pl.BlockSpec((4, DIFF_QK_DIM), lambda i: (0, 0)),
                  pl.BlockSpec((1, HEAD_DIM), lambda i: (0, 0)),
                  pl.BlockSpec((4, LANES), lambda i: (0, 0))],
        out_specs=pl.BlockSpec((None, l, N_BRANCHES * BRANCH_WIDTH), lambda i: (i, 0, 0)),
        out_shape=jax.ShapeDtypeStruct((b, l, N_BRANCHES * BRANCH_WIDTH), BF16),
        compiler_params=_cparams(("parallel",)),
        name="attn_ctx",
    )(p_ctx, lam_p, subg.reshape(1, HEAD_DIM), sink_tab.reshape(4, LANES))


def _merge_kernel(o0, o1, o2, o3, g0, g1, g2, g3, wb_ref, wo_ref, x_ref, mod_ref, ng_ref, *rest,
                  with_router):
    if with_router:
        router_ref, xo_ref, h_ref, lg_ref, acc_ref = rest
    else:
        xo_ref, h_ref, acc_ref = rest
    nc = pl.program_id(2)

    @pl.when(nc == 0)
    def _():
        acc_ref[...] = jnp.zeros_like(acc_ref)

    s = None
    for i, (o_ref, g_ref) in enumerate(((o0, g0), (o1, g1), (o2, g2), (o3, g3))):
        y = _dot(o_ref[...], wb_ref[i])
        term = jax.nn.sigmoid(g_ref[...].astype(F32)) * y
        s = term if s is None else s + term
    acc_ref[...] += _dot(s.astype(BF16), wo_ref[...])

    @pl.when(nc == pl.num_programs(2) - 1)
    def _():
        xn = x_ref[...] + mod_ref[2:3, :] * acc_ref[...]
        xo_ref[...] = xn
        hn = _norm_mod(xn, ng_ref[...], mod_ref, 3, 4)
        h_ref[...] = hn.astype(h_ref.dtype)
        if with_router:
            lg_ref[...] = jnp.dot(hn, router_ref[...], preferred_element_type=F32,
                                  precision=lax.Precision.HIGHEST)


def _merge_call(o_parts, gates, w_branch, w_out, x, mod, next_g, router=None):
    b, r, d = x.shape
    nch = 4
    dc = d // nch
    tm = min(r, 512)
    with_router = router is not None

    def o_spec(cb):
        return pl.BlockSpec((None, tm, BRANCH_WIDTH), lambda i, t, c: (i, t, cb))

    def g_spec(br):
        return pl.BlockSpec((None, tm, dc), lambda i, t, c: (i, t, br * nch + c))

    in_specs = ([o_spec(cb) for _, cb in o_parts] + [g_spec(br) for br in range(N_BRANCHES)] + [
        pl.BlockSpec((N_BRANCHES, BRANCH_WIDTH, dc), lambda i, t, c: (0, 0, c)),
        pl.BlockSpec((dc, d), lambda i, t, c: (c, 0)),
        pl.BlockSpec((None, tm, d), lambda i, t, c: (i, t, 0)),
        pl.BlockSpec((None, 6, d), lambda i, t, c: (i, 0, 0)),
        pl.BlockSpec((1, d), lambda i, t, c: (0, 0))])
    args = [a for a, _ in o_parts] + [gates] * N_BRANCHES + [w_branch, w_out, x, mod, next_g.reshape(1, d)]
    row_spec = pl.BlockSpec((None, tm, d), lambda i, t, c: (i, t, 0))
    out_specs = [row_spec, row_spec]
    out_shape = [jax.ShapeDtypeStruct((b, r, d), F32), jax.ShapeDtypeStruct((b, r, d), BF16)]
    if with_router:
        in_specs.append(pl.BlockSpec((d, LANES), lambda i, t, c: (0, 0)))
        args.append(router)
        out_specs.append(pl.BlockSpec((None, tm, LANES), lambda i, t, c: (i, t, 0)))
        out_shape.append(jax.ShapeDtypeStruct((b, r, LANES), F32))
    return pl.pallas_call(
        functools.partial(_merge_kernel, with_router=with_router),
        grid=(b, r // tm, nch),
        in_specs=in_specs,
        out_specs=out_specs,
        out_shape=out_shape,
        scratch_shapes=[pltpu.VMEM((tm, d), F32)],
        compiler_params=_cparams(("parallel", "parallel", "arbitrary")),
        name="merge",
    )(*args)


def _swiglu_partial(h, wg, wu, wd):
    gate = _dot(h, wg)
    up = _dot(h, wu)
    act = gate * jax.nn.sigmoid(gate) * up
    return _dot(act.astype(BF16), wd)


def _ffn_kernel(h_ref, wg_ref, wu_ref, wd_ref, x_ref, mod_ref, ng_ref, nmod_ref, xo_ref, hn_ref,
                acc_ref):
    f = pl.program_id(2)

    @pl.when(f == 0)
    def _():
        acc_ref[...] = jnp.zeros_like(acc_ref)

    acc_ref[...] += _swiglu_partial(h_ref[...], wg_ref[...], wu_ref[...], wd_ref[...])

    @pl.when(f == pl.num_programs(2) - 1)
    def _():
        xn = x_ref[...] + mod_ref[5:6, :] * acc_ref[...]
        xo_ref[...] = xn
        hn_ref[...] = _norm_mod(xn, ng_ref[...], nmod_ref, 0, 1).astype(hn_ref.dtype)


def _ffn_call(h, wg, wu, wd, x, mod, next_g, next_mod):
    b, r, d = x.shape
    dff = wg.shape[1]
    tf = 512
    tm = min(r, 512)
    row_spec = pl.BlockSpec((None, tm, d), lambda i, t, f: (i, t, 0))
    mod_spec = pl.BlockSpec((None, 6, d), lambda i, t, f: (i, 0, 0))
    return pl.pallas_call(
        _ffn_kernel,
        grid=(b, r // tm, dff // tf),
        in_specs=[row_spec,
                  pl.BlockSpec((d, tf), lambda i, t, f: (0, f)),
                  pl.BlockSpec((d, tf), lambda i, t, f: (0, f)),
                  pl.BlockSpec((tf, d), lambda i, t, f: (f, 0)),
                  row_spec, mod_spec,
                  pl.BlockSpec((1, d), lambda i, t, f: (0, 0)),
                  mod_spec],
        out_specs=[row_spec, row_spec],
        out_shape=[jax.ShapeDtypeStruct((b, r, d), F32), jax.ShapeDtypeStruct((b, r, d), BF16)],
        scratch_shapes=[pltpu.VMEM((tm, d), F32)],
        compiler_params=_cparams(("parallel", "parallel", "arbitrary")),
        name="ffn_dense",
    )(h, wg, wu, wd, x, mod, next_g.reshape(1, d), next_mod)


def _route_kernel(lg_ref, idx_ref, w_ref):
    lg = lg_ref[...]
    lane = lax.broadcasted_iota(jnp.int32, lg.shape, 1)
    valid = lane < N_EXPERTS
    mx = jnp.max(jnp.where(valid, lg, -jnp.inf), axis=-1, keepdims=True)
    e = jnp.where(valid, jnp.exp(lg - mx), 0.0)
    p = e / jnp.sum(e, axis=-1, keepdims=True)
    p1 = jnp.max(p, axis=-1, keepdims=True)
    i1 = jnp.min(jnp.where(p == p1, lane, LANES), axis=-1, keepdims=True)
    rest = jnp.where(jnp.logical_or(lane == i1, jnp.logical_not(valid)), -1.0, p)
    p2 = jnp.max(rest, axis=-1, keepdims=True)
    i2 = jnp.min(jnp.where(rest == p2, lane, LANES), axis=-1, keepdims=True)
    tot = p1 + p2
    idx_ref[...] = jnp.where(lane == 0, i1, jnp.where(lane == 1, i2, 0))
    w_ref[...] = jnp.where(lane == 0, p1 / tot, jnp.where(lane == 1, p2 / tot, 0.0))


def _route_call(logits):
    r = logits.shape[0]
    tm = min(r, 1024)
    spec = pl.BlockSpec((tm, LANES), lambda i: (i, 0))
    return pl.pallas_call(
        _route_kernel,
        grid=(r // tm,),
        in_specs=[spec],
        out_specs=[spec, spec],
        out_shape=[jax.ShapeDtypeStruct((r, LANES), jnp.int32), jax.ShapeDtypeStruct((r, LANES), F32)],
        compiler_params=_cparams(("parallel",)),
        name="route_top2",
    )(logits)


MOE_HALF = 512
MOE_TILE = 2 * MOE_HALF
MOE_VMEM_LIMIT = 60 * 1024 * 1024


def _moe_kernel(te_ref, nr_ref, x_ref, wg_ref, wu_ref, wd_ref, o_ref, acc_ref):
    i = pl.program_id(0)
    f = pl.program_id(1)

    @pl.when(f == 0)
    def _():
        acc_ref[...] = jnp.zeros_like(acc_ref)

    for half in range(MOE_TILE // MOE_HALF):
        rows = slice(half * MOE_HALF, (half + 1) * MOE_HALF)

        @pl.when(nr_ref[i] > half * MOE_HALF)
        def _():
            acc_ref[rows, :] += _swiglu_partial(x_ref[rows, :], wg_ref[...].astype(BF16),
                                                wu_ref[...].astype(BF16), wd_ref[...].astype(BF16))

    @pl.when(f == pl.num_programs(1) - 1)
    def _():
        o_ref[...] = acc_ref[...].astype(o_ref.dtype)


def _moe_call(xs, wg, wu, wd, tile_expert, tile_rows):
    r, d = xs.shape
    dff = wg.shape[2]
    tf = 512
    nf = dff // tf

    def fidx(i, f, nr):
        return jnp.where(nr[i] > 0, f, nf - 1)

    grid_spec = pltpu.PrefetchScalarGridSpec(
        num_scalar_prefetch=2,
        grid=(r // MOE_TILE, nf),
        in_specs=[pl.BlockSpec((MOE_TILE, d), lambda i, f, te, nr: (i, 0), pipeline_mode=pl.Buffered(1)),
                  pl.BlockSpec((None, d, tf), lambda i, f, te, nr: (te[i], 0, fidx(i, f, nr))),
                  pl.BlockSpec((None, d, tf), lambda i, f, te, nr: (te[i], 0, fidx(i, f, nr))),
                  pl.BlockSpec((None, tf, d), lambda i, f, te, nr: (te[i], fidx(i, f, nr), 0))],
        out_specs=pl.BlockSpec((MOE_TILE, d), lambda i, f, te, nr: (i, 0), pipeline_mode=pl.Buffered(1)),
        scratch_shapes=[pltpu.VMEM((MOE_TILE, d), F32)])
    return pl.pallas_call(
        _moe_kernel,
        grid_spec=grid_spec,
        out_shape=jax.ShapeDtypeStruct((r, d), BF16),
        compiler_params=pltpu.CompilerParams(dimension_semantics=("arbitrary", "arbitrary"),
                                             vmem_limit_bytes=MOE_VMEM_LIMIT),
        name="moe_experts",
    )(tile_expert, tile_rows, xs, wg, wu, wd)


def _moe_plan(idx):
    t = idx.shape[0]
    a = t * TOP_K
    flat_e = idx.reshape(a)
    onehot = (flat_e[:, None] == jnp.arange(N_EXPERTS, dtype=jnp.int32)[None, :]).astype(jnp.int32)
    csum = jnp.cumsum(onehot, axis=0)
    rank = jnp.sum((csum - onehot) * onehot, axis=1)
    counts = csum[-1]
    padded = ((counts + MOE_TILE - 1) // MOE_TILE) * MOE_TILE
    ends = jnp.cumsum(padded)
    offs = ends - padded
    dest = offs[flat_e] + rank
    rows = a + N_EXPERTS * MOE_TILE
    src_tok = jnp.zeros((rows,), jnp.int32).at[dest].set(jnp.arange(a, dtype=jnp.int32) // TOP_K)
    tile_start = jnp.arange(rows // MOE_TILE, dtype=jnp.int32) * MOE_TILE
    past = jnp.sum((tile_start[:, None] >= ends[None, :]).astype(jnp.int32), axis=1)
    tile_expert = jnp.minimum(past, N_EXPERTS - 1)
    tile_rows = jnp.where(past < N_EXPERTS,
                          jnp.clip((offs + counts)[tile_expert] - tile_start, 0, MOE_TILE), 0)
    return dest.reshape(t, TOP_K), src_tok, tile_expert, tile_rows.astype(jnp.int32)


def _final_kernel(x_ref, y0_ref, y1_ref, w_ref, mod_ref, g_ref, o_ref):
    y = w_ref[:, 0:1] * y0_ref[...].astype(F32) + w_ref[:, 1:2] * y1_ref[...].astype(F32)
    xn = x_ref[...] + mod_ref[5:6, :] * y
    o_ref[...] = _rms(xn) * g_ref[...]


def _final_call(x, y0, y1, w, mod, g):
    b, r, d = x.shape
    tm = min(r, 512)
    row_spec = pl.BlockSpec((None, tm, d), lambda i, t: (i, t, 0))
    return pl.pallas_call(
        _final_kernel,
        grid=(b, r // tm),
        in_specs=[row_spec, row_spec, row_spec,
                  pl.BlockSpec((None, tm, LANES), lambda i, t: (i, t, 0)),
                  pl.BlockSpec((None, 6, d), lambda i, t: (i, 0, 0)),
                  pl.BlockSpec((1, d), lambda i, t: (0, 0))],
        out_specs=row_spec,
        out_shape=jax.ShapeDtypeStruct((b, r, d), F32),
        compiler_params=_cparams(("parallel", "parallel")),
        name="final_norm",
    )(x, y0, y1, w, mod, g.reshape(1, d))


def _rope_tables(n):
    pos = jnp.arange(n)
    rows = (pos // GRID_W).astype(F32)
    cols = (pos % GRID_W).astype(F32)
    lane = np.arange(LANES)
    out = []
    for hw in (32, 16):
        period = 4 * hw
        u = lane % period
        use_cols = (u // (2 * hw)) == 1
        w = u % (2 * hw)
        freqs = jnp.asarray(ROPE_THETA ** (-(w % hw).astype(np.float32) / hw), F32)
        p = jnp.where(jnp.asarray(use_cols)[None, :], cols[:, None], rows[:, None])
        ang = p * freqs[None, :]
        sign = jnp.asarray(np.where(w < hw, -1.0, 1.0), F32)
        out += [jnp.cos(ang), jnp.sin(ang) * sign[None, :]]
    return out


def kernel(x, c, ctx, c_ctx, attn_norm_g, ffn_norm_g, ada_w, ada_b, w_in, qk_norm_g, diff_lambda,
           diff_subln_g, na_rpb, swa_sink, w_branch, w_out, ffn_w_gate, ffn_w_up, ffn_w_down,
           moe_router, moe_w_gate, moe_w_up, moe_w_down, final_norm_g):
    b, n, d = x.shape
    l = ctx.shape[1]
    depth = w_in.shape[0]
    assert depth == 2, "laid out for one dense layer followed by one routed last layer"
    rows_n = n // GRID_W

    lat_tables = _rope_tables(n)
    ones = jnp.ones((l, LANES), F32)
    zeros = jnp.zeros((l, LANES), F32)
    ctx_tables = [ones, zeros, ones, zeros]

    cvec = jnp.zeros((8, d), F32).at[:b].set(c).at[b].set(c_ctx)
    mods = []
    for i in range(depth):
        m = _adaln(cvec, ada_w[i], ada_b[i]).reshape(8, 6, d)
        mods.append((m[:b], jnp.broadcast_to(m[b:b + 1], (b, 6, d))))

    def mixers(i, h_lat, h_ctx):
        lambda_init = 0.8 - 0.6 * math.exp(-0.3 * i)
        w_qkv = w_in[i, :, :QKV_COLS].astype(BF16)
        sink_tab = jnp.broadcast_to((swa_sink[i].astype(F32) * LOG2E)[:, None], (4, LANES))
        p_lat = _qkv_call(h_lat, w_qkv, lat_tables, qk_norm_g[i])
        p_ctx = _qkv_call(h_ctx, w_qkv, ctx_tables, qk_norm_g[i])
        o_lat = [_gqa_call(p_lat, p_ctx),
                 _diff_call(p_lat, p_ctx, diff_lambda[i], diff_subln_g[i], lambda_init),
                 _na_call(p_lat, p_ctx, _na_bias_tables(na_rpb[i], rows_n)),
                 _swa_call(p_lat, p_ctx, sink_tab.reshape(2, 2, LANES))]
        return p_ctx, o_lat, sink_tab, lambda_init

    mod_lat, mod_ctx = mods[0]
    h_lat = _norm_mod_call(x, attn_norm_g[0], mod_lat, 0, 1)
    h_ctx = _norm_mod_call(ctx, attn_norm_g[0], mod_ctx, 0, 1)
    p_ctx, o_lat, sink_tab, lambda_init = mixers(0, h_lat, h_ctx)
    w_gate = w_in[0, :, QKV_COLS:].astype(BF16)
    wb = w_branch[0].astype(BF16)
    wo = w_out[0].astype(BF16)
    x_lat, h_lat = _merge_call([(o, 0) for o in o_lat], _matmul_call(h_lat, w_gate), wb, wo, x,
                               mod_lat, ffn_norm_g[0])
    o_ctx = _ctx_attn_call(p_ctx, diff_lambda[0], diff_subln_g[0], sink_tab, lambda_init)
    x_ctx, h_ctx = _merge_call([(o_ctx, k) for k in range(N_BRANCHES)], _matmul_call(h_ctx, w_gate),
                               wb, wo, ctx, mod_ctx, ffn_norm_g[0])
    wg = ffn_w_gate[0].astype(BF16)
    wu = ffn_w_up[0].astype(BF16)
    wd = ffn_w_down[0].astype(BF16)
    _, h_ctx = _ffn_call(h_ctx, wg, wu, wd, x_ctx, mod_ctx, attn_norm_g[1], mods[1][1])
    x_lat, h_lat = _ffn_call(h_lat, wg, wu, wd, x_lat, mod_lat, attn_norm_g[1], mods[1][0])

    mod_lat, _ = mods[1]
    _, o_lat, _, _ = mixers(1, h_lat, h_ctx)
    router = jnp.zeros((d, LANES), F32).at[:, :N_EXPERTS].set(moe_router[0])
    x_lat, h_lat, logits = _merge_call(
        [(o, 0) for o in o_lat], _matmul_call(h_lat, w_in[1, :, QKV_COLS:].astype(BF16)),
        w_branch[1].astype(BF16), w_out[1].astype(BF16), x_lat, mod_lat, ffn_norm_g[1], router)

    idx_pad, wts_pad = _route_call(logits.reshape(b * n, LANES))
    dest, src_tok, tile_expert, tile_rows = _moe_plan(idx_pad[:, :TOP_K])
    xs = jnp.take(h_lat.reshape(b * n, d), src_tok, axis=0)
    ys = _moe_call(xs, moe_w_gate[0], moe_w_up[0], moe_w_down[0], tile_expert, tile_rows)
    y0 = jnp.take(ys, dest[:, 0], axis=0).reshape(b, n, d)
    y1 = jnp.take(ys, dest[:, 1], axis=0).reshape(b, n, d)
    return _final_call(x_lat, y0, y1, wts_pad.reshape(b, n, LANES), mod_lat, final_norm_g)
```

```python
import functools
import math

import numpy as np
import jax
import jax.numpy as jnp
from jax import lax
from jax.experimental import pallas as pl
from jax.experimental.pallas import tpu as pltpu

F32 = jnp.float32
BF16 = jnp.bfloat16

GRID_W = 64
HEAD_DIM = 128
N_BRANCHES = 4
BRANCH_WIDTH = 4 * HEAD_DIM
DIFF_QK_DIM = 64
NA_ROWS = 8
NA_COLS = 16
SWA_WINDOW = 128
N_EXPERTS = 8
TOP_K = 2
NORM_EPS = 1e-6
ROPE_THETA = 10000.0
NEG_INF = -1e30
LOG2E = math.log2(math.e)

QKV_COLS = 5120
COL_GQA_Q, COL_GQA_K, COL_GQA_V = 0, 512, 768
COL_DIFF_Q, COL_DIFF_K, COL_DIFF_V = 1024, 1536, 2048
COL_NA_Q, COL_NA_K, COL_NA_V = 2560, 3072, 3584
COL_SWA_Q, COL_SWA_K, COL_SWA_V = 4096, 4608, 4864

LANES = 128
VMEM_LIMIT = 56 * 1024 * 1024

QS128 = HEAD_DIM ** -0.5 * LOG2E
QS64 = DIFF_QK_DIM ** -0.5 * LOG2E


def _cparams(sem):
    return pltpu.CompilerParams(dimension_semantics=sem, vmem_limit_bytes=VMEM_LIMIT)


def _dot(a, b):
    return jnp.dot(a, b, preferred_element_type=F32)


def _dot_nt(a, b):
    return lax.dot_general(a, b, (((1,), (1,)), ((), ())), preferred_element_type=F32)


def _rms(x):
    return x * lax.rsqrt(jnp.mean(x * x, axis=-1, keepdims=True) + NORM_EPS)


def _norm_mod(x, g, mod_ref, shift_idx, scale_idx):
    y = _rms(x) * g
    return y * (1.0 + mod_ref[scale_idx:scale_idx + 1, :]) + mod_ref[shift_idx:shift_idx + 1, :]


def _adaln_kernel(c_ref, w_ref, b_ref, o_ref):
    c = c_ref[...]
    s = c * jax.nn.sigmoid(c)
    o_ref[...] = jnp.dot(s, w_ref[...], preferred_element_type=F32,
                         precision=lax.Precision.HIGHEST) + b_ref[...]


def _adaln(cvec, w, b):
    rows, d = cvec.shape
    cols = w.shape[1]
    tn = 1024 if cols % 1024 == 0 else cols
    return pl.pallas_call(
        _adaln_kernel,
        grid=(cols // tn,),
        in_specs=[pl.BlockSpec((rows, d), lambda j: (0, 0)),
                  pl.BlockSpec((d, tn), lambda j: (0, j)),
                  pl.BlockSpec((1, tn), lambda j: (0, j))],
        out_specs=pl.BlockSpec((rows, tn), lambda j: (0, j)),
        out_shape=jax.ShapeDtypeStruct((rows, cols), F32),
        compiler_params=_cparams(("arbitrary",)),
        name="adaln",
    )(cvec, w, b.reshape(1, cols))


def _norm_mod_kernel(x_ref, g_ref, mod_ref, h_ref, *, shift_idx, scale_idx):
    h_ref[...] = _norm_mod(x_ref[...], g_ref[...], mod_ref, shift_idx, scale_idx).astype(BF16)


def _norm_mod_call(x, g, mod, shift_idx, scale_idx):
    b, r, d = x.shape
    tm = min(r, 512)
    return pl.pallas_call(
        functools.partial(_norm_mod_kernel, shift_idx=shift_idx, scale_idx=scale_idx),
        grid=(b, r // tm),
        in_specs=[pl.BlockSpec((None, tm, d), lambda i, t: (i, t, 0)),
                  pl.BlockSpec((1, d), lambda i, t: (0, 0)),
                  pl.BlockSpec((None, 6, d), lambda i, t: (i, 0, 0))],
        out_specs=pl.BlockSpec((None, tm, d), lambda i, t: (i, t, 0)),
        out_shape=jax.ShapeDtypeStruct((b, r, d), BF16),
        compiler_params=_cparams(("parallel", "parallel")),
        name="norm_mod",
    )(x, g.reshape(1, d), mod)


QKV_TILE = 4 * LANES
_PLAIN = (None, None, 1.0)
_QKV_TILE_OPS = {
    0: [(0, 32, QS128)] * 4,
    1: [(1, 32, 1.0)] * 2 + [_PLAIN] * 2,
    2: [(None, 16, QS64)] * 4,
    3: [(None, 16, 1.0)] * 4,
    5: [(None, None, QS128)] * 4,
    8: [(None, 32, QS128)] * 4,
    9: [(None, 32, 1.0)] * 2 + [_PLAIN] * 2,
}


def _rope(y, cos, sin, hw):
    lane = lax.broadcasted_iota(jnp.int32, y.shape, 1)
    first = (lane % (2 * hw)) < hw
    partner = jnp.where(first, pltpu.roll(y, LANES - hw, 1), pltpu.roll(y, hw, 1))
    return y * cos + partner * sin


def _qkv_kernel(h_ref, w_ref, cos32_ref, sin32_ref, cos16_ref, sin16_ref, g_ref, o_ref):
    h = h_ref[...]
    for j in range(o_ref.shape[1] // QKV_TILE):
        acc = _dot(h, w_ref[:, j * QKV_TILE:(j + 1) * QKV_TILE])
        for c, (norm_row, hw, scale) in enumerate(_QKV_TILE_OPS.get(j, [_PLAIN] * 4)):
            y = acc[:, c * LANES:(c + 1) * LANES]
            if norm_row is not None:
                y = _rms(y) * g_ref[norm_row:norm_row + 1, :]
            if hw == 32:
                y = _rope(y, cos32_ref[...], sin32_ref[...], 32)
            elif hw == 16:
                y = _rope(y, cos16_ref[...], sin16_ref[...], 16)
            if scale != 1.0:
                y = y * scale
            col = j * QKV_TILE + c * LANES
            o_ref[:, col:col + LANES] = y.astype(o_ref.dtype)


def _qkv_call(h, w_qkv, tables, qk_g):
    b, r, d = h.shape
    cols = w_qkv.shape[1]
    tm = min(r, 512)
    tab_spec = pl.BlockSpec((tm, LANES), lambda i, t: (t, 0))
    return pl.pallas_call(
        _qkv_kernel,
        grid=(b, r // tm),
        in_specs=[pl.BlockSpec((None, tm, d), lambda i, t: (i, t, 0)),
                  pl.BlockSpec((d, cols), lambda i, t: (0, 0), pipeline_mode=pl.Buffered(1)),
                  tab_spec, tab_spec, tab_spec, tab_spec,
                  pl.BlockSpec((2, LANES), lambda i, t: (0, 0))],
        out_specs=pl.BlockSpec((None, tm, cols), lambda i, t: (i, t, 0)),
        out_shape=jax.ShapeDtypeStruct((b, r, cols), BF16),
        compiler_params=_cparams(("parallel", "parallel")),
        name="qkv_proj",
    )(h, w_qkv, *tables, qk_g)


def _matmul_kernel(h_ref, w_ref, o_ref):
    o_ref[...] = _dot(h_ref[...], w_ref[...].astype(BF16)).astype(o_ref.dtype)


def _gate_proj_call(h, w_in):
    b, r, d = h.shape
    tn = 512
    cols = w_in.shape[1] - QKV_COLS
    col0 = QKV_COLS // tn
    tm = 2048 if r % 2048 == 0 else r
    return pl.pallas_call(
        _matmul_kernel,
        grid=(b, r // tm, cols // tn),
        in_specs=[pl.BlockSpec((None, tm, d), lambda i, t, j: (i, t, 0)),
                  pl.BlockSpec((d, tn), lambda i, t, j: (0, col0 + j))],
        out_specs=pl.BlockSpec((None, tm, tn), lambda i, t, j: (i, t, j)),
        out_shape=jax.ShapeDtypeStruct((b, r, cols), BF16),
        compiler_params=_cparams(("parallel", "parallel", "arbitrary")),
        name="gate_proj",
    )(h, w_in)


def _diff_lambda(lam_ref, lambda_init):
    a = jnp.sum(lam_ref[0:1, :] * lam_ref[1:2, :], axis=-1, keepdims=True)
    b = jnp.sum(lam_ref[2:3, :] * lam_ref[3:4, :], axis=-1, keepdims=True)
    return jnp.exp(a) - jnp.exp(b) + lambda_init


def _split_maps(q):
    lane = lax.broadcasted_iota(jnp.int32, q.shape, 1)
    zero = jnp.zeros_like(q)
    return jnp.concatenate([jnp.where(lane < DIFF_QK_DIM, q, zero),
                            jnp.where(lane >= DIFF_QK_DIM, q, zero)], axis=0)


FLASH_ROW_BLOCK = 256


def _flash_pair(q2, kc_ref, vc_ref, kl_ref, vl_ref, tk):
    m_rows = q2.shape[0]
    n_lat = kl_ref.shape[0]
    nb = m_rows // FLASH_ROW_BLOCK
    qs = [q2[i * FLASH_ROW_BLOCK:(i + 1) * FLASH_ROW_BLOCK] for i in range(nb)]

    def step(k, v, carry):
        v1 = jnp.concatenate([v, jnp.ones_like(v)], axis=1)
        out = []
        for q, (m, acc) in zip(qs, carry):
            s = _dot_nt(q, k)
            m_new = jnp.maximum(m, jnp.max(s, axis=-1, keepdims=True))
            p = jnp.exp2(s - m_new)
            acc = jnp.exp2(m - m_new) * acc + _dot(p.astype(BF16), v1)
            out.append((m_new, acc))
        return tuple(out)

    carry = tuple((jnp.full((FLASH_ROW_BLOCK, 1), NEG_INF, F32),
                   jnp.zeros((FLASH_ROW_BLOCK, 2 * HEAD_DIM), F32)) for _ in range(nb))
    carry = step(kc_ref[...], vc_ref[...], carry)

    for c in range(n_lat // tk):
        carry = step(kl_ref[c * tk:(c + 1) * tk, :], vl_ref[c * tk:(c + 1) * tk, :], carry)
    return jnp.concatenate([acc[:, :HEAD_DIM] / acc[:, HEAD_DIM:] for _, acc in carry], axis=0)


def _gqa_kernel(q_ref, kc_ref, vc_ref, kl_ref, vl_ref, o_ref, *, tk):
    tq = q_ref.shape[0]
    q = q_ref[...]
    q2 = jnp.concatenate([q[:, :HEAD_DIM], q[:, HEAD_DIM:]], axis=0)
    o = _flash_pair(q2, kc_ref, vc_ref, kl_ref, vl_ref, tk)
    o_ref[:, :HEAD_DIM] = o[:tq].astype(o_ref.dtype)
    o_ref[:, HEAD_DIM:] = o[tq:].astype(o_ref.dtype)


def _diff_kernel(q_ref, kc_ref, vc_ref, kl_ref, vl_ref, lam_ref, subg_ref, o_ref, *, tk, lambda_init):
    tq = q_ref.shape[0]
    o = _flash_pair(_split_maps(q_ref[...]), kc_ref, vc_ref, kl_ref, vl_ref, tk)
    lam = _diff_lambda(lam_ref, lambda_init)
    d = o[:tq] - lam * o[tq:]
    o_ref[...] = (_rms(d) * subg_ref[...] * (1.0 - lambda_init)).astype(o_ref.dtype)


def _gqa_call(p_lat, p_ctx):
    b, n, _ = p_lat.shape
    l = p_ctx.shape[1]
    tq = 512
    tk = 512
    hd = HEAD_DIM
    kcol, vcol = COL_GQA_K // hd, COL_GQA_V // hd
    return pl.pallas_call(
        functools.partial(_gqa_kernel, tk=tk),
        grid=(b, 2, n // tq),
        in_specs=[pl.BlockSpec((None, tq, 2 * hd), lambda i, k, t: (i, t, k)),
                  pl.BlockSpec((None, l, hd), lambda i, k, t: (i, 0, kcol + k)),
                  pl.BlockSpec((None, l, hd), lambda i, k, t: (i, 0, vcol + k)),
                  pl.BlockSpec((None, n, hd), lambda i, k, t: (i, 0, kcol + k)),
                  pl.BlockSpec((None, n, hd), lambda i, k, t: (i, 0, vcol + k))],
        out_specs=pl.BlockSpec((None, tq, 2 * hd), lambda i, k, t: (i, t, k)),
        out_shape=jax.ShapeDtypeStruct((b, n, BRANCH_WIDTH), BF16),
        compiler_params=_cparams(("parallel", "parallel", "arbitrary")),
        name="attn_gqa",
    )(p_lat, p_ctx, p_ctx, p_lat, p_lat)


def _diff_call(p_lat, p_ctx, lam_p, subg, lambda_init):
    b, n, _ = p_lat.shape
    l = p_ctx.shape[1]
    tq = 512
    tk = 512
    hd = HEAD_DIM
    qcol, kcol, vcol = COL_DIFF_Q // hd, COL_DIFF_K // hd, COL_DIFF_V // hd
    return pl.pallas_call(
        functools.partial(_diff_kernel, tk=tk, lambda_init=lambda_init),
        grid=(b, 4, n // tq),
        in_specs=[pl.BlockSpec((None, tq, hd), lambda i, h, t: (i, t, qcol + h)),
                  pl.BlockSpec((None, l, hd), lambda i, h, t: (i, 0, kcol + h)),
                  pl.BlockSpec((None, l, hd), lambda i, h, t: (i, 0, vcol + h)),
                  pl.BlockSpec((None, n, hd), lambda i, h, t: (i, 0, kcol + h)),
                  pl.BlockSpec((None, n, hd), lambda i, h, t: (i, 0, vcol + h)),
                  pl.BlockSpec((4, DIFF_QK_DIM), lambda i, h, t: (0, 0)),
                  pl.BlockSpec((1, hd), lambda i, h, t: (0, 0))],
        out_specs=pl.BlockSpec((None, tq, hd), lambda i, h, t: (i, t, h)),
        out_shape=jax.ShapeDtypeStruct((b, n, BRANCH_WIDTH), BF16),
        compiler_params=_cparams(("parallel", "parallel", "arbitrary")),
        name="attn_diff",
    )(p_lat, p_ctx, p_ctx, p_lat, p_lat, lam_p, subg.reshape(1, hd))


NA_GROUP_ROWS = 8
NA_WIN_ROWS = 2 * NA_GROUP_ROWS


def _na_window_start(g, rows_n):
    return jnp.clip(g * NA_GROUP_ROWS - NA_ROWS // 2, 0, rows_n - NA_WIN_ROWS)


NA_DI = 2 * NA_ROWS - 1
NA_TAB_PAD = NA_WIN_ROWS
NA_TAB_LANES = 3072


def _na_bias_tables(rpb):
    nh = rpb.shape[0]
    qc = np.arange(GRID_W)[:, None]
    kc = np.arange(GRID_W)[None, :]
    cstart = np.clip(qc - NA_COLS // 2, 0, GRID_W - NA_COLS)
    ok_col = (kc >= cstart) & (kc < cstart + NA_COLS)
    dj = np.clip(kc - qc + NA_COLS - 1, 0, 2 * NA_COLS - 2)
    sel_j = (dj[..., None] == np.arange(2 * NA_COLS - 1)) & ok_col[..., None]
    by_col = jnp.einsum("hij,qkj->hqik", rpb.astype(F32), jnp.asarray(sel_j, F32),
                        precision=lax.Precision.HIGHEST)
    by_col = jnp.where(ok_col[None, :, None, :], by_col * LOG2E, NEG_INF)
    strip = by_col.reshape(nh, GRID_W, NA_DI * GRID_W)
    total = NA_TAB_LANES + GRID_W
    strip = jnp.pad(strip, ((0, 0), (0, 0), (NA_TAB_PAD * GRID_W, total - (NA_TAB_PAD + NA_DI) * GRID_W)),
                    constant_values=NEG_INF)
    return jnp.stack([strip[:, :, :NA_TAB_LANES], strip[:, :, GRID_W:]], axis=1)


def _na_kernel(q_ref, kc_ref, vc_ref, kl_ref, vl_ref, tab_ref, o_ref, *, rows_n):
    g = pl.program_id(2)
    win = NA_WIN_ROWS * GRID_W
    ws = _na_window_start(g, rows_n)
    off = pl.multiple_of(ws * GRID_W, 256)
    q = q_ref[...]
    lane = lax.broadcasted_iota(jnp.int32, (GRID_W, win), 1)
    bias = []
    for a in range(NA_GROUP_ROWS):
        r = g * NA_GROUP_ROWS + a
        rs = jnp.clip(r - NA_ROWS // 2, 0, rows_n - NA_ROWS)
        blk = ws - r + NA_ROWS - 1 + NA_TAB_PAD
        start = pl.multiple_of((blk >> 1) * (2 * GRID_W), 2 * GRID_W)
        strip = tab_ref[blk & 1, :, pl.ds(start, win)]
        lo = (rs - ws) * GRID_W
        in_rows = jnp.logical_and(lane >= lo, lane < lo + NA_ROWS * GRID_W)
        bias.append(jnp.where(in_rows, strip, NEG_INF))
    s_b = _dot_nt(q, kl_ref[pl.ds(off, win), :]) + jnp.concatenate(bias, axis=0)
    s_c = _dot_nt(q, kc_ref[...])
    m = jnp.maximum(jnp.max(s_b, axis=-1, keepdims=True), jnp.max(s_c, axis=-1, keepdims=True))
    p_b = jnp.exp2(s_b - m)
    p_c = jnp.exp2(s_c - m)
    l = jnp.sum(p_b, axis=-1, keepdims=True) + jnp.sum(p_c, axis=-1, keepdims=True)
    o = _dot(p_c.astype(BF16), vc_ref[...]) + _dot(p_b.astype(BF16), vl_ref[pl.ds(off, win), :])
    o_ref[...] = (o / l).astype(o_ref.dtype)


def _na_call(p_lat, p_ctx, bias):
    b, n, _ = p_lat.shape
    l = p_ctx.shape[1]
    hd = HEAD_DIM
    rows_n = n // GRID_W
    n_groups = rows_n // NA_GROUP_ROWS
    tq = NA_GROUP_ROWS * GRID_W
    win = NA_WIN_ROWS * GRID_W
    qcol, kcol, vcol = COL_NA_Q // hd, COL_NA_K // hd, COL_NA_V // hd

    return pl.pallas_call(
        functools.partial(_na_kernel, rows_n=rows_n),
        grid=(b, 4, n_groups),
        in_specs=[pl.BlockSpec((None, tq, hd), lambda i, h, g: (i, g, qcol + h)),
                  pl.BlockSpec((None, l, hd), lambda i, h, g: (i, 0, kcol + h)),
                  pl.BlockSpec((None, l, hd), lambda i, h, g: (i, 0, vcol + h)),
                  pl.BlockSpec((None, n, hd), lambda i, h, g: (i, 0, kcol + h)),
                  pl.BlockSpec((None, n, hd), lambda i, h, g: (i, 0, vcol + h)),
                  pl.BlockSpec((None, 2, GRID_W, NA_TAB_LANES), lambda i, h, g: (h, 0, 0, 0))],
        out_specs=pl.BlockSpec((None, tq, hd), lambda i, h, g: (i, g, h)),
        out_shape=jax.ShapeDtypeStruct((b, n, BRANCH_WIDTH), BF16),
        compiler_params=_cparams(("parallel", "parallel", "arbitrary")),
        name="attn_na",
    )(p_lat, p_ctx, p_ctx, p_lat, p_lat, bias)


def _swa_kernel(q_ref, kc_ref, vc_ref, kl_ref, vl_ref, sink_ref, o_ref, *, win):
    t = pl.program_id(2)
    tq = q_ref.shape[0]
    n = kl_ref.shape[0]
    q0 = t * tq
    ws = pl.multiple_of(jnp.clip(q0 - SWA_WINDOW, 0, n - win), SWA_WINDOW)
    kb = kl_ref[pl.ds(ws, win), :]
    vb = vl_ref[pl.ds(ws, win), :]
    row = lax.broadcasted_iota(jnp.int32, (tq, win), 0)
    col = lax.broadcasted_iota(jnp.int32, (tq, win), 1)
    valid = jnp.abs(row + (q0 - ws) - col) <= SWA_WINDOW
    for gi in range(2):
        q = q_ref[:, gi * HEAD_DIM:(gi + 1) * HEAD_DIM]
        sink = sink_ref[gi:gi + 1, 0:1]
        s_b = jnp.where(valid, _dot_nt(q, kb), NEG_INF)
        s_c = _dot_nt(q, kc_ref[...])
        m = jnp.maximum(jnp.maximum(jnp.max(s_b, axis=-1, keepdims=True),
                                    jnp.max(s_c, axis=-1, keepdims=True)), sink)
        e_b = jnp.exp2(s_b - m)
        e_c = jnp.exp2(s_c - m)
        den = (jnp.sum(e_b, axis=-1, keepdims=True) + jnp.sum(e_c, axis=-1, keepdims=True)
               + jnp.exp2(sink - m))
        o = _dot(e_c.astype(BF16), vc_ref[...]) + _dot(e_b.astype(BF16), vb)
        o_ref[:, gi * HEAD_DIM:(gi + 1) * HEAD_DIM] = (o / den).astype(o_ref.dtype)


def _swa_call(p_lat, p_ctx, sink_tab):
    b, n, _ = p_lat.shape
    l = p_ctx.shape[1]
    hd = HEAD_DIM
    tq = 512
    win = tq + 2 * SWA_WINDOW
    qcol, kcol, vcol = COL_SWA_Q // (2 * hd), COL_SWA_K // hd, COL_SWA_V // hd
    return pl.pallas_call(
        functools.partial(_swa_kernel, win=win),
        grid=(b, 2, n // tq),
        in_specs=[pl.BlockSpec((None, tq, 2 * hd), lambda i, k, t: (i, t, qcol + k)),
                  pl.BlockSpec((None, l, hd), lambda i, k, t: (i, 0, kcol + k)),
                  pl.BlockSpec((None, l, hd), lambda i, k, t: (i, 0, vcol + k)),
                  pl.BlockSpec((None, n, hd), lambda i, k, t: (i, 0, kcol + k)),
                  pl.BlockSpec((None, n, hd), lambda i, k, t: (i, 0, vcol + k)),
                  pl.BlockSpec((None, 2, LANES), lambda i, k, t: (k, 0, 0))],
        out_specs=pl.BlockSpec((None, tq, 2 * hd), lambda i, k, t: (i, t, k)),
        out_shape=jax.ShapeDtypeStruct((b, n, BRANCH_WIDTH), BF16),
        compiler_params=_cparams(("parallel", "parallel", "arbitrary")),
        name="attn_swa",
    )(p_lat, p_ctx, p_ctx, p_lat, p_lat, sink_tab)


def _softmax_attend(q, k, v, sink=None):
    s = _dot_nt(q, k)
    m = jnp.max(s, axis=-1, keepdims=True)
    if sink is not None:
        m = jnp.maximum(m, sink)
    e = jnp.exp2(s - m)
    den = jnp.sum(e, axis=-1, keepdims=True)
    if sink is not None:
        den = den + jnp.exp2(sink - m)
    return _dot(e.astype(BF16), v) / den


def _ctx_attn_kernel(p_ref, lam_ref, subg_ref, sink_ref, o_ref, *, lambda_init):
    hd = HEAD_DIM

    def col(c0, h):
        return p_ref[:, c0 + h * hd:c0 + (h + 1) * hd]

    lam = _diff_lambda(lam_ref, lambda_init)
    for h in range(4):
        o = _softmax_attend(col(COL_GQA_Q, h), col(COL_GQA_K, h // 2), col(COL_GQA_V, h // 2))
        o_ref[:, h * hd:(h + 1) * hd] = o.astype(o_ref.dtype)
    for h in range(4):
        q = col(COL_DIFF_Q, h)
        tq = q.shape[0]
        o2 = _softmax_attend(_split_maps(q), col(COL_DIFF_K, h), col(COL_DIFF_V, h))
        d = o2[:tq] - lam * o2[tq:]
        d = _rms(d) * subg_ref[...] * (1.0 - lambda_init)
        o_ref[:, BRANCH_WIDTH + h * hd:BRANCH_WIDTH + (h + 1) * hd] = d.astype(o_ref.dtype)
    for h in range(4):
        o = _softmax_attend(col(COL_NA_Q, h), col(COL_NA_K, h), col(COL_NA_V, h))
        o_ref[:, 2 * BRANCH_WIDTH + h * hd:2 * BRANCH_WIDTH + (h + 1) * hd] = o.astype(o_ref.dtype)
    for h in range(4):
        o = _softmax_attend(col(COL_SWA_Q, h), col(COL_SWA_K, h // 2), col(COL_SWA_V, h // 2),
                            sink=sink_ref[h:h + 1, 0:1])
        o_ref[:, 3 * BRANCH_WIDTH + h * hd:3 * BRANCH_WIDTH + (h + 1) * hd] = o.astype(o_ref.dtype)


def _ctx_attn_call(p_ctx, lam_p, subg, sink_tab, lambda_init):
    b, l, cols = p_ctx.shape
    return pl.pallas_call(
        functools.partial(_ctx_attn_kernel, lambda_init=lambda_init),
        grid=(b,),
        in_specs=[pl.BlockSpec((None, l, cols), lambda i: (i, 0, 0)),
                  pl.BlockSpec((4, DIFF_QK_DIM), lambda i: (0, 0)),
                  pl.BlockSpec((1, HEAD_DIM), lambda i: (0, 0)),
                  pl.BlockSpec((4, LANES), lambda i: (0, 0))],
        out_specs=pl.BlockSpec((None, l, N_BRANCHES * BRANCH_WIDTH), lambda i: (i, 0, 0)),
        out_shape=jax.ShapeDtypeStruct((b, l, N_BRANCHES * BRANCH_WIDTH), BF16),
        compiler_params=_cparams(("parallel",)),
        name="attn_ctx",
    )(p_ctx, lam_p, subg.reshape(1, HEAD_DIM), sink_tab.reshape(4, LANES))


def _merge_kernel(o0, o1, o2, o3, g0, g1, g2, g3, wb_ref, wo_ref, x_ref, mod_ref, ng_ref, *rest,
                  with_router):
    if with_router:
        router_ref, xo_ref, h_ref, lg_ref, acc_ref = rest
    else:
        xo_ref, h_ref, acc_ref = rest
    nc = pl.program_id(2)

    @pl.when(nc == 0)
    def _():
        acc_ref[...] = jnp.zeros_like(acc_ref)

    s = None
    for i, (o_ref, g_ref) in enumerate(((o0, g0), (o1, g1), (o2, g2), (o3, g3))):
        y = _dot(o_ref[...], wb_ref[i])
        term = jax.nn.sigmoid(g_ref[...].astype(F32)) * y
        s = term if s is None else s + term
    acc_ref[...] += _dot(s.astype(BF16), wo_ref[...])

    @pl.when(nc == pl.num_programs(2) - 1)
    def _():
        xn = x_ref[...] + mod_ref[2:3, :] * acc_ref[...]
        xo_ref[...] = xn
        hn = _norm_mod(xn, ng_ref[...], mod_ref, 3, 4)
        h_ref[...] = hn.astype(h_ref.dtype)
        if with_router:
            lg_ref[...] = jnp.dot(hn, router_ref[...], preferred_element_type=F32,
                                  precision=lax.Precision.HIGHEST)


def _merge_call(o_parts, gates, w_branch, w_out, x, mod, next_g, router=None):
    b, r, d = x.shape
    nch = 4
    dc = d // nch
    tm = min(r, 512)
    with_router = router is not None

    def o_spec(cb):
        return pl.BlockSpec((None, tm, BRANCH_WIDTH), lambda i, t, c: (i, t, cb))

    def g_spec(br):
        return pl.BlockSpec((None, tm, dc), lambda i, t, c: (i, t, br * nch + c))

    in_specs = ([o_spec(cb) for _, cb in o_parts] + [g_spec(br) for br in range(N_BRANCHES)] + [
        pl.BlockSpec((N_BRANCHES, BRANCH_WIDTH, dc), lambda i, t, c: (0, 0, c)),
        pl.BlockSpec((dc, d), lambda i, t, c: (c, 0)),
        pl.BlockSpec((None, tm, d), lambda i, t, c: (i, t, 0)),
        pl.BlockSpec((None, 6, d), lambda i, t, c: (i, 0, 0)),
        pl.BlockSpec((1, d), lambda i, t, c: (0, 0))])
    args = [a for a, _ in o_parts] + [gates] * N_BRANCHES + [w_branch, w_out, x, mod, next_g.reshape(1, d)]
    row_spec = pl.BlockSpec((None, tm, d), lambda i, t, c: (i, t, 0))
    out_specs = [row_spec, row_spec]
    out_shape = [jax.ShapeDtypeStruct((b, r, d), F32), jax.ShapeDtypeStruct((b, r, d), BF16)]
    if with_router:
        in_specs.append(pl.BlockSpec((d, LANES), lambda i, t, c: (0, 0)))
        args.append(router)
        out_specs.append(pl.BlockSpec((None, tm, LANES), lambda i, t, c: (i, t, 0)))
        out_shape.append(jax.ShapeDtypeStruct((b, r, LANES), F32))
    return pl.pallas_call(
        functools.partial(_merge_kernel, with_router=with_router),
        grid=(b, r // tm, nch),
        in_specs=in_specs,
        out_specs=out_specs,
        out_shape=out_shape,
        scratch_shapes=[pltpu.VMEM((tm, d), F32)],
        compiler_params=_cparams(("parallel", "parallel", "arbitrary")),
        name="merge",
    )(*args)


def _swiglu_partial(h, wg, wu, wd):
    gate = _dot(h, wg)
    up = _dot(h, wu)
    act = gate * jax.nn.sigmoid(gate) * up
    return _dot(act.astype(BF16), wd)


def _ffn_kernel(h_ref, wg_ref, wu_ref, wd_ref, x_ref, mod_ref, ng_ref, nmod_ref, xo_ref, hn_ref,
                acc_ref):
    f = pl.program_id(2)

    @pl.when(f == 0)
    def _():
        acc_ref[...] = jnp.zeros_like(acc_ref)

    acc_ref[...] += _swiglu_partial(h_ref[...], wg_ref[...], wu_ref[...], wd_ref[...])

    @pl.when(f == pl.num_programs(2) - 1)
    def _():
        xn = x_ref[...] + mod_ref[5:6, :] * acc_ref[...]
        xo_ref[...] = xn
        hn_ref[...] = _norm_mod(xn, ng_ref[...], nmod_ref, 0, 1).astype(hn_ref.dtype)


def _ffn_call(h, wg, wu, wd, x, mod, next_g, next_mod):
    b, r, d = x.shape
    dff = wg.shape[1]
    tf = 512
    tm = min(r, 512)
    row_spec = pl.BlockSpec((None, tm, d), lambda i, t, f: (i, t, 0))
    mod_spec = pl.BlockSpec((None, 6, d), lambda i, t, f: (i, 0, 0))
    return pl.pallas_call(
        _ffn_kernel,
        grid=(b, r // tm, dff // tf),
        in_specs=[row_spec,
                  pl.BlockSpec((d, tf), lambda i, t, f: (0, f)),
                  pl.BlockSpec((d, tf), lambda i, t, f: (0, f)),
                  pl.BlockSpec((tf, d), lambda i, t, f: (f, 0)),
                  row_spec, mod_spec,
                  pl.BlockSpec((1, d), lambda i, t, f: (0, 0)),
                  mod_spec],
        out_specs=[row_spec, row_spec],
        out_shape=[jax.ShapeDtypeStruct((b, r, d), F32), jax.ShapeDtypeStruct((b, r, d), BF16)],
        scratch_shapes=[pltpu.VMEM((tm, d), F32)],
        compiler_params=_cparams(("parallel", "parallel", "arbitrary")),
        name="ffn_dense",
    )(h, wg, wu, wd, x, mod, next_g.reshape(1, d), next_mod)


def _route_kernel(lg_ref, idx_ref, w_ref):
    lg = lg_ref[...]
    lane = lax.broadcasted_iota(jnp.int32, lg.shape, 1)
    valid = lane < N_EXPERTS
    mx = jnp.max(jnp.where(valid, lg, -jnp.inf), axis=-1, keepdims=True)
    e = jnp.where(valid, jnp.exp(lg - mx), 0.0)
    p = e / jnp.sum(e, axis=-1, keepdims=True)
    p1 = jnp.max(p, axis=-1, keepdims=True)
    i1 = jnp.min(jnp.where(p == p1, lane, LANES), axis=-1, keepdims=True)
    rest = jnp.where(jnp.logical_or(lane == i1, jnp.logical_not(valid)), -1.0, p)
    p2 = jnp.max(rest, axis=-1, keepdims=True)
    i2 = jnp.min(jnp.where(rest == p2, lane, LANES), axis=-1, keepdims=True)
    tot = p1 + p2
    idx_ref[...] = jnp.where(lane == 0, i1, jnp.where(lane == 1, i2, 0))
    w_ref[...] = jnp.where(lane == 0, p1 / tot, jnp.where(lane == 1, p2 / tot, 0.0))


def _route_call(logits):
    r = logits.shape[0]
    tm = min(r, 1024)
    spec = pl.BlockSpec((tm, LANES), lambda i: (i, 0))
    return pl.pallas_call(
        _route_kernel,
        grid=(r // tm,),
        in_specs=[spec],
        out_specs=[spec, spec],
        out_shape=[jax.ShapeDtypeStruct((r, LANES), jnp.int32), jax.ShapeDtypeStruct((r, LANES), F32)],
        compiler_params=_cparams(("parallel",)),
        name="route_top2",
    )(logits)


MOE_HALF = 512
MOE_TILE = 2 * MOE_HALF
MOE_VMEM_LIMIT = 60 * 1024 * 1024


def _moe_kernel(te_ref, nr_ref, x_ref, wg_ref, wu_ref, wd_ref, o_ref, acc_ref):
    i = pl.program_id(0)
    f = pl.program_id(1)

    @pl.when(f == 0)
    def _():
        acc_ref[...] = jnp.zeros_like(acc_ref)

    for half in range(MOE_TILE // MOE_HALF):
        rows = slice(half * MOE_HALF, (half + 1) * MOE_HALF)

        @pl.when(nr_ref[i] > half * MOE_HALF)
        def _():
            acc_ref[rows, :] += _swiglu_partial(x_ref[rows, :], wg_ref[...].astype(BF16),
                                                wu_ref[...].astype(BF16), wd_ref[...].astype(BF16))

    @pl.when(f == pl.num_programs(1) - 1)
    def _():
        o_ref[...] = acc_ref[...].astype(o_ref.dtype)


def _moe_call(xs, wg, wu, wd, tile_expert, tile_rows):
    r, d = xs.shape
    dff = wg.shape[2]
    tf = 512
    nf = dff // tf

    def fidx(i, f, nr):
        return jnp.where(nr[i] > 0, f, nf - 1)

    grid_spec = pltpu.PrefetchScalarGridSpec(
        num_scalar_prefetch=2,
        grid=(r // MOE_TILE, nf),
        in_specs=[pl.BlockSpec((MOE_TILE, d), lambda i, f, te, nr: (i, 0), pipeline_mode=pl.Buffered(1)),
                  pl.BlockSpec((None, d, tf), lambda i, f, te, nr: (te[i], 0, fidx(i, f, nr))),
                  pl.BlockSpec((None, d, tf), lambda i, f, te, nr: (te[i], 0, fidx(i, f, nr))),
                  pl.BlockSpec((None, tf, d), lambda i, f, te, nr: (te[i], fidx(i, f, nr), 0))],
        out_specs=pl.BlockSpec((MOE_TILE, d), lambda i, f, te, nr: (i, 0), pipeline_mode=pl.Buffered(1)),
        scratch_shapes=[pltpu.VMEM((MOE_TILE, d), F32)])
    return pl.pallas_call(
        _moe_kernel,
        grid_spec=grid_spec,
        out_shape=jax.ShapeDtypeStruct((r, d), BF16),
        compiler_params=pltpu.CompilerParams(dimension_semantics=("arbitrary", "arbitrary"),
                                             vmem_limit_bytes=MOE_VMEM_LIMIT),
        name="moe_experts",
    )(tile_expert, tile_rows, xs, wg, wu, wd)


def _moe_plan(idx):
    t = idx.shape[0]
    a = t * TOP_K
    flat_e = idx.reshape(a)
    onehot = (flat_e[:, None] == jnp.arange(N_EXPERTS, dtype=jnp.int32)[None, :]).astype(jnp.int32)
    csum = jnp.cumsum(onehot, axis=0)
    rank = jnp.sum((csum - onehot) * onehot, axis=1)
    counts = csum[-1]
    padded = ((counts + MOE_TILE - 1) // MOE_TILE) * MOE_TILE
    ends = jnp.cumsum(padded)
    offs = ends - padded
    dest = offs[flat_e] + rank
    rows = a + N_EXPERTS * MOE_TILE
    src_tok = (jnp.arange(rows, dtype=jnp.int32) % t).at[dest].set(jnp.arange(a, dtype=jnp.int32) // TOP_K)
    tile_start = jnp.arange(rows // MOE_TILE, dtype=jnp.int32) * MOE_TILE
    past = jnp.sum((tile_start[:, None] >= ends[None, :]).astype(jnp.int32), axis=1)
    tile_expert = jnp.minimum(past, N_EXPERTS - 1)
    tile_rows = jnp.where(past < N_EXPERTS,
                          jnp.clip((offs + counts)[tile_expert] - tile_start, 0, MOE_TILE), 0)
    return dest.reshape(t, TOP_K), src_tok, tile_expert, tile_rows.astype(jnp.int32)


def _final_kernel(x_ref, y0_ref, y1_ref, w_ref, mod_ref, g_ref, o_ref):
    y = w_ref[:, 0:1] * y0_ref[...].astype(F32) + w_ref[:, 1:2] * y1_ref[...].astype(F32)
    xn = x_ref[...] + mod_ref[5:6, :] * y
    o_ref[...] = _rms(xn) * g_ref[...]


def _final_call(x, y0, y1, w, mod, g):
    b, r, d = x.shape
    tm = min(r, 512)
    row_spec = pl.BlockSpec((None, tm, d), lambda i, t: (i, t, 0))
    return pl.pallas_call(
        _final_kernel,
        grid=(b, r // tm),
        in_specs=[row_spec, row_spec, row_spec,
                  pl.BlockSpec((None, tm, LANES), lambda i, t: (i, t, 0)),
                  pl.BlockSpec((None, 6, d), lambda i, t: (i, 0, 0)),
                  pl.BlockSpec((1, d), lambda i, t: (0, 0))],
        out_specs=row_spec,
        out_shape=jax.ShapeDtypeStruct((b, r, d), F32),
        compiler_params=_cparams(("parallel", "parallel")),
        name="final_norm",
    )(x, y0, y1, w, mod, g.reshape(1, d))


def _rope_tables(n):
    pos = jnp.arange(n)
    rows = (pos // GRID_W).astype(F32)
    cols = (pos % GRID_W).astype(F32)
    lane = np.arange(LANES)
    out = []
    for hw in (32, 16):
        period = 4 * hw
        u = lane % period
        use_cols = (u // (2 * hw)) == 1
        w = u % (2 * hw)
        freqs = jnp.asarray(ROPE_THETA ** (-(w % hw).astype(np.float32) / hw), F32)
        p = jnp.where(jnp.asarray(use_cols)[None, :], cols[:, None], rows[:, None])
        ang = p * freqs[None, :]
        sign = jnp.asarray(np.where(w < hw, -1.0, 1.0), F32)
        out += [jnp.cos(ang), jnp.sin(ang) * sign[None, :]]
    return out


def kernel(x, c, ctx, c_ctx, attn_norm_g, ffn_norm_g, ada_w, ada_b, w_in, qk_norm_g, diff_lambda,
           diff_subln_g, na_rpb, swa_sink, w_branch, w_out, ffn_w_gate, ffn_w_up, ffn_w_down,
           moe_router, moe_w_gate, moe_w_up, moe_w_down, final_norm_g):
    b, n, d = x.shape
    l = ctx.shape[1]
    depth = w_in.shape[0]
    assert depth == 2, "laid out for one dense layer followed by one routed last layer"
    rows_n = n // GRID_W

    lat_tables = _rope_tables(n)
    ones = jnp.ones((l, LANES), F32)
    zeros = jnp.zeros((l, LANES), F32)
    ctx_tables = [ones, zeros, ones, zeros]

    cvec = jnp.zeros((8, d), F32).at[:b].set(c).at[b].set(c_ctx)
    mods = []
    for i in range(depth):
        m = _adaln(cvec, ada_w[i], ada_b[i]).reshape(8, 6, d)
        mods.append((m[:b], jnp.broadcast_to(m[b:b + 1], (b, 6, d))))

    def mixers(i, h_lat, h_ctx):
        lambda_init = 0.8 - 0.6 * math.exp(-0.3 * i)
        w_qkv = w_in[i, :, :QKV_COLS].astype(BF16)
        sink_tab = jnp.broadcast_to((swa_sink[i].astype(F32) * LOG2E)[:, None], (4, LANES))
        p_lat = _qkv_call(h_lat, w_qkv, lat_tables, qk_norm_g[i])
        p_ctx = _qkv_call(h_ctx, w_qkv, ctx_tables, qk_norm_g[i])
        o_lat = [_gqa_call(p_lat, p_ctx),
                 _diff_call(p_lat, p_ctx, diff_lambda[i], diff_subln_g[i], lambda_init),
                 _na_call(p_lat, p_ctx, _na_bias_tables(na_rpb[i])),
                 _swa_call(p_lat, p_ctx, sink_tab.reshape(2, 2, LANES))]
        return p_ctx, o_lat, sink_tab, lambda_init

    mod_lat, mod_ctx = mods[0]
    h_lat = _norm_mod_call(x, attn_norm_g[0], mod_lat, 0, 1)
    h_ctx = _norm_mod_call(ctx, attn_norm_g[0], mod_ctx, 0, 1)
    p_ctx, o_lat, sink_tab, lambda_init = mixers(0, h_lat, h_ctx)
    wb = w_branch[0].astype(BF16)
    wo = w_out[0].astype(BF16)
    x_lat, h_lat = _merge_call([(o, 0) for o in o_lat], _gate_proj_call(h_lat, w_in[0]), wb, wo, x,
                               mod_lat, ffn_norm_g[0])
    o_ctx = _ctx_attn_call(p_ctx, diff_lambda[0], diff_subln_g[0], sink_tab, lambda_init)
    x_ctx, h_ctx = _merge_call([(o_ctx, k) for k in range(N_BRANCHES)], _gate_proj_call(h_ctx, w_in[0]),
                               wb, wo, ctx, mod_ctx, ffn_norm_g[0])
    wg = ffn_w_gate[0].astype(BF16)
    wu = ffn_w_up[0].astype(BF16)
    wd = ffn_w_down[0].astype(BF16)
    _, h_ctx = _ffn_call(h_ctx, wg, wu, wd, x_ctx, mod_ctx, attn_norm_g[1], mods[1][1])
    x_lat, h_lat = _ffn_call(h_lat, wg, wu, wd, x_lat, mod_lat, attn_norm_g[1], mods[1][0])

    mod_lat, _ = mods[1]
    _, o_lat, _, _ = mixers(1, h_lat, h_ctx)
    router = jnp.zeros((d, LANES), F32).at[:, :N_EXPERTS].set(moe_router[0])
    x_lat, h_lat, logits = _merge_call(
        [(o, 0) for o in o_lat], _gate_proj_call(h_lat, w_in[1]),
        w_branch[1].astype(BF16), w_out[1].astype(BF16), x_lat, mod_lat, ffn_norm_g[1], router)

    idx_pad, wts_pad = _route_call(logits.reshape(b * n, LANES))
    dest, src_tok, tile_expert, tile_rows = _moe_plan(idx_pad[:, :TOP_K])
    xs = jnp.take(h_lat.reshape(b * n, d), src_tok, axis=0)
    ys = _moe_call(xs, moe_w_gate[0], moe_w_up[0], moe_w_down[0], tile_expert, tile_rows)
    y0 = jnp.take(ys, dest[:, 0], axis=0).reshape(b, n, d)
    y1 = jnp.take(ys, dest[:, 1], axis=0).reshape(b, n, d)
    return _final_call(x_lat, y0, y1, wts_pad.reshape(b, n, LANES), mod_lat, final_norm_g)
```

```python
import functools
import math

import numpy as np
import jax
import jax.numpy as jnp
from jax import lax
from jax.experimental import pallas as pl
from jax.experimental.pallas import tpu as pltpu

F32 = jnp.float32
BF16 = jnp.bfloat16

GRID_W = 64
HEAD_DIM = 128
N_BRANCHES = 4
BRANCH_WIDTH = 4 * HEAD_DIM
DIFF_QK_DIM = 64
NA_ROWS = 8
NA_COLS = 16
SWA_WINDOW = 128
N_EXPERTS = 8
TOP_K = 2
NORM_EPS = 1e-6
ROPE_THETA = 10000.0
NEG_INF = -1e30
LOG2E = math.log2(math.e)

QKV_COLS = 5120
COL_GQA_Q, COL_GQA_K, COL_GQA_V = 0, 512, 768
COL_DIFF_Q, COL_DIFF_K, COL_DIFF_V = 1024, 1536, 2048
COL_NA_Q, COL_NA_K, COL_NA_V = 2560, 3072, 3584
COL_SWA_Q, COL_SWA_K, COL_SWA_V = 4096, 4608, 4864

LANES = 128
VMEM_LIMIT = 56 * 1024 * 1024

QS128 = HEAD_DIM ** -0.5 * LOG2E
QS64 = DIFF_QK_DIM ** -0.5 * LOG2E


def _cparams(sem):
    return pltpu.CompilerParams(dimension_semantics=sem, vmem_limit_bytes=VMEM_LIMIT)


def _dot(a, b):
    return jnp.dot(a, b, preferred_element_type=F32)


def _dot_nt(a, b):
    return lax.dot_general(a, b, (((1,), (1,)), ((), ())), preferred_element_type=F32)


def _rms(x):
    return x * lax.rsqrt(jnp.mean(x * x, axis=-1, keepdims=True) + NORM_EPS)


def _norm_mod(x, g, mod_ref, shift_idx, scale_idx):
    y = _rms(x) * g
    return y * (1.0 + mod_ref[scale_idx:scale_idx + 1, :]) + mod_ref[shift_idx:shift_idx + 1, :]


def _adaln_kernel(c_ref, w_ref, b_ref, o_ref):
    c = c_ref[...]
    s = c * jax.nn.sigmoid(c)
    o_ref[...] = jnp.dot(s, w_ref[...], preferred_element_type=F32,
                         precision=lax.Precision.HIGHEST) + b_ref[...]


def _adaln(cvec, w, b, layer):
    rows, d = cvec.shape
    cols = w.shape[2]
    tn = 1024 if cols % 1024 == 0 else cols
    return pl.pallas_call(
        _adaln_kernel,
        grid=(cols // tn,),
        in_specs=[pl.BlockSpec((rows, d), lambda j: (0, 0)),
                  pl.BlockSpec((None, d, tn), lambda j: (layer, 0, j)),
                  pl.BlockSpec((None, 1, tn), lambda j: (layer, 0, j))],
        out_specs=pl.BlockSpec((rows, tn), lambda j: (0, j)),
        out_shape=jax.ShapeDtypeStruct((rows, cols), F32),
        compiler_params=_cparams(("arbitrary",)),
        name="adaln",
    )(cvec, w, b.reshape(b.shape[0], 1, cols))


def _norm_mod_kernel(x_ref, g_ref, mod_ref, h_ref, *, shift_idx, scale_idx):
    h_ref[...] = _norm_mod(x_ref[...], g_ref[...], mod_ref, shift_idx, scale_idx).astype(BF16)


def _norm_mod_call(x, g, mod, shift_idx, scale_idx):
    b, r, d = x.shape
    tm = min(r, 512)
    return pl.pallas_call(
        functools.partial(_norm_mod_kernel, shift_idx=shift_idx, scale_idx=scale_idx),
        grid=(b, r // tm),
        in_specs=[pl.BlockSpec((None, tm, d), lambda i, t: (i, t, 0)),
                  pl.BlockSpec((1, d), lambda i, t: (0, 0)),
                  pl.BlockSpec((None, 6, d), lambda i, t: (i, 0, 0))],
        out_specs=pl.BlockSpec((None, tm, d), lambda i, t: (i, t, 0)),
        out_shape=jax.ShapeDtypeStruct((b, r, d), BF16),
        compiler_params=_cparams(("parallel", "parallel")),
        name="norm_mod",
    )(x, g.reshape(1, d), mod)


QKV_TILE = 4 * LANES
_PLAIN = (None, None, 1.0)
_QKV_TILE_OPS = {
    0: [(0, 32, QS128)] * 4,
    1: [(1, 32, 1.0)] * 2 + [_PLAIN] * 2,
    2: [(None, 16, QS64)] * 4,
    3: [(None, 16, 1.0)] * 4,
    5: [(None, None, QS128)] * 4,
    8: [(None, 32, QS128)] * 4,
    9: [(None, 32, 1.0)] * 2 + [_PLAIN] * 2,
}


def _rope(y, cos, sin, hw):
    lane = lax.broadcasted_iota(jnp.int32, y.shape, 1)
    first = (lane % (2 * hw)) < hw
    partner = jnp.where(first, pltpu.roll(y, LANES - hw, 1), pltpu.roll(y, hw, 1))
    return y * cos + partner * sin


def _qkv_kernel(h_ref, w_ref, cos32_ref, sin32_ref, cos16_ref, sin16_ref, g_ref, o_ref):
    h = h_ref[...]
    for j in range(o_ref.shape[1] // QKV_TILE):
        acc = _dot(h, w_ref[:, j * QKV_TILE:(j + 1) * QKV_TILE])
        for c, (norm_row, hw, scale) in enumerate(_QKV_TILE_OPS.get(j, [_PLAIN] * 4)):
            y = acc[:, c * LANES:(c + 1) * LANES]
            if norm_row is not None:
                y = _rms(y) * g_ref[norm_row:norm_row + 1, :]
            if hw == 32:
                y = _rope(y, cos32_ref[...], sin32_ref[...], 32)
            elif hw == 16:
                y = _rope(y, cos16_ref[...], sin16_ref[...], 16)
            if scale != 1.0:
                y = y * scale
            col = j * QKV_TILE + c * LANES
            o_ref[:, col:col + LANES] = y.astype(o_ref.dtype)


def _qkv_call(h, w_qkv, tables, qk_g):
    b, r, d = h.shape
    cols = w_qkv.shape[1]
    tm = min(r, 512)
    tab_spec = pl.BlockSpec((tm, LANES), lambda i, t: (t, 0))
    return pl.pallas_call(
        _qkv_kernel,
        grid=(b, r // tm),
        in_specs=[pl.BlockSpec((None, tm, d), lambda i, t: (i, t, 0)),
                  pl.BlockSpec((d, cols), lambda i, t: (0, 0), pipeline_mode=pl.Buffered(1)),
                  tab_spec, tab_spec, tab_spec, tab_spec,
                  pl.BlockSpec((2, LANES), lambda i, t: (0, 0))],
        out_specs=pl.BlockSpec((None, tm, cols), lambda i, t: (i, t, 0)),
        out_shape=jax.ShapeDtypeStruct((b, r, cols), BF16),
        compiler_params=_cparams(("parallel", "parallel")),
        name="qkv_proj",
    )(h, w_qkv, *tables, qk_g)


def _matmul_kernel(h_ref, w_ref, o_ref):
    o_ref[...] = _dot(h_ref[...], w_ref[...].astype(BF16)).astype(o_ref.dtype)


def _gate_proj_call(h, w_in, layer):
    b, r, d = h.shape
    tn = 512
    cols = w_in.shape[2] - QKV_COLS
    col0 = QKV_COLS // tn
    tm = 2048 if r % 2048 == 0 else r
    return pl.pallas_call(
        _matmul_kernel,
        grid=(b, r // tm, cols // tn),
        in_specs=[pl.BlockSpec((None, tm, d), lambda i, t, j: (i, t, 0)),
                  pl.BlockSpec((None, d, tn), lambda i, t, j: (layer, 0, col0 + j))],
        out_specs=pl.BlockSpec((None, tm, tn), lambda i, t, j: (i, t, j)),
        out_shape=jax.ShapeDtypeStruct((b, r, cols), BF16),
        compiler_params=_cparams(("parallel", "parallel", "arbitrary")),
        name="gate_proj",
    )(h, w_in)


def _cast_kernel(w_ref, o_ref):
    o_ref[...] = w_ref[...].astype(o_ref.dtype)


def _qkv_weights_call(w_in, layer):
    d = w_in.shape[1]
    return pl.pallas_call(
        _cast_kernel,
        grid=(QKV_COLS // QKV_TILE,),
        in_specs=[pl.BlockSpec((None, d, QKV_TILE), lambda j: (layer, 0, j))],
        out_specs=pl.BlockSpec((d, QKV_TILE), lambda j: (0, j)),
        out_shape=jax.ShapeDtypeStruct((d, QKV_COLS), BF16),
        compiler_params=_cparams(("parallel",)),
        name="qkv_weights",
    )(w_in)


def _diff_lambda(lam_ref, lambda_init):
    a = jnp.sum(lam_ref[0:1, :] * lam_ref[1:2, :], axis=-1, keepdims=True)
    b = jnp.sum(lam_ref[2:3, :] * lam_ref[3:4, :], axis=-1, keepdims=True)
    return jnp.exp(a) - jnp.exp(b) + lambda_init


def _split_maps(q):
    lane = lax.broadcasted_iota(jnp.int32, q.shape, 1)
    zero = jnp.zeros_like(q)
    return jnp.concatenate([jnp.where(lane < DIFF_QK_DIM, q, zero),
                            jnp.where(lane >= DIFF_QK_DIM, q, zero)], axis=0)


FLASH_ROW_BLOCK = 256


def _flash_pair(q2, kc_ref, vc_ref, kl_ref, vl_ref, tk):
    m_rows = q2.shape[0]
    n_lat = kl_ref.shape[0]
    nb = m_rows // FLASH_ROW_BLOCK
    qs = [q2[i * FLASH_ROW_BLOCK:(i + 1) * FLASH_ROW_BLOCK] for i in range(nb)]

    def step(k, v, carry):
        v1 = jnp.concatenate([v, jnp.ones_like(v)], axis=1)
        out = []
        for q, (m, acc) in zip(qs, carry):
            s = _dot_nt(q, k)
            m_new = jnp.maximum(m, jnp.max(s, axis=-1, keepdims=True))
            p = jnp.exp2(s - m_new)
            acc = jnp.exp2(m - m_new) * acc + _dot(p.astype(BF16), v1)
            out.append((m_new, acc))
        return tuple(out)

    carry = tuple((jnp.full((FLASH_ROW_BLOCK, 1), NEG_INF, F32),
                   jnp.zeros((FLASH_ROW_BLOCK, 2 * HEAD_DIM), F32)) for _ in range(nb))
    carry = step(kc_ref[...], vc_ref[...], carry)

    for c in range(n_lat // tk):
        carry = step(kl_ref[c * tk:(c + 1) * tk, :], vl_ref[c * tk:(c + 1) * tk, :], carry)
    return jnp.concatenate([acc[:, :HEAD_DIM] / acc[:, HEAD_DIM:] for _, acc in carry], axis=0)


def _gqa_kernel(q_ref, kc_ref, vc_ref, kl_ref, vl_ref, o_ref, *, tk):
    tq = q_ref.shape[0]
    q = q_ref[...]
    q2 = jnp.concatenate([q[:, :HEAD_DIM], q[:, HEAD_DIM:]], axis=0)
    o = _flash_pair(q2, kc_ref, vc_ref, kl_ref, vl_ref, tk)
    o_ref[:, :HEAD_DIM] = o[:tq].astype(o_ref.dtype)
    o_ref[:, HEAD_DIM:] = o[tq:].astype(o_ref.dtype)


def _diff_kernel(q_ref, kc_ref, vc_ref, kl_ref, vl_ref, lam_ref, subg_ref, o_ref, *, tk, lambda_init):
    tq = q_ref.shape[0]
    o = _flash_pair(_split_maps(q_ref[...]), kc_ref, vc_ref, kl_ref, vl_ref, tk)
    lam = _diff_lambda(lam_ref, lambda_init)
    d = o[:tq] - lam * o[tq:]
    o_ref[...] = (_rms(d) * subg_ref[...] * (1.0 - lambda_init)).astype(o_ref.dtype)


def _gqa_call(p_lat, p_ctx):
    b, n, _ = p_lat.shape
    l = p_ctx.shape[1]
    tq = 512
    tk = 512
    hd = HEAD_DIM
    kcol, vcol = COL_GQA_K // hd, COL_GQA_V // hd
    return pl.pallas_call(
        functools.partial(_gqa_kernel, tk=tk),
        grid=(b, 2, n // tq),
        in_specs=[pl.BlockSpec((None, tq, 2 * hd), lambda i, k, t: (i, t, k)),
                  pl.BlockSpec((None, l, hd), lambda i, k, t: (i, 0, kcol + k)),
                  pl.BlockSpec((None, l, hd), lambda i, k, t: (i, 0, vcol + k)),
                  pl.BlockSpec((None, n, hd), lambda i, k, t: (i, 0, kcol + k)),
                  pl.BlockSpec((None, n, hd), lambda i, k, t: (i, 0, vcol + k))],
        out_specs=pl.BlockSpec((None, tq, 2 * hd), lambda i, k, t: (i, t, k)),
        out_shape=jax.ShapeDtypeStruct((b, n, BRANCH_WIDTH), BF16),
        compiler_params=_cparams(("parallel", "parallel", "arbitrary")),
        name="attn_gqa",
    )(p_lat, p_ctx, p_ctx, p_lat, p_lat)


def _diff_call(p_lat, p_ctx, lam_p, subg, lambda_init):
    b, n, _ = p_lat.shape
    l = p_ctx.shape[1]
    tq = 512
    tk = 512
    hd = HEAD_DIM
    qcol, kcol, vcol = COL_DIFF_Q // hd, COL_DIFF_K // hd, COL_DIFF_V // hd
    return pl.pallas_call(
        functools.partial(_diff_kernel, tk=tk, lambda_init=lambda_init),
        grid=(b, 4, n // tq),
        in_specs=[pl.BlockSpec((None, tq, hd), lambda i, h, t: (i, t, qcol + h)),
                  pl.BlockSpec((None, l, hd), lambda i, h, t: (i, 0, kcol + h)),
                  pl.BlockSpec((None, l, hd), lambda i, h, t: (i, 0, vcol + h)),
                  pl.BlockSpec((None, n, hd), lambda i, h, t: (i, 0, kcol + h)),
                  pl.BlockSpec((None, n, hd), lambda i, h, t: (i, 0, vcol + h)),
                  pl.BlockSpec((4, DIFF_QK_DIM), lambda i, h, t: (0, 0)),
                  pl.BlockSpec((1, hd), lambda i, h, t: (0, 0))],
        out_specs=pl.BlockSpec((None, tq, hd), lambda i, h, t: (i, t, h)),
        out_shape=jax.ShapeDtypeStruct((b, n, BRANCH_WIDTH), BF16),
        compiler_params=_cparams(("parallel", "parallel", "arbitrary")),
        name="attn_diff",
    )(p_lat, p_ctx, p_ctx, p_lat, p_lat, lam_p, subg.reshape(1, hd))


NA_GROUP_ROWS = 8
NA_WIN_ROWS = 2 * NA_GROUP_ROWS


def _na_window_start(g, rows_n):
    return jnp.clip(g * NA_GROUP_ROWS - NA_ROWS // 2, 0, rows_n - NA_WIN_ROWS)


NA_DI = 2 * NA_ROWS - 1
NA_TAB_PAD = NA_WIN_ROWS
NA_TAB_LANES = 3072


def _na_bias_tables(rpb):
    nh = rpb.shape[0]
    qc = np.arange(GRID_W)[:, None]
    kc = np.arange(GRID_W)[None, :]
    cstart = np.clip(qc - NA_COLS // 2, 0, GRID_W - NA_COLS)
    ok_col = (kc >= cstart) & (kc < cstart + NA_COLS)
    dj = np.clip(kc - qc + NA_COLS - 1, 0, 2 * NA_COLS - 2)
    sel_j = (dj[..., None] == np.arange(2 * NA_COLS - 1)) & ok_col[..., None]
    by_col = jnp.einsum("hij,qkj->hqik", rpb.astype(F32), jnp.asarray(sel_j, F32),
                        precision=lax.Precision.HIGHEST)
    by_col = jnp.where(ok_col[None, :, None, :], by_col * LOG2E, NEG_INF)
    strip = by_col.reshape(nh, GRID_W, NA_DI * GRID_W)
    total = NA_TAB_LANES + GRID_W
    strip = jnp.pad(strip, ((0, 0), (0, 0), (NA_TAB_PAD * GRID_W, total - (NA_TAB_PAD + NA_DI) * GRID_W)),
                    constant_values=NEG_INF)
    return jnp.stack([strip[:, :, :NA_TAB_LANES], strip[:, :, GRID_W:]], axis=1)


def _na_kernel(q_ref, kc_ref, vc_ref, kl_ref, vl_ref, tab_ref, o_ref, *, rows_n):
    g = pl.program_id(2)
    win = NA_WIN_ROWS * GRID_W
    ws = _na_window_start(g, rows_n)
    off = pl.multiple_of(ws * GRID_W, 256)
    q = q_ref[...]
    lane = lax.broadcasted_iota(jnp.int32, (GRID_W, win), 1)
    bias = []
    for a in range(NA_GROUP_ROWS):
        r = g * NA_GROUP_ROWS + a
        rs = jnp.clip(r - NA_ROWS // 2, 0, rows_n - NA_ROWS)
        blk = ws - r + NA_ROWS - 1 + NA_TAB_PAD
        start = pl.multiple_of((blk >> 1) * (2 * GRID_W), 2 * GRID_W)
        strip = tab_ref[blk & 1, :, pl.ds(start, win)]
        lo = (rs - ws) * GRID_W
        in_rows = jnp.logical_and(lane >= lo, lane < lo + NA_ROWS * GRID_W)
        bias.append(jnp.where(in_rows, strip, NEG_INF))
    s_b = _dot_nt(q, kl_ref[pl.ds(off, win), :]) + jnp.concatenate(bias, axis=0)
    s_c = _dot_nt(q, kc_ref[...])
    m = jnp.maximum(jnp.max(s_b, axis=-1, keepdims=True), jnp.max(s_c, axis=-1, keepdims=True))
    p_b = jnp.exp2(s_b - m)
    p_c = jnp.exp2(s_c - m)
    l = jnp.sum(p_b, axis=-1, keepdims=True) + jnp.sum(p_c, axis=-1, keepdims=True)
    o = _dot(p_c.astype(BF16), vc_ref[...]) + _dot(p_b.astype(BF16), vl_ref[pl.ds(off, win), :])
    o_ref[...] = (o / l).astype(o_ref.dtype)


def _na_call(p_lat, p_ctx, bias):
    b, n, _ = p_lat.shape
    l = p_ctx.shape[1]
    hd = HEAD_DIM
    rows_n = n // GRID_W
    n_groups = rows_n // NA_GROUP_ROWS
    tq = NA_GROUP_ROWS * GRID_W
    win = NA_WIN_ROWS * GRID_W
    qcol, kcol, vcol = COL_NA_Q // hd, COL_NA_K // hd, COL_NA_V // hd

    return pl.pallas_call(
        functools.partial(_na_kernel, rows_n=rows_n),
        grid=(b, 4, n_groups),
        in_specs=[pl.BlockSpec((None, tq, hd), lambda i, h, g: (i, g, qcol + h)),
                  pl.BlockSpec((None, l, hd), lambda i, h, g: (i, 0, kcol + h)),
                  pl.BlockSpec((None, l, hd), lambda i, h, g: (i, 0, vcol + h)),
                  pl.BlockSpec((None, n, hd), lambda i, h, g: (i, 0, kcol + h)),
                  pl.BlockSpec((None, n, hd), lambda i, h, g: (i, 0, vcol + h)),
                  pl.BlockSpec((None, 2, GRID_W, NA_TAB_LANES), lambda i, h, g: (h, 0, 0, 0))],
        out_specs=pl.BlockSpec((None, tq, hd), lambda i, h, g: (i, g, h)),
        out_shape=jax.ShapeDtypeStruct((b, n, BRANCH_WIDTH), BF16),
        compiler_params=_cparams(("parallel", "parallel", "arbitrary")),
        name="attn_na",
    )(p_lat, p_ctx, p_ctx, p_lat, p_lat, bias)


def _swa_kernel(q_ref, kc_ref, vc_ref, kl_ref, vl_ref, sink_ref, o_ref, *, win):
    t = pl.program_id(2)
    tq = q_ref.shape[0]
    n = kl_ref.shape[0]
    q0 = t * tq
    ws = pl.multiple_of(jnp.clip(q0 - SWA_WINDOW, 0, n - win), SWA_WINDOW)
    kb = kl_ref[pl.ds(ws, win), :]
    vb = vl_ref[pl.ds(ws, win), :]
    row = lax.broadcasted_iota(jnp.int32, (tq, win), 0)
    col = lax.broadcasted_iota(jnp.int32, (tq, win), 1)
    valid = jnp.abs(row + (q0 - ws) - col) <= SWA_WINDOW
    for gi in range(2):
        q = q_ref[:, gi * HEAD_DIM:(gi + 1) * HEAD_DIM]
        sink = sink_ref[gi:gi + 1, 0:1]
        s_b = jnp.where(valid, _dot_nt(q, kb), NEG_INF)
        s_c = _dot_nt(q, kc_ref[...])
        m = jnp.maximum(jnp.maximum(jnp.max(s_b, axis=-1, keepdims=True),
                                    jnp.max(s_c, axis=-1, keepdims=True)), sink)
        e_b = jnp.exp2(s_b - m)
        e_c = jnp.exp2(s_c - m)
        den = (jnp.sum(e_b, axis=-1, keepdims=True) + jnp.sum(e_c, axis=-1, keepdims=True)
               + jnp.exp2(sink - m))
        o = _dot(e_c.astype(BF16), vc_ref[...]) + _dot(e_b.astype(BF16), vb)
        o_ref[:, gi * HEAD_DIM:(gi + 1) * HEAD_DIM] = (o / den).astype(o_ref.dtype)


def _swa_call(p_lat, p_ctx, sink_tab):
    b, n, _ = p_lat.shape
    l = p_ctx.shape[1]
    hd = HEAD_DIM
    tq = 512
    win = tq + 2 * SWA_WINDOW
    qcol, kcol, vcol = COL_SWA_Q // (2 * hd), COL_SWA_K // hd, COL_SWA_V // hd
    return pl.pallas_call(
        functools.partial(_swa_kernel, win=win),
        grid=(b, 2, n // tq),
        in_specs=[pl.BlockSpec((None, tq, 2 * hd), lambda i, k, t: (i, t, qcol + k)),
                  pl.BlockSpec((None, l, hd), lambda i, k, t: (i, 0, kcol + k)),
                  pl.BlockSpec((None, l, hd), lambda i, k, t: (i, 0, vcol + k)),
                  pl.BlockSpec((None, n, hd), lambda i, k, t: (i, 0, kcol + k)),
                  pl.BlockSpec((None, n, hd), lambda i, k, t: (i, 0, vcol + k)),
                  pl.BlockSpec((None, 2, LANES), lambda i, k, t: (k, 0, 0))],
        out_specs=pl.BlockSpec((None, tq, 2 * hd), lambda i, k, t: (i, t, k)),
        out_shape=jax.ShapeDtypeStruct((b, n, BRANCH_WIDTH), BF16),
        compiler_params=_cparams(("parallel", "parallel", "arbitrary")),
        name="attn_swa",
    )(p_lat, p_ctx, p_ctx, p_lat, p_lat, sink_tab)


def _softmax_attend(q, k, v, sink=None):
    s = _dot_nt(q, k)
    m = jnp.max(s, axis=-1, keepdims=True)
    if sink is not None:
        m = jnp.maximum(m, sink)
    e = jnp.exp2(s - m)
    den = jnp.sum(e, axis=-1, keepdims=True)
    if sink is not None:
        den = den + jnp.exp2(sink - m)
    return _dot(e.astype(BF16), v) / den


def _ctx_attn_kernel(p_ref, lam_ref, subg_ref, sink_ref, o_ref, *, lambda_init):
    hd = HEAD_DIM

    def col(c0, h):
        return p_ref[:, c0 + h * hd:c0 + (h + 1) * hd]

    lam = _diff_lambda(lam_ref, lambda_init)
    for h in range(4):
        o = _softmax_attend(col(COL_GQA_Q, h), col(COL_GQA_K, h // 2), col(COL_GQA_V, h // 2))
        o_ref[:, h * hd:(h + 1) * hd] = o.astype(o_ref.dtype)
    for h in range(4):
        q = col(COL_DIFF_Q, h)
        tq = q.shape[0]
        o2 = _softmax_attend(_split_maps(q), col(COL_DIFF_K, h), col(COL_DIFF_V, h))
        d = o2[:tq] - lam * o2[tq:]
        d = _rms(d) * subg_ref[...] * (1.0 - lambda_init)
        o_ref[:, BRANCH_WIDTH + h * hd:BRANCH_WIDTH + (h + 1) * hd] = d.astype(o_ref.dtype)
    for h in range(4):
        o = _softmax_attend(col(COL_NA_Q, h), col(COL_NA_K, h), col(COL_NA_V, h))
        o_ref[:, 2 * BRANCH_WIDTH + h * hd:2 * BRANCH_WIDTH + (h + 1) * hd] = o.astype(o_ref.dtype)
    for h in range(4):
        o = _softmax_attend(col(COL_SWA_Q, h), col(COL_SWA_K, h // 2), col(COL_SWA_V, h // 2),
                            sink=sink_ref[h:h + 1, 0:1])
        o_ref[:, 3 * BRANCH_WIDTH + h * hd:3 * BRANCH_WIDTH + (h + 1) * hd] = o.astype(o_ref.dtype)


def _ctx_attn_call(p_ctx, lam_p, subg, sink_tab, lambda_init):
    b, l, cols = p_ctx.shape
    return pl.pallas_call(
        functools.partial(_ctx_attn_kernel, lambda_init=lambda_init),
        grid=(b,),
        in_specs=[pl.BlockSpec((None, l, cols), lambda i: (i, 0, 0)),
                  pl.BlockSpec((4, DIFF_QK_DIM), lambda i: (0, 0)),
                  pl.BlockSpec((1, HEAD_DIM), lambda i: (0, 0)),
                  pl.BlockSpec((4, LANES), lambda i: (0, 0))],
        out_specs=pl.BlockSpec((None, l, N_BRANCHES * BRANCH_WIDTH), lambda i: (i, 0, 0)),
        out_shape=jax.ShapeDtypeStruct((b, l, N_BRANCHES * BRANCH_WIDTH), BF16),
        compiler_params=_cparams(("parallel",)),
        name="attn_ctx",
    )(p_ctx, lam_p, subg.reshape(1, HEAD_DIM), sink_tab.reshape(4, LANES))


def _merge_kernel(o0, o1, o2, o3, g0, g1, g2, g3, wb_ref, wo_ref, x_ref, mod_ref, ng_ref, *rest,
                  with_router):
    if with_router:
        router_ref, xo_ref, h_ref, lg_ref, acc_ref = rest
    else:
        xo_ref, h_ref, acc_ref = rest
    nc = pl.program_id(2)

    @pl.when(nc == 0)
    def _():
        acc_ref[...] = jnp.zeros_like(acc_ref)

    s = None
    for i, (o_ref, g_ref) in enumerate(((o0, g0), (o1, g1), (o2, g2), (o3, g3))):
        y = _dot(o_ref[...], wb_ref[i])
        term = jax.nn.sigmoid(g_ref[...].astype(F32)) * y
        s = term if s is None else s + term
    acc_ref[...] += _dot(s.astype(BF16), wo_ref[...])

    @pl.when(nc == pl.num_programs(2) - 1)
    def _():
        xn = x_ref[...] + mod_ref[2:3, :] * acc_ref[...]
        xo_ref[...] = xn
        hn = _norm_mod(xn, ng_ref[...], mod_ref, 3, 4)
        h_ref[...] = hn.astype(h_ref.dtype)
        if with_router:
            lg_ref[...] = jnp.dot(hn, router_ref[...], preferred_element_type=F32,
                                  precision=lax.Precision.HIGHEST)


def _merge_call(o_parts, gates, w_branch, w_out, x, mod, next_g, router=None):
    b, r, d = x.shape
    nch = 4
    dc = d // nch
    tm = min(r, 512)
    with_router = router is not None

    def o_spec(cb):
        return pl.BlockSpec((None, tm, BRANCH_WIDTH), lambda i, t, c: (i, t, cb))

    def g_spec(br):
        return pl.BlockSpec((None, tm, dc), lambda i, t, c: (i, t, br * nch + c))

    in_specs = ([o_spec(cb) for _, cb in o_parts] + [g_spec(br) for br in range(N_BRANCHES)] + [
        pl.BlockSpec((N_BRANCHES, BRANCH_WIDTH, dc), lambda i, t, c: (0, 0, c)),
        pl.BlockSpec((dc, d), lambda i, t, c: (c, 0)),
        pl.BlockSpec((None, tm, d), lambda i, t, c: (i, t, 0)),
        pl.BlockSpec((None, 6, d), lambda i, t, c: (i, 0, 0)),
        pl.BlockSpec((1, d), lambda i, t, c: (0, 0))])
    args = [a for a, _ in o_parts] + [gates] * N_BRANCHES + [w_branch, w_out, x, mod, next_g.reshape(1, d)]
    row_spec = pl.BlockSpec((None, tm, d), lambda i, t, c: (i, t, 0))
    out_specs = [row_spec, row_spec]
    out_shape = [jax.ShapeDtypeStruct((b, r, d), F32), jax.ShapeDtypeStruct((b, r, d), BF16)]
    if with_router:
        in_specs.append(pl.BlockSpec((d, LANES), lambda i, t, c: (0, 0)))
        args.append(router)
        out_specs.append(pl.BlockSpec((None, tm, LANES), lambda i, t, c: (i, t, 0)))
        out_shape.append(jax.ShapeDtypeStruct((b, r, LANES), F32))
    return pl.pallas_call(
        functools.partial(_merge_kernel, with_router=with_router),
        grid=(b, r // tm, nch),
        in_specs=in_specs,
        out_specs=out_specs,
        out_shape=out_shape,
        scratch_shapes=[pltpu.VMEM((tm, d), F32)],
        compiler_params=_cparams(("parallel", "parallel", "arbitrary")),
        name="merge",
    )(*args)


def _swiglu_partial(h, wg, wu, wd):
    gate = _dot(h, wg)
    up = _dot(h, wu)
    act = gate * jax.nn.sigmoid(gate) * up
    return _dot(act.astype(BF16), wd)


def _ffn_kernel(h_ref, wg_ref, wu_ref, wd_ref, x_ref, mod_ref, ng_ref, nmod_ref, xo_ref, hn_ref,
                acc_ref):
    f = pl.program_id(2)

    @pl.when(f == 0)
    def _():
        acc_ref[...] = jnp.zeros_like(acc_ref)

    acc_ref[...] += _swiglu_partial(h_ref[...], wg_ref[...], wu_ref[...], wd_ref[...])

    @pl.when(f == pl.num_programs(2) - 1)
    def _():
        xn = x_ref[...] + mod_ref[5:6, :] * acc_ref[...]
        xo_ref[...] = xn
        hn_ref[...] = _norm_mod(xn, ng_ref[...], nmod_ref, 0, 1).astype(hn_ref.dtype)


def _ffn_call(h, wg, wu, wd, x, mod, next_g, next_mod):
    b, r, d = x.shape
    dff = wg.shape[1]
    tf = 512
    tm = min(r, 512)
    row_spec = pl.BlockSpec((None, tm, d), lambda i, t, f: (i, t, 0))
    mod_spec = pl.BlockSpec((None, 6, d), lambda i, t, f: (i, 0, 0))
    return pl.pallas_call(
        _ffn_kernel,
        grid=(b, r // tm, dff // tf),
        in_specs=[row_spec,
                  pl.BlockSpec((d, tf), lambda i, t, f: (0, f)),
                  pl.BlockSpec((d, tf), lambda i, t, f: (0, f)),
                  pl.BlockSpec((tf, d), lambda i, t, f: (f, 0)),
                  row_spec, mod_spec,
                  pl.BlockSpec((1, d), lambda i, t, f: (0, 0)),
                  mod_spec],
        out_specs=[row_spec, row_spec],
        out_shape=[jax.ShapeDtypeStruct((b, r, d), F32), jax.ShapeDtypeStruct((b, r, d), BF16)],
        scratch_shapes=[pltpu.VMEM((tm, d), F32)],
        compiler_params=_cparams(("parallel", "parallel", "arbitrary")),
        name="ffn_dense",
    )(h, wg, wu, wd, x, mod, next_g.reshape(1, d), next_mod)


def _route_kernel(lg_ref, idx_ref, w_ref):
    lg = lg_ref[...]
    lane = lax.broadcasted_iota(jnp.int32, lg.shape, 1)
    valid = lane < N_EXPERTS
    mx = jnp.max(jnp.where(valid, lg, -jnp.inf), axis=-1, keepdims=True)
    e = jnp.where(valid, jnp.exp(lg - mx), 0.0)
    p = e / jnp.sum(e, axis=-1, keepdims=True)
    p1 = jnp.max(p, axis=-1, keepdims=True)
    i1 = jnp.min(jnp.where(p == p1, lane, LANES), axis=-1, keepdims=True)
    rest = jnp.where(jnp.logical_or(lane == i1, jnp.logical_not(valid)), -1.0, p)
    p2 = jnp.max(rest, axis=-1, keepdims=True)
    i2 = jnp.min(jnp.where(rest == p2, lane, LANES), axis=-1, keepdims=True)
    tot = p1 + p2
    idx_ref[...] = jnp.where(lane == 0, i1, jnp.where(lane == 1, i2, 0))
    w_ref[...] = jnp.where(lane == 0, p1 / tot, jnp.where(lane == 1, p2 / tot, 0.0))


def _route_call(logits):
    r = logits.shape[0]
    tm = min(r, 1024)
    spec = pl.BlockSpec((tm, LANES), lambda i: (i, 0))
    return pl.pallas_call(
        _route_kernel,
        grid=(r // tm,),
        in_specs=[spec],
        out_specs=[spec, spec],
        out_shape=[jax.ShapeDtypeStruct((r, LANES), jnp.int32), jax.ShapeDtypeStruct((r, LANES), F32)],
        compiler_params=_cparams(("parallel",)),
        name="route_top2",
    )(logits)


MOE_HALF = 512
MOE_TILE = 2 * MOE_HALF
MOE_VMEM_LIMIT = 60 * 1024 * 1024


def _moe_kernel(te_ref, nr_ref, x_ref, wg_ref, wu_ref, wd_ref, o_ref, acc_ref):
    i = pl.program_id(0)
    f = pl.program_id(1)

    @pl.when(f == 0)
    def _():
        acc_ref[...] = jnp.zeros_like(acc_ref)

    for half in range(MOE_TILE // MOE_HALF):
        rows = slice(half * MOE_HALF, (half + 1) * MOE_HALF)

        @pl.when(nr_ref[i] > half * MOE_HALF)
        def _():
            acc_ref[rows, :] += _swiglu_partial(x_ref[rows, :], wg_ref[...].astype(BF16),
                                                wu_ref[...].astype(BF16), wd_ref[...].astype(BF16))

    @pl.when(f == pl.num_programs(1) - 1)
    def _():
        o_ref[...] = acc_ref[...].astype(o_ref.dtype)


def _moe_call(xs, wg, wu, wd, tile_expert, tile_rows):
    r, d = xs.shape
    dff = wg.shape[2]
    tf = 512
    nf = dff // tf

    def fidx(i, f, nr):
        return jnp.where(nr[i] > 0, f, nf - 1)

    grid_spec = pltpu.PrefetchScalarGridSpec(
        num_scalar_prefetch=2,
        grid=(r // MOE_TILE, nf),
        in_specs=[pl.BlockSpec((MOE_TILE, d), lambda i, f, te, nr: (i, 0), pipeline_mode=pl.Buffered(1)),
                  pl.BlockSpec((None, d, tf), lambda i, f, te, nr: (te[i], 0, fidx(i, f, nr))),
                  pl.BlockSpec((None, d, tf), lambda i, f, te, nr: (te[i], 0, fidx(i, f, nr))),
                  pl.BlockSpec((None, tf, d), lambda i, f, te, nr: (te[i], fidx(i, f, nr), 0))],
        out_specs=pl.BlockSpec((MOE_TILE, d), lambda i, f, te, nr: (i, 0), pipeline_mode=pl.Buffered(1)),
        scratch_shapes=[pltpu.VMEM((MOE_TILE, d), F32)])
    return pl.pallas_call(
        _moe_kernel,
        grid_spec=grid_spec,
        out_shape=jax.ShapeDtypeStruct((r, d), BF16),
        compiler_params=pltpu.CompilerParams(dimension_semantics=("arbitrary", "arbitrary"),
                                             vmem_limit_bytes=MOE_VMEM_LIMIT),
        name="moe_experts",
    )(tile_expert, tile_rows, xs, wg, wu, wd)


def _moe_plan(idx):
    t = idx.shape[0]
    a = t * TOP_K
    flat_e = idx.reshape(a)
    onehot = (flat_e[:, None] == jnp.arange(N_EXPERTS, dtype=jnp.int32)[None, :]).astype(jnp.int32)
    csum = jnp.cumsum(onehot, axis=0)
    rank = jnp.sum((csum - onehot) * onehot, axis=1)
    counts = csum[-1]
    padded = ((counts + MOE_TILE - 1) // MOE_TILE) * MOE_TILE
    ends = jnp.cumsum(padded)
    offs = ends - padded
    dest = offs[flat_e] + rank
    rows = a + N_EXPERTS * MOE_TILE
    src_tok = (jnp.arange(rows, dtype=jnp.int32) % t).at[dest].set(jnp.arange(a, dtype=jnp.int32) // TOP_K)
    tile_start = jnp.arange(rows // MOE_TILE, dtype=jnp.int32) * MOE_TILE
    past = jnp.sum((tile_start[:, None] >= ends[None, :]).astype(jnp.int32), axis=1)
    tile_expert = jnp.minimum(past, N_EXPERTS - 1)
    tile_rows = jnp.where(past < N_EXPERTS,
                          jnp.clip((offs + counts)[tile_expert] - tile_start, 0, MOE_TILE), 0)
    return dest.reshape(t, TOP_K), src_tok, tile_expert, tile_rows.astype(jnp.int32)


def _final_kernel(x_ref, y0_ref, y1_ref, w_ref, mod_ref, g_ref, o_ref):
    y = w_ref[:, 0:1] * y0_ref[...].astype(F32) + w_ref[:, 1:2] * y1_ref[...].astype(F32)
    xn = x_ref[...] + mod_ref[5:6, :] * y
    o_ref[...] = _rms(xn) * g_ref[...]


def _final_call(x, y0, y1, w, mod, g):
    b, r, d = x.shape
    tm = min(r, 512)
    row_spec = pl.BlockSpec((None, tm, d), lambda i, t: (i, t, 0))
    return pl.pallas_call(
        _final_kernel,
        grid=(b, r // tm),
        in_specs=[row_spec, row_spec, row_spec,
                  pl.BlockSpec((None, tm, LANES), lambda i, t: (i, t, 0)),
                  pl.BlockSpec((None, 6, d), lambda i, t: (i, 0, 0)),
                  pl.BlockSpec((1, d), lambda i, t: (0, 0))],
        out_specs=row_spec,
        out_shape=jax.ShapeDtypeStruct((b, r, d), F32),
        compiler_params=_cparams(("parallel", "parallel")),
        name="final_norm",
    )(x, y0, y1, w, mod, g.reshape(1, d))


def _rope_tables(n):
    pos = jnp.arange(n)
    rows = (pos // GRID_W).astype(F32)
    cols = (pos % GRID_W).astype(F32)
    lane = np.arange(LANES)
    out = []
    for hw in (32, 16):
        period = 4 * hw
        u = lane % period
        use_cols = (u // (2 * hw)) == 1
        w = u % (2 * hw)
        freqs = jnp.asarray(ROPE_THETA ** (-(w % hw).astype(np.float32) / hw), F32)
        p = jnp.where(jnp.asarray(use_cols)[None, :], cols[:, None], rows[:, None])
        ang = p * freqs[None, :]
        sign = jnp.asarray(np.where(w < hw, -1.0, 1.0), F32)
        out += [jnp.cos(ang), jnp.sin(ang) * sign[None, :]]
    return out


def kernel(x, c, ctx, c_ctx, attn_norm_g, ffn_norm_g, ada_w, ada_b, w_in, qk_norm_g, diff_lambda,
           diff_subln_g, na_rpb, swa_sink, w_branch, w_out, ffn_w_gate, ffn_w_up, ffn_w_down,
           moe_router, moe_w_gate, moe_w_up, moe_w_down, final_norm_g):
    b, n, d = x.shape
    l = ctx.shape[1]
    depth = w_in.shape[0]
    assert depth == 2, "laid out for one dense layer followed by one routed last layer"
    rows_n = n // GRID_W

    lat_tables = _rope_tables(n)
    ones = jnp.ones((l, LANES), F32)
    zeros = jnp.zeros((l, LANES), F32)
    ctx_tables = [ones, zeros, ones, zeros]

    cvec = jnp.zeros((8, d), F32).at[:b].set(c).at[b].set(c_ctx)
    mods = []
    for i in range(depth):
        m = _adaln(cvec, ada_w, ada_b, i).reshape(8, 6, d)
        mods.append((m[:b], jnp.broadcast_to(m[b:b + 1], (b, 6, d))))

    def mixers(i, h_lat, h_ctx):
        lambda_init = 0.8 - 0.6 * math.exp(-0.3 * i)
        w_qkv = _qkv_weights_call(w_in, i)
        sink_tab = jnp.broadcast_to((swa_sink[i].astype(F32) * LOG2E)[:, None], (4, LANES))
        p_lat = _qkv_call(h_lat, w_qkv, lat_tables, qk_norm_g[i])
        p_ctx = _qkv_call(h_ctx, w_qkv, ctx_tables, qk_norm_g[i])
        o_lat = [_gqa_call(p_lat, p_ctx),
                 _diff_call(p_lat, p_ctx, diff_lambda[i], diff_subln_g[i], lambda_init),
                 _na_call(p_lat, p_ctx, _na_bias_tables(na_rpb[i])),
                 _swa_call(p_lat, p_ctx, sink_tab.reshape(2, 2, LANES))]
        return p_ctx, o_lat, sink_tab, lambda_init

    mod_lat, mod_ctx = mods[0]
    h_lat = _norm_mod_call(x, attn_norm_g[0], mod_lat, 0, 1)
    h_ctx = _norm_mod_call(ctx, attn_norm_g[0], mod_ctx, 0, 1)
    p_ctx, o_lat, sink_tab, lambda_init = mixers(0, h_lat, h_ctx)
    wb = w_branch[0].astype(BF16)
    wo = w_out[0].astype(BF16)
    x_lat, h_lat = _merge_call([(o, 0) for o in o_lat], _gate_proj_call(h_lat, w_in, 0), wb, wo, x,
                               mod_lat, ffn_norm_g[0])
    o_ctx = _ctx_attn_call(p_ctx, diff_lambda[0], diff_subln_g[0], sink_tab, lambda_init)
    x_ctx, h_ctx = _merge_call([(o_ctx, k) for k in range(N_BRANCHES)], _gate_proj_call(h_ctx, w_in, 0),
                               wb, wo, ctx, mod_ctx, ffn_norm_g[0])
    wg = ffn_w_gate[0].astype(BF16)
    wu = ffn_w_up[0].astype(BF16)
    wd = ffn_w_down[0].astype(BF16)
    _, h_ctx = _ffn_call(h_ctx, wg, wu, wd, x_ctx, mod_ctx, attn_norm_g[1], mods[1][1])
    x_lat, h_lat = _ffn_call(h_lat, wg, wu, wd, x_lat, mod_lat, attn_norm_g[1], mods[1][0])

    mod_lat, _ = mods[1]
    _, o_lat, _, _ = mixers(1, h_lat, h_ctx)
    router = jnp.zeros((d, LANES), F32).at[:, :N_EXPERTS].set(moe_router[0])
    x_lat, h_lat, logits = _merge_call(
        [(o, 0) for o in o_lat], _gate_proj_call(h_lat, w_in, 1),
        w_branch[1].astype(BF16), w_out[1].astype(BF16), x_lat, mod_lat, ffn_norm_g[1], router)

    idx_pad, wts_pad = _route_call(logits.reshape(b * n, LANES))
    dest, src_tok, tile_expert, tile_rows = _moe_plan(idx_pad[:, :TOP_K])
    xs = h_lat.reshape(b * n, d).at[src_tok].get(mode="promise_in_bounds")
    ys = _moe_call(xs, moe_w_gate[0], moe_w_up[0], moe_w_down[0], tile_expert, tile_rows)
    y0 = ys.at[dest[:, 0]].get(mode="promise_in_bounds").reshape(b, n, d)
    y1 = ys.at[dest[:, 1]].get(mode="promise_in_bounds").reshape(b, n, d)
    return _final_call(x_lat, y0, y1, wts_pad.reshape(b, n, LANES), mod_lat, final_norm_g)
```

```python
import functools
import math

import numpy as np
import jax
import jax.numpy as jnp
from jax import lax
from jax.experimental import pallas as pl
from jax.experimental.pallas import tpu as pltpu

F32 = jnp.float32
BF16 = jnp.bfloat16

GRID_W = 64
HEAD_DIM = 128
N_BRANCHES = 4
BRANCH_WIDTH = 4 * HEAD_DIM
DIFF_QK_DIM = 64
NA_ROWS = 8
NA_COLS = 16
SWA_WINDOW = 128
N_EXPERTS = 8
TOP_K = 2
NORM_EPS = 1e-6
ROPE_THETA = 10000.0
NEG_INF = -1e30
LOG2E = math.log2(math.e)

QKV_COLS = 5120
COL_GQA_Q, COL_GQA_K, COL_GQA_V = 0, 512, 768
COL_DIFF_Q, COL_DIFF_K, COL_DIFF_V = 1024, 1536, 2048
COL_NA_Q, COL_NA_K, COL_NA_V = 2560, 3072, 3584
COL_SWA_Q, COL_SWA_K, COL_SWA_V = 4096, 4608, 4864

LANES = 128
VMEM_LIMIT = 56 * 1024 * 1024

QS128 = HEAD_DIM ** -0.5 * LOG2E
QS64 = DIFF_QK_DIM ** -0.5 * LOG2E


def _cparams(sem):
    return pltpu.CompilerParams(dimension_semantics=sem, vmem_limit_bytes=VMEM_LIMIT)


def _dot(a, b):
    return jnp.dot(a, b, preferred_element_type=F32)


def _dot_nt(a, b):
    return lax.dot_general(a, b, (((1,), (1,)), ((), ())), preferred_element_type=F32)


def _rms(x):
    return x * lax.rsqrt(jnp.mean(x * x, axis=-1, keepdims=True) + NORM_EPS)


def _norm_mod(x, g, mod_ref, shift_idx, scale_idx):
    y = _rms(x) * g
    return y * (1.0 + mod_ref[scale_idx:scale_idx + 1, :]) + mod_ref[shift_idx:shift_idx + 1, :]


def _adaln_kernel(c_ref, w_ref, b_ref, o_ref):
    c = c_ref[...]
    s = c * jax.nn.sigmoid(c)
    o_ref[...] = jnp.dot(s, w_ref[...], preferred_element_type=F32,
                         precision=lax.Precision.HIGHEST) + b_ref[...]


def _adaln(cvec, w, b, layer):
    rows, d = cvec.shape
    cols = w.shape[2]
    tn = 1024 if cols % 1024 == 0 else cols
    return pl.pallas_call(
        _adaln_kernel,
        grid=(cols // tn,),
        in_specs=[pl.BlockSpec((rows, d), lambda j: (0, 0)),
                  pl.BlockSpec((None, d, tn), lambda j: (layer, 0, j)),
                  pl.BlockSpec((None, 1, tn), lambda j: (layer, 0, j))],
        out_specs=pl.BlockSpec((rows, tn), lambda j: (0, j)),
        out_shape=jax.ShapeDtypeStruct((rows, cols), F32),
        compiler_params=_cparams(("arbitrary",)),
        name="adaln",
    )(cvec, w, b.reshape(b.shape[0], 1, cols))


def _norm_mod_kernel(x_ref, g_ref, mod_ref, h_ref, *, shift_idx, scale_idx):
    h_ref[...] = _norm_mod(x_ref[...], g_ref[...], mod_ref, shift_idx, scale_idx).astype(BF16)


def _norm_mod_call(x, g, mod, shift_idx, scale_idx):
    b, r, d = x.shape
    tm = min(r, 512)
    return pl.pallas_call(
        functools.partial(_norm_mod_kernel, shift_idx=shift_idx, scale_idx=scale_idx),
        grid=(b, r // tm),
        in_specs=[pl.BlockSpec((None, tm, d), lambda i, t: (i, t, 0)),
                  pl.BlockSpec((1, d), lambda i, t: (0, 0)),
                  pl.BlockSpec((None, 6, d), lambda i, t: (i, 0, 0))],
        out_specs=pl.BlockSpec((None, tm, d), lambda i, t: (i, t, 0)),
        out_shape=jax.ShapeDtypeStruct((b, r, d), BF16),
        compiler_params=_cparams(("parallel", "parallel")),
        name="norm_mod",
    )(x, g.reshape(1, d), mod)


QKV_TILE = 4 * LANES
_PLAIN = (None, None, 1.0)
_QKV_TILE_OPS = {
    0: [(0, 32, QS128)] * 4,
    1: [(1, 32, 1.0)] * 2 + [_PLAIN] * 2,
    2: [(None, 16, QS64)] * 4,
    3: [(None, 16, 1.0)] * 4,
    5: [(None, None, QS128)] * 4,
    8: [(None, 32, QS128)] * 4,
    9: [(None, 32, 1.0)] * 2 + [_PLAIN] * 2,
}


def _rope(y, cos, sin, hw):
    lane = lax.broadcasted_iota(jnp.int32, y.shape, 1)
    first = (lane % (2 * hw)) < hw
    partner = jnp.where(first, pltpu.roll(y, LANES - hw, 1), pltpu.roll(y, hw, 1))
    return y * cos + partner * sin


def _qkv_kernel(h_ref, w_ref, cos32_ref, sin32_ref, cos16_ref, sin16_ref, g_ref, o_ref):
    h = h_ref[...]
    for j in range(o_ref.shape[1] // QKV_TILE):
        acc = _dot(h, w_ref[:, j * QKV_TILE:(j + 1) * QKV_TILE])
        for c, (norm_row, hw, scale) in enumerate(_QKV_TILE_OPS.get(j, [_PLAIN] * 4)):
            y = acc[:, c * LANES:(c + 1) * LANES]
            if norm_row is not None:
                y = _rms(y) * g_ref[norm_row:norm_row + 1, :]
            if hw == 32:
                y = _rope(y, cos32_ref[...], sin32_ref[...], 32)
            elif hw == 16:
                y = _rope(y, cos16_ref[...], sin16_ref[...], 16)
            if scale != 1.0:
                y = y * scale
            col = j * QKV_TILE + c * LANES
            o_ref[:, col:col + LANES] = y.astype(o_ref.dtype)


def _qkv_call(h, w_qkv, tables, qk_g):
    b, r, d = h.shape
    cols = w_qkv.shape[1]
    tm = min(r, 512)
    tab_spec = pl.BlockSpec((tm, LANES), lambda i, t: (t, 0))
    return pl.pallas_call(
        _qkv_kernel,
        grid=(b, r // tm),
        in_specs=[pl.BlockSpec((None, tm, d), lambda i, t: (i, t, 0)),
                  pl.BlockSpec((d, cols), lambda i, t: (0, 0), pipeline_mode=pl.Buffered(1)),
                  tab_spec, tab_spec, tab_spec, tab_spec,
                  pl.BlockSpec((2, LANES), lambda i, t: (0, 0))],
        out_specs=pl.BlockSpec((None, tm, cols), lambda i, t: (i, t, 0)),
        out_shape=jax.ShapeDtypeStruct((b, r, cols), BF16),
        compiler_params=_cparams(("parallel", "parallel")),
        name="qkv_proj",
    )(h, w_qkv, *tables, qk_g)


def _matmul_kernel(h_ref, w_ref, o_ref):
    o_ref[...] = _dot(h_ref[...], w_ref[...].astype(BF16)).astype(o_ref.dtype)


def _gate_proj_call(h, w_in, layer):
    b, r, d = h.shape
    tn = 512
    cols = w_in.shape[2] - QKV_COLS
    col0 = QKV_COLS // tn
    tm = 2048 if r % 2048 == 0 else r
    return pl.pallas_call(
        _matmul_kernel,
        grid=(b, r // tm, cols // tn),
        in_specs=[pl.BlockSpec((None, tm, d), lambda i, t, j: (i, t, 0)),
                  pl.BlockSpec((None, d, tn), lambda i, t, j: (layer, 0, col0 + j))],
        out_specs=pl.BlockSpec((None, tm, tn), lambda i, t, j: (i, t, j)),
        out_shape=jax.ShapeDtypeStruct((b, r, cols), BF16),
        compiler_params=_cparams(("parallel", "parallel", "arbitrary")),
        name="gate_proj",
    )(h, w_in)


def _cast_kernel(w_ref, o_ref):
    o_ref[...] = w_ref[...].astype(o_ref.dtype)


def _qkv_weights_call(w_in, layer):
    d = w_in.shape[1]
    return pl.pallas_call(
        _cast_kernel,
        grid=(QKV_COLS // QKV_TILE,),
        in_specs=[pl.BlockSpec((None, d, QKV_TILE), lambda j: (layer, 0, j))],
        out_specs=pl.BlockSpec((d, QKV_TILE), lambda j: (0, j)),
        out_shape=jax.ShapeDtypeStruct((d, QKV_COLS), BF16),
        compiler_params=_cparams(("parallel",)),
        name="qkv_weights",
    )(w_in)


def _diff_lambda(lam_ref, lambda_init):
    a = jnp.sum(lam_ref[0:1, :] * lam_ref[1:2, :], axis=-1, keepdims=True)
    b = jnp.sum(lam_ref[2:3, :] * lam_ref[3:4, :], axis=-1, keepdims=True)
    return jnp.exp(a) - jnp.exp(b) + lambda_init


def _split_maps(q):
    lane = lax.broadcasted_iota(jnp.int32, q.shape, 1)
    zero = jnp.zeros_like(q)
    return jnp.concatenate([jnp.where(lane < DIFF_QK_DIM, q, zero),
                            jnp.where(lane >= DIFF_QK_DIM, q, zero)], axis=0)


FLASH_ROW_BLOCK = 256


def _flash_pair(q2, kc_ref, vc_ref, kl_ref, vl_ref, tk):
    m_rows = q2.shape[0]
    n_lat = kl_ref.shape[0]
    nb = m_rows // FLASH_ROW_BLOCK
    qs = [q2[i * FLASH_ROW_BLOCK:(i + 1) * FLASH_ROW_BLOCK] for i in range(nb)]

    def step(k, v, carry):
        v1 = jnp.concatenate([v, jnp.ones_like(v)], axis=1)
        out = []
        for q, (m, acc) in zip(qs, carry):
            s = _dot_nt(q, k)
            m_new = jnp.maximum(m, jnp.max(s, axis=-1, keepdims=True))
            p = jnp.exp2(s - m_new)
            acc = jnp.exp2(m - m_new) * acc + _dot(p.astype(BF16), v1)
            out.append((m_new, acc))
        return tuple(out)

    carry = tuple((jnp.full((FLASH_ROW_BLOCK, 1), NEG_INF, F32),
                   jnp.zeros((FLASH_ROW_BLOCK, 2 * HEAD_DIM), F32)) for _ in range(nb))
    carry = step(kc_ref[...], vc_ref[...], carry)

    for c in range(n_lat // tk):
        carry = step(kl_ref[c * tk:(c + 1) * tk, :], vl_ref[c * tk:(c + 1) * tk, :], carry)
    return jnp.concatenate([acc[:, :HEAD_DIM] / acc[:, HEAD_DIM:] for _, acc in carry], axis=0)


def _gqa_kernel(q_ref, kc_ref, vc_ref, kl_ref, vl_ref, o_ref, *, tk):
    tq = q_ref.shape[0]
    q = q_ref[...]
    q2 = jnp.concatenate([q[:, :HEAD_DIM], q[:, HEAD_DIM:]], axis=0)
    o = _flash_pair(q2, kc_ref, vc_ref, kl_ref, vl_ref, tk)
    o_ref[:, :HEAD_DIM] = o[:tq].astype(o_ref.dtype)
    o_ref[:, HEAD_DIM:] = o[tq:].astype(o_ref.dtype)


def _diff_kernel(q_ref, kc_ref, vc_ref, kl_ref, vl_ref, lam_ref, subg_ref, o_ref, *, tk, lambda_init):
    tq = q_ref.shape[0]
    o = _flash_pair(_split_maps(q_ref[...]), kc_ref, vc_ref, kl_ref, vl_ref, tk)
    lam = _diff_lambda(lam_ref, lambda_init)
    d = o[:tq] - lam * o[tq:]
    o_ref[...] = (_rms(d) * subg_ref[...] * (1.0 - lambda_init)).astype(o_ref.dtype)


def _gqa_call(p_lat, p_ctx):
    b, n, _ = p_lat.shape
    l = p_ctx.shape[1]
    tq = 512
    tk = 512
    hd = HEAD_DIM
    kcol, vcol = COL_GQA_K // hd, COL_GQA_V // hd
    return pl.pallas_call(
        functools.partial(_gqa_kernel, tk=tk),
        grid=(b, 2, n // tq),
        in_specs=[pl.BlockSpec((None, tq, 2 * hd), lambda i, k, t: (i, t, k)),
                  pl.BlockSpec((None, l, hd), lambda i, k, t: (i, 0, kcol + k)),
                  pl.BlockSpec((None, l, hd), lambda i, k, t: (i, 0, vcol + k)),
                  pl.BlockSpec((None, n, hd), lambda i, k, t: (i, 0, kcol + k)),
                  pl.BlockSpec((None, n, hd), lambda i, k, t: (i, 0, vcol + k))],
        out_specs=pl.BlockSpec((None, tq, 2 * hd), lambda i, k, t: (i, t, k)),
        out_shape=jax.ShapeDtypeStruct((b, n, BRANCH_WIDTH), BF16),
        compiler_params=_cparams(("parallel", "parallel", "arbitrary")),
        name="attn_gqa",
    )(p_lat, p_ctx, p_ctx, p_lat, p_lat)


def _diff_call(p_lat, p_ctx, lam_p, subg, lambda_init):
    b, n, _ = p_lat.shape
    l = p_ctx.shape[1]
    tq = 512
    tk = 512
    hd = HEAD_DIM
    qcol, kcol, vcol = COL_DIFF_Q // hd, COL_DIFF_K // hd, COL_DIFF_V // hd
    return pl.pallas_call(
        functools.partial(_diff_kernel, tk=tk, lambda_init=lambda_init),
        grid=(b, 4, n // tq),
        in_specs=[pl.BlockSpec((None, tq, hd), lambda i, h, t: (i, t, qcol + h)),
                  pl.BlockSpec((None, l, hd), lambda i, h, t: (i, 0, kcol + h)),
                  pl.BlockSpec((None, l, hd), lambda i, h, t: (i, 0, vcol + h)),
                  pl.BlockSpec((None, n, hd), lambda i, h, t: (i, 0, kcol + h)),
                  pl.BlockSpec((None, n, hd), lambda i, h, t: (i, 0, vcol + h)),
                  pl.BlockSpec((4, DIFF_QK_DIM), lambda i, h, t: (0, 0)),
                  pl.BlockSpec((1, hd), lambda i, h, t: (0, 0))],
        out_specs=pl.BlockSpec((None, tq, hd), lambda i, h, t: (i, t, h)),
        out_shape=jax.ShapeDtypeStruct((b, n, BRANCH_WIDTH), BF16),
        compiler_params=_cparams(("parallel", "parallel", "arbitrary")),
        name="attn_diff",
    )(p_lat, p_ctx, p_ctx, p_lat, p_lat, lam_p, subg.reshape(1, hd))


NA_GROUP_ROWS = 8
NA_BLOCK_ROWS = 4
NA_BLOCK_WIN_ROWS = NA_BLOCK_ROWS + NA_ROWS
NA_WIN_ROWS = 2 * NA_GROUP_ROWS


def _with_ones(v):
    return jnp.concatenate([v, jnp.ones_like(v)], axis=1)


def _band_ctx_attend(s_b, s_c, vb1, vc1, sink=None):
    m = jnp.maximum(jnp.max(s_b, axis=-1, keepdims=True), jnp.max(s_c, axis=-1, keepdims=True))
    if sink is not None:
        m = jnp.maximum(m, sink)
    acc = _dot(jnp.exp2(s_c - m).astype(BF16), vc1) + _dot(jnp.exp2(s_b - m).astype(BF16), vb1)
    den = acc[:, HEAD_DIM:]
    if sink is not None:
        den = den + jnp.exp2(sink - m)
    return acc[:, :HEAD_DIM] / den


NA_DI = 2 * NA_ROWS - 1
NA_TAB_PAD = NA_WIN_ROWS
NA_TAB_LANES = 3072


def _na_bias_tables(rpb):
    nh = rpb.shape[0]
    qc = np.arange(GRID_W)[:, None]
    kc = np.arange(GRID_W)[None, :]
    cstart = np.clip(qc - NA_COLS // 2, 0, GRID_W - NA_COLS)
    ok_col = (kc >= cstart) & (kc < cstart + NA_COLS)
    dj = np.clip(kc - qc + NA_COLS - 1, 0, 2 * NA_COLS - 2)
    sel_j = (dj[..., None] == np.arange(2 * NA_COLS - 1)) & ok_col[..., None]
    by_col = jnp.einsum("hij,qkj->hqik", rpb.astype(F32), jnp.asarray(sel_j, F32),
                        precision=lax.Precision.HIGHEST)
    by_col = jnp.where(ok_col[None, :, None, :], by_col * LOG2E, NEG_INF)
    strip = by_col.reshape(nh, GRID_W, NA_DI * GRID_W)
    total = NA_TAB_LANES + GRID_W
    strip = jnp.pad(strip, ((0, 0), (0, 0), (NA_TAB_PAD * GRID_W, total - (NA_TAB_PAD + NA_DI) * GRID_W)),
                    constant_values=NEG_INF)
    return jnp.stack([strip[:, :, :NA_TAB_LANES], strip[:, :, GRID_W:]], axis=1)


def _na_kernel(q_ref, kc_ref, vc_ref, kl_ref, vl_ref, tab_ref, o_ref, *, rows_n):
    g = pl.program_id(2)
    win = NA_BLOCK_WIN_ROWS * GRID_W
    blk_q = NA_BLOCK_ROWS * GRID_W
    kc = kc_ref[...]
    vc1 = _with_ones(vc_ref[...])
    lane = lax.broadcasted_iota(jnp.int32, (GRID_W, win), 1)
    for rb in range(NA_GROUP_ROWS // NA_BLOCK_ROWS):
        r0 = g * NA_GROUP_ROWS + rb * NA_BLOCK_ROWS
        ws = jnp.clip(r0 - NA_ROWS // 2, 0, rows_n - NA_BLOCK_WIN_ROWS)
        off = pl.multiple_of(ws * GRID_W, NA_BLOCK_ROWS * GRID_W)
        bias = []
        for a in range(NA_BLOCK_ROWS):
            r = r0 + a
            rs = jnp.clip(r - NA_ROWS // 2, 0, rows_n - NA_ROWS)
            blk = ws - r + NA_ROWS - 1 + NA_TAB_PAD
            start = pl.multiple_of((blk >> 1) * (2 * GRID_W), 2 * GRID_W)
            strip = tab_ref[blk & 1, :, pl.ds(start, win)]
            lo = (rs - ws) * GRID_W
            in_rows = jnp.logical_and(lane >= lo, lane < lo + NA_ROWS * GRID_W)
            bias.append(jnp.where(in_rows, strip, NEG_INF))
        q = q_ref[rb * blk_q:(rb + 1) * blk_q, :]
        s_b = _dot_nt(q, kl_ref[pl.ds(off, win), :]) + jnp.concatenate(bias, axis=0)
        o = _band_ctx_attend(s_b, _dot_nt(q, kc), _with_ones(vl_ref[pl.ds(off, win), :]), vc1)
        o_ref[rb * blk_q:(rb + 1) * blk_q, :] = o.astype(o_ref.dtype)


def _na_call(p_lat, p_ctx, bias):
    b, n, _ = p_lat.shape
    l = p_ctx.shape[1]
    hd = HEAD_DIM
    rows_n = n // GRID_W
    n_groups = rows_n // NA_GROUP_ROWS
    tq = NA_GROUP_ROWS * GRID_W
    qcol, kcol, vcol = COL_NA_Q // hd, COL_NA_K // hd, COL_NA_V // hd

    return pl.pallas_call(
        functools.partial(_na_kernel, rows_n=rows_n),
        grid=(b, 4, n_groups),
        in_specs=[pl.BlockSpec((None, tq, hd), lambda i, h, g: (i, g, qcol + h)),
                  pl.BlockSpec((None, l, hd), lambda i, h, g: (i, 0, kcol + h)),
                  pl.BlockSpec((None, l, hd), lambda i, h, g: (i, 0, vcol + h)),
                  pl.BlockSpec((None, n, hd), lambda i, h, g: (i, 0, kcol + h)),
                  pl.BlockSpec((None, n, hd), lambda i, h, g: (i, 0, vcol + h)),
                  pl.BlockSpec((None, 2, GRID_W, NA_TAB_LANES), lambda i, h, g: (h, 0, 0, 0))],
        out_specs=pl.BlockSpec((None, tq, hd), lambda i, h, g: (i, g, h)),
        out_shape=jax.ShapeDtypeStruct((b, n, BRANCH_WIDTH), BF16),
        compiler_params=_cparams(("parallel", "parallel", "arbitrary")),
        name="attn_na",
    )(p_lat, p_ctx, p_ctx, p_lat, p_lat, bias)


SWA_BLOCK = 256


def _swa_kernel(q_ref, kc_ref, vc_ref, kl_ref, vl_ref, sink_ref, o_ref, *, win):
    t = pl.program_id(2)
    tq = q_ref.shape[0]
    n = kl_ref.shape[0]
    kc = kc_ref[...]
    vc1 = _with_ones(vc_ref[...])
    row = lax.broadcasted_iota(jnp.int32, (SWA_BLOCK, win), 0)
    col = lax.broadcasted_iota(jnp.int32, (SWA_BLOCK, win), 1)
    for rb in range(tq // SWA_BLOCK):
        qs = t * tq + rb * SWA_BLOCK
        ws = pl.multiple_of(jnp.clip(qs - SWA_WINDOW, 0, n - win), SWA_WINDOW)
        kb = kl_ref[pl.ds(ws, win), :]
        vb1 = _with_ones(vl_ref[pl.ds(ws, win), :])
        valid = jnp.abs(row + (qs - ws) - col) <= SWA_WINDOW
        rows = slice(rb * SWA_BLOCK, (rb + 1) * SWA_BLOCK)
        for gi in range(2):
            q = q_ref[rows, gi * HEAD_DIM:(gi + 1) * HEAD_DIM]
            s_b = jnp.where(valid, _dot_nt(q, kb), NEG_INF)
            o = _band_ctx_attend(s_b, _dot_nt(q, kc), vb1, vc1, sink=sink_ref[gi:gi + 1, 0:1])
            o_ref[rows, gi * HEAD_DIM:(gi + 1) * HEAD_DIM] = o.astype(o_ref.dtype)


def _swa_call(p_lat, p_ctx, sink_tab):
    b, n, _ = p_lat.shape
    l = p_ctx.shape[1]
    hd = HEAD_DIM
    tq = 512
    win = SWA_BLOCK + 2 * SWA_WINDOW
    qcol, kcol, vcol = COL_SWA_Q // (2 * hd), COL_SWA_K // hd, COL_SWA_V // hd
    return pl.pallas_call(
        functools.partial(_swa_kernel, win=win),
        grid=(b, 2, n // tq),
        in_specs=[pl.BlockSpec((None, tq, 2 * hd), lambda i, k, t: (i, t, qcol + k)),
                  pl.BlockSpec((None, l, hd), lambda i, k, t: (i, 0, kcol + k)),
                  pl.BlockSpec((None, l, hd), lambda i, k, t: (i, 0, vcol + k)),
                  pl.BlockSpec((None, n, hd), lambda i, k, t: (i, 0, kcol + k)),
                  pl.BlockSpec((None, n, hd), lambda i, k, t: (i, 0, vcol + k)),
                  pl.BlockSpec((None, 2, LANES), lambda i, k, t: (k, 0, 0))],
        out_specs=pl.BlockSpec((None, tq, 2 * hd), lambda i, k, t: (i, t, k)),
        out_shape=jax.ShapeDtypeStruct((b, n, BRANCH_WIDTH), BF16),
        compiler_params=_cparams(("parallel", "parallel", "arbitrary")),
        name="attn_swa",
    )(p_lat, p_ctx, p_ctx, p_lat, p_lat, sink_tab)


def _softmax_attend(q, k, v, sink=None):
    s = _dot_nt(q, k)
    m = jnp.max(s, axis=-1, keepdims=True)
    if sink is not None:
        m = jnp.maximum(m, sink)
    e = jnp.exp2(s - m)
    den = jnp.sum(e, axis=-1, keepdims=True)
    if sink is not None:
        den = den + jnp.exp2(sink - m)
    return _dot(e.astype(BF16), v) / den


def _ctx_attn_kernel(p_ref, lam_ref, subg_ref, sink_ref, o_ref, *, lambda_init):
    hd = HEAD_DIM

    def col(c0, h):
        return p_ref[:, c0 + h * hd:c0 + (h + 1) * hd]

    lam = _diff_lambda(lam_ref, lambda_init)
    for h in range(4):
        o = _softmax_attend(col(COL_GQA_Q, h), col(COL_GQA_K, h // 2), col(COL_GQA_V, h // 2))
        o_ref[:, h * hd:(h + 1) * hd] = o.astype(o_ref.dtype)
    for h in range(4):
        q = col(COL_DIFF_Q, h)
        tq = q.shape[0]
        o2 = _softmax_attend(_split_maps(q), col(COL_DIFF_K, h), col(COL_DIFF_V, h))
        d = o2[:tq] - lam * o2[tq:]
        d = _rms(d) * subg_ref[...] * (1.0 - lambda_init)
        o_ref[:, BRANCH_WIDTH + h * hd:BRANCH_WIDTH + (h + 1) * hd] = d.astype(o_ref.dtype)
    for h in range(4):
        o = _softmax_attend(col(COL_NA_Q, h), col(COL_NA_K, h), col(COL_NA_V, h))
        o_ref[:, 2 * BRANCH_WIDTH + h * hd:2 * BRANCH_WIDTH + (h + 1) * hd] = o.astype(o_ref.dtype)
    for h in range(4):
        o = _softmax_attend(col(COL_SWA_Q, h), col(COL_SWA_K, h // 2), col(COL_SWA_V, h // 2),
                            sink=sink_ref[h:h + 1, 0:1])
        o_ref[:, 3 * BRANCH_WIDTH + h * hd:3 * BRANCH_WIDTH + (h + 1) * hd] = o.astype(o_ref.dtype)


def _ctx_attn_call(p_ctx, lam_p, subg, sink_tab, lambda_init):
    b, l, cols = p_ctx.shape
    return pl.pallas_call(
        functools.partial(_ctx_attn_kernel, lambda_init=lambda_init),
        grid=(b,),
        in_specs=[pl.BlockSpec((None, l, cols), lambda i: (i, 0, 0)),
                  pl.BlockSpec((4, DIFF_QK_DIM), lambda i: (0, 0)),
                  pl.BlockSpec((1, HEAD_DIM), lambda i: (0, 0)),
                  pl.BlockSpec((4, LANES), lambda i: (0, 0))],
        out_specs=pl.BlockSpec((None, l, N_BRANCHES * BRANCH_WIDTH), lambda i: (i, 0, 0)),
        out_shape=jax.ShapeDtypeStruct((b, l, N_BRANCHES * BRANCH_WIDTH), BF16),
        compiler_params=_cparams(("parallel",)),
        name="attn_ctx",
    )(p_ctx, lam_p, subg.reshape(1, HEAD_DIM), sink_tab.reshape(4, LANES))


MERGE_CHUNKS = 4


def _merge_kernel(o0, o1, o2, o3, g_ref, wb_ref, wo_ref, x_ref, mod_ref, ng_ref, *rest, with_router):
    if with_router:
        router_ref, xo_ref, h_ref, lg_ref = rest
    else:
        xo_ref, h_ref = rest
    d = x_ref.shape[1]
    dc = d // MERGE_CHUNKS
    o_vals = [o[...] for o in (o0, o1, o2, o3)]
    z = None
    for c in range(MERGE_CHUNKS):
        s = None
        for i, o in enumerate(o_vals):
            y = _dot(o, wb_ref[i, :, c * dc:(c + 1) * dc])
            gate = g_ref[:, i * d + c * dc:i * d + (c + 1) * dc].astype(F32)
            term = jax.nn.sigmoid(gate) * y
            s = term if s is None else s + term
        zc = _dot(s.astype(BF16), wo_ref[c * dc:(c + 1) * dc, :])
        z = zc if z is None else z + zc
    xn = x_ref[...] + mod_ref[2:3, :] * z
    xo_ref[...] = xn
    hn = _norm_mod(xn, ng_ref[...], mod_ref, 3, 4)
    h_ref[...] = hn.astype(h_ref.dtype)
    if with_router:
        lg_ref[...] = _dot_split(hn, router_ref[...])


def _dot_split(a, b):
    a_hi = a.astype(BF16)
    a_lo = (a - a_hi.astype(F32)).astype(BF16)
    b_hi = b.astype(BF16)
    b_lo = (b - b_hi.astype(F32)).astype(BF16)
    n = b.shape[1]
    both = _dot(a_hi, jnp.concatenate([b_hi, b_lo], axis=1))
    return both[:, :n] + both[:, n:] + _dot(a_lo, b_hi)


def _merge_call(o_parts, gates, w_branch, w_out, x, mod, next_g, router=None):
    b, r, d = x.shape
    tm = min(r, 256)
    with_router = router is not None
    whole = dict(pipeline_mode=pl.Buffered(1))

    def o_spec(cb):
        return pl.BlockSpec((None, tm, BRANCH_WIDTH), lambda i, t: (i, t, cb))

    row_spec = pl.BlockSpec((None, tm, d), lambda i, t: (i, t, 0))
    in_specs = [o_spec(cb) for _, cb in o_parts] + [
        pl.BlockSpec((None, tm, N_BRANCHES * d), lambda i, t: (i, t, 0)),
        pl.BlockSpec((N_BRANCHES, BRANCH_WIDTH, d), lambda i, t: (0, 0, 0), **whole),
        pl.BlockSpec((d, d), lambda i, t: (0, 0), **whole),
        row_spec,
        pl.BlockSpec((None, 6, d), lambda i, t: (i, 0, 0)),
        pl.BlockSpec((1, d), lambda i, t: (0, 0))]
    args = [a for a, _ in o_parts] + [gates, w_branch, w_out, x, mod, next_g.reshape(1, d)]
    out_specs = [row_spec, row_spec]
    out_shape = [jax.ShapeDtypeStruct((b, r, d), F32), jax.ShapeDtypeStruct((b, r, d), BF16)]
    if with_router:
        in_specs.append(pl.BlockSpec((d, LANES), lambda i, t: (0, 0), **whole))
        args.append(router)
        out_specs.append(pl.BlockSpec((None, tm, LANES), lambda i, t: (i, t, 0)))
        out_shape.append(jax.ShapeDtypeStruct((b, r, LANES), F32))
    return pl.pallas_call(
        functools.partial(_merge_kernel, with_router=with_router),
        grid=(b, r // tm),
        in_specs=in_specs,
        out_specs=out_specs,
        out_shape=out_shape,
        compiler_params=_cparams(("parallel", "parallel")),
        name="merge",
    )(*args)


def _swiglu_partial(h, wg, wu, wd):
    gate = _dot(h, wg)
    up = _dot(h, wu)
    act = gate * jax.nn.sigmoid(gate) * up
    return _dot(act.astype(BF16), wd)


def _ffn_kernel(h_ref, wg_ref, wu_ref, wd_ref, x_ref, mod_ref, ng_ref, nmod_ref, xo_ref, hn_ref,
                acc_ref):
    f = pl.program_id(2)

    @pl.when(f == 0)
    def _():
        acc_ref[...] = jnp.zeros_like(acc_ref)

    acc_ref[...] += _swiglu_partial(h_ref[...], wg_ref[...], wu_ref[...], wd_ref[...])

    @pl.when(f == pl.num_programs(2) - 1)
    def _():
        xn = x_ref[...] + mod_ref[5:6, :] * acc_ref[...]
        xo_ref[...] = xn
        hn_ref[...] = _norm_mod(xn, ng_ref[...], nmod_ref, 0, 1).astype(hn_ref.dtype)


def _ffn_call(h, wg, wu, wd, x, mod, next_g, next_mod):
    b, r, d = x.shape
    dff = wg.shape[1]
    tf = 512
    tm = min(r, 512)
    row_spec = pl.BlockSpec((None, tm, d), lambda i, t, f: (i, t, 0))
    mod_spec = pl.BlockSpec((None, 6, d), lambda i, t, f: (i, 0, 0))
    return pl.pallas_call(
        _ffn_kernel,
        grid=(b, r // tm, dff // tf),
        in_specs=[row_spec,
                  pl.BlockSpec((d, tf), lambda i, t, f: (0, f)),
                  pl.BlockSpec((d, tf), lambda i, t, f: (0, f)),
                  pl.BlockSpec((tf, d), lambda i, t, f: (f, 0)),
                  row_spec, mod_spec,
                  pl.BlockSpec((1, d), lambda i, t, f: (0, 0)),
                  mod_spec],
        out_specs=[row_spec, row_spec],
        out_shape=[jax.ShapeDtypeStruct((b, r, d), F32), jax.ShapeDtypeStruct((b, r, d), BF16)],
        scratch_shapes=[pltpu.VMEM((tm, d), F32)],
        compiler_params=_cparams(("parallel", "parallel", "arbitrary")),
        name="ffn_dense",
    )(h, wg, wu, wd, x, mod, next_g.reshape(1, d), next_mod)


def _route_kernel(lg_ref, idx_ref, w_ref):
    lg = lg_ref[...]
    lane = lax.broadcasted_iota(jnp.int32, lg.shape, 1)
    valid = lane < N_EXPERTS
    mx = jnp.max(jnp.where(valid, lg, -jnp.inf), axis=-1, keepdims=True)
    e = jnp.where(valid, jnp.exp(lg - mx), 0.0)
    p = e / jnp.sum(e, axis=-1, keepdims=True)
    p1 = jnp.max(p, axis=-1, keepdims=True)
    i1 = jnp.min(jnp.where(p == p1, lane, LANES), axis=-1, keepdims=True)
    rest = jnp.where(jnp.logical_or(lane == i1, jnp.logical_not(valid)), -1.0, p)
    p2 = jnp.max(rest, axis=-1, keepdims=True)
    i2 = jnp.min(jnp.where(rest == p2, lane, LANES), axis=-1, keepdims=True)
    tot = p1 + p2
    idx_ref[...] = jnp.where(lane == 0, i1, jnp.where(lane == 1, i2, 0))
    w_ref[...] = jnp.where(lane == 0, p1 / tot, jnp.where(lane == 1, p2 / tot, 0.0))


def _route_call(logits):
    r = logits.shape[0]
    tm = min(r, 1024)
    spec = pl.BlockSpec((tm, LANES), lambda i: (i, 0))
    return pl.pallas_call(
        _route_kernel,
        grid=(r // tm,),
        in_specs=[spec],
        out_specs=[spec, spec],
        out_shape=[jax.ShapeDtypeStruct((r, LANES), jnp.int32), jax.ShapeDtypeStruct((r, LANES), F32)],
        compiler_params=_cparams(("parallel",)),
        name="route_top2",
    )(logits)


MOE_HALF = 512
MOE_TILE = 2 * MOE_HALF
MOE_VMEM_LIMIT = 60 * 1024 * 1024


def _moe_kernel(te_ref, nr_ref, x_ref, wg_ref, wu_ref, wd_ref, o_ref, acc_ref):
    i = pl.program_id(0)
    f = pl.program_id(1)

    @pl.when(f == 0)
    def _():
        acc_ref[...] = jnp.zeros_like(acc_ref)

    for half in range(MOE_TILE // MOE_HALF):
        rows = slice(half * MOE_HALF, (half + 1) * MOE_HALF)

        @pl.when(nr_ref[i] > half * MOE_HALF)
        def _():
            acc_ref[rows, :] += _swiglu_partial(x_ref[rows, :], wg_ref[...].astype(BF16),
                                                wu_ref[...].astype(BF16), wd_ref[...].astype(BF16))

    @pl.when(f == pl.num_programs(1) - 1)
    def _():
        o_ref[...] = acc_ref[...].astype(o_ref.dtype)


def _moe_call(xs, wg, wu, wd, tile_expert, tile_rows):
    r, d = xs.shape
    dff = wg.shape[2]
    tf = 512
    nf = dff // tf

    def fidx(i, f, nr):
        return jnp.where(nr[i] > 0, f, nf - 1)

    grid_spec = pltpu.PrefetchScalarGridSpec(
        num_scalar_prefetch=2,
        grid=(r // MOE_TILE, nf),
        in_specs=[pl.BlockSpec((MOE_TILE, d), lambda i, f, te, nr: (i, 0), pipeline_mode=pl.Buffered(1)),
                  pl.BlockSpec((None, d, tf), lambda i, f, te, nr: (te[i], 0, fidx(i, f, nr))),
                  pl.BlockSpec((None, d, tf), lambda i, f, te, nr: (te[i], 0, fidx(i, f, nr))),
                  pl.BlockSpec((None, tf, d), lambda i, f, te, nr: (te[i], fidx(i, f, nr), 0))],
        out_specs=pl.BlockSpec((MOE_TILE, d), lambda i, f, te, nr: (i, 0), pipeline_mode=pl.Buffered(1)),
        scratch_shapes=[pltpu.VMEM((MOE_TILE, d), F32)])
    return pl.pallas_call(
        _moe_kernel,
        grid_spec=grid_spec,
        out_shape=jax.ShapeDtypeStruct((r, d), BF16),
        compiler_params=pltpu.CompilerParams(dimension_semantics=("arbitrary", "arbitrary"),
                                             vmem_limit_bytes=MOE_VMEM_LIMIT),
        name="moe_experts",
    )(tile_expert, tile_rows, xs, wg, wu, wd)


def _moe_plan(idx):
    t = idx.shape[0]
    a = t * TOP_K
    flat_e = idx.reshape(a)
    onehot = (flat_e[:, None] == jnp.arange(N_EXPERTS, dtype=jnp.int32)[None, :]).astype(jnp.int32)
    csum = jnp.cumsum(onehot, axis=0)
    rank = jnp.sum((csum - onehot) * onehot, axis=1)
    counts = csum[-1]
    padded = ((counts + MOE_TILE - 1) // MOE_TILE) * MOE_TILE
    ends = jnp.cumsum(padded)
    offs = ends - padded
    dest = offs[flat_e] + rank
    rows = a + N_EXPERTS * MOE_TILE
    src_tok = (jnp.arange(rows, dtype=jnp.int32) % t).at[dest].set(jnp.arange(a, dtype=jnp.int32) // TOP_K)
    tile_start = jnp.arange(rows // MOE_TILE, dtype=jnp.int32) * MOE_TILE
    past = jnp.sum((tile_start[:, None] >= ends[None, :]).astype(jnp.int32), axis=1)
    tile_expert = jnp.minimum(past, N_EXPERTS - 1)
    tile_rows = jnp.where(past < N_EXPERTS,
                          jnp.clip((offs + counts)[tile_expert] - tile_start, 0, MOE_TILE), 0)
    return dest.reshape(t, TOP_K), src_tok, tile_expert, tile_rows.astype(jnp.int32)


def _final_kernel(x_ref, y0_ref, y1_ref, w_ref, mod_ref, g_ref, o_ref):
    y = w_ref[:, 0:1] * y0_ref[...].astype(F32) + w_ref[:, 1:2] * y1_ref[...].astype(F32)
    xn = x_ref[...] + mod_ref[5:6, :] * y
    o_ref[...] = _rms(xn) * g_ref[...]


def _final_call(x, y0, y1, w, mod, g):
    b, r, d = x.shape
    tm = min(r, 512)
    row_spec = pl.BlockSpec((None, tm, d), lambda i, t: (i, t, 0))
    return pl.pallas_call(
        _final_kernel,
        grid=(b, r // tm),
        in_specs=[row_spec, row_spec, row_spec,
                  pl.BlockSpec((None, tm, LANES), lambda i, t: (i, t, 0)),
                  pl.BlockSpec((None, 6, d), lambda i, t: (i, 0, 0)),
                  pl.BlockSpec((1, d), lambda i, t: (0, 0))],
        out_specs=row_spec,
        out_shape=jax.ShapeDtypeStruct((b, r, d), F32),
        compiler_params=_cparams(("parallel", "parallel")),
        name="final_norm",
    )(x, y0, y1, w, mod, g.reshape(1, d))


def _rope_tables(n):
    pos = jnp.arange(n)
    rows = (pos // GRID_W).astype(F32)
    cols = (pos % GRID_W).astype(F32)
    lane = np.arange(LANES)
    out = []
    for hw in (32, 16):
        period = 4 * hw
        u = lane % period
        use_cols = (u // (2 * hw)) == 1
        w = u % (2 * hw)
        freqs = jnp.asarray(ROPE_THETA ** (-(w % hw).astype(np.float32) / hw), F32)
        p = jnp.where(jnp.asarray(use_cols)[None, :], cols[:, None], rows[:, None])
        ang = p * freqs[None, :]
        sign = jnp.asarray(np.where(w < hw, -1.0, 1.0), F32)
        out += [jnp.cos(ang), jnp.sin(ang) * sign[None, :]]
    return out


def kernel(x, c, ctx, c_ctx, attn_norm_g, ffn_norm_g, ada_w, ada_b, w_in, qk_norm_g, diff_lambda,
           diff_subln_g, na_rpb, swa_sink, w_branch, w_out, ffn_w_gate, ffn_w_up, ffn_w_down,
           moe_router, moe_w_gate, moe_w_up, moe_w_down, final_norm_g):
    b, n, d = x.shape
    l = ctx.shape[1]
    depth = w_in.shape[0]
    assert depth == 2, "laid out for one dense layer followed by one routed last layer"
    rows_n = n // GRID_W

    lat_tables = _rope_tables(n)
    ones = jnp.ones((l, LANES), F32)
    zeros = jnp.zeros((l, LANES), F32)
    ctx_tables = [ones, zeros, ones, zeros]

    cvec = jnp.zeros((8, d), F32).at[:b].set(c).at[b].set(c_ctx)
    mods = []
    for i in range(depth):
        m = _adaln(cvec, ada_w, ada_b, i).reshape(8, 6, d)
        mods.append((m[:b], jnp.broadcast_to(m[b:b + 1], (b, 6, d))))

    def mixers(i, h_lat, h_ctx):
        lambda_init = 0.8 - 0.6 * math.exp(-0.3 * i)
        w_qkv = _qkv_weights_call(w_in, i)
        sink_tab = jnp.broadcast_to((swa_sink[i].astype(F32) * LOG2E)[:, None], (4, LANES))
        p_lat = _qkv_call(h_lat, w_qkv, lat_tables, qk_norm_g[i])
        p_ctx = _qkv_call(h_ctx, w_qkv, ctx_tables, qk_norm_g[i])
        o_lat = [_gqa_call(p_lat, p_ctx),
                 _diff_call(p_lat, p_ctx, diff_lambda[i], diff_subln_g[i], lambda_init),
                 _na_call(p_lat, p_ctx, _na_bias_tables(na_rpb[i])),
                 _swa_call(p_lat, p_ctx, sink_tab.reshape(2, 2, LANES))]
        return p_ctx, o_lat, sink_tab, lambda_init

    mod_lat, mod_ctx = mods[0]
    h_lat = _norm_mod_call(x, attn_norm_g[0], mod_lat, 0, 1)
    h_ctx = _norm_mod_call(ctx, attn_norm_g[0], mod_ctx, 0, 1)
    p_ctx, o_lat, sink_tab, lambda_init = mixers(0, h_lat, h_ctx)
    wb = w_branch[0].astype(BF16)
    wo = w_out[0].astype(BF16)
    x_lat, h_lat = _merge_call([(o, 0) for o in o_lat], _gate_proj_call(h_lat, w_in, 0), wb, wo, x,
                               mod_lat, ffn_norm_g[0])
    o_ctx = _ctx_attn_call(p_ctx, diff_lambda[0], diff_subln_g[0], sink_tab, lambda_init)
    x_ctx, h_ctx = _merge_call([(o_ctx, k) for k in range(N_BRANCHES)], _gate_proj_call(h_ctx, w_in, 0),
                               wb, wo, ctx, mod_ctx, ffn_norm_g[0])
    wg = ffn_w_gate[0].astype(BF16)
    wu = ffn_w_up[0].astype(BF16)
    wd = ffn_w_down[0].astype(BF16)
    _, h_ctx = _ffn_call(h_ctx, wg, wu, wd, x_ctx, mod_ctx, attn_norm_g[1], mods[1][1])
    x_lat, h_lat = _ffn_call(h_lat, wg, wu, wd, x_lat, mod_lat, attn_norm_g[1], mods[1][0])

    mod_lat, _ = mods[1]
    _, o_lat, _, _ = mixers(1, h_lat, h_ctx)
    router = jnp.zeros((d, LANES), F32).at[:, :N_EXPERTS].set(moe_router[0])
    x_lat, h_lat, logits = _merge_call(
        [(o, 0) for o in o_lat], _gate_proj_call(h_lat, w_in, 1),
        w_branch[1].astype(BF16), w_out[1].astype(BF16), x_lat, mod_lat, ffn_norm_g[1], router)

    idx_pad, wts_pad = _route_call(logits.reshape(b * n, LANES))
    dest, src_tok, tile_expert, tile_rows = _moe_plan(idx_pad[:, :TOP_K])
    xs = h_lat.reshape(b * n, d).at[src_tok].get(mode="promise_in_bounds")
    ys = _moe_call(xs, moe_w_gate[0], moe_w_up[0], moe_w_down[0], tile_expert, tile_rows)
    y0 = ys.at[dest[:, 0]].get(mode="promise_in_bounds").reshape(b, n, d)
    y1 = ys.at[dest[:, 1]].get(mode="promise_in_bounds").reshape(b, n, d)
    return _final_call(x_lat, y0, y1, wts_pad.reshape(b, n, LANES), mod_lat, final_norm_g)
```

```python
import functools
import math

import numpy as np
import jax
import jax.numpy as jnp
from jax import lax
from jax.experimental import pallas as pl
from jax.experimental.pallas import tpu as pltpu

F32 = jnp.float32
BF16 = jnp.bfloat16

GRID_W = 64
HEAD_DIM = 128
N_BRANCHES = 4
BRANCH_WIDTH = 4 * HEAD_DIM
DIFF_QK_DIM = 64
NA_ROWS = 8
NA_COLS = 16
SWA_WINDOW = 128
N_EXPERTS = 8
TOP_K = 2
NORM_EPS = 1e-6
ROPE_THETA = 10000.0
NEG_INF = -1e30
LOG2E = math.log2(math.e)

QKV_COLS = 5120
COL_GQA_Q, COL_GQA_K, COL_GQA_V = 0, 512, 768
COL_DIFF_Q, COL_DIFF_K, COL_DIFF_V = 1024, 1536, 2048
COL_NA_Q, COL_NA_K, COL_NA_V = 2560, 3072, 3584
COL_SWA_Q, COL_SWA_K, COL_SWA_V = 4096, 4608, 4864

LANES = 128
VMEM_LIMIT = 56 * 1024 * 1024

QS128 = HEAD_DIM ** -0.5 * LOG2E
QS64 = DIFF_QK_DIM ** -0.5 * LOG2E


def _cparams(sem):
    return pltpu.CompilerParams(dimension_semantics=sem, vmem_limit_bytes=VMEM_LIMIT)


def _dot(a, b):
    return jnp.dot(a, b, preferred_element_type=F32)


def _dot_nt(a, b):
    return lax.dot_general(a, b, (((1,), (1,)), ((), ())), preferred_element_type=F32)


def _rms(x):
    return x * lax.rsqrt(jnp.mean(x * x, axis=-1, keepdims=True) + NORM_EPS)


def _norm_mod(x, g, mod_ref, shift_idx, scale_idx):
    y = _rms(x) * g
    return y * (1.0 + mod_ref[scale_idx:scale_idx + 1, :]) + mod_ref[shift_idx:shift_idx + 1, :]


def _adaln_kernel(c_ref, w_ref, b_ref, o_ref):
    c = c_ref[...]
    s = c * jax.nn.sigmoid(c)
    o_ref[...] = jnp.dot(s, w_ref[...], preferred_element_type=F32,
                         precision=lax.Precision.HIGHEST) + b_ref[...]


def _adaln(cvec, w, b, layer):
    rows, d = cvec.shape
    cols = w.shape[2]
    tn = 1024 if cols % 1024 == 0 else cols
    return pl.pallas_call(
        _adaln_kernel,
        grid=(cols // tn,),
        in_specs=[pl.BlockSpec((rows, d), lambda j: (0, 0)),
                  pl.BlockSpec((None, d, tn), lambda j: (layer, 0, j)),
                  pl.BlockSpec((None, 1, tn), lambda j: (layer, 0, j))],
        out_specs=pl.BlockSpec((rows, tn), lambda j: (0, j)),
        out_shape=jax.ShapeDtypeStruct((rows, cols), F32),
        compiler_params=_cparams(("arbitrary",)),
        name="adaln",
    )(cvec, w, b.reshape(b.shape[0], 1, cols))


def _norm_mod_kernel(x_ref, g_ref, mod_ref, h_ref, *, shift_idx, scale_idx):
    h_ref[...] = _norm_mod(x_ref[...], g_ref[...], mod_ref, shift_idx, scale_idx).astype(BF16)


def _norm_mod_call(x, g, mod, shift_idx, scale_idx):
    b, r, d = x.shape
    tm = min(r, 512)
    return pl.pallas_call(
        functools.partial(_norm_mod_kernel, shift_idx=shift_idx, scale_idx=scale_idx),
        grid=(b, r // tm),
        in_specs=[pl.BlockSpec((None, tm, d), lambda i, t: (i, t, 0)),
                  pl.BlockSpec((1, d), lambda i, t: (0, 0)),
                  pl.BlockSpec((None, 6, d), lambda i, t: (i, 0, 0))],
        out_specs=pl.BlockSpec((None, tm, d), lambda i, t: (i, t, 0)),
        out_shape=jax.ShapeDtypeStruct((b, r, d), BF16),
        compiler_params=_cparams(("parallel", "parallel")),
        name="norm_mod",
    )(x, g.reshape(1, d), mod)


QKV_TILE = 4 * LANES
_PLAIN = (None, None, 1.0)
_QKV_TILE_OPS = {
    0: [(0, 32, QS128)] * 4,
    1: [(1, 32, 1.0)] * 2 + [_PLAIN] * 2,
    2: [(None, 16, QS64)] * 4,
    3: [(None, 16, 1.0)] * 4,
    5: [(None, None, QS128)] * 4,
    8: [(None, 32, QS128)] * 4,
    9: [(None, 32, 1.0)] * 2 + [_PLAIN] * 2,
}


def _rope(y, cos, sin, hw):
    lane = lax.broadcasted_iota(jnp.int32, y.shape, 1)
    first = (lane % (2 * hw)) < hw
    partner = jnp.where(first, pltpu.roll(y, LANES - hw, 1), pltpu.roll(y, hw, 1))
    return y * cos + partner * sin


def _qkv_kernel(h_ref, w_ref, cos32_ref, sin32_ref, cos16_ref, sin16_ref, g_ref, o_ref):
    h = h_ref[...]
    for j in range(o_ref.shape[1] // QKV_TILE):
        acc = _dot(h, w_ref[:, j * QKV_TILE:(j + 1) * QKV_TILE])
        for c, (norm_row, hw, scale) in enumerate(_QKV_TILE_OPS.get(j, [_PLAIN] * 4)):
            y = acc[:, c * LANES:(c + 1) * LANES]
            if norm_row is not None:
                y = _rms(y) * g_ref[norm_row:norm_row + 1, :]
            if hw == 32:
                y = _rope(y, cos32_ref[...], sin32_ref[...], 32)
            elif hw == 16:
                y = _rope(y, cos16_ref[...], sin16_ref[...], 16)
            if scale != 1.0:
                y = y * scale
            col = j * QKV_TILE + c * LANES
            o_ref[:, col:col + LANES] = y.astype(o_ref.dtype)


def _qkv_call(h, w_qkv, tables, qk_g):
    b, r, d = h.shape
    cols = w_qkv.shape[1]
    tm = min(r, 512)
    tab_spec = pl.BlockSpec((tm, LANES), lambda i, t: (t, 0))
    return pl.pallas_call(
        _qkv_kernel,
        grid=(b, r // tm),
        in_specs=[pl.BlockSpec((None, tm, d), lambda i, t: (i, t, 0)),
                  pl.BlockSpec((d, cols), lambda i, t: (0, 0), pipeline_mode=pl.Buffered(1)),
                  tab_spec, tab_spec, tab_spec, tab_spec,
                  pl.BlockSpec((2, LANES), lambda i, t: (0, 0))],
        out_specs=pl.BlockSpec((None, tm, cols), lambda i, t: (i, t, 0)),
        out_shape=jax.ShapeDtypeStruct((b, r, cols), BF16),
        compiler_params=_cparams(("parallel", "parallel")),
        name="qkv_proj",
    )(h, w_qkv, *tables, qk_g)


def _matmul_kernel(h_ref, w_ref, o_ref):
    o_ref[...] = _dot(h_ref[...], w_ref[...].astype(BF16)).astype(o_ref.dtype)


def _gate_proj_call(h, w_in, layer):
    b, r, d = h.shape
    tn = 512
    cols = w_in.shape[2] - QKV_COLS
    col0 = QKV_COLS // tn
    tm = 2048 if r % 2048 == 0 else r
    return pl.pallas_call(
        _matmul_kernel,
        grid=(b, r // tm, cols // tn),
        in_specs=[pl.BlockSpec((None, tm, d), lambda i, t, j: (i, t, 0)),
                  pl.BlockSpec((None, d, tn), lambda i, t, j: (layer, 0, col0 + j))],
        out_specs=pl.BlockSpec((None, tm, tn), lambda i, t, j: (i, t, j)),
        out_shape=jax.ShapeDtypeStruct((b, r, cols), BF16),
        compiler_params=_cparams(("parallel", "parallel", "arbitrary")),
        name="gate_proj",
    )(h, w_in)


def _cast_kernel(w_ref, o_ref):
    o_ref[...] = w_ref[...].astype(o_ref.dtype)


def _qkv_weights_call(w_in, layer):
    d = w_in.shape[1]
    return pl.pallas_call(
        _cast_kernel,
        grid=(QKV_COLS // QKV_TILE,),
        in_specs=[pl.BlockSpec((None, d, QKV_TILE), lambda j: (layer, 0, j))],
        out_specs=pl.BlockSpec((d, QKV_TILE), lambda j: (0, j)),
        out_shape=jax.ShapeDtypeStruct((d, QKV_COLS), BF16),
        compiler_params=_cparams(("parallel",)),
        name="qkv_weights",
    )(w_in)


def _diff_lambda(lam_ref, lambda_init):
    a = jnp.sum(lam_ref[0:1, :] * lam_ref[1:2, :], axis=-1, keepdims=True)
    b = jnp.sum(lam_ref[2:3, :] * lam_ref[3:4, :], axis=-1, keepdims=True)
    return jnp.exp(a) - jnp.exp(b) + lambda_init


def _split_maps(q):
    lane = lax.broadcasted_iota(jnp.int32, q.shape, 1)
    zero = jnp.zeros_like(q)
    return jnp.concatenate([jnp.where(lane < DIFF_QK_DIM, q, zero),
                            jnp.where(lane >= DIFF_QK_DIM, q, zero)], axis=0)


FLASH_ROW_BLOCK = 256


def _flash_pair(q2, kc_ref, vc_ref, kl_ref, vl_ref, tk):
    m_rows = q2.shape[0]
    n_lat = kl_ref.shape[0]
    nb = m_rows // FLASH_ROW_BLOCK
    qs = [q2[i * FLASH_ROW_BLOCK:(i + 1) * FLASH_ROW_BLOCK] for i in range(nb)]

    def step(k, v, carry):
        v1 = jnp.concatenate([v, jnp.ones_like(v)], axis=1)
        out = []
        for q, (m, acc) in zip(qs, carry):
            s = _dot_nt(q, k)
            m_new = jnp.maximum(m, jnp.max(s, axis=-1, keepdims=True))
            p = jnp.exp2(s - m_new)
            acc = jnp.exp2(m - m_new) * acc + _dot(p.astype(BF16), v1)
            out.append((m_new, acc))
        return tuple(out)

    carry = tuple((jnp.full((FLASH_ROW_BLOCK, 1), NEG_INF, F32),
                   jnp.zeros((FLASH_ROW_BLOCK, 2 * HEAD_DIM), F32)) for _ in range(nb))
    carry = step(kc_ref[...], vc_ref[...], carry)

    for c in range(n_lat // tk):
        carry = step(kl_ref[c * tk:(c + 1) * tk, :], vl_ref[c * tk:(c + 1) * tk, :], carry)
    return jnp.concatenate([acc[:, :HEAD_DIM] / acc[:, HEAD_DIM:] for _, acc in carry], axis=0)


def _gqa_kernel(q_ref, kc_ref, vc_ref, kl_ref, vl_ref, o_ref, *, tk):
    tq = q_ref.shape[0]
    q = q_ref[...]
    q2 = jnp.concatenate([q[:, :HEAD_DIM], q[:, HEAD_DIM:]], axis=0)
    o = _flash_pair(q2, kc_ref, vc_ref, kl_ref, vl_ref, tk)
    o_ref[:, :HEAD_DIM] = o[:tq].astype(o_ref.dtype)
    o_ref[:, HEAD_DIM:] = o[tq:].astype(o_ref.dtype)


def _diff_kernel(q_ref, kc_ref, vc_ref, kl_ref, vl_ref, lam_ref, subg_ref, o_ref, *, tk, lambda_init):
    tq = q_ref.shape[0]
    o = _flash_pair(_split_maps(q_ref[...]), kc_ref, vc_ref, kl_ref, vl_ref, tk)
    lam = _diff_lambda(lam_ref, lambda_init)
    d = o[:tq] - lam * o[tq:]
    o_ref[...] = (_rms(d) * subg_ref[...] * (1.0 - lambda_init)).astype(o_ref.dtype)


def _gqa_call(p_lat, p_ctx):
    b, n, _ = p_lat.shape
    l = p_ctx.shape[1]
    tq = 512
    tk = 512
    hd = HEAD_DIM
    kcol, vcol = COL_GQA_K // hd, COL_GQA_V // hd
    return pl.pallas_call(
        functools.partial(_gqa_kernel, tk=tk),
        grid=(b, 2, n // tq),
        in_specs=[pl.BlockSpec((None, tq, 2 * hd), lambda i, k, t: (i, t, k)),
                  pl.BlockSpec((None, l, hd), lambda i, k, t: (i, 0, kcol + k)),
                  pl.BlockSpec((None, l, hd), lambda i, k, t: (i, 0, vcol + k)),
                  pl.BlockSpec((None, n, hd), lambda i, k, t: (i, 0, kcol + k)),
                  pl.BlockSpec((None, n, hd), lambda i, k, t: (i, 0, vcol + k))],
        out_specs=pl.BlockSpec((None, tq, 2 * hd), lambda i, k, t: (i, t, k)),
        out_shape=jax.ShapeDtypeStruct((b, n, BRANCH_WIDTH), BF16),
        compiler_params=_cparams(("parallel", "parallel", "arbitrary")),
        name="attn_gqa",
    )(p_lat, p_ctx, p_ctx, p_lat, p_lat)


def _diff_call(p_lat, p_ctx, lam_p, subg, lambda_init):
    b, n, _ = p_lat.shape
    l = p_ctx.shape[1]
    tq = 512
    tk = 512
    hd = HEAD_DIM
    qcol, kcol, vcol = COL_DIFF_Q // hd, COL_DIFF_K // hd, COL_DIFF_V // hd
    return pl.pallas_call(
        functools.partial(_diff_kernel, tk=tk, lambda_init=lambda_init),
        grid=(b, 4, n // tq),
        in_specs=[pl.BlockSpec((None, tq, hd), lambda i, h, t: (i, t, qcol + h)),
                  pl.BlockSpec((None, l, hd), lambda i, h, t: (i, 0, kcol + h)),
                  pl.BlockSpec((None, l, hd), lambda i, h, t: (i, 0, vcol + h)),
                  pl.BlockSpec((None, n, hd), lambda i, h, t: (i, 0, kcol + h)),
                  pl.BlockSpec((None, n, hd), lambda i, h, t: (i, 0, vcol + h)),
                  pl.BlockSpec((4, DIFF_QK_DIM), lambda i, h, t: (0, 0)),
                  pl.BlockSpec((1, hd), lambda i, h, t: (0, 0))],
        out_specs=pl.BlockSpec((None, tq, hd), lambda i, h, t: (i, t, h)),
        out_shape=jax.ShapeDtypeStruct((b, n, BRANCH_WIDTH), BF16),
        compiler_params=_cparams(("parallel", "parallel", "arbitrary")),
        name="attn_diff",
    )(p_lat, p_ctx, p_ctx, p_lat, p_lat, lam_p, subg.reshape(1, hd))


NA_GROUP_ROWS = 8
NA_BLOCK_ROWS = 4
NA_BLOCK_WIN_ROWS = NA_BLOCK_ROWS + NA_ROWS
NA_WIN_ROWS = 2 * NA_GROUP_ROWS


def _with_ones(v):
    return jnp.concatenate([v, jnp.ones_like(v)], axis=1)


def _band_ctx_attend(s_b, s_c, vb1, vc1, sink=None):
    m = jnp.maximum(jnp.max(s_b, axis=-1, keepdims=True), jnp.max(s_c, axis=-1, keepdims=True))
    if sink is not None:
        m = jnp.maximum(m, sink)
    acc = _dot(jnp.exp2(s_c - m).astype(BF16), vc1) + _dot(jnp.exp2(s_b - m).astype(BF16), vb1)
    den = acc[:, HEAD_DIM:]
    if sink is not None:
        den = den + jnp.exp2(sink - m)
    return acc[:, :HEAD_DIM] / den


NA_DI = 2 * NA_ROWS - 1
NA_TAB_PAD = NA_WIN_ROWS
NA_TAB_LANES = 3072


def _na_bias_tables(rpb):
    nh = rpb.shape[0]
    qc = np.arange(GRID_W)[:, None]
    kc = np.arange(GRID_W)[None, :]
    cstart = np.clip(qc - NA_COLS // 2, 0, GRID_W - NA_COLS)
    ok_col = (kc >= cstart) & (kc < cstart + NA_COLS)
    dj = np.clip(kc - qc + NA_COLS - 1, 0, 2 * NA_COLS - 2)
    sel_j = (dj[..., None] == np.arange(2 * NA_COLS - 1)) & ok_col[..., None]
    by_col = jnp.einsum("hij,qkj->hqik", rpb.astype(F32), jnp.asarray(sel_j, F32),
                        precision=lax.Precision.HIGHEST)
    by_col = jnp.where(ok_col[None, :, None, :], by_col * LOG2E, NEG_INF)
    strip = by_col.reshape(nh, GRID_W, NA_DI * GRID_W)
    total = NA_TAB_LANES + GRID_W
    strip = jnp.pad(strip, ((0, 0), (0, 0), (NA_TAB_PAD * GRID_W, total - (NA_TAB_PAD + NA_DI) * GRID_W)),
                    constant_values=NEG_INF)
    return jnp.stack([strip[:, :, :NA_TAB_LANES], strip[:, :, GRID_W:]], axis=1)


def _na_kernel(q_ref, kc_ref, vc_ref, kl_ref, vl_ref, tab_ref, o_ref, *, rows_n):
    g = pl.program_id(2)
    win = NA_BLOCK_WIN_ROWS * GRID_W
    blk_q = NA_BLOCK_ROWS * GRID_W
    kc = kc_ref[...]
    vc1 = _with_ones(vc_ref[...])
    lane = lax.broadcasted_iota(jnp.int32, (GRID_W, win), 1)
    for rb in range(NA_GROUP_ROWS // NA_BLOCK_ROWS):
        r0 = g * NA_GROUP_ROWS + rb * NA_BLOCK_ROWS
        ws = jnp.clip(r0 - NA_ROWS // 2, 0, rows_n - NA_BLOCK_WIN_ROWS)
        off = pl.multiple_of(ws * GRID_W, NA_BLOCK_ROWS * GRID_W)
        bias = []
        for a in range(NA_BLOCK_ROWS):
            r = r0 + a
            rs = jnp.clip(r - NA_ROWS // 2, 0, rows_n - NA_ROWS)
            blk = ws - r + NA_ROWS - 1 + NA_TAB_PAD
            start = pl.multiple_of((blk >> 1) * (2 * GRID_W), 2 * GRID_W)
            strip = tab_ref[blk & 1, :, pl.ds(start, win)]
            lo = (rs - ws) * GRID_W
            in_rows = jnp.logical_and(lane >= lo, lane < lo + NA_ROWS * GRID_W)
            bias.append(jnp.where(in_rows, strip, NEG_INF))
        q = q_ref[rb * blk_q:(rb + 1) * blk_q, :]
        s_b = _dot_nt(q, kl_ref[pl.ds(off, win), :]) + jnp.concatenate(bias, axis=0)
        o = _band_ctx_attend(s_b, _dot_nt(q, kc), _with_ones(vl_ref[pl.ds(off, win), :]), vc1)
        o_ref[rb * blk_q:(rb + 1) * blk_q, :] = o.astype(o_ref.dtype)


def _na_call(p_lat, p_ctx, bias):
    b, n, _ = p_lat.shape
    l = p_ctx.shape[1]
    hd = HEAD_DIM
    rows_n = n // GRID_W
    n_groups = rows_n // NA_GROUP_ROWS
    tq = NA_GROUP_ROWS * GRID_W
    qcol, kcol, vcol = COL_NA_Q // hd, COL_NA_K // hd, COL_NA_V // hd

    return pl.pallas_call(
        functools.partial(_na_kernel, rows_n=rows_n),
        grid=(b, 4, n_groups),
        in_specs=[pl.BlockSpec((None, tq, hd), lambda i, h, g: (i, g, qcol + h)),
                  pl.BlockSpec((None, l, hd), lambda i, h, g: (i, 0, kcol + h)),
                  pl.BlockSpec((None, l, hd), lambda i, h, g: (i, 0, vcol + h)),
                  pl.BlockSpec((None, n, hd), lambda i, h, g: (i, 0, kcol + h)),
                  pl.BlockSpec((None, n, hd), lambda i, h, g: (i, 0, vcol + h)),
                  pl.BlockSpec((None, 2, GRID_W, NA_TAB_LANES), lambda i, h, g: (h, 0, 0, 0))],
        out_specs=pl.BlockSpec((None, tq, hd), lambda i, h, g: (i, g, h)),
        out_shape=jax.ShapeDtypeStruct((b, n, BRANCH_WIDTH), BF16),
        compiler_params=_cparams(("parallel", "parallel", "arbitrary")),
        name="attn_na",
    )(p_lat, p_ctx, p_ctx, p_lat, p_lat, bias)


SWA_BLOCK = 256


def _swa_kernel(q_ref, kc_ref, vc_ref, kl_ref, vl_ref, sink_ref, o_ref, *, win):
    t = pl.program_id(2)
    tq = q_ref.shape[0]
    n = kl_ref.shape[0]
    kc = kc_ref[...]
    vc1 = _with_ones(vc_ref[...])
    row = lax.broadcasted_iota(jnp.int32, (SWA_BLOCK, win), 0)
    col = lax.broadcasted_iota(jnp.int32, (SWA_BLOCK, win), 1)
    for rb in range(tq // SWA_BLOCK):
        qs = t * tq + rb * SWA_BLOCK
        ws = pl.multiple_of(jnp.clip(qs - SWA_WINDOW, 0, n - win), SWA_WINDOW)
        kb = kl_ref[pl.ds(ws, win), :]
        vb1 = _with_ones(vl_ref[pl.ds(ws, win), :])
        valid = jnp.abs(row + (qs - ws) - col) <= SWA_WINDOW
        rows = slice(rb * SWA_BLOCK, (rb + 1) * SWA_BLOCK)
        for gi in range(2):
            q = q_ref[rows, gi * HEAD_DIM:(gi + 1) * HEAD_DIM]
            s_b = jnp.where(valid, _dot_nt(q, kb), NEG_INF)
            o = _band_ctx_attend(s_b, _dot_nt(q, kc), vb1, vc1, sink=sink_ref[gi:gi + 1, 0:1])
            o_ref[rows, gi * HEAD_DIM:(gi + 1) * HEAD_DIM] = o.astype(o_ref.dtype)


def _swa_call(p_lat, p_ctx, sink_tab):
    b, n, _ = p_lat.shape
    l = p_ctx.shape[1]
    hd = HEAD_DIM
    tq = 512
    win = SWA_BLOCK + 2 * SWA_WINDOW
    qcol, kcol, vcol = COL_SWA_Q // (2 * hd), COL_SWA_K // hd, COL_SWA_V // hd
    return pl.pallas_call(
        functools.partial(_swa_kernel, win=win),
        grid=(b, 2, n // tq),
        in_specs=[pl.BlockSpec((None, tq, 2 * hd), lambda i, k, t: (i, t, qcol + k)),
                  pl.BlockSpec((None, l, hd), lambda i, k, t: (i, 0, kcol + k)),
                  pl.BlockSpec((None, l, hd), lambda i, k, t: (i, 0, vcol + k)),
                  pl.BlockSpec((None, n, hd), lambda i, k, t: (i, 0, kcol + k)),
                  pl.BlockSpec((None, n, hd), lambda i, k, t: (i, 0, vcol + k)),
                  pl.BlockSpec((None, 2, LANES), lambda i, k, t: (k, 0, 0))],
        out_specs=pl.BlockSpec((None, tq, 2 * hd), lambda i, k, t: (i, t, k)),
        out_shape=jax.ShapeDtypeStruct((b, n, BRANCH_WIDTH), BF16),
        compiler_params=_cparams(("parallel", "parallel", "arbitrary")),
        name="attn_swa",
    )(p_lat, p_ctx, p_ctx, p_lat, p_lat, sink_tab)


def _softmax_attend(q, k, v, sink=None):
    s = _dot_nt(q, k)
    m = jnp.max(s, axis=-1, keepdims=True)
    if sink is not None:
        m = jnp.maximum(m, sink)
    e = jnp.exp2(s - m)
    den = jnp.sum(e, axis=-1, keepdims=True)
    if sink is not None:
        den = den + jnp.exp2(sink - m)
    return _dot(e.astype(BF16), v) / den


def _ctx_attn_kernel(p_ref, lam_ref, subg_ref, sink_ref, o_ref, *, lambda_init):
    hd = HEAD_DIM

    def col(c0, h):
        return p_ref[:, c0 + h * hd:c0 + (h + 1) * hd]

    lam = _diff_lambda(lam_ref, lambda_init)
    for h in range(4):
        o = _softmax_attend(col(COL_GQA_Q, h), col(COL_GQA_K, h // 2), col(COL_GQA_V, h // 2))
        o_ref[:, h * hd:(h + 1) * hd] = o.astype(o_ref.dtype)
    for h in range(4):
        q = col(COL_DIFF_Q, h)
        tq = q.shape[0]
        o2 = _softmax_attend(_split_maps(q), col(COL_DIFF_K, h), col(COL_DIFF_V, h))
        d = o2[:tq] - lam * o2[tq:]
        d = _rms(d) * subg_ref[...] * (1.0 - lambda_init)
        o_ref[:, BRANCH_WIDTH + h * hd:BRANCH_WIDTH + (h + 1) * hd] = d.astype(o_ref.dtype)
    for h in range(4):
        o = _softmax_attend(col(COL_NA_Q, h), col(COL_NA_K, h), col(COL_NA_V, h))
        o_ref[:, 2 * BRANCH_WIDTH + h * hd:2 * BRANCH_WIDTH + (h + 1) * hd] = o.astype(o_ref.dtype)
    for h in range(4):
        o = _softmax_attend(col(COL_SWA_Q, h), col(COL_SWA_K, h // 2), col(COL_SWA_V, h // 2),
                            sink=sink_ref[h:h + 1, 0:1])
        o_ref[:, 3 * BRANCH_WIDTH + h * hd:3 * BRANCH_WIDTH + (h + 1) * hd] = o.astype(o_ref.dtype)


def _ctx_attn_call(p_ctx, lam_p, subg, sink_tab, lambda_init):
    b, l, cols = p_ctx.shape
    return pl.pallas_call(
        functools.partial(_ctx_attn_kernel, lambda_init=lambda_init),
        grid=(b,),
        in_specs=[pl.BlockSpec((None, l, cols), lambda i: (i, 0, 0)),
                  pl.BlockSpec((4, DIFF_QK_DIM), lambda i: (0, 0)),
                  pl.BlockSpec((1, HEAD_DIM), lambda i: (0, 0)),
                  pl.BlockSpec((4, LANES), lambda i: (0, 0))],
        out_specs=pl.BlockSpec((None, l, N_BRANCHES * BRANCH_WIDTH), lambda i: (i, 0, 0)),
        out_shape=jax.ShapeDtypeStruct((b, l, N_BRANCHES * BRANCH_WIDTH), BF16),
        compiler_params=_cparams(("parallel",)),
        name="attn_ctx",
    )(p_ctx, lam_p, subg.reshape(1, HEAD_DIM), sink_tab.reshape(4, LANES))


MERGE_CHUNKS = 4


def _merge_kernel(o0, o1, o2, o3, g_ref, wb_ref, wo_ref, x_ref, mod_ref, ng_ref, *rest, with_router):
    if with_router:
        router_ref, xo_ref, h_ref, lg_ref = rest
    else:
        xo_ref, h_ref = rest
    d = x_ref.shape[1]
    dc = d // MERGE_CHUNKS
    o_vals = [o[...] for o in (o0, o1, o2, o3)]
    z = None
    for c in range(MERGE_CHUNKS):
        s = None
        for i, o in enumerate(o_vals):
            y = _dot(o, wb_ref[i, :, c * dc:(c + 1) * dc])
            gate = g_ref[:, i * d + c * dc:i * d + (c + 1) * dc].astype(F32)
            term = jax.nn.sigmoid(gate) * y
            s = term if s is None else s + term
        zc = _dot(s.astype(BF16), wo_ref[c * dc:(c + 1) * dc, :])
        z = zc if z is None else z + zc
    xn = x_ref[...] + mod_ref[2:3, :] * z
    xo_ref[...] = xn
    hn = _norm_mod(xn, ng_ref[...], mod_ref, 3, 4)
    h_ref[...] = hn.astype(h_ref.dtype)
    if with_router:
        lg_ref[...] = _dot_split(hn, router_ref[...])


def _dot_split(a, b):
    a_hi = a.astype(BF16)
    a_lo = (a - a_hi.astype(F32)).astype(BF16)
    b_hi = b.astype(BF16)
    b_lo = (b - b_hi.astype(F32)).astype(BF16)
    n = b.shape[1]
    both = _dot(a_hi, jnp.concatenate([b_hi, b_lo], axis=1))
    return both[:, :n] + both[:, n:] + _dot(a_lo, b_hi)


def _merge_call(o_parts, gates, w_branch, w_out, x, mod, next_g, router=None):
    b, r, d = x.shape
    tm = min(r, 256)
    with_router = router is not None
    whole = dict(pipeline_mode=pl.Buffered(1))

    def o_spec(cb):
        return pl.BlockSpec((None, tm, BRANCH_WIDTH), lambda i, t: (i, t, cb))

    row_spec = pl.BlockSpec((None, tm, d), lambda i, t: (i, t, 0))
    in_specs = [o_spec(cb) for _, cb in o_parts] + [
        pl.BlockSpec((None, tm, N_BRANCHES * d), lambda i, t: (i, t, 0)),
        pl.BlockSpec((N_BRANCHES, BRANCH_WIDTH, d), lambda i, t: (0, 0, 0), **whole),
        pl.BlockSpec((d, d), lambda i, t: (0, 0), **whole),
        row_spec,
        pl.BlockSpec((None, 6, d), lambda i, t: (i, 0, 0)),
        pl.BlockSpec((1, d), lambda i, t: (0, 0))]
    args = [a for a, _ in o_parts] + [gates, w_branch, w_out, x, mod, next_g.reshape(1, d)]
    out_specs = [row_spec, row_spec]
    out_shape = [jax.ShapeDtypeStruct((b, r, d), F32), jax.ShapeDtypeStruct((b, r, d), BF16)]
    if with_router:
        in_specs.append(pl.BlockSpec((d, LANES), lambda i, t: (0, 0), **whole))
        args.append(router)
        out_specs.append(pl.BlockSpec((None, tm, LANES), lambda i, t: (i, t, 0)))
        out_shape.append(jax.ShapeDtypeStruct((b, r, LANES), F32))
    return pl.pallas_call(
        functools.partial(_merge_kernel, with_router=with_router),
        grid=(b, r // tm),
        in_specs=in_specs,
        out_specs=out_specs,
        out_shape=out_shape,
        compiler_params=_cparams(("parallel", "parallel")),
        name="merge",
    )(*args)


def _swiglu_partial(h, wg, wu, wd):
    gate = _dot(h, wg)
    up = _dot(h, wu)
    act = gate * jax.nn.sigmoid(gate) * up
    return _dot(act.astype(BF16), wd)


def _ffn_kernel(h_ref, wg_ref, wu_ref, wd_ref, x_ref, mod_ref, ng_ref, nmod_ref, xo_ref, hn_ref,
                acc_ref):
    f = pl.program_id(2)

    @pl.when(f == 0)
    def _():
        acc_ref[...] = jnp.zeros_like(acc_ref)

    acc_ref[...] += _swiglu_partial(h_ref[...], wg_ref[...], wu_ref[...], wd_ref[...])

    @pl.when(f == pl.num_programs(2) - 1)
    def _():
        xn = x_ref[...] + mod_ref[5:6, :] * acc_ref[...]
        xo_ref[...] = xn
        hn_ref[...] = _norm_mod(xn, ng_ref[...], nmod_ref, 0, 1).astype(hn_ref.dtype)


def _ffn_call(h, wg, wu, wd, x, mod, next_g, next_mod):
    b, r, d = x.shape
    dff = wg.shape[1]
    tf = 512
    tm = min(r, 512)
    row_spec = pl.BlockSpec((None, tm, d), lambda i, t, f: (i, t, 0))
    mod_spec = pl.BlockSpec((None, 6, d), lambda i, t, f: (i, 0, 0))
    return pl.pallas_call(
        _ffn_kernel,
        grid=(b, r // tm, dff // tf),
        in_specs=[row_spec,
                  pl.BlockSpec((d, tf), lambda i, t, f: (0, f)),
                  pl.BlockSpec((d, tf), lambda i, t, f: (0, f)),
                  pl.BlockSpec((tf, d), lambda i, t, f: (f, 0)),
                  row_spec, mod_spec,
                  pl.BlockSpec((1, d), lambda i, t, f: (0, 0)),
                  mod_spec],
        out_specs=[row_spec, row_spec],
        out_shape=[jax.ShapeDtypeStruct((b, r, d), F32), jax.ShapeDtypeStruct((b, r, d), BF16)],
        scratch_shapes=[pltpu.VMEM((tm, d), F32)],
        compiler_params=_cparams(("parallel", "parallel", "arbitrary")),
        name="ffn_dense",
    )(h, wg, wu, wd, x, mod, next_g.reshape(1, d), next_mod)


def _route_kernel(lg_ref, idx_ref, w_ref):
    lg = lg_ref[...]
    lane = lax.broadcasted_iota(jnp.int32, lg.shape, 1)
    valid = lane < N_EXPERTS
    mx = jnp.max(jnp.where(valid, lg, -jnp.inf), axis=-1, keepdims=True)
    e = jnp.where(valid, jnp.exp(lg - mx), 0.0)
    p = e / jnp.sum(e, axis=-1, keepdims=True)
    p1 = jnp.max(p, axis=-1, keepdims=True)
    i1 = jnp.min(jnp.where(p == p1, lane, LANES), axis=-1, keepdims=True)
    rest = jnp.where(jnp.logical_or(lane == i1, jnp.logical_not(valid)), -1.0, p)
    p2 = jnp.max(rest, axis=-1, keepdims=True)
    i2 = jnp.min(jnp.where(rest == p2, lane, LANES), axis=-1, keepdims=True)
    tot = p1 + p2
    idx_ref[...] = jnp.where(lane == 0, i1, jnp.where(lane == 1, i2, 0))
    w_ref[...] = jnp.where(lane == 0, p1 / tot, jnp.where(lane == 1, p2 / tot, 0.0))


def _route_call(logits):
    r = logits.shape[0]
    tm = min(r, 1024)
    spec = pl.BlockSpec((tm, LANES), lambda i: (i, 0))
    return pl.pallas_call(
        _route_kernel,
        grid=(r // tm,),
        in_specs=[spec],
        out_specs=[spec, spec],
        out_shape=[jax.ShapeDtypeStruct((r, LANES), jnp.int32), jax.ShapeDtypeStruct((r, LANES), F32)],
        compiler_params=_cparams(("parallel",)),
        name="route_top2",
    )(logits)


MOE_HALF = 512
MOE_TILE = 4 * MOE_HALF
MOE_FF_TILE = 256
MOE_VMEM_LIMIT = 60 * 1024 * 1024


def _moe_kernel(te_ref, nr_ref, x_ref, wg_ref, wu_ref, wd_ref, o_ref, acc_ref):
    i = pl.program_id(0)
    f = pl.program_id(1)

    @pl.when(f == 0)
    def _():
        acc_ref[...] = jnp.zeros_like(acc_ref)

    for half in range(MOE_TILE // MOE_HALF):
        rows = slice(half * MOE_HALF, (half + 1) * MOE_HALF)

        @pl.when(nr_ref[i] > half * MOE_HALF)
        def _():
            acc_ref[rows, :] += _swiglu_partial(x_ref[rows, :], wg_ref[...].astype(BF16),
                                                wu_ref[...].astype(BF16), wd_ref[...].astype(BF16))

    @pl.when(f == pl.num_programs(1) - 1)
    def _():
        o_ref[...] = acc_ref[...].astype(o_ref.dtype)


def _moe_call(xs, wg, wu, wd, tile_expert, tile_rows):
    r, d = xs.shape
    dff = wg.shape[2]
    tf = MOE_FF_TILE
    nf = dff // tf

    def fidx(i, f, nr):
        return jnp.where(nr[i] > 0, f, nf - 1)

    grid_spec = pltpu.PrefetchScalarGridSpec(
        num_scalar_prefetch=2,
        grid=(r // MOE_TILE, nf),
        in_specs=[pl.BlockSpec((MOE_TILE, d), lambda i, f, te, nr: (i, 0), pipeline_mode=pl.Buffered(1)),
                  pl.BlockSpec((None, d, tf), lambda i, f, te, nr: (te[i], 0, fidx(i, f, nr))),
                  pl.BlockSpec((None, d, tf), lambda i, f, te, nr: (te[i], 0, fidx(i, f, nr))),
                  pl.BlockSpec((None, tf, d), lambda i, f, te, nr: (te[i], fidx(i, f, nr), 0))],
        out_specs=pl.BlockSpec((MOE_TILE, d), lambda i, f, te, nr: (i, 0), pipeline_mode=pl.Buffered(1)),
        scratch_shapes=[pltpu.VMEM((MOE_TILE, d), F32)])
    return pl.pallas_call(
        _moe_kernel,
        grid_spec=grid_spec,
        out_shape=jax.ShapeDtypeStruct((r, d), BF16),
        compiler_params=pltpu.CompilerParams(dimension_semantics=("arbitrary", "arbitrary"),
                                             vmem_limit_bytes=MOE_VMEM_LIMIT),
        name="moe_experts",
    )(tile_expert, tile_rows, xs, wg, wu, wd)


def _moe_plan(idx):
    t = idx.shape[0]
    a = t * TOP_K
    flat_e = idx.reshape(a)
    onehot = (flat_e[:, None] == jnp.arange(N_EXPERTS, dtype=jnp.int32)[None, :]).astype(jnp.int32)
    csum = jnp.cumsum(onehot, axis=0)
    rank = jnp.sum((csum - onehot) * onehot, axis=1)
    counts = csum[-1]
    padded = ((counts + MOE_TILE - 1) // MOE_TILE) * MOE_TILE
    ends = jnp.cumsum(padded)
    offs = ends - padded
    dest = offs[flat_e] + rank
    rows = a + N_EXPERTS * MOE_TILE
    src_tok = (jnp.arange(rows, dtype=jnp.int32) % t).at[dest].set(jnp.arange(a, dtype=jnp.int32) // TOP_K)
    tile_start = jnp.arange(rows // MOE_TILE, dtype=jnp.int32) * MOE_TILE
    past = jnp.sum((tile_start[:, None] >= ends[None, :]).astype(jnp.int32), axis=1)
    tile_expert = jnp.minimum(past, N_EXPERTS - 1)
    tile_rows = jnp.where(past < N_EXPERTS,
                          jnp.clip((offs + counts)[tile_expert] - tile_start, 0, MOE_TILE), 0)
    return dest.reshape(t, TOP_K), src_tok, tile_expert, tile_rows.astype(jnp.int32)


def _final_kernel(x_ref, y0_ref, y1_ref, w_ref, mod_ref, g_ref, o_ref):
    y = w_ref[:, 0:1] * y0_ref[...].astype(F32) + w_ref[:, 1:2] * y1_ref[...].astype(F32)
    xn = x_ref[...] + mod_ref[5:6, :] * y
    o_ref[...] = _rms(xn) * g_ref[...]


def _final_call(x, y0, y1, w, mod, g):
    b, r, d = x.shape
    tm = min(r, 512)
    row_spec = pl.BlockSpec((None, tm, d), lambda i, t: (i, t, 0))
    return pl.pallas_call(
        _final_kernel,
        grid=(b, r // tm),
        in_specs=[row_spec, row_spec, row_spec,
                  pl.BlockSpec((None, tm, LANES), lambda i, t: (i, t, 0)),
                  pl.BlockSpec((None, 6, d), lambda i, t: (i, 0, 0)),
                  pl.BlockSpec((1, d), lambda i, t: (0, 0))],
        out_specs=row_spec,
        out_shape=jax.ShapeDtypeStruct((b, r, d), F32),
        compiler_params=_cparams(("parallel", "parallel")),
        name="final_norm",
    )(x, y0, y1, w, mod, g.reshape(1, d))


def _rope_tables(n):
    pos = jnp.arange(n)
    rows = (pos // GRID_W).astype(F32)
    cols = (pos % GRID_W).astype(F32)
    lane = np.arange(LANES)
    out = []
    for hw in (32, 16):
        period = 4 * hw
        u = lane % period
        use_cols = (u // (2 * hw)) == 1
        w = u % (2 * hw)
        freqs = jnp.asarray(ROPE_THETA ** (-(w % hw).astype(np.float32) / hw), F32)
        p = jnp.where(jnp.asarray(use_cols)[None, :], cols[:, None], rows[:, None])
        ang = p * freqs[None, :]
        sign = jnp.asarray(np.where(w < hw, -1.0, 1.0), F32)
        out += [jnp.cos(ang), jnp.sin(ang) * sign[None, :]]
    return out


def kernel(x, c, ctx, c_ctx, attn_norm_g, ffn_norm_g, ada_w, ada_b, w_in, qk_norm_g, diff_lambda,
           diff_subln_g, na_rpb, swa_sink, w_branch, w_out, ffn_w_gate, ffn_w_up, ffn_w_down,
           moe_router, moe_w_gate, moe_w_up, moe_w_down, final_norm_g):
    b, n, d = x.shape
    l = ctx.shape[1]
    depth = w_in.shape[0]
    assert depth == 2, "laid out for one dense layer followed by one routed last layer"
    rows_n = n // GRID_W

    lat_tables = _rope_tables(n)
    ones = jnp.ones((l, LANES), F32)
    zeros = jnp.zeros((l, LANES), F32)
    ctx_tables = [ones, zeros, ones, zeros]

    cvec = jnp.zeros((8, d), F32).at[:b].set(c).at[b].set(c_ctx)
    mods = []
    for i in range(depth):
        m = _adaln(cvec, ada_w, ada_b, i).reshape(8, 6, d)
        mods.append((m[:b], jnp.broadcast_to(m[b:b + 1], (b, 6, d))))

    def mixers(i, h_lat, h_ctx):
        lambda_init = 0.8 - 0.6 * math.exp(-0.3 * i)
        w_qkv = _qkv_weights_call(w_in, i)
        sink_tab = jnp.broadcast_to((swa_sink[i].astype(F32) * LOG2E)[:, None], (4, LANES))
        p_lat = _qkv_call(h_lat, w_qkv, lat_tables, qk_norm_g[i])
        p_ctx = _qkv_call(h_ctx, w_qkv, ctx_tables, qk_norm_g[i])
        o_lat = [_gqa_call(p_lat, p_ctx),
                 _diff_call(p_lat, p_ctx, diff_lambda[i], diff_subln_g[i], lambda_init),
                 _na_call(p_lat, p_ctx, _na_bias_tables(na_rpb[i])),
                 _swa_call(p_lat, p_ctx, sink_tab.reshape(2, 2, LANES))]
        return p_ctx, o_lat, sink_tab, lambda_init

    mod_lat, mod_ctx = mods[0]
    h_lat = _norm_mod_call(x, attn_norm_g[0], mod_lat, 0, 1)
    h_ctx = _norm_mod_call(ctx, attn_norm_g[0], mod_ctx, 0, 1)
    p_ctx, o_lat, sink_tab, lambda_init = mixers(0, h_lat, h_ctx)
    wb = w_branch[0].astype(BF16)
    wo = w_out[0].astype(BF16)
    x_lat, h_lat = _merge_call([(o, 0) for o in o_lat], _gate_proj_call(h_lat, w_in, 0), wb, wo, x,
                               mod_lat, ffn_norm_g[0])
    o_ctx = _ctx_attn_call(p_ctx, diff_lambda[0], diff_subln_g[0], sink_tab, lambda_init)
    x_ctx, h_ctx = _merge_call([(o_ctx, k) for k in range(N_BRANCHES)], _gate_proj_call(h_ctx, w_in, 0),
                               wb, wo, ctx, mod_ctx, ffn_norm_g[0])
    wg = ffn_w_gate[0].astype(BF16)
    wu = ffn_w_up[0].astype(BF16)
    wd = ffn_w_down[0].astype(BF16)
    _, h_ctx = _ffn_call(h_ctx, wg, wu, wd, x_ctx, mod_ctx, attn_norm_g[1], mods[1][1])
    x_lat, h_lat = _ffn_call(h_lat, wg, wu, wd, x_lat, mod_lat, attn_norm_g[1], mods[1][0])

    mod_lat, _ = mods[1]
    _, o_lat, _, _ = mixers(1, h_lat, h_ctx)
    router = jnp.zeros((d, LANES), F32).at[:, :N_EXPERTS].set(moe_router[0])
    x_lat, h_lat, logits = _merge_call(
        [(o, 0) for o in o_lat], _gate_proj_call(h_lat, w_in, 1),
        w_branch[1].astype(BF16), w_out[1].astype(BF16), x_lat, mod_lat, ffn_norm_g[1], router)

    idx_pad, wts_pad = _route_call(logits.reshape(b * n, LANES))
    dest, src_tok, tile_expert, tile_rows = _moe_plan(idx_pad[:, :TOP_K])
    xs = h_lat.reshape(b * n, d).at[src_tok].get(mode="promise_in_bounds")
    ys = _moe_call(xs, moe_w_gate[0], moe_w_up[0], moe_w_down[0], tile_expert, tile_rows)
    y0 = ys.at[dest[:, 0]].get(mode="promise_in_bounds").reshape(b, n, d)
    y1 = ys.at[dest[:, 1]].get(mode="promise_in_bounds").reshape(b, n, d)
    return _final_call(x_lat, y0, y1, wts_pad.reshape(b, n, LANES), mod_lat, final_norm_g)
```

```python
import functools
import math

import numpy as np
import jax
import jax.numpy as jnp
from jax import lax
from jax.experimental import pallas as pl
from jax.experimental.pallas import tpu as pltpu

F32 = jnp.float32
BF16 = jnp.bfloat16

GRID_W = 64
HEAD_DIM = 128
N_BRANCHES = 4
BRANCH_WIDTH = 4 * HEAD_DIM
DIFF_QK_DIM = 64
NA_ROWS = 8
NA_COLS = 16
SWA_WINDOW = 128
N_EXPERTS = 8
TOP_K = 2
NORM_EPS = 1e-6
ROPE_THETA = 10000.0
NEG_INF = -1e30
LOG2E = math.log2(math.e)

QKV_COLS = 5120
COL_GQA_Q, COL_GQA_K, COL_GQA_V = 0, 512, 768
COL_DIFF_Q, COL_DIFF_K, COL_DIFF_V = 1024, 1536, 2048
COL_NA_Q, COL_NA_K, COL_NA_V = 2560, 3072, 3584
COL_SWA_Q, COL_SWA_K, COL_SWA_V = 4096, 4608, 4864

LANES = 128
VMEM_LIMIT = 56 * 1024 * 1024

QS128 = HEAD_DIM ** -0.5 * LOG2E
QS64 = DIFF_QK_DIM ** -0.5 * LOG2E


def _cparams(sem):
    return pltpu.CompilerParams(dimension_semantics=sem, vmem_limit_bytes=VMEM_LIMIT)


def _dot(a, b):
    return jnp.dot(a, b, preferred_element_type=F32)


def _dot_nt(a, b):
    return lax.dot_general(a, b, (((1,), (1,)), ((), ())), preferred_element_type=F32)


def _rms(x):
    return x * lax.rsqrt(jnp.mean(x * x, axis=-1, keepdims=True) + NORM_EPS)


def _norm_mod(x, g, mod_ref, shift_idx, scale_idx):
    y = _rms(x) * g
    return y * (1.0 + mod_ref[scale_idx:scale_idx + 1, :]) + mod_ref[shift_idx:shift_idx + 1, :]


def _adaln_kernel(c_ref, w_ref, b_ref, o_ref):
    c = c_ref[...]
    s = c * jax.nn.sigmoid(c)
    o_ref[...] = jnp.dot(s, w_ref[...], preferred_element_type=F32,
                         precision=lax.Precision.HIGHEST) + b_ref[...]


def _adaln(cvec, w, b, layer):
    rows, d = cvec.shape
    cols = w.shape[2]
    tn = 2048 if cols % 2048 == 0 else cols
    return pl.pallas_call(
        _adaln_kernel,
        grid=(cols // tn,),
        in_specs=[pl.BlockSpec((rows, d), lambda j: (0, 0)),
                  pl.BlockSpec((None, d, tn), lambda j: (layer, 0, j)),
                  pl.BlockSpec((None, 1, tn), lambda j: (layer, 0, j))],
        out_specs=pl.BlockSpec((rows, tn), lambda j: (0, j)),
        out_shape=jax.ShapeDtypeStruct((rows, cols), F32),
        compiler_params=_cparams(("arbitrary",)),
        name="adaln",
    )(cvec, w, b.reshape(b.shape[0], 1, cols))


def _norm_mod_kernel(x_ref, g_ref, mod_ref, h_ref, *, shift_idx, scale_idx):
    h_ref[...] = _norm_mod(x_ref[...], g_ref[...], mod_ref, shift_idx, scale_idx).astype(BF16)


def _norm_mod_call(x, g, mod, shift_idx, scale_idx):
    b, r, d = x.shape
    tm = min(r, 512)
    return pl.pallas_call(
        functools.partial(_norm_mod_kernel, shift_idx=shift_idx, scale_idx=scale_idx),
        grid=(b, r // tm),
        in_specs=[pl.BlockSpec((None, tm, d), lambda i, t: (i, t, 0)),
                  pl.BlockSpec((1, d), lambda i, t: (0, 0)),
                  pl.BlockSpec((None, 6, d), lambda i, t: (i, 0, 0))],
        out_specs=pl.BlockSpec((None, tm, d), lambda i, t: (i, t, 0)),
        out_shape=jax.ShapeDtypeStruct((b, r, d), BF16),
        compiler_params=_cparams(("parallel", "parallel")),
        name="norm_mod",
    )(x, g.reshape(1, d), mod)


QKV_TILE = 4 * LANES
_PLAIN = (None, None, 1.0)
_QKV_TILE_OPS = {
    0: [(0, 32, QS128)] * 4,
    1: [(1, 32, 1.0)] * 2 + [_PLAIN] * 2,
    2: [(None, 16, QS64)] * 4,
    3: [(None, 16, 1.0)] * 4,
    5: [(None, None, QS128)] * 4,
    8: [(None, 32, QS128)] * 4,
    9: [(None, 32, 1.0)] * 2 + [_PLAIN] * 2,
}


def _rope(y, cos, sin, hw):
    lane = lax.broadcasted_iota(jnp.int32, y.shape, 1)
    first = (lane % (2 * hw)) < hw
    partner = jnp.where(first, pltpu.roll(y, LANES - hw, 1), pltpu.roll(y, hw, 1))
    return y * cos + partner * sin


def _qkv_kernel(h_ref, w_ref, cos32_ref, sin32_ref, cos16_ref, sin16_ref, g_ref, o_ref):
    h = h_ref[...]
    for j in range(o_ref.shape[1] // QKV_TILE):
        acc = _dot(h, w_ref[:, j * QKV_TILE:(j + 1) * QKV_TILE])
        for c, (norm_row, hw, scale) in enumerate(_QKV_TILE_OPS.get(j, [_PLAIN] * 4)):
            y = acc[:, c * LANES:(c + 1) * LANES]
            if norm_row is not None:
                y = _rms(y) * g_ref[norm_row:norm_row + 1, :]
            if hw == 32:
                y = _rope(y, cos32_ref[...], sin32_ref[...], 32)
            elif hw == 16:
                y = _rope(y, cos16_ref[...], sin16_ref[...], 16)
            if scale != 1.0:
                y = y * scale
            col = j * QKV_TILE + c * LANES
            o_ref[:, col:col + LANES] = y.astype(o_ref.dtype)


def _qkv_call(h, w_qkv, tables, qk_g):
    b, r, d = h.shape
    cols = w_qkv.shape[1]
    tm = min(r, 512)
    tab_spec = pl.BlockSpec((tm, LANES), lambda i, t: (t, 0))
    return pl.pallas_call(
        _qkv_kernel,
        grid=(b, r // tm),
        in_specs=[pl.BlockSpec((None, tm, d), lambda i, t: (i, t, 0)),
                  pl.BlockSpec((d, cols), lambda i, t: (0, 0), pipeline_mode=pl.Buffered(1)),
                  tab_spec, tab_spec, tab_spec, tab_spec,
                  pl.BlockSpec((2, LANES), lambda i, t: (0, 0))],
        out_specs=pl.BlockSpec((None, tm, cols), lambda i, t: (i, t, 0)),
        out_shape=jax.ShapeDtypeStruct((b, r, cols), BF16),
        compiler_params=_cparams(("parallel", "parallel")),
        name="qkv_proj",
    )(h, w_qkv, *tables, qk_g)


def _matmul_kernel(h_ref, w_ref, o_ref):
    o_ref[...] = _dot(h_ref[...], w_ref[...].astype(BF16)).astype(o_ref.dtype)


def _gate_proj_call(h, w_in, layer):
    b, r, d = h.shape
    tn = 512
    cols = w_in.shape[2] - QKV_COLS
    col0 = QKV_COLS // tn
    tm = 2048 if r % 2048 == 0 else r
    return pl.pallas_call(
        _matmul_kernel,
        grid=(b, r // tm, cols // tn),
        in_specs=[pl.BlockSpec((None, tm, d), lambda i, t, j: (i, t, 0)),
                  pl.BlockSpec((None, d, tn), lambda i, t, j: (layer, 0, col0 + j))],
        out_specs=pl.BlockSpec((None, tm, tn), lambda i, t, j: (i, t, j)),
        out_shape=jax.ShapeDtypeStruct((b, r, cols), BF16),
        compiler_params=_cparams(("parallel", "parallel", "arbitrary")),
        name="gate_proj",
    )(h, w_in)


def _cast_kernel(w_ref, o_ref):
    o_ref[...] = w_ref[...].astype(o_ref.dtype)


def _qkv_weights_call(w_in, layer):
    d = w_in.shape[1]
    return pl.pallas_call(
        _cast_kernel,
        grid=(QKV_COLS // QKV_TILE,),
        in_specs=[pl.BlockSpec((None, d, QKV_TILE), lambda j: (layer, 0, j))],
        out_specs=pl.BlockSpec((d, QKV_TILE), lambda j: (0, j)),
        out_shape=jax.ShapeDtypeStruct((d, QKV_COLS), BF16),
        compiler_params=_cparams(("parallel",)),
        name="qkv_weights",
    )(w_in)


def _diff_lambda(lam_ref, lambda_init):
    a = jnp.sum(lam_ref[0:1, :] * lam_ref[1:2, :], axis=-1, keepdims=True)
    b = jnp.sum(lam_ref[2:3, :] * lam_ref[3:4, :], axis=-1, keepdims=True)
    return jnp.exp(a) - jnp.exp(b) + lambda_init


def _split_maps(q):
    lane = lax.broadcasted_iota(jnp.int32, q.shape, 1)
    zero = jnp.zeros_like(q)
    return jnp.concatenate([jnp.where(lane < DIFF_QK_DIM, q, zero),
                            jnp.where(lane >= DIFF_QK_DIM, q, zero)], axis=0)


FLASH_ROW_BLOCK = 256


def _flash_pair(q2, kc_ref, vc_ref, kl_ref, vl_ref, tk):
    m_rows = q2.shape[0]
    n_lat = kl_ref.shape[0]
    nb = m_rows // FLASH_ROW_BLOCK
    qs = [q2[i * FLASH_ROW_BLOCK:(i + 1) * FLASH_ROW_BLOCK] for i in range(nb)]

    def step(k, v, carry):
        v1 = jnp.concatenate([v, jnp.ones_like(v)], axis=1)
        out = []
        for q, (m, acc) in zip(qs, carry):
            s = _dot_nt(q, k)
            m_new = jnp.maximum(m, jnp.max(s, axis=-1, keepdims=True))
            p = jnp.exp2(s - m_new)
            acc = jnp.exp2(m - m_new) * acc + _dot(p.astype(BF16), v1)
            out.append((m_new, acc))
        return tuple(out)

    carry = tuple((jnp.full((FLASH_ROW_BLOCK, 1), NEG_INF, F32),
                   jnp.zeros((FLASH_ROW_BLOCK, 2 * HEAD_DIM), F32)) for _ in range(nb))
    carry = step(kc_ref[...], vc_ref[...], carry)

    for c in range(n_lat // tk):
        carry = step(kl_ref[c * tk:(c + 1) * tk, :], vl_ref[c * tk:(c + 1) * tk, :], carry)
    return jnp.concatenate([acc[:, :HEAD_DIM] / acc[:, HEAD_DIM:] for _, acc in carry], axis=0)


def _gqa_kernel(q_ref, kc_ref, vc_ref, kl_ref, vl_ref, o_ref, *, tk):
    tq = q_ref.shape[0]
    q = q_ref[...]
    q2 = jnp.concatenate([q[:, :HEAD_DIM], q[:, HEAD_DIM:]], axis=0)
    o = _flash_pair(q2, kc_ref, vc_ref, kl_ref, vl_ref, tk)
    o_ref[:, :HEAD_DIM] = o[:tq].astype(o_ref.dtype)
    o_ref[:, HEAD_DIM:] = o[tq:].astype(o_ref.dtype)


def _diff_kernel(q_ref, kc_ref, vc_ref, kl_ref, vl_ref, lam_ref, subg_ref, o_ref, *, tk, lambda_init):
    tq = q_ref.shape[0]
    o = _flash_pair(_split_maps(q_ref[...]), kc_ref, vc_ref, kl_ref, vl_ref, tk)
    lam = _diff_lambda(lam_ref, lambda_init)
    d = o[:tq] - lam * o[tq:]
    o_ref[...] = (_rms(d) * subg_ref[...] * (1.0 - lambda_init)).astype(o_ref.dtype)


def _gqa_call(p_lat, p_ctx):
    b, n, _ = p_lat.shape
    l = p_ctx.shape[1]
    tq = 512
    tk = 512
    hd = HEAD_DIM
    kcol, vcol = COL_GQA_K // hd, COL_GQA_V // hd
    return pl.pallas_call(
        functools.partial(_gqa_kernel, tk=tk),
        grid=(b, 2, n // tq),
        in_specs=[pl.BlockSpec((None, tq, 2 * hd), lambda i, k, t: (i, t, k)),
                  pl.BlockSpec((None, l, hd), lambda i, k, t: (i, 0, kcol + k)),
                  pl.BlockSpec((None, l, hd), lambda i, k, t: (i, 0, vcol + k)),
                  pl.BlockSpec((None, n, hd), lambda i, k, t: (i, 0, kcol + k)),
                  pl.BlockSpec((None, n, hd), lambda i, k, t: (i, 0, vcol + k))],
        out_specs=pl.BlockSpec((None, tq, 2 * hd), lambda i, k, t: (i, t, k)),
        out_shape=jax.ShapeDtypeStruct((b, n, BRANCH_WIDTH), BF16),
        compiler_params=_cparams(("parallel", "parallel", "arbitrary")),
        name="attn_gqa",
    )(p_lat, p_ctx, p_ctx, p_lat, p_lat)


def _diff_call(p_lat, p_ctx, lam_p, subg, lambda_init):
    b, n, _ = p_lat.shape
    l = p_ctx.shape[1]
    tq = 512
    tk = 512
    hd = HEAD_DIM
    qcol, kcol, vcol = COL_DIFF_Q // hd, COL_DIFF_K // hd, COL_DIFF_V // hd
    return pl.pallas_call(
        functools.partial(_diff_kernel, tk=tk, lambda_init=lambda_init),
        grid=(b, 4, n // tq),
        in_specs=[pl.BlockSpec((None, tq, hd), lambda i, h, t: (i, t, qcol + h)),
                  pl.BlockSpec((None, l, hd), lambda i, h, t: (i, 0, kcol + h)),
                  pl.BlockSpec((None, l, hd), lambda i, h, t: (i, 0, vcol + h)),
                  pl.BlockSpec((None, n, hd), lambda i, h, t: (i, 0, kcol + h)),
                  pl.BlockSpec((None, n, hd), lambda i, h, t: (i, 0, vcol + h)),
                  pl.BlockSpec((4, DIFF_QK_DIM), lambda i, h, t: (0, 0)),
                  pl.BlockSpec((1, hd), lambda i, h, t: (0, 0))],
        out_specs=pl.BlockSpec((None, tq, hd), lambda i, h, t: (i, t, h)),
        out_shape=jax.ShapeDtypeStruct((b, n, BRANCH_WIDTH), BF16),
        compiler_params=_cparams(("parallel", "parallel", "arbitrary")),
        name="attn_diff",
    )(p_lat, p_ctx, p_ctx, p_lat, p_lat, lam_p, subg.reshape(1, hd))


NA_GROUP_ROWS = 8
NA_BLOCK_ROWS = 4
NA_BLOCK_WIN_ROWS = NA_BLOCK_ROWS + NA_ROWS
NA_WIN_ROWS = 2 * NA_GROUP_ROWS


def _with_ones(v):
    return jnp.concatenate([v, jnp.ones_like(v)], axis=1)


def _band_ctx_attend(s_b, s_c, vb1, vc1, sink=None):
    m = jnp.maximum(jnp.max(s_b, axis=-1, keepdims=True), jnp.max(s_c, axis=-1, keepdims=True))
    if sink is not None:
        m = jnp.maximum(m, sink)
    acc = _dot(jnp.exp2(s_c - m).astype(BF16), vc1) + _dot(jnp.exp2(s_b - m).astype(BF16), vb1)
    den = acc[:, HEAD_DIM:]
    if sink is not None:
        den = den + jnp.exp2(sink - m)
    return acc[:, :HEAD_DIM] / den


NA_DI = 2 * NA_ROWS - 1
NA_TAB_PAD = NA_WIN_ROWS
NA_TAB_LANES = 3072


def _na_bias_tables(rpb):
    nh = rpb.shape[0]
    qc = np.arange(GRID_W)[:, None]
    kc = np.arange(GRID_W)[None, :]
    cstart = np.clip(qc - NA_COLS // 2, 0, GRID_W - NA_COLS)
    ok_col = (kc >= cstart) & (kc < cstart + NA_COLS)
    dj = np.clip(kc - qc + NA_COLS - 1, 0, 2 * NA_COLS - 2)
    sel_j = (dj[..., None] == np.arange(2 * NA_COLS - 1)) & ok_col[..., None]
    by_col = jnp.einsum("hij,qkj->hqik", rpb.astype(F32), jnp.asarray(sel_j, F32),
                        precision=lax.Precision.HIGHEST)
    by_col = jnp.where(ok_col[None, :, None, :], by_col * LOG2E, NEG_INF)
    strip = by_col.reshape(nh, GRID_W, NA_DI * GRID_W)
    total = NA_TAB_LANES + GRID_W
    strip = jnp.pad(strip, ((0, 0), (0, 0), (NA_TAB_PAD * GRID_W, total - (NA_TAB_PAD + NA_DI) * GRID_W)),
                    constant_values=NEG_INF)
    return jnp.stack([strip[:, :, :NA_TAB_LANES], strip[:, :, GRID_W:]], axis=1)


def _na_kernel(q_ref, kc_ref, vc_ref, kl_ref, vl_ref, tab_ref, o_ref, *, rows_n):
    g = pl.program_id(2)
    win = NA_BLOCK_WIN_ROWS * GRID_W
    blk_q = NA_BLOCK_ROWS * GRID_W
    kc = kc_ref[...]
    vc1 = _with_ones(vc_ref[...])
    lane = lax.broadcasted_iota(jnp.int32, (GRID_W, win), 1)
    for rb in range(NA_GROUP_ROWS // NA_BLOCK_ROWS):
        r0 = g * NA_GROUP_ROWS + rb * NA_BLOCK_ROWS
        ws = jnp.clip(r0 - NA_ROWS // 2, 0, rows_n - NA_BLOCK_WIN_ROWS)
        off = pl.multiple_of(ws * GRID_W, NA_BLOCK_ROWS * GRID_W)
        bias = []
        for a in range(NA_BLOCK_ROWS):
            r = r0 + a
            rs = jnp.clip(r - NA_ROWS // 2, 0, rows_n - NA_ROWS)
            blk = ws - r + NA_ROWS - 1 + NA_TAB_PAD
            start = pl.multiple_of((blk >> 1) * (2 * GRID_W), 2 * GRID_W)
            strip = tab_ref[blk & 1, :, pl.ds(start, win)]
            lo = (rs - ws) * GRID_W
            in_rows = jnp.logical_and(lane >= lo, lane < lo + NA_ROWS * GRID_W)
            bias.append(jnp.where(in_rows, strip, NEG_INF))
        q = q_ref[rb * blk_q:(rb + 1) * blk_q, :]
        s_b = _dot_nt(q, kl_ref[pl.ds(off, win), :]) + jnp.concatenate(bias, axis=0)
        o = _band_ctx_attend(s_b, _dot_nt(q, kc), _with_ones(vl_ref[pl.ds(off, win), :]), vc1)
        o_ref[rb * blk_q:(rb + 1) * blk_q, :] = o.astype(o_ref.dtype)


def _na_call(p_lat, p_ctx, bias):
    b, n, _ = p_lat.shape
    l = p_ctx.shape[1]
    hd = HEAD_DIM
    rows_n = n // GRID_W
    n_groups = rows_n // NA_GROUP_ROWS
    tq = NA_GROUP_ROWS * GRID_W
    qcol, kcol, vcol = COL_NA_Q // hd, COL_NA_K // hd, COL_NA_V // hd

    return pl.pallas_call(
        functools.partial(_na_kernel, rows_n=rows_n),
        grid=(b, 4, n_groups),
        in_specs=[pl.BlockSpec((None, tq, hd), lambda i, h, g: (i, g, qcol + h)),
                  pl.BlockSpec((None, l, hd), lambda i, h, g: (i, 0, kcol + h)),
                  pl.BlockSpec((None, l, hd), lambda i, h, g: (i, 0, vcol + h)),
                  pl.BlockSpec((None, n, hd), lambda i, h, g: (i, 0, kcol + h)),
                  pl.BlockSpec((None, n, hd), lambda i, h, g: (i, 0, vcol + h)),
                  pl.BlockSpec((None, 2, GRID_W, NA_TAB_LANES), lambda i, h, g: (h, 0, 0, 0))],
        out_specs=pl.BlockSpec((None, tq, hd), lambda i, h, g: (i, g, h)),
        out_shape=jax.ShapeDtypeStruct((b, n, BRANCH_WIDTH), BF16),
        compiler_params=_cparams(("parallel", "parallel", "arbitrary")),
        name="attn_na",
    )(p_lat, p_ctx, p_ctx, p_lat, p_lat, bias)


SWA_BLOCK = 256


def _swa_kernel(q_ref, kc_ref, vc_ref, kl_ref, vl_ref, sink_ref, o_ref, *, win):
    t = pl.program_id(2)
    tq = q_ref.shape[0]
    n = kl_ref.shape[0]
    kc = kc_ref[...]
    vc1 = _with_ones(vc_ref[...])
    row = lax.broadcasted_iota(jnp.int32, (SWA_BLOCK, win), 0)
    col = lax.broadcasted_iota(jnp.int32, (SWA_BLOCK, win), 1)
    for rb in range(tq // SWA_BLOCK):
        qs = t * tq + rb * SWA_BLOCK
        ws = pl.multiple_of(jnp.clip(qs - SWA_WINDOW, 0, n - win), SWA_WINDOW)
        kb = kl_ref[pl.ds(ws, win), :]
        vb1 = _with_ones(vl_ref[pl.ds(ws, win), :])
        valid = jnp.abs(row + (qs - ws) - col) <= SWA_WINDOW
        rows = slice(rb * SWA_BLOCK, (rb + 1) * SWA_BLOCK)
        for gi in range(2):
            q = q_ref[rows, gi * HEAD_DIM:(gi + 1) * HEAD_DIM]
            s_b = jnp.where(valid, _dot_nt(q, kb), NEG_INF)
            o = _band_ctx_attend(s_b, _dot_nt(q, kc), vb1, vc1, sink=sink_ref[gi:gi + 1, 0:1])
            o_ref[rows, gi * HEAD_DIM:(gi + 1) * HEAD_DIM] = o.astype(o_ref.dtype)


def _swa_call(p_lat, p_ctx, sink_tab):
    b, n, _ = p_lat.shape
    l = p_ctx.shape[1]
    hd = HEAD_DIM
    tq = 512
    win = SWA_BLOCK + 2 * SWA_WINDOW
    qcol, kcol, vcol = COL_SWA_Q // (2 * hd), COL_SWA_K // hd, COL_SWA_V // hd
    return pl.pallas_call(
        functools.partial(_swa_kernel, win=win),
        grid=(b, 2, n // tq),
        in_specs=[pl.BlockSpec((None, tq, 2 * hd), lambda i, k, t: (i, t, qcol + k)),
                  pl.BlockSpec((None, l, hd), lambda i, k, t: (i, 0, kcol + k)),
                  pl.BlockSpec((None, l, hd), lambda i, k, t: (i, 0, vcol + k)),
                  pl.BlockSpec((None, n, hd), lambda i, k, t: (i, 0, kcol + k)),
                  pl.BlockSpec((None, n, hd), lambda i, k, t: (i, 0, vcol + k)),
                  pl.BlockSpec((None, 2, LANES), lambda i, k, t: (k, 0, 0))],
        out_specs=pl.BlockSpec((None, tq, 2 * hd), lambda i, k, t: (i, t, k)),
        out_shape=jax.ShapeDtypeStruct((b, n, BRANCH_WIDTH), BF16),
        compiler_params=_cparams(("parallel", "parallel", "arbitrary")),
        name="attn_swa",
    )(p_lat, p_ctx, p_ctx, p_lat, p_lat, sink_tab)


def _softmax_attend(q, k, v, sink=None):
    s = _dot_nt(q, k)
    m = jnp.max(s, axis=-1, keepdims=True)
    if sink is not None:
        m = jnp.maximum(m, sink)
    e = jnp.exp2(s - m)
    den = jnp.sum(e, axis=-1, keepdims=True)
    if sink is not None:
        den = den + jnp.exp2(sink - m)
    return _dot(e.astype(BF16), v) / den


def _ctx_attn_kernel(p_ref, lam_ref, subg_ref, sink_ref, o_ref, *, lambda_init):
    hd = HEAD_DIM

    def col(c0, h):
        return p_ref[:, c0 + h * hd:c0 + (h + 1) * hd]

    lam = _diff_lambda(lam_ref, lambda_init)
    for h in range(4):
        o = _softmax_attend(col(COL_GQA_Q, h), col(COL_GQA_K, h // 2), col(COL_GQA_V, h // 2))
        o_ref[:, h * hd:(h + 1) * hd] = o.astype(o_ref.dtype)
    for h in range(4):
        q = col(COL_DIFF_Q, h)
        tq = q.shape[0]
        o2 = _softmax_attend(_split_maps(q), col(COL_DIFF_K, h), col(COL_DIFF_V, h))
        d = o2[:tq] - lam * o2[tq:]
        d = _rms(d) * subg_ref[...] * (1.0 - lambda_init)
        o_ref[:, BRANCH_WIDTH + h * hd:BRANCH_WIDTH + (h + 1) * hd] = d.astype(o_ref.dtype)
    for h in range(4):
        o = _softmax_attend(col(COL_NA_Q, h), col(COL_NA_K, h), col(COL_NA_V, h))
        o_ref[:, 2 * BRANCH_WIDTH + h * hd:2 * BRANCH_WIDTH + (h + 1) * hd] = o.astype(o_ref.dtype)
    for h in range(4):
        o = _softmax_attend(col(COL_SWA_Q, h), col(COL_SWA_K, h // 2), col(COL_SWA_V, h // 2),
                            sink=sink_ref[h:h + 1, 0:1])
        o_ref[:, 3 * BRANCH_WIDTH + h * hd:3 * BRANCH_WIDTH + (h + 1) * hd] = o.astype(o_ref.dtype)


def _ctx_attn_call(p_ctx, lam_p, subg, sink_tab, lambda_init):
    b, l, cols = p_ctx.shape
    return pl.pallas_call(
        functools.partial(_ctx_attn_kernel, lambda_init=lambda_init),
        grid=(b,),
        in_specs=[pl.BlockSpec((None, l, cols), lambda i: (i, 0, 0)),
                  pl.BlockSpec((4, DIFF_QK_DIM), lambda i: (0, 0)),
                  pl.BlockSpec((1, HEAD_DIM), lambda i: (0, 0)),
                  pl.BlockSpec((4, LANES), lambda i: (0, 0))],
        out_specs=pl.BlockSpec((None, l, N_BRANCHES * BRANCH_WIDTH), lambda i: (i, 0, 0)),
        out_shape=jax.ShapeDtypeStruct((b, l, N_BRANCHES * BRANCH_WIDTH), BF16),
        compiler_params=_cparams(("parallel",)),
        name="attn_ctx",
    )(p_ctx, lam_p, subg.reshape(1, HEAD_DIM), sink_tab.reshape(4, LANES))


MERGE_CHUNKS = 4


def _merge_kernel(o0, o1, o2, o3, g_ref, wb_ref, wo_ref, x_ref, mod_ref, ng_ref, *rest, with_router):
    if with_router:
        router_ref, xo_ref, h_ref, lg_ref = rest
    else:
        xo_ref, h_ref = rest
    d = x_ref.shape[1]
    dc = d // MERGE_CHUNKS
    o_vals = [o[...] for o in (o0, o1, o2, o3)]
    z = None
    for c in range(MERGE_CHUNKS):
        s = None
        for i, o in enumerate(o_vals):
            y = _dot(o, wb_ref[i, :, c * dc:(c + 1) * dc])
            gate = g_ref[:, i * d + c * dc:i * d + (c + 1) * dc].astype(F32)
            term = jax.nn.sigmoid(gate) * y
            s = term if s is None else s + term
        zc = _dot(s.astype(BF16), wo_ref[c * dc:(c + 1) * dc, :])
        z = zc if z is None else z + zc
    xn = x_ref[...] + mod_ref[2:3, :] * z
    xo_ref[...] = xn
    hn = _norm_mod(xn, ng_ref[...], mod_ref, 3, 4)
    h_ref[...] = hn.astype(h_ref.dtype)
    if with_router:
        lg_ref[...] = _dot_split(hn, router_ref[...])


def _dot_split(a, b):
    a_hi = a.astype(BF16)
    a_lo = (a - a_hi.astype(F32)).astype(BF16)
    b_hi = b.astype(BF16)
    b_lo = (b - b_hi.astype(F32)).astype(BF16)
    n = b.shape[1]
    both = _dot(a_hi, jnp.concatenate([b_hi, b_lo], axis=1))
    return both[:, :n] + both[:, n:] + _dot(a_lo, b_hi)


def _merge_call(o_parts, gates, w_branch, w_out, x, mod, next_g, router=None):
    b, r, d = x.shape
    tm = min(r, 256)
    with_router = router is not None
    whole = dict(pipeline_mode=pl.Buffered(1))

    def o_spec(cb):
        return pl.BlockSpec((None, tm, BRANCH_WIDTH), lambda i, t: (i, t, cb))

    row_spec = pl.BlockSpec((None, tm, d), lambda i, t: (i, t, 0))
    in_specs = [o_spec(cb) for _, cb in o_parts] + [
        pl.BlockSpec((None, tm, N_BRANCHES * d), lambda i, t: (i, t, 0)),
        pl.BlockSpec((N_BRANCHES, BRANCH_WIDTH, d), lambda i, t: (0, 0, 0), **whole),
        pl.BlockSpec((d, d), lambda i, t: (0, 0), **whole),
        row_spec,
        pl.BlockSpec((None, 6, d), lambda i, t: (i, 0, 0)),
        pl.BlockSpec((1, d), lambda i, t: (0, 0))]
    args = [a for a, _ in o_parts] + [gates, w_branch, w_out, x, mod, next_g.reshape(1, d)]
    out_specs = [row_spec, row_spec]
    out_shape = [jax.ShapeDtypeStruct((b, r, d), F32), jax.ShapeDtypeStruct((b, r, d), BF16)]
    if with_router:
        in_specs.append(pl.BlockSpec((d, LANES), lambda i, t: (0, 0), **whole))
        args.append(router)
        out_specs.append(pl.BlockSpec((None, tm, LANES), lambda i, t: (i, t, 0)))
        out_shape.append(jax.ShapeDtypeStruct((b, r, LANES), F32))
    return pl.pallas_call(
        functools.partial(_merge_kernel, with_router=with_router),
        grid=(b, r // tm),
        in_specs=in_specs,
        out_specs=out_specs,
        out_shape=out_shape,
        compiler_params=_cparams(("parallel", "parallel")),
        name="merge",
    )(*args)


def _swiglu_partial(h, wg, wu, wd):
    gate = _dot(h, wg)
    up = _dot(h, wu)
    act = gate * jax.nn.sigmoid(gate) * up
    return _dot(act.astype(BF16), wd)


def _ffn_kernel(h_ref, wg_ref, wu_ref, wd_ref, x_ref, mod_ref, ng_ref, nmod_ref, xo_ref, hn_ref,
                acc_ref):
    f = pl.program_id(2)

    @pl.when(f == 0)
    def _():
        acc_ref[...] = jnp.zeros_like(acc_ref)

    acc_ref[...] += _swiglu_partial(h_ref[...], wg_ref[...], wu_ref[...], wd_ref[...])

    @pl.when(f == pl.num_programs(2) - 1)
    def _():
        xn = x_ref[...] + mod_ref[5:6, :] * acc_ref[...]
        xo_ref[...] = xn
        hn_ref[...] = _norm_mod(xn, ng_ref[...], nmod_ref, 0, 1).astype(hn_ref.dtype)


def _ffn_call(h, wg, wu, wd, x, mod, next_g, next_mod):
    b, r, d = x.shape
    dff = wg.shape[1]
    tf = 512
    tm = min(r, 512)
    row_spec = pl.BlockSpec((None, tm, d), lambda i, t, f: (i, t, 0))
    mod_spec = pl.BlockSpec((None, 6, d), lambda i, t, f: (i, 0, 0))
    return pl.pallas_call(
        _ffn_kernel,
        grid=(b, r // tm, dff // tf),
        in_specs=[row_spec,
                  pl.BlockSpec((d, tf), lambda i, t, f: (0, f)),
                  pl.BlockSpec((d, tf), lambda i, t, f: (0, f)),
                  pl.BlockSpec((tf, d), lambda i, t, f: (f, 0)),
                  row_spec, mod_spec,
                  pl.BlockSpec((1, d), lambda i, t, f: (0, 0)),
                  mod_spec],
        out_specs=[row_spec, row_spec],
        out_shape=[jax.ShapeDtypeStruct((b, r, d), F32), jax.ShapeDtypeStruct((b, r, d), BF16)],
        scratch_shapes=[pltpu.VMEM((tm, d), F32)],
        compiler_params=_cparams(("parallel", "parallel", "arbitrary")),
        name="ffn_dense",
    )(h, wg, wu, wd, x, mod, next_g.reshape(1, d), next_mod)


def _route_kernel(lg_ref, idx_ref, w_ref):
    lg = lg_ref[...]
    lane = lax.broadcasted_iota(jnp.int32, lg.shape, 1)
    valid = lane < N_EXPERTS
    mx = jnp.max(jnp.where(valid, lg, -jnp.inf), axis=-1, keepdims=True)
    e = jnp.where(valid, jnp.exp(lg - mx), 0.0)
    p = e / jnp.sum(e, axis=-1, keepdims=True)
    p1 = jnp.max(p, axis=-1, keepdims=True)
    i1 = jnp.min(jnp.where(p == p1, lane, LANES), axis=-1, keepdims=True)
    rest = jnp.where(jnp.logical_or(lane == i1, jnp.logical_not(valid)), -1.0, p)
    p2 = jnp.max(rest, axis=-1, keepdims=True)
    i2 = jnp.min(jnp.where(rest == p2, lane, LANES), axis=-1, keepdims=True)
    tot = p1 + p2
    idx_ref[...] = jnp.where(lane == 0, i1, jnp.where(lane == 1, i2, 0))
    w_ref[...] = jnp.where(lane == 0, p1 / tot, jnp.where(lane == 1, p2 / tot, 0.0))


def _route_call(logits):
    r = logits.shape[0]
    tm = min(r, 1024)
    spec = pl.BlockSpec((tm, LANES), lambda i: (i, 0))
    return pl.pallas_call(
        _route_kernel,
        grid=(r // tm,),
        in_specs=[spec],
        out_specs=[spec, spec],
        out_shape=[jax.ShapeDtypeStruct((r, LANES), jnp.int32), jax.ShapeDtypeStruct((r, LANES), F32)],
        compiler_params=_cparams(("parallel",)),
        name="route_top2",
    )(logits)


MOE_HALF = 512
MOE_TILE = 2 * MOE_HALF
MOE_FF_TILE = 512
MOE_VMEM_LIMIT = 60 * 1024 * 1024


def _moe_kernel(te_ref, nr_ref, x_ref, wg_ref, wu_ref, wd_ref, o_ref, acc_ref):
    i = pl.program_id(0)
    f = pl.program_id(1)

    @pl.when(f == 0)
    def _():
        acc_ref[...] = jnp.zeros_like(acc_ref)

    for half in range(MOE_TILE // MOE_HALF):
        rows = slice(half * MOE_HALF, (half + 1) * MOE_HALF)

        @pl.when(nr_ref[i] > half * MOE_HALF)
        def _():
            acc_ref[rows, :] += _swiglu_partial(x_ref[rows, :], wg_ref[...].astype(BF16),
                                                wu_ref[...].astype(BF16), wd_ref[...].astype(BF16))

    @pl.when(f == pl.num_programs(1) - 1)
    def _():
        o_ref[...] = acc_ref[...].astype(o_ref.dtype)


def _moe_call(xs, wg, wu, wd, tile_expert, tile_rows):
    r, d = xs.shape
    dff = wg.shape[2]
    tf = MOE_FF_TILE
    nf = dff // tf

    def fidx(i, f, nr):
        return jnp.where(nr[i] > 0, f, nf - 1)

    grid_spec = pltpu.PrefetchScalarGridSpec(
        num_scalar_prefetch=2,
        grid=(r // MOE_TILE, nf),
        in_specs=[pl.BlockSpec((MOE_TILE, d), lambda i, f, te, nr: (i, 0), pipeline_mode=pl.Buffered(1)),
                  pl.BlockSpec((None, d, tf), lambda i, f, te, nr: (te[i], 0, fidx(i, f, nr))),
                  pl.BlockSpec((None, d, tf), lambda i, f, te, nr: (te[i], 0, fidx(i, f, nr))),
                  pl.BlockSpec((None, tf, d), lambda i, f, te, nr: (te[i], fidx(i, f, nr), 0))],
        out_specs=pl.BlockSpec((MOE_TILE, d), lambda i, f, te, nr: (i, 0), pipeline_mode=pl.Buffered(1)),
        scratch_shapes=[pltpu.VMEM((MOE_TILE, d), F32)])
    return pl.pallas_call(
        _moe_kernel,
        grid_spec=grid_spec,
        out_shape=jax.ShapeDtypeStruct((r, d), BF16),
        compiler_params=pltpu.CompilerParams(dimension_semantics=("arbitrary", "arbitrary"),
                                             vmem_limit_bytes=MOE_VMEM_LIMIT),
        name="moe_experts",
    )(tile_expert, tile_rows, xs, wg, wu, wd)


def _moe_plan(idx):
    t = idx.shape[0]
    a = t * TOP_K
    flat_e = idx.reshape(a)
    onehot = (flat_e[:, None] == jnp.arange(N_EXPERTS, dtype=jnp.int32)[None, :]).astype(jnp.int32)
    csum = jnp.cumsum(onehot, axis=0)
    rank = jnp.sum((csum - onehot) * onehot, axis=1)
    counts = csum[-1]
    padded = ((counts + MOE_TILE - 1) // MOE_TILE) * MOE_TILE
    ends = jnp.cumsum(padded)
    offs = ends - padded
    dest = offs[flat_e] + rank
    rows = a + N_EXPERTS * MOE_TILE
    src_tok = (jnp.arange(rows, dtype=jnp.int32) % t).at[dest].set(jnp.arange(a, dtype=jnp.int32) // TOP_K)
    tile_start = jnp.arange(rows // MOE_TILE, dtype=jnp.int32) * MOE_TILE
    past = jnp.sum((tile_start[:, None] >= ends[None, :]).astype(jnp.int32), axis=1)
    tile_expert = jnp.minimum(past, N_EXPERTS - 1)
    tile_rows = jnp.where(past < N_EXPERTS,
                          jnp.clip((offs + counts)[tile_expert] - tile_start, 0, MOE_TILE), 0)
    return dest.reshape(t, TOP_K), src_tok, tile_expert, tile_rows.astype(jnp.int32)


def _final_kernel(x_ref, y0_ref, y1_ref, w_ref, mod_ref, g_ref, o_ref):
    y = w_ref[:, 0:1] * y0_ref[...].astype(F32) + w_ref[:, 1:2] * y1_ref[...].astype(F32)
    xn = x_ref[...] + mod_ref[5:6, :] * y
    o_ref[...] = _rms(xn) * g_ref[...]


def _final_call(x, y0, y1, w, mod, g):
    b, r, d = x.shape
    tm = min(r, 512)
    row_spec = pl.BlockSpec((None, tm, d), lambda i, t: (i, t, 0))
    return pl.pallas_call(
        _final_kernel,
        grid=(b, r // tm),
        in_specs=[row_spec, row_spec, row_spec,
                  pl.BlockSpec((None, tm, LANES), lambda i, t: (i, t, 0)),
                  pl.BlockSpec((None, 6, d), lambda i, t: (i, 0, 0)),
                  pl.BlockSpec((1, d), lambda i, t: (0, 0))],
        out_specs=row_spec,
        out_shape=jax.ShapeDtypeStruct((b, r, d), F32),
        compiler_params=_cparams(("parallel", "parallel")),
        name="final_norm",
    )(x, y0, y1, w, mod, g.reshape(1, d))


def _rope_tables(n):
    pos = jnp.arange(n)
    rows = (pos // GRID_W).astype(F32)
    cols = (pos % GRID_W).astype(F32)
    lane = np.arange(LANES)
    out = []
    for hw in (32, 16):
        period = 4 * hw
        u = lane % period
        use_cols = (u // (2 * hw)) == 1
        w = u % (2 * hw)
        freqs = jnp.asarray(ROPE_THETA ** (-(w % hw).astype(np.float32) / hw), F32)
        p = jnp.where(jnp.asarray(use_cols)[None, :], cols[:, None], rows[:, None])
        ang = p * freqs[None, :]
        sign = jnp.asarray(np.where(w < hw, -1.0, 1.0), F32)
        out += [jnp.cos(ang), jnp.sin(ang) * sign[None, :]]
    return out


def kernel(x, c, ctx, c_ctx, attn_norm_g, ffn_norm_g, ada_w, ada_b, w_in, qk_norm_g, diff_lambda,
           diff_subln_g, na_rpb, swa_sink, w_branch, w_out, ffn_w_gate, ffn_w_up, ffn_w_down,
           moe_router, moe_w_gate, moe_w_up, moe_w_down, final_norm_g):
    b, n, d = x.shape
    l = ctx.shape[1]
    depth = w_in.shape[0]
    assert depth == 2, "laid out for one dense layer followed by one routed last layer"
    rows_n = n // GRID_W

    lat_tables = _rope_tables(n)
    ctx_flat = ctx.reshape(1, b * l, d)
    ones = jnp.ones((b * l, LANES), F32)
    zeros = jnp.zeros((b * l, LANES), F32)
    ctx_tables = [ones, zeros, ones, zeros]

    cvec = jnp.zeros((8, d), F32).at[:b].set(c).at[b].set(c_ctx)
    mods = []
    for i in range(depth):
        m = _adaln(cvec, ada_w, ada_b, i).reshape(8, 6, d)
        mods.append((m[:b], m[b:b + 1]))

    def mixers(i, h_lat, h_ctx):
        lambda_init = 0.8 - 0.6 * math.exp(-0.3 * i)
        w_qkv = _qkv_weights_call(w_in, i)
        sink_tab = jnp.broadcast_to((swa_sink[i].astype(F32) * LOG2E)[:, None], (4, LANES))
        p_lat = _qkv_call(h_lat, w_qkv, lat_tables, qk_norm_g[i])
        p_ctx = _qkv_call(h_ctx, w_qkv, ctx_tables, qk_norm_g[i]).reshape(b, l, QKV_COLS)
        o_lat = [_gqa_call(p_lat, p_ctx),
                 _diff_call(p_lat, p_ctx, diff_lambda[i], diff_subln_g[i], lambda_init),
                 _na_call(p_lat, p_ctx, _na_bias_tables(na_rpb[i])),
                 _swa_call(p_lat, p_ctx, sink_tab.reshape(2, 2, LANES))]
        return p_ctx, o_lat, sink_tab, lambda_init

    mod_lat, mod_ctx = mods[0]
    h_lat = _norm_mod_call(x, attn_norm_g[0], mod_lat, 0, 1)
    h_ctx = _norm_mod_call(ctx_flat, attn_norm_g[0], mod_ctx, 0, 1)
    p_ctx, o_lat, sink_tab, lambda_init = mixers(0, h_lat, h_ctx)
    wb = w_branch[0].astype(BF16)
    wo = w_out[0].astype(BF16)
    x_lat, h_lat = _merge_call([(o, 0) for o in o_lat], _gate_proj_call(h_lat, w_in, 0), wb, wo, x,
                               mod_lat, ffn_norm_g[0])
    o_ctx = _ctx_attn_call(p_ctx, diff_lambda[0], diff_subln_g[0], sink_tab, lambda_init)
    o_ctx = o_ctx.reshape(1, b * l, N_BRANCHES * BRANCH_WIDTH)
    x_ctx, h_ctx = _merge_call([(o_ctx, k) for k in range(N_BRANCHES)], _gate_proj_call(h_ctx, w_in, 0),
                               wb, wo, ctx_flat, mod_ctx, ffn_norm_g[0])
    wg = ffn_w_gate[0].astype(BF16)
    wu = ffn_w_up[0].astype(BF16)
    wd = ffn_w_down[0].astype(BF16)
    _, h_ctx = _ffn_call(h_ctx, wg, wu, wd, x_ctx, mod_ctx, attn_norm_g[1], mods[1][1])
    x_lat, h_lat = _ffn_call(h_lat, wg, wu, wd, x_lat, mod_lat, attn_norm_g[1], mods[1][0])

    mod_lat, _ = mods[1]
    _, o_lat, _, _ = mixers(1, h_lat, h_ctx)
    router = jnp.zeros((d, LANES), F32).at[:, :N_EXPERTS].set(moe_router[0])
    x_lat, h_lat, logits = _merge_call(
        [(o, 0) for o in o_lat], _gate_proj_call(h_lat, w_in, 1),
        w_branch[1].astype(BF16), w_out[1].astype(BF16), x_lat, mod_lat, ffn_norm_g[1], router)

    idx_pad, wts_pad = _route_call(logits.reshape(b * n, LANES))
    dest, src_tok, tile_expert, tile_rows = _moe_plan(idx_pad[:, :TOP_K])
    xs = h_lat.reshape(b * n, d).at[src_tok].get(mode="promise_in_bounds")
    ys = _moe_call(xs, moe_w_gate[0], moe_w_up[0], moe_w_down[0], tile_expert, tile_rows)
    y0 = ys.at[dest[:, 0]].get(mode="promise_in_bounds").reshape(b, n, d)
    y1 = ys.at[dest[:, 1]].get(mode="promise_in_bounds").reshape(b, n, d)
    return _final_call(x_lat, y0, y1, wts_pad.reshape(b, n, LANES), mod_lat, final_norm_g)
```

```python
import functools
import math

import numpy as np
import jax
import jax.numpy as jnp
from jax import lax
from jax.experimental import pallas as pl
from jax.experimental.pallas import tpu as pltpu

F32 = jnp.float32
BF16 = jnp.bfloat16

GRID_W = 64
HEAD_DIM = 128
N_BRANCHES = 4
BRANCH_WIDTH = 4 * HEAD_DIM
DIFF_QK_DIM = 64
NA_ROWS = 8
NA_COLS = 16
SWA_WINDOW = 128
N_EXPERTS = 8
TOP_K = 2
NORM_EPS = 1e-6
ROPE_THETA = 10000.0
NEG_INF = -1e30
LOG2E = math.log2(math.e)

QKV_COLS = 5120
COL_GQA_Q, COL_GQA_K, COL_GQA_V = 0, 512, 768
COL_DIFF_Q, COL_DIFF_K, COL_DIFF_V = 1024, 1536, 2048
COL_NA_Q, COL_NA_K, COL_NA_V = 2560, 3072, 3584
COL_SWA_Q, COL_SWA_K, COL_SWA_V = 4096, 4608, 4864

LANES = 128
VMEM_LIMIT = 56 * 1024 * 1024

QS128 = HEAD_DIM ** -0.5 * LOG2E
QS64 = DIFF_QK_DIM ** -0.5 * LOG2E


def _cparams(sem):
    return pltpu.CompilerParams(dimension_semantics=sem, vmem_limit_bytes=VMEM_LIMIT)


def _dot(a, b):
    return jnp.dot(a, b, preferred_element_type=F32)


def _dot_nt(a, b):
    return lax.dot_general(a, b, (((1,), (1,)), ((), ())), preferred_element_type=F32)


def _rms(x):
    return x * lax.rsqrt(jnp.mean(x * x, axis=-1, keepdims=True) + NORM_EPS)


def _norm_mod(x, g, mod_ref, shift_idx, scale_idx):
    y = _rms(x) * g
    return y * (1.0 + mod_ref[scale_idx:scale_idx + 1, :]) + mod_ref[shift_idx:shift_idx + 1, :]


def _adaln_kernel(c_ref, w_ref, b_ref, o_ref):
    c = c_ref[...]
    s = c * jax.nn.sigmoid(c)
    o_ref[...] = jnp.dot(s, w_ref[...], preferred_element_type=F32,
                         precision=lax.Precision.HIGHEST) + b_ref[...]


def _adaln(cvec, w, b, layer):
    rows, d = cvec.shape
    cols = w.shape[2]
    tn = 2048 if cols % 2048 == 0 else cols
    return pl.pallas_call(
        _adaln_kernel,
        grid=(cols // tn,),
        in_specs=[pl.BlockSpec((rows, d), lambda j: (0, 0)),
                  pl.BlockSpec((None, d, tn), lambda j: (layer, 0, j)),
                  pl.BlockSpec((None, 1, tn), lambda j: (layer, 0, j))],
        out_specs=pl.BlockSpec((rows, tn), lambda j: (0, j)),
        out_shape=jax.ShapeDtypeStruct((rows, cols), F32),
        compiler_params=_cparams(("arbitrary",)),
        name="adaln",
    )(cvec, w, b.reshape(b.shape[0], 1, cols))


def _norm_mod_kernel(x_ref, g_ref, mod_ref, h_ref, *, shift_idx, scale_idx):
    h_ref[...] = _norm_mod(x_ref[...], g_ref[...], mod_ref, shift_idx, scale_idx).astype(BF16)


def _norm_mod_call(x, g, mod, shift_idx, scale_idx):
    b, r, d = x.shape
    tm = min(r, 512)
    return pl.pallas_call(
        functools.partial(_norm_mod_kernel, shift_idx=shift_idx, scale_idx=scale_idx),
        grid=(b, r // tm),
        in_specs=[pl.BlockSpec((None, tm, d), lambda i, t: (i, t, 0)),
                  pl.BlockSpec((1, d), lambda i, t: (0, 0)),
                  pl.BlockSpec((None, 6, d), lambda i, t: (i, 0, 0))],
        out_specs=pl.BlockSpec((None, tm, d), lambda i, t: (i, t, 0)),
        out_shape=jax.ShapeDtypeStruct((b, r, d), BF16),
        compiler_params=_cparams(("parallel", "parallel")),
        name="norm_mod",
    )(x, g.reshape(1, d), mod)


QKV_TILE = 4 * LANES
_PLAIN = (None, None, 1.0)
_QKV_TILE_OPS = {
    0: [(0, 32, QS128)] * 4,
    1: [(1, 32, 1.0)] * 2 + [_PLAIN] * 2,
    2: [(None, 16, QS64)] * 4,
    3: [(None, 16, 1.0)] * 4,
    5: [(None, None, QS128)] * 4,
    8: [(None, 32, QS128)] * 4,
    9: [(None, 32, 1.0)] * 2 + [_PLAIN] * 2,
}


def _rope(y, cos, sin, hw):
    lane = lax.broadcasted_iota(jnp.int32, y.shape, 1)
    first = (lane % (2 * hw)) < hw
    partner = jnp.where(first, pltpu.roll(y, LANES - hw, 1), pltpu.roll(y, hw, 1))
    return y * cos + partner * sin


def _qkv_kernel(h_ref, w_ref, cos32_ref, sin32_ref, cos16_ref, sin16_ref, g_ref, o_ref):
    h = h_ref[...]
    for j in range(o_ref.shape[1] // QKV_TILE):
        acc = _dot(h, w_ref[:, j * QKV_TILE:(j + 1) * QKV_TILE])
        for c, (norm_row, hw, scale) in enumerate(_QKV_TILE_OPS.get(j, [_PLAIN] * 4)):
            y = acc[:, c * LANES:(c + 1) * LANES]
            if norm_row is not None:
                y = _rms(y) * g_ref[norm_row:norm_row + 1, :]
            if hw == 32:
                y = _rope(y, cos32_ref[...], sin32_ref[...], 32)
            elif hw == 16:
                y = _rope(y, cos16_ref[...], sin16_ref[...], 16)
            if scale != 1.0:
                y = y * scale
            col = j * QKV_TILE + c * LANES
            o_ref[:, col:col + LANES] = y.astype(o_ref.dtype)


def _qkv_call(h, w_qkv, tables, qk_g):
    b, r, d = h.shape
    cols = w_qkv.shape[1]
    tm = min(r, 512)
    tab_spec = pl.BlockSpec((tm, LANES), lambda i, t: (t, 0))
    return pl.pallas_call(
        _qkv_kernel,
        grid=(b, r // tm),
        in_specs=[pl.BlockSpec((None, tm, d), lambda i, t: (i, t, 0)),
                  pl.BlockSpec((d, cols), lambda i, t: (0, 0), pipeline_mode=pl.Buffered(1)),
                  tab_spec, tab_spec, tab_spec, tab_spec,
                  pl.BlockSpec((2, LANES), lambda i, t: (0, 0))],
        out_specs=pl.BlockSpec((None, tm, cols), lambda i, t: (i, t, 0)),
        out_shape=jax.ShapeDtypeStruct((b, r, cols), BF16),
        compiler_params=_cparams(("parallel", "parallel")),
        name="qkv_proj",
    )(h, w_qkv, *tables, qk_g)


def _matmul_kernel(h_ref, w_ref, o_ref):
    o_ref[...] = _dot(h_ref[...], w_ref[...].astype(BF16)).astype(o_ref.dtype)


def _gate_proj_call(h, w_in, layer):
    b, r, d = h.shape
    tn = 512
    cols = w_in.shape[2] - QKV_COLS
    col0 = QKV_COLS // tn
    tm = 2048 if r % 2048 == 0 else r
    return pl.pallas_call(
        _matmul_kernel,
        grid=(b, r // tm, cols // tn),
        in_specs=[pl.BlockSpec((None, tm, d), lambda i, t, j: (i, t, 0)),
                  pl.BlockSpec((None, d, tn), lambda i, t, j: (layer, 0, col0 + j))],
        out_specs=pl.BlockSpec((None, tm, tn), lambda i, t, j: (i, t, j)),
        out_shape=jax.ShapeDtypeStruct((b, r, cols), BF16),
        compiler_params=_cparams(("parallel", "parallel", "arbitrary")),
        name="gate_proj",
    )(h, w_in)


def _cast_kernel(w_ref, o_ref):
    o_ref[...] = w_ref[...].astype(o_ref.dtype)


def _qkv_weights_call(w_in, layer):
    d = w_in.shape[1]
    return pl.pallas_call(
        _cast_kernel,
        grid=(QKV_COLS // QKV_TILE,),
        in_specs=[pl.BlockSpec((None, d, QKV_TILE), lambda j: (layer, 0, j))],
        out_specs=pl.BlockSpec((d, QKV_TILE), lambda j: (0, j)),
        out_shape=jax.ShapeDtypeStruct((d, QKV_COLS), BF16),
        compiler_params=_cparams(("parallel",)),
        name="qkv_weights",
    )(w_in)


def _diff_lambda(lam_ref, lambda_init):
    a = jnp.sum(lam_ref[0:1, :] * lam_ref[1:2, :], axis=-1, keepdims=True)
    b = jnp.sum(lam_ref[2:3, :] * lam_ref[3:4, :], axis=-1, keepdims=True)
    return jnp.exp(a) - jnp.exp(b) + lambda_init


def _split_maps(q):
    lane = lax.broadcasted_iota(jnp.int32, q.shape, 1)
    zero = jnp.zeros_like(q)
    return jnp.concatenate([jnp.where(lane < DIFF_QK_DIM, q, zero),
                            jnp.where(lane >= DIFF_QK_DIM, q, zero)], axis=0)


FLASH_ROW_BLOCK = 128
FLASH_KV_CHUNK = 512


def _flash_pair(q2, kc_ref, vc_ref, kl_ref, vl_ref, tk):
    m_rows = q2.shape[0]
    n_lat = kl_ref.shape[0]
    nb = m_rows // FLASH_ROW_BLOCK
    qs = [q2[i * FLASH_ROW_BLOCK:(i + 1) * FLASH_ROW_BLOCK] for i in range(nb)]

    def step(k, v, carry):
        v1 = jnp.concatenate([v, jnp.ones_like(v)], axis=1)
        out = []
        for q, (m, acc) in zip(qs, carry):
            s = _dot_nt(q, k)
            m_new = jnp.maximum(m, jnp.max(s, axis=-1, keepdims=True))
            p = jnp.exp2(s - m_new)
            acc = jnp.exp2(m - m_new) * acc + _dot(p.astype(BF16), v1)
            out.append((m_new, acc))
        return tuple(out)

    carry = tuple((jnp.full((FLASH_ROW_BLOCK, 1), NEG_INF, F32),
                   jnp.zeros((FLASH_ROW_BLOCK, 2 * HEAD_DIM), F32)) for _ in range(nb))
    carry = step(kc_ref[...], vc_ref[...], carry)

    for c in range(n_lat // tk):
        carry = step(kl_ref[c * tk:(c + 1) * tk, :], vl_ref[c * tk:(c + 1) * tk, :], carry)
    return jnp.concatenate([acc[:, :HEAD_DIM] / acc[:, HEAD_DIM:] for _, acc in carry], axis=0)


def _gqa_kernel(q_ref, kc_ref, vc_ref, kl_ref, vl_ref, o_ref, *, tk):
    tq = q_ref.shape[0]
    q = q_ref[...]
    q2 = jnp.concatenate([q[:, :HEAD_DIM], q[:, HEAD_DIM:]], axis=0)
    o = _flash_pair(q2, kc_ref, vc_ref, kl_ref, vl_ref, tk)
    o_ref[:, :HEAD_DIM] = o[:tq].astype(o_ref.dtype)
    o_ref[:, HEAD_DIM:] = o[tq:].astype(o_ref.dtype)


def _diff_kernel(q_ref, kc_ref, vc_ref, kl_ref, vl_ref, lam_ref, subg_ref, o_ref, *, tk, lambda_init):
    tq = q_ref.shape[0]
    o = _flash_pair(_split_maps(q_ref[...]), kc_ref, vc_ref, kl_ref, vl_ref, tk)
    lam = _diff_lambda(lam_ref, lambda_init)
    d = o[:tq] - lam * o[tq:]
    o_ref[...] = (_rms(d) * subg_ref[...] * (1.0 - lambda_init)).astype(o_ref.dtype)


def _gqa_call(p_lat, p_ctx):
    b, n, _ = p_lat.shape
    l = p_ctx.shape[1]
    tq = 512
    tk = FLASH_KV_CHUNK
    hd = HEAD_DIM
    kcol, vcol = COL_GQA_K // hd, COL_GQA_V // hd
    return pl.pallas_call(
        functools.partial(_gqa_kernel, tk=tk),
        grid=(b, 2, n // tq),
        in_specs=[pl.BlockSpec((None, tq, 2 * hd), lambda i, k, t: (i, t, k)),
                  pl.BlockSpec((None, l, hd), lambda i, k, t: (i, 0, kcol + k)),
                  pl.BlockSpec((None, l, hd), lambda i, k, t: (i, 0, vcol + k)),
                  pl.BlockSpec((None, n, hd), lambda i, k, t: (i, 0, kcol + k)),
                  pl.BlockSpec((None, n, hd), lambda i, k, t: (i, 0, vcol + k))],
        out_specs=pl.BlockSpec((None, tq, 2 * hd), lambda i, k, t: (i, t, k)),
        out_shape=jax.ShapeDtypeStruct((b, n, BRANCH_WIDTH), BF16),
        compiler_params=_cparams(("parallel", "parallel", "arbitrary")),
        name="attn_gqa",
    )(p_lat, p_ctx, p_ctx, p_lat, p_lat)


def _diff_call(p_lat, p_ctx, lam_p, subg, lambda_init):
    b, n, _ = p_lat.shape
    l = p_ctx.shape[1]
    tq = 512
    tk = FLASH_KV_CHUNK
    hd = HEAD_DIM
    qcol, kcol, vcol = COL_DIFF_Q // hd, COL_DIFF_K // hd, COL_DIFF_V // hd
    return pl.pallas_call(
        functools.partial(_diff_kernel, tk=tk, lambda_init=lambda_init),
        grid=(b, 4, n // tq),
        in_specs=[pl.BlockSpec((None, tq, hd), lambda i, h, t: (i, t, qcol + h)),
                  pl.BlockSpec((None, l, hd), lambda i, h, t: (i, 0, kcol + h)),
                  pl.BlockSpec((None, l, hd), lambda i, h, t: (i, 0, vcol + h)),
                  pl.BlockSpec((None, n, hd), lambda i, h, t: (i, 0, kcol + h)),
                  pl.BlockSpec((None, n, hd), lambda i, h, t: (i, 0, vcol + h)),
                  pl.BlockSpec((4, DIFF_QK_DIM), lambda i, h, t: (0, 0)),
                  pl.BlockSpec((1, hd), lambda i, h, t: (0, 0))],
        out_specs=pl.BlockSpec((None, tq, hd), lambda i, h, t: (i, t, h)),
        out_shape=jax.ShapeDtypeStruct((b, n, BRANCH_WIDTH), BF16),
        compiler_params=_cparams(("parallel", "parallel", "arbitrary")),
        name="attn_diff",
    )(p_lat, p_ctx, p_ctx, p_lat, p_lat, lam_p, subg.reshape(1, hd))


NA_GROUP_ROWS = 8
NA_BLOCK_ROWS = 4
NA_BLOCK_WIN_ROWS = NA_BLOCK_ROWS + NA_ROWS
NA_WIN_ROWS = 2 * NA_GROUP_ROWS


def _with_ones(v):
    return jnp.concatenate([v, jnp.ones_like(v)], axis=1)


def _band_ctx_attend(s_b, s_c, vb1, vc1, sink=None):
    m = jnp.maximum(jnp.max(s_b, axis=-1, keepdims=True), jnp.max(s_c, axis=-1, keepdims=True))
    if sink is not None:
        m = jnp.maximum(m, sink)
    acc = _dot(jnp.exp2(s_c - m).astype(BF16), vc1) + _dot(jnp.exp2(s_b - m).astype(BF16), vb1)
    den = acc[:, HEAD_DIM:]
    if sink is not None:
        den = den + jnp.exp2(sink - m)
    return acc[:, :HEAD_DIM] / den


NA_DI = 2 * NA_ROWS - 1
NA_TAB_PAD = NA_WIN_ROWS
NA_TAB_LANES = 3072


def _na_bias_tables(rpb):
    nh = rpb.shape[0]
    qc = np.arange(GRID_W)[:, None]
    kc = np.arange(GRID_W)[None, :]
    cstart = np.clip(qc - NA_COLS // 2, 0, GRID_W - NA_COLS)
    ok_col = (kc >= cstart) & (kc < cstart + NA_COLS)
    dj = np.clip(kc - qc + NA_COLS - 1, 0, 2 * NA_COLS - 2)
    sel_j = (dj[..., None] == np.arange(2 * NA_COLS - 1)) & ok_col[..., None]
    by_col = jnp.einsum("hij,qkj->hqik", rpb.astype(F32), jnp.asarray(sel_j, F32),
                        precision=lax.Precision.HIGHEST)
    by_col = jnp.where(ok_col[None, :, None, :], by_col * LOG2E, NEG_INF)
    strip = by_col.reshape(nh, GRID_W, NA_DI * GRID_W)
    total = NA_TAB_LANES + GRID_W
    strip = jnp.pad(strip, ((0, 0), (0, 0), (NA_TAB_PAD * GRID_W, total - (NA_TAB_PAD + NA_DI) * GRID_W)),
                    constant_values=NEG_INF)
    return jnp.stack([strip[:, :, :NA_TAB_LANES], strip[:, :, GRID_W:]], axis=1)


def _na_kernel(q_ref, kc_ref, vc_ref, kl_ref, vl_ref, tab_ref, o_ref, *, rows_n):
    g = pl.program_id(2)
    win = NA_BLOCK_WIN_ROWS * GRID_W
    blk_q = NA_BLOCK_ROWS * GRID_W
    kc = kc_ref[...]
    vc1 = _with_ones(vc_ref[...])
    lane = lax.broadcasted_iota(jnp.int32, (GRID_W, win), 1)
    for rb in range(NA_GROUP_ROWS // NA_BLOCK_ROWS):
        r0 = g * NA_GROUP_ROWS + rb * NA_BLOCK_ROWS
        ws = jnp.clip(r0 - NA_ROWS // 2, 0, rows_n - NA_BLOCK_WIN_ROWS)
        off = pl.multiple_of(ws * GRID_W, NA_BLOCK_ROWS * GRID_W)
        bias = []
        for a in range(NA_BLOCK_ROWS):
            r = r0 + a
            rs = jnp.clip(r - NA_ROWS // 2, 0, rows_n - NA_ROWS)
            blk = ws - r + NA_ROWS - 1 + NA_TAB_PAD
            start = pl.multiple_of((blk >> 1) * (2 * GRID_W), 2 * GRID_W)
            strip = tab_ref[blk & 1, :, pl.ds(start, win)]
            lo = (rs - ws) * GRID_W
            in_rows = jnp.logical_and(lane >= lo, lane < lo + NA_ROWS * GRID_W)
            bias.append(jnp.where(in_rows, strip, NEG_INF))
        q = q_ref[rb * blk_q:(rb + 1) * blk_q, :]
        s_b = _dot_nt(q, kl_ref[pl.ds(off, win), :]) + jnp.concatenate(bias, axis=0)
        o = _band_ctx_attend(s_b, _dot_nt(q, kc), _with_ones(vl_ref[pl.ds(off, win), :]), vc1)
        o_ref[rb * blk_q:(rb + 1) * blk_q, :] = o.astype(o_ref.dtype)


def _na_call(p_lat, p_ctx, bias):
    b, n, _ = p_lat.shape
    l = p_ctx.shape[1]
    hd = HEAD_DIM
    rows_n = n // GRID_W
    n_groups = rows_n // NA_GROUP_ROWS
    tq = NA_GROUP_ROWS * GRID_W
    qcol, kcol, vcol = COL_NA_Q // hd, COL_NA_K // hd, COL_NA_V // hd

    return pl.pallas_call(
        functools.partial(_na_kernel, rows_n=rows_n),
        grid=(b, 4, n_groups),
        in_specs=[pl.BlockSpec((None, tq, hd), lambda i, h, g: (i, g, qcol + h)),
                  pl.BlockSpec((None, l, hd), lambda i, h, g: (i, 0, kcol + h)),
                  pl.BlockSpec((None, l, hd), lambda i, h, g: (i, 0, vcol + h)),
                  pl.BlockSpec((None, n, hd), lambda i, h, g: (i, 0, kcol + h)),
                  pl.BlockSpec((None, n, hd), lambda i, h, g: (i, 0, vcol + h)),
                  pl.BlockSpec((None, 2, GRID_W, NA_TAB_LANES), lambda i, h, g: (h, 0, 0, 0))],
        out_specs=pl.BlockSpec((None, tq, hd), lambda i, h, g: (i, g, h)),
        out_shape=jax.ShapeDtypeStruct((b, n, BRANCH_WIDTH), BF16),
        compiler_params=_cparams(("parallel", "parallel", "arbitrary")),
        name="attn_na",
    )(p_lat, p_ctx, p_ctx, p_lat, p_lat, bias)


SWA_BLOCK = 256


def _swa_kernel(q_ref, kc_ref, vc_ref, kl_ref, vl_ref, sink_ref, o_ref, *, win):
    t = pl.program_id(2)
    tq = q_ref.shape[0]
    n = kl_ref.shape[0]
    kc = kc_ref[...]
    vc1 = _with_ones(vc_ref[...])
    row = lax.broadcasted_iota(jnp.int32, (SWA_BLOCK, win), 0)
    col = lax.broadcasted_iota(jnp.int32, (SWA_BLOCK, win), 1)
    for rb in range(tq // SWA_BLOCK):
        qs = t * tq + rb * SWA_BLOCK
        ws = pl.multiple_of(jnp.clip(qs - SWA_WINDOW, 0, n - win), SWA_WINDOW)
        kb = kl_ref[pl.ds(ws, win), :]
        vb1 = _with_ones(vl_ref[pl.ds(ws, win), :])
        valid = jnp.abs(row + (qs - ws) - col) <= SWA_WINDOW
        rows = slice(rb * SWA_BLOCK, (rb + 1) * SWA_BLOCK)
        for gi in range(2):
            q = q_ref[rows, gi * HEAD_DIM:(gi + 1) * HEAD_DIM]
            s_b = jnp.where(valid, _dot_nt(q, kb), NEG_INF)
            o = _band_ctx_attend(s_b, _dot_nt(q, kc), vb1, vc1, sink=sink_ref[gi:gi + 1, 0:1])
            o_ref[rows, gi * HEAD_DIM:(gi + 1) * HEAD_DIM] = o.astype(o_ref.dtype)


def _swa_call(p_lat, p_ctx, sink_tab):
    b, n, _ = p_lat.shape
    l = p_ctx.shape[1]
    hd = HEAD_DIM
    tq = 512
    win = SWA_BLOCK + 2 * SWA_WINDOW
    qcol, kcol, vcol = COL_SWA_Q // (2 * hd), COL_SWA_K // hd, COL_SWA_V // hd
    return pl.pallas_call(
        functools.partial(_swa_kernel, win=win),
        grid=(b, 2, n // tq),
        in_specs=[pl.BlockSpec((None, tq, 2 * hd), lambda i, k, t: (i, t, qcol + k)),
                  pl.BlockSpec((None, l, hd), lambda i, k, t: (i, 0, kcol + k)),
                  pl.BlockSpec((None, l, hd), lambda i, k, t: (i, 0, vcol + k)),
                  pl.BlockSpec((None, n, hd), lambda i, k, t: (i, 0, kcol + k)),
                  pl.BlockSpec((None, n, hd), lambda i, k, t: (i, 0, vcol + k)),
                  pl.BlockSpec((None, 2, LANES), lambda i, k, t: (k, 0, 0))],
        out_specs=pl.BlockSpec((None, tq, 2 * hd), lambda i, k, t: (i, t, k)),
        out_shape=jax.ShapeDtypeStruct((b, n, BRANCH_WIDTH), BF16),
        compiler_params=_cparams(("parallel", "parallel", "arbitrary")),
        name="attn_swa",
    )(p_lat, p_ctx, p_ctx, p_lat, p_lat, sink_tab)


def _softmax_attend(q, k, v, sink=None):
    s = _dot_nt(q, k)
    m = jnp.max(s, axis=-1, keepdims=True)
    if sink is not None:
        m = jnp.maximum(m, sink)
    e = jnp.exp2(s - m)
    den = jnp.sum(e, axis=-1, keepdims=True)
    if sink is not None:
        den = den + jnp.exp2(sink - m)
    return _dot(e.astype(BF16), v) / den


def _ctx_attn_kernel(p_ref, lam_ref, subg_ref, sink_ref, o_ref, *, lambda_init):
    hd = HEAD_DIM

    def col(c0, h):
        return p_ref[:, c0 + h * hd:c0 + (h + 1) * hd]

    lam = _diff_lambda(lam_ref, lambda_init)
    for h in range(4):
        o = _softmax_attend(col(COL_GQA_Q, h), col(COL_GQA_K, h // 2), col(COL_GQA_V, h // 2))
        o_ref[:, h * hd:(h + 1) * hd] = o.astype(o_ref.dtype)
    for h in range(4):
        q = col(COL_DIFF_Q, h)
        tq = q.shape[0]
        o2 = _softmax_attend(_split_maps(q), col(COL_DIFF_K, h), col(COL_DIFF_V, h))
        d = o2[:tq] - lam * o2[tq:]
        d = _rms(d) * subg_ref[...] * (1.0 - lambda_init)
        o_ref[:, BRANCH_WIDTH + h * hd:BRANCH_WIDTH + (h + 1) * hd] = d.astype(o_ref.dtype)
    for h in range(4):
        o = _softmax_attend(col(COL_NA_Q, h), col(COL_NA_K, h), col(COL_NA_V, h))
        o_ref[:, 2 * BRANCH_WIDTH + h * hd:2 * BRANCH_WIDTH + (h + 1) * hd] = o.astype(o_ref.dtype)
    for h in range(4):
        o = _softmax_attend(col(COL_SWA_Q, h), col(COL_SWA_K, h // 2), col(COL_SWA_V, h // 2),
                            sink=sink_ref[h:h + 1, 0:1])
        o_ref[:, 3 * BRANCH_WIDTH + h * hd:3 * BRANCH_WIDTH + (h + 1) * hd] = o.astype(o_ref.dtype)


def _ctx_attn_call(p_ctx, lam_p, subg, sink_tab, lambda_init):
    b, l, cols = p_ctx.shape
    return pl.pallas_call(
        functools.partial(_ctx_attn_kernel, lambda_init=lambda_init),
        grid=(b,),
        in_specs=[pl.BlockSpec((None, l, cols), lambda i: (i, 0, 0)),
                  pl.BlockSpec((4, DIFF_QK_DIM), lambda i: (0, 0)),
                  pl.BlockSpec((1, HEAD_DIM), lambda i: (0, 0)),
                  pl.BlockSpec((4, LANES), lambda i: (0, 0))],
        out_specs=pl.BlockSpec((None, l, N_BRANCHES * BRANCH_WIDTH), lambda i: (i, 0, 0)),
        out_shape=jax.ShapeDtypeStruct((b, l, N_BRANCHES * BRANCH_WIDTH), BF16),
        compiler_params=_cparams(("parallel",)),
        name="attn_ctx",
    )(p_ctx, lam_p, subg.reshape(1, HEAD_DIM), sink_tab.reshape(4, LANES))


MERGE_CHUNKS = 4


def _merge_kernel(o0, o1, o2, o3, g_ref, wb_ref, wo_ref, x_ref, mod_ref, ng_ref, *rest, with_router):
    if with_router:
        router_ref, xo_ref, h_ref, lg_ref = rest
    else:
        xo_ref, h_ref = rest
    d = x_ref.shape[1]
    dc = d // MERGE_CHUNKS
    o_vals = [o[...] for o in (o0, o1, o2, o3)]
    z = None
    for c in range(MERGE_CHUNKS):
        s = None
        for i, o in enumerate(o_vals):
            y = _dot(o, wb_ref[i, :, c * dc:(c + 1) * dc])
            gate = g_ref[:, i * d + c * dc:i * d + (c + 1) * dc].astype(F32)
            term = jax.nn.sigmoid(gate) * y
            s = term if s is None else s + term
        zc = _dot(s.astype(BF16), wo_ref[c * dc:(c + 1) * dc, :])
        z = zc if z is None else z + zc
    xn = x_ref[...] + mod_ref[2:3, :] * z
    xo_ref[...] = xn
    hn = _norm_mod(xn, ng_ref[...], mod_ref, 3, 4)
    h_ref[...] = hn.astype(h_ref.dtype)
    if with_router:
        lg_ref[...] = _dot_split(hn, router_ref[...])


def _dot_split(a, b):
    a_hi = a.astype(BF16)
    a_lo = (a - a_hi.astype(F32)).astype(BF16)
    b_hi = b.astype(BF16)
    b_lo = (b - b_hi.astype(F32)).astype(BF16)
    n = b.shape[1]
    both = _dot(a_hi, jnp.concatenate([b_hi, b_lo], axis=1))
    return both[:, :n] + both[:, n:] + _dot(a_lo, b_hi)


def _merge_call(o_parts, gates, w_branch, w_out, x, mod, next_g, router=None):
    b, r, d = x.shape
    tm = min(r, 256)
    with_router = router is not None
    whole = dict(pipeline_mode=pl.Buffered(1))

    def o_spec(cb):
        return pl.BlockSpec((None, tm, BRANCH_WIDTH), lambda i, t: (i, t, cb))

    row_spec = pl.BlockSpec((None, tm, d), lambda i, t: (i, t, 0))
    in_specs = [o_spec(cb) for _, cb in o_parts] + [
        pl.BlockSpec((None, tm, N_BRANCHES * d), lambda i, t: (i, t, 0)),
        pl.BlockSpec((N_BRANCHES, BRANCH_WIDTH, d), lambda i, t: (0, 0, 0), **whole),
        pl.BlockSpec((d, d), lambda i, t: (0, 0), **whole),
        row_spec,
        pl.BlockSpec((None, 6, d), lambda i, t: (i, 0, 0)),
        pl.BlockSpec((1, d), lambda i, t: (0, 0))]
    args = [a for a, _ in o_parts] + [gates, w_branch, w_out, x, mod, next_g.reshape(1, d)]
    out_specs = [row_spec, row_spec]
    out_shape = [jax.ShapeDtypeStruct((b, r, d), F32), jax.ShapeDtypeStruct((b, r, d), BF16)]
    if with_router:
        in_specs.append(pl.BlockSpec((d, LANES), lambda i, t: (0, 0), **whole))
        args.append(router)
        out_specs.append(pl.BlockSpec((None, tm, LANES), lambda i, t: (i, t, 0)))
        out_shape.append(jax.ShapeDtypeStruct((b, r, LANES), F32))
    return pl.pallas_call(
        functools.partial(_merge_kernel, with_router=with_router),
        grid=(b, r // tm),
        in_specs=in_specs,
        out_specs=out_specs,
        out_shape=out_shape,
        compiler_params=_cparams(("parallel", "parallel")),
        name="merge",
    )(*args)


def _swiglu_partial(h, wg, wu, wd):
    gate = _dot(h, wg)
    up = _dot(h, wu)
    act = gate * jax.nn.sigmoid(gate) * up
    return _dot(act.astype(BF16), wd)


FFN_ROW_BLOCK = 512


def _ffn_kernel(h_ref, wg_ref, wu_ref, wd_ref, x_ref, mod_ref, ng_ref, nmod_ref, xo_ref, hn_ref,
                acc_ref):
    f = pl.program_id(2)

    @pl.when(f == 0)
    def _():
        acc_ref[...] = jnp.zeros_like(acc_ref)

    for r0 in range(0, h_ref.shape[0], FFN_ROW_BLOCK):
        rows = slice(r0, r0 + FFN_ROW_BLOCK)
        acc_ref[rows, :] += _swiglu_partial(h_ref[rows, :], wg_ref[...], wu_ref[...], wd_ref[...])

    @pl.when(f == pl.num_programs(2) - 1)
    def _():
        xn = x_ref[...] + mod_ref[5:6, :] * acc_ref[...]
        xo_ref[...] = xn
        hn_ref[...] = _norm_mod(xn, ng_ref[...], nmod_ref, 0, 1).astype(hn_ref.dtype)


def _ffn_call(h, wg, wu, wd, x, mod, next_g, next_mod):
    b, r, d = x.shape
    dff = wg.shape[1]
    tf = 512
    tm = 2 * FFN_ROW_BLOCK if r % (2 * FFN_ROW_BLOCK) == 0 else min(r, FFN_ROW_BLOCK)
    row_spec = pl.BlockSpec((None, tm, d), lambda i, t, f: (i, t, 0), pipeline_mode=pl.Buffered(1))
    mod_spec = pl.BlockSpec((None, 6, d), lambda i, t, f: (i, 0, 0))
    return pl.pallas_call(
        _ffn_kernel,
        grid=(b, r // tm, dff // tf),
        in_specs=[row_spec,
                  pl.BlockSpec((d, tf), lambda i, t, f: (0, f)),
                  pl.BlockSpec((d, tf), lambda i, t, f: (0, f)),
                  pl.BlockSpec((tf, d), lambda i, t, f: (f, 0)),
                  row_spec, mod_spec,
                  pl.BlockSpec((1, d), lambda i, t, f: (0, 0)),
                  mod_spec],
        out_specs=[row_spec, row_spec],
        out_shape=[jax.ShapeDtypeStruct((b, r, d), F32), jax.ShapeDtypeStruct((b, r, d), BF16)],
        scratch_shapes=[pltpu.VMEM((tm, d), F32)],
        compiler_params=_cparams(("parallel", "parallel", "arbitrary")),
        name="ffn_dense",
    )(h, wg, wu, wd, x, mod, next_g.reshape(1, d), next_mod)


def _route_kernel(lg_ref, idx_ref, w_ref):
    lg = lg_ref[...]
    lane = lax.broadcasted_iota(jnp.int32, lg.shape, 1)
    valid = lane < N_EXPERTS
    mx = jnp.max(jnp.where(valid, lg, -jnp.inf), axis=-1, keepdims=True)
    e = jnp.where(valid, jnp.exp(lg - mx), 0.0)
    p = e / jnp.sum(e, axis=-1, keepdims=True)
    p1 = jnp.max(p, axis=-1, keepdims=True)
    i1 = jnp.min(jnp.where(p == p1, lane, LANES), axis=-1, keepdims=True)
    rest = jnp.where(jnp.logical_or(lane == i1, jnp.logical_not(valid)), -1.0, p)
    p2 = jnp.max(rest, axis=-1, keepdims=True)
    i2 = jnp.min(jnp.where(rest == p2, lane, LANES), axis=-1, keepdims=True)
    tot = p1 + p2
    idx_ref[...] = jnp.where(lane == 0, i1, jnp.where(lane == 1, i2, 0))
    w_ref[...] = jnp.where(lane == 0, p1 / tot, jnp.where(lane == 1, p2 / tot, 0.0))


def _route_call(logits):
    r = logits.shape[0]
    tm = min(r, 1024)
    spec = pl.BlockSpec((tm, LANES), lambda i: (i, 0))
    return pl.pallas_call(
        _route_kernel,
        grid=(r // tm,),
        in_specs=[spec],
        out_specs=[spec, spec],
        out_shape=[jax.ShapeDtypeStruct((r, LANES), jnp.int32), jax.ShapeDtypeStruct((r, LANES), F32)],
        compiler_params=_cparams(("parallel",)),
        name="route_top2",
    )(logits)


MOE_HALF = 512
MOE_TILE = 2 * MOE_HALF
MOE_FF_TILE = 512
MOE_VMEM_LIMIT = 60 * 1024 * 1024


def _moe_kernel(te_ref, nr_ref, x_ref, wg_ref, wu_ref, wd_ref, o_ref, acc_ref):
    i = pl.program_id(0)
    f = pl.program_id(1)

    @pl.when(f == 0)
    def _():
        acc_ref[...] = jnp.zeros_like(acc_ref)

    for half in range(MOE_TILE // MOE_HALF):
        rows = slice(half * MOE_HALF, (half + 1) * MOE_HALF)

        @pl.when(nr_ref[i] > half * MOE_HALF)
        def _():
            acc_ref[rows, :] += _swiglu_partial(x_ref[rows, :], wg_ref[...].astype(BF16),
                                                wu_ref[...].astype(BF16), wd_ref[...].astype(BF16))

    @pl.when(f == pl.num_programs(1) - 1)
    def _():
        o_ref[...] = acc_ref[...].astype(o_ref.dtype)


def _moe_call(xs, wg, wu, wd, tile_expert, tile_rows):
    r, d = xs.shape
    dff = wg.shape[2]
    tf = MOE_FF_TILE
    nf = dff // tf

    def fidx(i, f, nr):
        return jnp.where(nr[i] > 0, f, nf - 1)

    grid_spec = pltpu.PrefetchScalarGridSpec(
        num_scalar_prefetch=2,
        grid=(r // MOE_TILE, nf),
        in_specs=[pl.BlockSpec((MOE_TILE, d), lambda i, f, te, nr: (i, 0), pipeline_mode=pl.Buffered(1)),
                  pl.BlockSpec((None, d, tf), lambda i, f, te, nr: (te[i], 0, fidx(i, f, nr))),
                  pl.BlockSpec((None, d, tf), lambda i, f, te, nr: (te[i], 0, fidx(i, f, nr))),
                  pl.BlockSpec((None, tf, d), lambda i, f, te, nr: (te[i], fidx(i, f, nr), 0))],
        out_specs=pl.BlockSpec((MOE_TILE, d), lambda i, f, te, nr: (i, 0), pipeline_mode=pl.Buffered(1)),
        scratch_shapes=[pltpu.VMEM((MOE_TILE, d), F32)])
    return pl.pallas_call(
        _moe_kernel,
        grid_spec=grid_spec,
        out_shape=jax.ShapeDtypeStruct((r, d), BF16),
        compiler_params=pltpu.CompilerParams(dimension_semantics=("arbitrary", "arbitrary"),
                                             vmem_limit_bytes=MOE_VMEM_LIMIT),
        name="moe_experts",
    )(tile_expert, tile_rows, xs, wg, wu, wd)


def _moe_plan(idx):
    t = idx.shape[0]
    a = t * TOP_K
    flat_e = idx.reshape(a)
    onehot = (flat_e[:, None] == jnp.arange(N_EXPERTS, dtype=jnp.int32)[None, :]).astype(jnp.int32)
    csum = jnp.cumsum(onehot, axis=0)
    rank = jnp.sum((csum - onehot) * onehot, axis=1)
    counts = csum[-1]
    padded = ((counts + MOE_TILE - 1) // MOE_TILE) * MOE_TILE
    ends = jnp.cumsum(padded)
    offs = ends - padded
    dest = offs[flat_e] + rank
    rows = a + N_EXPERTS * MOE_TILE
    src_tok = (jnp.arange(rows, dtype=jnp.int32) % t).at[dest].set(jnp.arange(a, dtype=jnp.int32) // TOP_K)
    tile_start = jnp.arange(rows // MOE_TILE, dtype=jnp.int32) * MOE_TILE
    past = jnp.sum((tile_start[:, None] >= ends[None, :]).astype(jnp.int32), axis=1)
    tile_expert = jnp.minimum(past, N_EXPERTS - 1)
    tile_rows = jnp.where(past < N_EXPERTS,
                          jnp.clip((offs + counts)[tile_expert] - tile_start, 0, MOE_TILE), 0)
    return dest.reshape(t, TOP_K), src_tok, tile_expert, tile_rows.astype(jnp.int32)


def _final_kernel(x_ref, y0_ref, y1_ref, w_ref, mod_ref, g_ref, o_ref):
    y = w_ref[:, 0:1] * y0_ref[...].astype(F32) + w_ref[:, 1:2] * y1_ref[...].astype(F32)
    xn = x_ref[...] + mod_ref[5:6, :] * y
    o_ref[...] = _rms(xn) * g_ref[...]


def _final_call(x, y0, y1, w, mod, g):
    b, r, d = x.shape
    tm = min(r, 512)
    row_spec = pl.BlockSpec((None, tm, d), lambda i, t: (i, t, 0))
    return pl.pallas_call(
        _final_kernel,
        grid=(b, r // tm),
        in_specs=[row_spec, row_spec, row_spec,
                  pl.BlockSpec((None, tm, LANES), lambda i, t: (i, t, 0)),
                  pl.BlockSpec((None, 6, d), lambda i, t: (i, 0, 0)),
                  pl.BlockSpec((1, d), lambda i, t: (0, 0))],
        out_specs=row_spec,
        out_shape=jax.ShapeDtypeStruct((b, r, d), F32),
        compiler_params=_cparams(("parallel", "parallel")),
        name="final_norm",
    )(x, y0, y1, w, mod, g.reshape(1, d))


def _rope_tables(n):
    pos = jnp.arange(n)
    rows = (pos // GRID_W).astype(F32)
    cols = (pos % GRID_W).astype(F32)
    lane = np.arange(LANES)
    out = []
    for hw in (32, 16):
        period = 4 * hw
        u = lane % period
        use_cols = (u // (2 * hw)) == 1
        w = u % (2 * hw)
        freqs = jnp.asarray(ROPE_THETA ** (-(w % hw).astype(np.float32) / hw), F32)
        p = jnp.where(jnp.asarray(use_cols)[None, :], cols[:, None], rows[:, None])
        ang = p * freqs[None, :]
        sign = jnp.asarray(np.where(w < hw, -1.0, 1.0), F32)
        out += [jnp.cos(ang), jnp.sin(ang) * sign[None, :]]
    return out


def kernel(x, c, ctx, c_ctx, attn_norm_g, ffn_norm_g, ada_w, ada_b, w_in, qk_norm_g, diff_lambda,
           diff_subln_g, na_rpb, swa_sink, w_branch, w_out, ffn_w_gate, ffn_w_up, ffn_w_down,
           moe_router, moe_w_gate, moe_w_up, moe_w_down, final_norm_g):
    b, n, d = x.shape
    l = ctx.shape[1]
    depth = w_in.shape[0]
    assert depth == 2, "laid out for one dense layer followed by one routed last layer"
    rows_n = n // GRID_W

    lat_tables = _rope_tables(n)
    ctx_flat = ctx.reshape(1, b * l, d)
    ones = jnp.ones((b * l, LANES), F32)
    zeros = jnp.zeros((b * l, LANES), F32)
    ctx_tables = [ones, zeros, ones, zeros]

    cvec = jnp.zeros((8, d), F32).at[:b].set(c).at[b].set(c_ctx)
    mods = []
    for i in range(depth):
        m = _adaln(cvec, ada_w, ada_b, i).reshape(8, 6, d)
        mods.append((m[:b], m[b:b + 1]))

    def mixers(i, h_lat, h_ctx):
        lambda_init = 0.8 - 0.6 * math.exp(-0.3 * i)
        w_qkv = _qkv_weights_call(w_in, i)
        sink_tab = jnp.broadcast_to((swa_sink[i].astype(F32) * LOG2E)[:, None], (4, LANES))
        p_lat = _qkv_call(h_lat, w_qkv, lat_tables, qk_norm_g[i])
        p_ctx = _qkv_call(h_ctx, w_qkv, ctx_tables, qk_norm_g[i]).reshape(b, l, QKV_COLS)
        o_lat = [_gqa_call(p_lat, p_ctx),
                 _diff_call(p_lat, p_ctx, diff_lambda[i], diff_subln_g[i], lambda_init),
                 _na_call(p_lat, p_ctx, _na_bias_tables(na_rpb[i])),
                 _swa_call(p_lat, p_ctx, sink_tab.reshape(2, 2, LANES))]
        return p_ctx, o_lat, sink_tab, lambda_init

    mod_lat, mod_ctx = mods[0]
    h_lat = _norm_mod_call(x, attn_norm_g[0], mod_lat, 0, 1)
    h_ctx = _norm_mod_call(ctx_flat, attn_norm_g[0], mod_ctx, 0, 1)
    p_ctx, o_lat, sink_tab, lambda_init = mixers(0, h_lat, h_ctx)
    wb = w_branch[0].astype(BF16)
    wo = w_out[0].astype(BF16)
    x_lat, h_lat = _merge_call([(o, 0) for o in o_lat], _gate_proj_call(h_lat, w_in, 0), wb, wo, x,
                               mod_lat, ffn_norm_g[0])
    o_ctx = _ctx_attn_call(p_ctx, diff_lambda[0], diff_subln_g[0], sink_tab, lambda_init)
    o_ctx = o_ctx.reshape(1, b * l, N_BRANCHES * BRANCH_WIDTH)
    x_ctx, h_ctx = _merge_call([(o_ctx, k) for k in range(N_BRANCHES)], _gate_proj_call(h_ctx, w_in, 0),
                               wb, wo, ctx_flat, mod_ctx, ffn_norm_g[0])
    wg = ffn_w_gate[0].astype(BF16)
    wu = ffn_w_up[0].astype(BF16)
    wd = ffn_w_down[0].astype(BF16)
    _, h_ctx = _ffn_call(h_ctx, wg, wu, wd, x_ctx, mod_ctx, attn_norm_g[1], mods[1][1])
    x_lat, h_lat = _ffn_call(h_lat, wg, wu, wd, x_lat, mod_lat, attn_norm_g[1], mods[1][0])

    mod_lat, _ = mods[1]
    _, o_lat, _, _ = mixers(1, h_lat, h_ctx)
    router = jnp.zeros((d, LANES), F32).at[:, :N_EXPERTS].set(moe_router[0])
    x_lat, h_lat, logits = _merge_call(
        [(o, 0) for o in o_lat], _gate_proj_call(h_lat, w_in, 1),
        w_branch[1].astype(BF16), w_out[1].astype(BF16), x_lat, mod_lat, ffn_norm_g[1], router)

    idx_pad, wts_pad = _route_call(logits.reshape(b * n, LANES))
    dest, src_tok, tile_expert, tile_rows = _moe_plan(idx_pad[:, :TOP_K])
    xs = h_lat.reshape(b * n, d).at[src_tok].get(mode="promise_in_bounds")
    ys = _moe_call(xs, moe_w_gate[0], moe_w_up[0], moe_w_down[0], tile_expert, tile_rows)
    y0 = ys.at[dest[:, 0]].get(mode="promise_in_bounds").reshape(b, n, d)
    y1 = ys.at[dest[:, 1]].get(mode="promise_in_bounds").reshape(b, n, d)
    return _final_call(x_lat, y0, y1, wts_pad.reshape(b, n, LANES), mod_lat, final_norm_g)
```

```python
import functools
import math

import numpy as np
import jax
import jax.numpy as jnp
from jax import lax
from jax.experimental import pallas as pl
from jax.experimental.pallas import tpu as pltpu

F32 = jnp.float32
BF16 = jnp.bfloat16

GRID_W = 64
HEAD_DIM = 128
N_BRANCHES = 4
BRANCH_WIDTH = 4 * HEAD_DIM
DIFF_QK_DIM = 64
NA_ROWS = 8
NA_COLS = 16
SWA_WINDOW = 128
N_EXPERTS = 8
TOP_K = 2
NORM_EPS = 1e-6
ROPE_THETA = 10000.0
NEG_INF = -1e30
LOG2E = math.log2(math.e)

QKV_COLS = 5120
COL_GQA_Q, COL_GQA_K, COL_GQA_V = 0, 512, 768
COL_DIFF_Q, COL_DIFF_K, COL_DIFF_V = 1024, 1536, 2048
COL_NA_Q, COL_NA_K, COL_NA_V = 2560, 3072, 3584
COL_SWA_Q, COL_SWA_K, COL_SWA_V = 4096, 4608, 4864

LANES = 128
VMEM_LIMIT = 56 * 1024 * 1024

QS128 = HEAD_DIM ** -0.5 * LOG2E
QS64 = DIFF_QK_DIM ** -0.5 * LOG2E


def _cparams(sem):
    return pltpu.CompilerParams(dimension_semantics=sem, vmem_limit_bytes=VMEM_LIMIT)


def _dot(a, b):
    return jnp.dot(a, b, preferred_element_type=F32)


def _dot_nt(a, b):
    return lax.dot_general(a, b, (((1,), (1,)), ((), ())), preferred_element_type=F32)


def _rms(x):
    return x * lax.rsqrt(jnp.mean(x * x, axis=-1, keepdims=True) + NORM_EPS)


def _norm_mod(x, g, mod_ref, shift_idx, scale_idx):
    y = _rms(x) * g
    return y * (1.0 + mod_ref[scale_idx:scale_idx + 1, :]) + mod_ref[shift_idx:shift_idx + 1, :]


def _adaln_kernel(c_ref, w_ref, b_ref, o_ref):
    c = c_ref[...]
    s = c * jax.nn.sigmoid(c)
    o_ref[...] = jnp.dot(s, w_ref[...], preferred_element_type=F32,
                         precision=lax.Precision.HIGHEST) + b_ref[...]


def _adaln(cvec, w, b, layer):
    rows, d = cvec.shape
    cols = w.shape[2]
    tn = 1024 if cols % 1024 == 0 else cols
    return pl.pallas_call(
        _adaln_kernel,
        grid=(cols // tn,),
        in_specs=[pl.BlockSpec((rows, d), lambda j: (0, 0)),
                  pl.BlockSpec((None, d, tn), lambda j: (layer, 0, j)),
                  pl.BlockSpec((None, 1, tn), lambda j: (layer, 0, j))],
        out_specs=pl.BlockSpec((rows, tn), lambda j: (0, j)),
        out_shape=jax.ShapeDtypeStruct((rows, cols), F32),
        compiler_params=_cparams(("arbitrary",)),
        name="adaln",
    )(cvec, w, b.reshape(b.shape[0], 1, cols))


def _norm_mod_kernel(x_ref, g_ref, mod_ref, h_ref, *, shift_idx, scale_idx):
    h_ref[...] = _norm_mod(x_ref[...], g_ref[...], mod_ref, shift_idx, scale_idx).astype(BF16)


def _norm_mod_call(x, g, mod, shift_idx, scale_idx):
    b, r, d = x.shape
    tm = min(r, 512)
    return pl.pallas_call(
        functools.partial(_norm_mod_kernel, shift_idx=shift_idx, scale_idx=scale_idx),
        grid=(b, r // tm),
        in_specs=[pl.BlockSpec((None, tm, d), lambda i, t: (i, t, 0)),
                  pl.BlockSpec((1, d), lambda i, t: (0, 0)),
                  pl.BlockSpec((None, 6, d), lambda i, t: (i, 0, 0))],
        out_specs=pl.BlockSpec((None, tm, d), lambda i, t: (i, t, 0)),
        out_shape=jax.ShapeDtypeStruct((b, r, d), BF16),
        compiler_params=_cparams(("parallel", "parallel")),
        name="norm_mod",
    )(x, g.reshape(1, d), mod)


QKV_TILE = 4 * LANES
_PLAIN = (None, None, 1.0)
_QKV_TILE_OPS = {
    0: [(0, 32, QS128)] * 4,
    1: [(1, 32, 1.0)] * 2 + [_PLAIN] * 2,
    2: [(None, 16, QS64)] * 4,
    3: [(None, 16, 1.0)] * 4,
    5: [(None, None, QS128)] * 4,
    8: [(None, 32, QS128)] * 4,
    9: [(None, 32, 1.0)] * 2 + [_PLAIN] * 2,
}


def _rope(y, cos, sin, hw):
    lane = lax.broadcasted_iota(jnp.int32, y.shape, 1)
    first = (lane % (2 * hw)) < hw
    partner = jnp.where(first, pltpu.roll(y, LANES - hw, 1), pltpu.roll(y, hw, 1))
    return y * cos + partner * sin


def _qkv_kernel(h_ref, w_ref, cos32_ref, sin32_ref, cos16_ref, sin16_ref, g_ref, o_ref):
    h = h_ref[...]
    for j in range(o_ref.shape[1] // QKV_TILE):
        acc = _dot(h, w_ref[:, j * QKV_TILE:(j + 1) * QKV_TILE])
        for c, (norm_row, hw, scale) in enumerate(_QKV_TILE_OPS.get(j, [_PLAIN] * 4)):
            y = acc[:, c * LANES:(c + 1) * LANES]
            if norm_row is not None:
                y = _rms(y) * g_ref[norm_row:norm_row + 1, :]
            if hw == 32:
                y = _rope(y, cos32_ref[...], sin32_ref[...], 32)
            elif hw == 16:
                y = _rope(y, cos16_ref[...], sin16_ref[...], 16)
            if scale != 1.0:
                y = y * scale
            col = j * QKV_TILE + c * LANES
            o_ref[:, col:col + LANES] = y.astype(o_ref.dtype)


def _qkv_call(h, w_qkv, tables, qk_g):
    b, r, d = h.shape
    cols = w_qkv.shape[1]
    tm = min(r, 512)
    tab_spec = pl.BlockSpec((tm, LANES), lambda i, t: (t, 0))
    return pl.pallas_call(
        _qkv_kernel,
        grid=(b, r // tm),
        in_specs=[pl.BlockSpec((None, tm, d), lambda i, t: (i, t, 0)),
                  pl.BlockSpec((d, cols), lambda i, t: (0, 0), pipeline_mode=pl.Buffered(1)),
                  tab_spec, tab_spec, tab_spec, tab_spec,
                  pl.BlockSpec((2, LANES), lambda i, t: (0, 0))],
        out_specs=pl.BlockSpec((None, tm, cols), lambda i, t: (i, t, 0)),
        out_shape=jax.ShapeDtypeStruct((b, r, cols), BF16),
        compiler_params=_cparams(("parallel", "parallel")),
        name="qkv_proj",
    )(h, w_qkv, *tables, qk_g)


def _matmul_kernel(h_ref, w_ref, o_ref):
    o_ref[...] = _dot(h_ref[...], w_ref[...].astype(BF16)).astype(o_ref.dtype)


def _gate_proj_call(h, w_in, layer):
    b, r, d = h.shape
    tn = 512
    cols = w_in.shape[2] - QKV_COLS
    col0 = QKV_COLS // tn
    tm = 2048 if r % 2048 == 0 else r
    return pl.pallas_call(
        _matmul_kernel,
        grid=(b, r // tm, cols // tn),
        in_specs=[pl.BlockSpec((None, tm, d), lambda i, t, j: (i, t, 0)),
                  pl.BlockSpec((None, d, tn), lambda i, t, j: (layer, 0, col0 + j))],
        out_specs=pl.BlockSpec((None, tm, tn), lambda i, t, j: (i, t, j)),
        out_shape=jax.ShapeDtypeStruct((b, r, cols), BF16),
        compiler_params=_cparams(("parallel", "parallel", "arbitrary")),
        name="gate_proj",
    )(h, w_in)


def _cast_kernel(w_ref, o_ref):
    o_ref[...] = w_ref[...].astype(o_ref.dtype)


def _qkv_weights_call(w_in, layer):
    d = w_in.shape[1]
    return pl.pallas_call(
        _cast_kernel,
        grid=(QKV_COLS // QKV_TILE,),
        in_specs=[pl.BlockSpec((None, d, QKV_TILE), lambda j: (layer, 0, j))],
        out_specs=pl.BlockSpec((d, QKV_TILE), lambda j: (0, j)),
        out_shape=jax.ShapeDtypeStruct((d, QKV_COLS), BF16),
        compiler_params=_cparams(("parallel",)),
        name="qkv_weights",
    )(w_in)


def _diff_lambda(lam_ref, lambda_init):
    a = jnp.sum(lam_ref[0:1, :] * lam_ref[1:2, :], axis=-1, keepdims=True)
    b = jnp.sum(lam_ref[2:3, :] * lam_ref[3:4, :], axis=-1, keepdims=True)
    return jnp.exp(a) - jnp.exp(b) + lambda_init


def _split_maps(q):
    lane = lax.broadcasted_iota(jnp.int32, q.shape, 1)
    zero = jnp.zeros_like(q)
    return jnp.concatenate([jnp.where(lane < DIFF_QK_DIM, q, zero),
                            jnp.where(lane >= DIFF_QK_DIM, q, zero)], axis=0)


FLASH_ROW_BLOCK = 128
FLASH_KV_CHUNK = 512


def _flash_pair(q2, kc_ref, vc_ref, kl_ref, vl_ref, tk):
    m_rows = q2.shape[0]
    n_lat = kl_ref.shape[0]
    nb = m_rows // FLASH_ROW_BLOCK
    qs = [q2[i * FLASH_ROW_BLOCK:(i + 1) * FLASH_ROW_BLOCK] for i in range(nb)]

    def step(k, v, carry):
        v1 = jnp.concatenate([v, jnp.ones_like(v)], axis=1)
        out = []
        for q, (m, acc) in zip(qs, carry):
            s = _dot_nt(q, k)
            m_new = jnp.maximum(m, jnp.max(s, axis=-1, keepdims=True))
            p = jnp.exp2(s - m_new)
            acc = jnp.exp2(m - m_new) * acc + _dot(p.astype(BF16), v1)
            out.append((m_new, acc))
        return tuple(out)

    carry = tuple((jnp.full((FLASH_ROW_BLOCK, 1), NEG_INF, F32),
                   jnp.zeros((FLASH_ROW_BLOCK, 2 * HEAD_DIM), F32)) for _ in range(nb))
    carry = step(kc_ref[...], vc_ref[...], carry)

    for c in range(n_lat // tk):
        carry = step(kl_ref[c * tk:(c + 1) * tk, :], vl_ref[c * tk:(c + 1) * tk, :], carry)
    return jnp.concatenate([acc[:, :HEAD_DIM] / acc[:, HEAD_DIM:] for _, acc in carry], axis=0)


def _gqa_kernel(q_ref, kc_ref, vc_ref, kl_ref, vl_ref, o_ref, *, tk):
    tq = q_ref.shape[0]
    q = q_ref[...]
    q2 = jnp.concatenate([q[:, :HEAD_DIM], q[:, HEAD_DIM:]], axis=0)
    o = _flash_pair(q2, kc_ref, vc_ref, kl_ref, vl_ref, tk)
    o_ref[:, :HEAD_DIM] = o[:tq].astype(o_ref.dtype)
    o_ref[:, HEAD_DIM:] = o[tq:].astype(o_ref.dtype)


def _diff_kernel(q_ref, kc_ref, vc_ref, kl_ref, vl_ref, lam_ref, subg_ref, o_ref, *, tk, lambda_init):
    tq = q_ref.shape[0]
    o = _flash_pair(_split_maps(q_ref[...]), kc_ref, vc_ref, kl_ref, vl_ref, tk)
    lam = _diff_lambda(lam_ref, lambda_init)
    d = o[:tq] - lam * o[tq:]
    o_ref[...] = (_rms(d) * subg_ref[...] * (1.0 - lambda_init)).astype(o_ref.dtype)


def _gqa_call(p_lat, p_ctx):
    b, n, _ = p_lat.shape
    l = p_ctx.shape[1]
    tq = 512
    tk = FLASH_KV_CHUNK
    hd = HEAD_DIM
    kcol, vcol = COL_GQA_K // hd, COL_GQA_V // hd
    return pl.pallas_call(
        functools.partial(_gqa_kernel, tk=tk),
        grid=(b, 2, n // tq),
        in_specs=[pl.BlockSpec((None, tq, 2 * hd), lambda i, k, t: (i, t, k)),
                  pl.BlockSpec((None, l, hd), lambda i, k, t: (i, 0, kcol + k)),
                  pl.BlockSpec((None, l, hd), lambda i, k, t: (i, 0, vcol + k)),
                  pl.BlockSpec((None, n, hd), lambda i, k, t: (i, 0, kcol + k)),
                  pl.BlockSpec((None, n, hd), lambda i, k, t: (i, 0, vcol + k))],
        out_specs=pl.BlockSpec((None, tq, 2 * hd), lambda i, k, t: (i, t, k)),
        out_shape=jax.ShapeDtypeStruct((b, n, BRANCH_WIDTH), BF16),
        compiler_params=_cparams(("parallel", "parallel", "arbitrary")),
        name="attn_gqa",
    )(p_lat, p_ctx, p_ctx, p_lat, p_lat)


def _diff_call(p_lat, p_ctx, lam_p, subg, lambda_init):
    b, n, _ = p_lat.shape
    l = p_ctx.shape[1]
    tq = 512
    tk = FLASH_KV_CHUNK
    hd = HEAD_DIM
    qcol, kcol, vcol = COL_DIFF_Q // hd, COL_DIFF_K // hd, COL_DIFF_V // hd
    return pl.pallas_call(
        functools.partial(_diff_kernel, tk=tk, lambda_init=lambda_init),
        grid=(b, 4, n // tq),
        in_specs=[pl.BlockSpec((None, tq, hd), lambda i, h, t: (i, t, qcol + h)),
                  pl.BlockSpec((None, l, hd), lambda i, h, t: (i, 0, kcol + h)),
                  pl.BlockSpec((None, l, hd), lambda i, h, t: (i, 0, vcol + h)),
                  pl.BlockSpec((None, n, hd), lambda i, h, t: (i, 0, kcol + h)),
                  pl.BlockSpec((None, n, hd), lambda i, h, t: (i, 0, vcol + h)),
                  pl.BlockSpec((4, DIFF_QK_DIM), lambda i, h, t: (0, 0)),
                  pl.BlockSpec((1, hd), lambda i, h, t: (0, 0))],
        out_specs=pl.BlockSpec((None, tq, hd), lambda i, h, t: (i, t, h)),
        out_shape=jax.ShapeDtypeStruct((b, n, BRANCH_WIDTH), BF16),
        compiler_params=_cparams(("parallel", "parallel", "arbitrary")),
        name="attn_diff",
    )(p_lat, p_ctx, p_ctx, p_lat, p_lat, lam_p, subg.reshape(1, hd))


NA_GROUP_ROWS = 8
NA_BLOCK_ROWS = 4
NA_BLOCK_WIN_ROWS = NA_BLOCK_ROWS + NA_ROWS
NA_WIN_ROWS = 2 * NA_GROUP_ROWS


def _with_ones(v):
    return jnp.concatenate([v, jnp.ones_like(v)], axis=1)


def _band_ctx_attend(s_b, s_c, vb1, vc1, sink=None):
    m = jnp.maximum(jnp.max(s_b, axis=-1, keepdims=True), jnp.max(s_c, axis=-1, keepdims=True))
    if sink is not None:
        m = jnp.maximum(m, sink)
    acc = _dot(jnp.exp2(s_c - m).astype(BF16), vc1) + _dot(jnp.exp2(s_b - m).astype(BF16), vb1)
    den = acc[:, HEAD_DIM:]
    if sink is not None:
        den = den + jnp.exp2(sink - m)
    return acc[:, :HEAD_DIM] / den


NA_DI = 2 * NA_ROWS - 1
NA_TAB_PAD = NA_WIN_ROWS
NA_TAB_LANES = 3072


def _na_bias_tables(rpb):
    nh = rpb.shape[0]
    qc = np.arange(GRID_W)[:, None]
    kc = np.arange(GRID_W)[None, :]
    cstart = np.clip(qc - NA_COLS // 2, 0, GRID_W - NA_COLS)
    ok_col = (kc >= cstart) & (kc < cstart + NA_COLS)
    dj = np.clip(kc - qc + NA_COLS - 1, 0, 2 * NA_COLS - 2)
    sel_j = (dj[..., None] == np.arange(2 * NA_COLS - 1)) & ok_col[..., None]
    by_col = jnp.einsum("hij,qkj->hqik", rpb.astype(F32), jnp.asarray(sel_j, F32),
                        precision=lax.Precision.HIGHEST)
    by_col = jnp.where(ok_col[None, :, None, :], by_col * LOG2E, NEG_INF)
    strip = by_col.reshape(nh, GRID_W, NA_DI * GRID_W)
    total = NA_TAB_LANES + GRID_W
    strip = jnp.pad(strip, ((0, 0), (0, 0), (NA_TAB_PAD * GRID_W, total - (NA_TAB_PAD + NA_DI) * GRID_W)),
                    constant_values=NEG_INF)
    return jnp.stack([strip[:, :, :NA_TAB_LANES], strip[:, :, GRID_W:]], axis=1)


def _na_kernel(q_ref, kc_ref, vc_ref, kl_ref, vl_ref, tab_ref, o_ref, *, rows_n):
    g = pl.program_id(2)
    win = NA_BLOCK_WIN_ROWS * GRID_W
    blk_q = NA_BLOCK_ROWS * GRID_W
    kc = kc_ref[...]
    vc1 = _with_ones(vc_ref[...])
    lane = lax.broadcasted_iota(jnp.int32, (GRID_W, win), 1)
    for rb in range(NA_GROUP_ROWS // NA_BLOCK_ROWS):
        r0 = g * NA_GROUP_ROWS + rb * NA_BLOCK_ROWS
        ws = jnp.clip(r0 - NA_ROWS // 2, 0, rows_n - NA_BLOCK_WIN_ROWS)
        off = pl.multiple_of(ws * GRID_W, NA_BLOCK_ROWS * GRID_W)
        bias = []
        for a in range(NA_BLOCK_ROWS):
            r = r0 + a
            rs = jnp.clip(r - NA_ROWS // 2, 0, rows_n - NA_ROWS)
            blk = ws - r + NA_ROWS - 1 + NA_TAB_PAD
            start = pl.multiple_of((blk >> 1) * (2 * GRID_W), 2 * GRID_W)
            strip = tab_ref[blk & 1, :, pl.ds(start, win)]
            lo = (rs - ws) * GRID_W
            in_rows = jnp.logical_and(lane >= lo, lane < lo + NA_ROWS * GRID_W)
            bias.append(jnp.where(in_rows, strip, NEG_INF))
        q = q_ref[rb * blk_q:(rb + 1) * blk_q, :]
        s_b = _dot_nt(q, kl_ref[pl.ds(off, win), :]) + jnp.concatenate(bias, axis=0)
        o = _band_ctx_attend(s_b, _dot_nt(q, kc), _with_ones(vl_ref[pl.ds(off, win), :]), vc1)
        o_ref[rb * blk_q:(rb + 1) * blk_q, :] = o.astype(o_ref.dtype)


def _na_call(p_lat, p_ctx, bias):
    b, n, _ = p_lat.shape
    l = p_ctx.shape[1]
    hd = HEAD_DIM
    rows_n = n // GRID_W
    n_groups = rows_n // NA_GROUP_ROWS
    tq = NA_GROUP_ROWS * GRID_W
    qcol, kcol, vcol = COL_NA_Q // hd, COL_NA_K // hd, COL_NA_V // hd

    return pl.pallas_call(
        functools.partial(_na_kernel, rows_n=rows_n),
        grid=(b, 4, n_groups),
        in_specs=[pl.BlockSpec((None, tq, hd), lambda i, h, g: (i, g, qcol + h)),
                  pl.BlockSpec((None, l, hd), lambda i, h, g: (i, 0, kcol + h)),
                  pl.BlockSpec((None, l, hd), lambda i, h, g: (i, 0, vcol + h)),
                  pl.BlockSpec((None, n, hd), lambda i, h, g: (i, 0, kcol + h)),
                  pl.BlockSpec((None, n, hd), lambda i, h, g: (i, 0, vcol + h)),
                  pl.BlockSpec((None, 2, GRID_W, NA_TAB_LANES), lambda i, h, g: (h, 0, 0, 0))],
        out_specs=pl.BlockSpec((None, tq, hd), lambda i, h, g: (i, g, h)),
        out_shape=jax.ShapeDtypeStruct((b, n, BRANCH_WIDTH), BF16),
        compiler_params=_cparams(("parallel", "parallel", "arbitrary")),
        name="attn_na",
    )(p_lat, p_ctx, p_ctx, p_lat, p_lat, bias)


SWA_BLOCK = 256


def _swa_kernel(q_ref, kc_ref, vc_ref, kl_ref, vl_ref, sink_ref, o_ref, *, win):
    t = pl.program_id(2)
    tq = q_ref.shape[0]
    n = kl_ref.shape[0]
    kc = kc_ref[...]
    vc1 = _with_ones(vc_ref[...])
    row = lax.broadcasted_iota(jnp.int32, (SWA_BLOCK, win), 0)
    col = lax.broadcasted_iota(jnp.int32, (SWA_BLOCK, win), 1)
    for rb in range(tq // SWA_BLOCK):
        qs = t * tq + rb * SWA_BLOCK
        ws = pl.multiple_of(jnp.clip(qs - SWA_WINDOW, 0, n - win), SWA_WINDOW)
        kb = kl_ref[pl.ds(ws, win), :]
        vb1 = _with_ones(vl_ref[pl.ds(ws, win), :])
        valid = jnp.abs(row + (qs - ws) - col) <= SWA_WINDOW
        rows = slice(rb * SWA_BLOCK, (rb + 1) * SWA_BLOCK)
        for gi in range(2):
            q = q_ref[rows, gi * HEAD_DIM:(gi + 1) * HEAD_DIM]
            s_b = jnp.where(valid, _dot_nt(q, kb), NEG_INF)
            o = _band_ctx_attend(s_b, _dot_nt(q, kc), vb1, vc1, sink=sink_ref[gi:gi + 1, 0:1])
            o_ref[rows, gi * HEAD_DIM:(gi + 1) * HEAD_DIM] = o.astype(o_ref.dtype)


def _swa_call(p_lat, p_ctx, sink_tab):
    b, n, _ = p_lat.shape
    l = p_ctx.shape[1]
    hd = HEAD_DIM
    tq = 512
    win = SWA_BLOCK + 2 * SWA_WINDOW
    qcol, kcol, vcol = COL_SWA_Q // (2 * hd), COL_SWA_K // hd, COL_SWA_V // hd
    return pl.pallas_call(
        functools.partial(_swa_kernel, win=win),
        grid=(b, 2, n // tq),
        in_specs=[pl.BlockSpec((None, tq, 2 * hd), lambda i, k, t: (i, t, qcol + k)),
                  pl.BlockSpec((None, l, hd), lambda i, k, t: (i, 0, kcol + k)),
                  pl.BlockSpec((None, l, hd), lambda i, k, t: (i, 0, vcol + k)),
                  pl.BlockSpec((None, n, hd), lambda i, k, t: (i, 0, kcol + k)),
                  pl.BlockSpec((None, n, hd), lambda i, k, t: (i, 0, vcol + k)),
                  pl.BlockSpec((None, 2, LANES), lambda i, k, t: (k, 0, 0))],
        out_specs=pl.BlockSpec((None, tq, 2 * hd), lambda i, k, t: (i, t, k)),
        out_shape=jax.ShapeDtypeStruct((b, n, BRANCH_WIDTH), BF16),
        compiler_params=_cparams(("parallel", "parallel", "arbitrary")),
        name="attn_swa",
    )(p_lat, p_ctx, p_ctx, p_lat, p_lat, sink_tab)


def _softmax_attend(q, k, v, sink=None):
    s = _dot_nt(q, k)
    m = jnp.max(s, axis=-1, keepdims=True)
    if sink is not None:
        m = jnp.maximum(m, sink)
    e = jnp.exp2(s - m)
    den = jnp.sum(e, axis=-1, keepdims=True)
    if sink is not None:
        den = den + jnp.exp2(sink - m)
    return _dot(e.astype(BF16), v) / den


def _ctx_attn_kernel(p_ref, lam_ref, subg_ref, sink_ref, o_ref, *, lambda_init):
    hd = HEAD_DIM

    def col(c0, h):
        return p_ref[:, c0 + h * hd:c0 + (h + 1) * hd]

    lam = _diff_lambda(lam_ref, lambda_init)
    for h in range(4):
        o = _softmax_attend(col(COL_GQA_Q, h), col(COL_GQA_K, h // 2), col(COL_GQA_V, h // 2))
        o_ref[:, h * hd:(h + 1) * hd] = o.astype(o_ref.dtype)
    for h in range(4):
        q = col(COL_DIFF_Q, h)
        tq = q.shape[0]
        o2 = _softmax_attend(_split_maps(q), col(COL_DIFF_K, h), col(COL_DIFF_V, h))
        d = o2[:tq] - lam * o2[tq:]
        d = _rms(d) * subg_ref[...] * (1.0 - lambda_init)
        o_ref[:, BRANCH_WIDTH + h * hd:BRANCH_WIDTH + (h + 1) * hd] = d.astype(o_ref.dtype)
    for h in range(4):
        o = _softmax_attend(col(COL_NA_Q, h), col(COL_NA_K, h), col(COL_NA_V, h))
        o_ref[:, 2 * BRANCH_WIDTH + h * hd:2 * BRANCH_WIDTH + (h + 1) * hd] = o.astype(o_ref.dtype)
    for h in range(4):
        o = _softmax_attend(col(COL_SWA_Q, h), col(COL_SWA_K, h // 2), col(COL_SWA_V, h // 2),
                            sink=sink_ref[h:h + 1, 0:1])
        o_ref[:, 3 * BRANCH_WIDTH + h * hd:3 * BRANCH_WIDTH + (h + 1) * hd] = o.astype(o_ref.dtype)


def _ctx_attn_call(p_ctx, lam_p, subg, sink_tab, lambda_init):
    b, l, cols = p_ctx.shape
    return pl.pallas_call(
        functools.partial(_ctx_attn_kernel, lambda_init=lambda_init),
        grid=(b,),
        in_specs=[pl.BlockSpec((None, l, cols), lambda i: (i, 0, 0)),
                  pl.BlockSpec((4, DIFF_QK_DIM), lambda i: (0, 0)),
                  pl.BlockSpec((1, HEAD_DIM), lambda i: (0, 0)),
                  pl.BlockSpec((4, LANES), lambda i: (0, 0))],
        out_specs=pl.BlockSpec((None, l, N_BRANCHES * BRANCH_WIDTH), lambda i: (i, 0, 0)),
        out_shape=jax.ShapeDtypeStruct((b, l, N_BRANCHES * BRANCH_WIDTH), BF16),
        compiler_params=_cparams(("parallel",)),
        name="attn_ctx",
    )(p_ctx, lam_p, subg.reshape(1, HEAD_DIM), sink_tab.reshape(4, LANES))


MERGE_CHUNKS = 4


def _merge_kernel(o0, o1, o2, o3, g_ref, wb_ref, wo_ref, x_ref, mod_ref, ng_ref, *rest, with_router):
    if with_router:
        router_ref, xo_ref, h_ref, lg_ref = rest
    else:
        xo_ref, h_ref = rest
    d = x_ref.shape[1]
    dc = d // MERGE_CHUNKS
    o_vals = [o[...] for o in (o0, o1, o2, o3)]
    z = None
    for c in range(MERGE_CHUNKS):
        s = None
        for i, o in enumerate(o_vals):
            y = _dot(o, wb_ref[i, :, c * dc:(c + 1) * dc])
            gate = g_ref[:, i * d + c * dc:i * d + (c + 1) * dc].astype(F32)
            term = jax.nn.sigmoid(gate) * y
            s = term if s is None else s + term
        zc = _dot(s.astype(BF16), wo_ref[c * dc:(c + 1) * dc, :])
        z = zc if z is None else z + zc
    xn = x_ref[...] + mod_ref[2:3, :] * z
    xo_ref[...] = xn
    hn = _norm_mod(xn, ng_ref[...], mod_ref, 3, 4)
    h_ref[...] = hn.astype(h_ref.dtype)
    if with_router:
        lg_ref[...] = _dot_split(hn, router_ref[...])


def _dot_split(a, b):
    a_hi = a.astype(BF16)
    a_lo = (a - a_hi.astype(F32)).astype(BF16)
    b_hi = b.astype(BF16)
    b_lo = (b - b_hi.astype(F32)).astype(BF16)
    n = b.shape[1]
    both = _dot(a_hi, jnp.concatenate([b_hi, b_lo], axis=1))
    return both[:, :n] + both[:, n:] + _dot(a_lo, b_hi)


def _merge_call(o_parts, gates, w_branch, w_out, x, mod, next_g, router=None):
    b, r, d = x.shape
    tm = min(r, 256)
    with_router = router is not None
    whole = dict(pipeline_mode=pl.Buffered(1))

    def o_spec(cb):
        return pl.BlockSpec((None, tm, BRANCH_WIDTH), lambda i, t: (i, t, cb))

    row_spec = pl.BlockSpec((None, tm, d), lambda i, t: (i, t, 0))
    in_specs = [o_spec(cb) for _, cb in o_parts] + [
        pl.BlockSpec((None, tm, N_BRANCHES * d), lambda i, t: (i, t, 0)),
        pl.BlockSpec((N_BRANCHES, BRANCH_WIDTH, d), lambda i, t: (0, 0, 0), **whole),
        pl.BlockSpec((d, d), lambda i, t: (0, 0), **whole),
        row_spec,
        pl.BlockSpec((None, 6, d), lambda i, t: (i, 0, 0)),
        pl.BlockSpec((1, d), lambda i, t: (0, 0))]
    args = [a for a, _ in o_parts] + [gates, w_branch, w_out, x, mod, next_g.reshape(1, d)]
    out_specs = [row_spec, row_spec]
    out_shape = [jax.ShapeDtypeStruct((b, r, d), F32), jax.ShapeDtypeStruct((b, r, d), BF16)]
    if with_router:
        in_specs.append(pl.BlockSpec((d, LANES), lambda i, t: (0, 0), **whole))
        args.append(router)
        out_specs.append(pl.BlockSpec((None, tm, LANES), lambda i, t: (i, t, 0)))
        out_shape.append(jax.ShapeDtypeStruct((b, r, LANES), F32))
    return pl.pallas_call(
        functools.partial(_merge_kernel, with_router=with_router),
        grid=(b, r // tm),
        in_specs=in_specs,
        out_specs=out_specs,
        out_shape=out_shape,
        compiler_params=_cparams(("parallel", "parallel")),
        name="merge",
    )(*args)


def _swiglu_partial(h, wg, wu, wd):
    gate = _dot(h, wg)
    up = _dot(h, wu)
    act = gate * jax.nn.sigmoid(gate) * up
    return _dot(act.astype(BF16), wd)


FFN_ROW_BLOCK = 512


def _ffn_kernel(h_ref, wg_ref, wu_ref, wd_ref, x_ref, mod_ref, ng_ref, nmod_ref, xo_ref, hn_ref,
                acc_ref):
    f = pl.program_id(2)

    @pl.when(f == 0)
    def _():
        acc_ref[...] = jnp.zeros_like(acc_ref)

    for r0 in range(0, h_ref.shape[0], FFN_ROW_BLOCK):
        rows = slice(r0, r0 + FFN_ROW_BLOCK)
        acc_ref[rows, :] += _swiglu_partial(h_ref[rows, :], wg_ref[...], wu_ref[...], wd_ref[...])

    @pl.when(f == pl.num_programs(2) - 1)
    def _():
        xn = x_ref[...] + mod_ref[5:6, :] * acc_ref[...]
        xo_ref[...] = xn
        hn_ref[...] = _norm_mod(xn, ng_ref[...], nmod_ref, 0, 1).astype(hn_ref.dtype)


def _ffn_call(h, wg, wu, wd, x, mod, next_g, next_mod):
    b, r, d = x.shape
    dff = wg.shape[1]
    tf = 512
    tm = min(r, FFN_ROW_BLOCK)
    row_spec = pl.BlockSpec((None, tm, d), lambda i, t, f: (i, t, 0))
    mod_spec = pl.BlockSpec((None, 6, d), lambda i, t, f: (i, 0, 0))
    return pl.pallas_call(
        _ffn_kernel,
        grid=(b, r // tm, dff // tf),
        in_specs=[row_spec,
                  pl.BlockSpec((d, tf), lambda i, t, f: (0, f)),
                  pl.BlockSpec((d, tf), lambda i, t, f: (0, f)),
                  pl.BlockSpec((tf, d), lambda i, t, f: (f, 0)),
                  row_spec, mod_spec,
                  pl.BlockSpec((1, d), lambda i, t, f: (0, 0)),
                  mod_spec],
        out_specs=[row_spec, row_spec],
        out_shape=[jax.ShapeDtypeStruct((b, r, d), F32), jax.ShapeDtypeStruct((b, r, d), BF16)],
        scratch_shapes=[pltpu.VMEM((tm, d), F32)],
        compiler_params=_cparams(("parallel", "parallel", "arbitrary")),
        name="ffn_dense",
    )(h, wg, wu, wd, x, mod, next_g.reshape(1, d), next_mod)


def _route_kernel(lg_ref, idx_ref, w_ref):
    lg = lg_ref[...]
    lane = lax.broadcasted_iota(jnp.int32, lg.shape, 1)
    valid = lane < N_EXPERTS
    mx = jnp.max(jnp.where(valid, lg, -jnp.inf), axis=-1, keepdims=True)
    e = jnp.where(valid, jnp.exp(lg - mx), 0.0)
    p = e / jnp.sum(e, axis=-1, keepdims=True)
    p1 = jnp.max(p, axis=-1, keepdims=True)
    i1 = jnp.min(jnp.where(p == p1, lane, LANES), axis=-1, keepdims=True)
    rest = jnp.where(jnp.logical_or(lane == i1, jnp.logical_not(valid)), -1.0, p)
    p2 = jnp.max(rest, axis=-1, keepdims=True)
    i2 = jnp.min(jnp.where(rest == p2, lane, LANES), axis=-1, keepdims=True)
    tot = p1 + p2
    idx_ref[...] = jnp.where(lane == 0, i1, jnp.where(lane == 1, i2, 0))
    w_ref[...] = jnp.where(lane == 0, p1 / tot, jnp.where(lane == 1, p2 / tot, 0.0))


def _route_call(logits):
    r = logits.shape[0]
    tm = min(r, 1024)
    spec = pl.BlockSpec((tm, LANES), lambda i: (i, 0))
    return pl.pallas_call(
        _route_kernel,
        grid=(r // tm,),
        in_specs=[spec],
        out_specs=[spec, spec],
        out_shape=[jax.ShapeDtypeStruct((r, LANES), jnp.int32), jax.ShapeDtypeStruct((r, LANES), F32)],
        compiler_params=_cparams(("parallel",)),
        name="route_top2",
    )(logits)


MOE_HALF = 512
MOE_TILE = 2 * MOE_HALF
MOE_FF_TILE = 512
MOE_VMEM_LIMIT = 60 * 1024 * 1024


def _moe_kernel(te_ref, nr_ref, x_ref, wg_ref, wu_ref, wd_ref, o_ref, acc_ref):
    i = pl.program_id(0)
    f = pl.program_id(1)

    @pl.when(f == 0)
    def _():
        acc_ref[...] = jnp.zeros_like(acc_ref)

    for half in range(MOE_TILE // MOE_HALF):
        rows = slice(half * MOE_HALF, (half + 1) * MOE_HALF)

        @pl.when(nr_ref[i] > half * MOE_HALF)
        def _():
            acc_ref[rows, :] += _swiglu_partial(x_ref[rows, :], wg_ref[...].astype(BF16),
                                                wu_ref[...].astype(BF16), wd_ref[...].astype(BF16))

    @pl.when(f == pl.num_programs(1) - 1)
    def _():
        o_ref[...] = acc_ref[...].astype(o_ref.dtype)


def _moe_call(xs, wg, wu, wd, tile_expert, tile_rows):
    r, d = xs.shape
    dff = wg.shape[2]
    tf = MOE_FF_TILE
    nf = dff // tf

    def fidx(i, f, nr):
        return jnp.where(nr[i] > 0, f, nf - 1)

    grid_spec = pltpu.PrefetchScalarGridSpec(
        num_scalar_prefetch=2,
        grid=(r // MOE_TILE, nf),
        in_specs=[pl.BlockSpec((MOE_TILE, d), lambda i, f, te, nr: (i, 0)),
                  pl.BlockSpec((None, d, tf), lambda i, f, te, nr: (te[i], 0, fidx(i, f, nr))),
                  pl.BlockSpec((None, d, tf), lambda i, f, te, nr: (te[i], 0, fidx(i, f, nr))),
                  pl.BlockSpec((None, tf, d), lambda i, f, te, nr: (te[i], fidx(i, f, nr), 0))],
        out_specs=pl.BlockSpec((MOE_TILE, d), lambda i, f, te, nr: (i, 0)),
        scratch_shapes=[pltpu.VMEM((MOE_TILE, d), F32)])
    return pl.pallas_call(
        _moe_kernel,
        grid_spec=grid_spec,
        out_shape=jax.ShapeDtypeStruct((r, d), BF16),
        compiler_params=pltpu.CompilerParams(dimension_semantics=("arbitrary", "arbitrary"),
                                             vmem_limit_bytes=MOE_VMEM_LIMIT),
        name="moe_experts",
    )(tile_expert, tile_rows, xs, wg, wu, wd)


def _moe_plan(idx):
    t = idx.shape[0]
    a = t * TOP_K
    flat_e = idx.reshape(a)
    onehot = (flat_e[:, None] == jnp.arange(N_EXPERTS, dtype=jnp.int32)[None, :]).astype(jnp.int32)
    csum = jnp.cumsum(onehot, axis=0)
    rank = jnp.sum((csum - onehot) * onehot, axis=1)
    counts = csum[-1]
    padded = ((counts + MOE_TILE - 1) // MOE_TILE) * MOE_TILE
    ends = jnp.cumsum(padded)
    offs = ends - padded
    dest = offs[flat_e] + rank
    rows = a + N_EXPERTS * MOE_TILE
    src_tok = (jnp.arange(rows, dtype=jnp.int32) % t).at[dest].set(jnp.arange(a, dtype=jnp.int32) // TOP_K)
    tile_start = jnp.arange(rows // MOE_TILE, dtype=jnp.int32) * MOE_TILE
    past = jnp.sum((tile_start[:, None] >= ends[None, :]).astype(jnp.int32), axis=1)
    tile_expert = jnp.minimum(past, N_EXPERTS - 1)
    tile_rows = jnp.where(past < N_EXPERTS,
                          jnp.clip((offs + counts)[tile_expert] - tile_start, 0, MOE_TILE), 0)
    return dest.reshape(t, TOP_K), src_tok, tile_expert, tile_rows.astype(jnp.int32)


def _final_kernel(x_ref, y0_ref, y1_ref, w_ref, mod_ref, g_ref, o_ref):
    y = w_ref[:, 0:1] * y0_ref[...].astype(F32) + w_ref[:, 1:2] * y1_ref[...].astype(F32)
    xn = x_ref[...] + mod_ref[5:6, :] * y
    o_ref[...] = _rms(xn) * g_ref[...]


def _final_call(x, y0, y1, w, mod, g):
    b, r, d = x.shape
    tm = min(r, 512)
    row_spec = pl.BlockSpec((None, tm, d), lambda i, t: (i, t, 0))
    return pl.pallas_call(
        _final_kernel,
        grid=(b, r // tm),
        in_specs=[row_spec, row_spec, row_spec,
                  pl.BlockSpec((None, tm, LANES), lambda i, t: (i, t, 0)),
                  pl.BlockSpec((None, 6, d), lambda i, t: (i, 0, 0)),
                  pl.BlockSpec((1, d), lambda i, t: (0, 0))],
        out_specs=row_spec,
        out_shape=jax.ShapeDtypeStruct((b, r, d), F32),
        compiler_params=_cparams(("parallel", "parallel")),
        name="final_norm",
    )(x, y0, y1, w, mod, g.reshape(1, d))


def _rope_tables(n):
    pos = jnp.arange(n)
    rows = (pos // GRID_W).astype(F32)
    cols = (pos % GRID_W).astype(F32)
    lane = np.arange(LANES)
    out = []
    for hw in (32, 16):
        period = 4 * hw
        u = lane % period
        use_cols = (u // (2 * hw)) == 1
        w = u % (2 * hw)
        freqs = jnp.asarray(ROPE_THETA ** (-(w % hw).astype(np.float32) / hw), F32)
        p = jnp.where(jnp.asarray(use_cols)[None, :], cols[:, None], rows[:, None])
        ang = p * freqs[None, :]
        sign = jnp.asarray(np.where(w < hw, -1.0, 1.0), F32)
        out += [jnp.cos(ang), jnp.sin(ang) * sign[None, :]]
    return out


def kernel(x, c, ctx, c_ctx, attn_norm_g, ffn_norm_g, ada_w, ada_b, w_in, qk_norm_g, diff_lambda,
           diff_subln_g, na_rpb, swa_sink, w_branch, w_out, ffn_w_gate, ffn_w_up, ffn_w_down,
           moe_router, moe_w_gate, moe_w_up, moe_w_down, final_norm_g):
    b, n, d = x.shape
    l = ctx.shape[1]
    depth = w_in.shape[0]
    assert depth == 2, "laid out for one dense layer followed by one routed last layer"
    rows_n = n // GRID_W

    lat_tables = _rope_tables(n)
    ctx_flat = ctx.reshape(1, b * l, d)
    ones = jnp.ones((b * l, LANES), F32)
    zeros = jnp.zeros((b * l, LANES), F32)
    ctx_tables = [ones, zeros, ones, zeros]

    cvec = jnp.zeros((8, d), F32).at[:b].set(c).at[b].set(c_ctx)
    mods = []
    for i in range(depth):
        m = _adaln(cvec, ada_w, ada_b, i).reshape(8, 6, d)
        mods.append((m[:b], m[b:b + 1]))

    def mixers(i, h_lat, h_ctx):
        lambda_init = 0.8 - 0.6 * math.exp(-0.3 * i)
        w_qkv = _qkv_weights_call(w_in, i)
        sink_tab = jnp.broadcast_to((swa_sink[i].astype(F32) * LOG2E)[:, None], (4, LANES))
        p_lat = _qkv_call(h_lat, w_qkv, lat_tables, qk_norm_g[i])
        p_ctx = _qkv_call(h_ctx, w_qkv, ctx_tables, qk_norm_g[i]).reshape(b, l, QKV_COLS)
        o_lat = [_gqa_call(p_lat, p_ctx),
                 _diff_call(p_lat, p_ctx, diff_lambda[i], diff_subln_g[i], lambda_init),
                 _na_call(p_lat, p_ctx, _na_bias_tables(na_rpb[i])),
                 _swa_call(p_lat, p_ctx, sink_tab.reshape(2, 2, LANES))]
        return p_ctx, o_lat, sink_tab, lambda_init

    mod_lat, mod_ctx = mods[0]
    h_lat = _norm_mod_call(x, attn_norm_g[0], mod_lat, 0, 1)
    h_ctx = _norm_mod_call(ctx_flat, attn_norm_g[0], mod_ctx, 0, 1)
    p_ctx, o_lat, sink_tab, lambda_init = mixers(0, h_lat, h_ctx)
    wb = w_branch[0].astype(BF16)
    wo = w_out[0].astype(BF16)
    x_lat, h_lat = _merge_call([(o, 0) for o in o_lat], _gate_proj_call(h_lat, w_in, 0), wb, wo, x,
                               mod_lat, ffn_norm_g[0])
    o_ctx = _ctx_attn_call(p_ctx, diff_lambda[0], diff_subln_g[0], sink_tab, lambda_init)
    o_ctx = o_ctx.reshape(1, b * l, N_BRANCHES * BRANCH_WIDTH)
    x_ctx, h_ctx = _merge_call([(o_ctx, k) for k in range(N_BRANCHES)], _gate_proj_call(h_ctx, w_in, 0),
                               wb, wo, ctx_flat, mod_ctx, ffn_norm_g[0])
    wg = ffn_w_gate[0].astype(BF16)
    wu = ffn_w_up[0].astype(BF16)
    wd = ffn_w_down[0].astype(BF16)
    _, h_ctx = _ffn_call(h_ctx, wg, wu, wd, x_ctx, mod_ctx, attn_norm_g[1], mods[1][1])
    x_lat, h_lat = _ffn_call(h_lat, wg, wu, wd, x_lat, mod_lat, attn_norm_g[1], mods[1][0])

    mod_lat, _ = mods[1]
    _, o_lat, _, _ = mixers(1, h_lat, h_ctx)
    router = jnp.zeros((d, LANES), F32).at[:, :N_EXPERTS].set(moe_router[0])
    x_lat, h_lat, logits = _merge_call(
        [(o, 0) for o in o_lat], _gate_proj_call(h_lat, w_in, 1),
        w_branch[1].astype(BF16), w_out[1].astype(BF16), x_lat, mod_lat, ffn_norm_g[1], router)

    idx_pad, wts_pad = _route_call(logits.reshape(b * n, LANES))
    dest, src_tok, tile_expert, tile_rows = _moe_plan(idx_pad[:, :TOP_K])
    xs = h_lat.reshape(b * n, d).at[src_tok].get(mode="promise_in_bounds")
    ys = _moe_call(xs, moe_w_gate[0], moe_w_up[0], moe_w_down[0], tile_expert, tile_rows)
    y0 = ys.at[dest[:, 0]].get(mode="promise_in_bounds").reshape(b, n, d)
    y1 = ys.at[dest[:, 1]].get(mode="promise_in_bounds").reshape(b, n, d)
    return _final_call(x_lat, y0, y1, wts_pad.reshape(b, n, LANES), mod_lat, final_norm_g)
```

```python
import functools
import math

import numpy as np
import jax
import jax.numpy as jnp
from jax import lax
from jax.experimental import pallas as pl
from jax.experimental.pallas import tpu as pltpu

F32 = jnp.float32
BF16 = jnp.bfloat16

GRID_W = 64
HEAD_DIM = 128
N_BRANCHES = 4
BRANCH_WIDTH = 4 * HEAD_DIM
DIFF_QK_DIM = 64
NA_ROWS = 8
NA_COLS = 16
SWA_WINDOW = 128
N_EXPERTS = 8
TOP_K = 2
NORM_EPS = 1e-6
ROPE_THETA = 10000.0
NEG_INF = -1e30
LOG2E = math.log2(math.e)

QKV_COLS = 5120
COL_GQA_Q, COL_GQA_K, COL_GQA_V = 0, 512, 768
COL_DIFF_Q, COL_DIFF_K, COL_DIFF_V = 1024, 1536, 2048
COL_NA_Q, COL_NA_K, COL_NA_V = 2560, 3072, 3584
COL_SWA_Q, COL_SWA_K, COL_SWA_V = 4096, 4608, 4864

LANES = 128
VMEM_LIMIT = 56 * 1024 * 1024

QS128 = HEAD_DIM ** -0.5 * LOG2E
QS64 = DIFF_QK_DIM ** -0.5 * LOG2E


def _cparams(sem):
    return pltpu.CompilerParams(dimension_semantics=sem, vmem_limit_bytes=VMEM_LIMIT)


def _dot(a, b):
    return jnp.dot(a, b, preferred_element_type=F32)


def _dot_nt(a, b):
    return lax.dot_general(a, b, (((1,), (1,)), ((), ())), preferred_element_type=F32)


def _rms(x):
    return x * lax.rsqrt(jnp.mean(x * x, axis=-1, keepdims=True) + NORM_EPS)


def _norm_mod(x, g, mod_ref, shift_idx, scale_idx):
    y = _rms(x) * g
    return y * (1.0 + mod_ref[scale_idx:scale_idx + 1, :]) + mod_ref[shift_idx:shift_idx + 1, :]


def _adaln_kernel(c_ref, w_ref, b_ref, o_ref):
    c = c_ref[...]
    s = c * jax.nn.sigmoid(c)
    o_ref[...] = jnp.dot(s, w_ref[...], preferred_element_type=F32,
                         precision=lax.Precision.HIGHEST) + b_ref[...]


def _adaln(cvec, w, b, layer):
    rows, d = cvec.shape
    cols = w.shape[2]
    tn = 1024 if cols % 1024 == 0 else cols
    return pl.pallas_call(
        _adaln_kernel,
        grid=(cols // tn,),
        in_specs=[pl.BlockSpec((rows, d), lambda j: (0, 0)),
                  pl.BlockSpec((None, d, tn), lambda j: (layer, 0, j)),
                  pl.BlockSpec((None, 1, tn), lambda j: (layer, 0, j))],
        out_specs=pl.BlockSpec((rows, tn), lambda j: (0, j)),
        out_shape=jax.ShapeDtypeStruct((rows, cols), F32),
        compiler_params=_cparams(("arbitrary",)),
        name="adaln",
    )(cvec, w, b.reshape(b.shape[0], 1, cols))


def _norm_mod_kernel(x_ref, g_ref, mod_ref, h_ref, *, shift_idx, scale_idx):
    h_ref[...] = _norm_mod(x_ref[...], g_ref[...], mod_ref, shift_idx, scale_idx).astype(BF16)


def _norm_mod_call(x, g, mod, shift_idx, scale_idx):
    b, r, d = x.shape
    tm = min(r, 512)
    return pl.pallas_call(
        functools.partial(_norm_mod_kernel, shift_idx=shift_idx, scale_idx=scale_idx),
        grid=(b, r // tm),
        in_specs=[pl.BlockSpec((None, tm, d), lambda i, t: (i, t, 0)),
                  pl.BlockSpec((1, d), lambda i, t: (0, 0)),
                  pl.BlockSpec((None, 6, d), lambda i, t: (i, 0, 0))],
        out_specs=pl.BlockSpec((None, tm, d), lambda i, t: (i, t, 0)),
        out_shape=jax.ShapeDtypeStruct((b, r, d), BF16),
        compiler_params=_cparams(("parallel", "parallel")),
        name="norm_mod",
    )(x, g.reshape(1, d), mod)


QKV_TILE = 4 * LANES
_PLAIN = (None, None, 1.0)
_QKV_TILE_OPS = {
    0: [(0, 32, QS128)] * 4,
    1: [(1, 32, 1.0)] * 2 + [_PLAIN] * 2,
    2: [(None, 16, QS64)] * 4,
    3: [(None, 16, 1.0)] * 4,
    5: [(None, None, QS128)] * 4,
    8: [(None, 32, QS128)] * 4,
    9: [(None, 32, 1.0)] * 2 + [_PLAIN] * 2,
}


def _rope(y, cos, sin, hw):
    lane = lax.broadcasted_iota(jnp.int32, y.shape, 1)
    first = (lane % (2 * hw)) < hw
    partner = jnp.where(first, pltpu.roll(y, LANES - hw, 1), pltpu.roll(y, hw, 1))
    return y * cos + partner * sin


def _qkv_kernel(h_ref, w_ref, cos32_ref, sin32_ref, cos16_ref, sin16_ref, g_ref, o_ref):
    h = h_ref[...]
    for j in range(o_ref.shape[1] // QKV_TILE):
        acc = _dot(h, w_ref[:, j * QKV_TILE:(j + 1) * QKV_TILE])
        for c, (norm_row, hw, scale) in enumerate(_QKV_TILE_OPS.get(j, [_PLAIN] * 4)):
            y = acc[:, c * LANES:(c + 1) * LANES]
            if norm_row is not None:
                y = _rms(y) * g_ref[norm_row:norm_row + 1, :]
            if hw == 32:
                y = _rope(y, cos32_ref[...], sin32_ref[...], 32)
            elif hw == 16:
                y = _rope(y, cos16_ref[...], sin16_ref[...], 16)
            if scale != 1.0:
                y = y * scale
            col = j * QKV_TILE + c * LANES
            o_ref[:, col:col + LANES] = y.astype(o_ref.dtype)


def _qkv_call(h, w_qkv, tables, qk_g):
    b, r, d = h.shape
    cols = w_qkv.shape[1]
    tm = min(r, 512)
    tab_spec = pl.BlockSpec((tm, LANES), lambda i, t: (t, 0))
    return pl.pallas_call(
        _qkv_kernel,
        grid=(b, r // tm),
        in_specs=[pl.BlockSpec((None, tm, d), lambda i, t: (i, t, 0)),
                  pl.BlockSpec((d, cols), lambda i, t: (0, 0), pipeline_mode=pl.Buffered(1)),
                  tab_spec, tab_spec, tab_spec, tab_spec,
                  pl.BlockSpec((2, LANES), lambda i, t: (0, 0))],
        out_specs=pl.BlockSpec((None, tm, cols), lambda i, t: (i, t, 0)),
        out_shape=jax.ShapeDtypeStruct((b, r, cols), BF16),
        compiler_params=_cparams(("parallel", "parallel")),
        name="qkv_proj",
    )(h, w_qkv, *tables, qk_g)


def _matmul_kernel(h_ref, w_ref, o_ref):
    o_ref[...] = _dot(h_ref[...], w_ref[...].astype(BF16)).astype(o_ref.dtype)


def _gate_proj_call(h, w_in, layer):
    b, r, d = h.shape
    tn = 1024
    cols = w_in.shape[2] - QKV_COLS
    col0 = QKV_COLS // tn
    tm = 2048 if r % 2048 == 0 else r
    return pl.pallas_call(
        _matmul_kernel,
        grid=(b, r // tm, cols // tn),
        in_specs=[pl.BlockSpec((None, tm, d), lambda i, t, j: (i, t, 0)),
                  pl.BlockSpec((None, d, tn), lambda i, t, j: (layer, 0, col0 + j))],
        out_specs=pl.BlockSpec((None, tm, tn), lambda i, t, j: (i, t, j)),
        out_shape=jax.ShapeDtypeStruct((b, r, cols), BF16),
        compiler_params=_cparams(("parallel", "parallel", "arbitrary")),
        name="gate_proj",
    )(h, w_in)


def _cast_kernel(w_ref, o_ref):
    o_ref[...] = w_ref[...].astype(o_ref.dtype)


def _qkv_weights_call(w_in, layer):
    d = w_in.shape[1]
    return pl.pallas_call(
        _cast_kernel,
        grid=(QKV_COLS // QKV_TILE,),
        in_specs=[pl.BlockSpec((None, d, QKV_TILE), lambda j: (layer, 0, j))],
        out_specs=pl.BlockSpec((d, QKV_TILE), lambda j: (0, j)),
        out_shape=jax.ShapeDtypeStruct((d, QKV_COLS), BF16),
        compiler_params=_cparams(("parallel",)),
        name="qkv_weights",
    )(w_in)


def _diff_lambda(lam_ref, lambda_init):
    a = jnp.sum(lam_ref[0:1, :] * lam_ref[1:2, :], axis=-1, keepdims=True)
    b = jnp.sum(lam_ref[2:3, :] * lam_ref[3:4, :], axis=-1, keepdims=True)
    return jnp.exp(a) - jnp.exp(b) + lambda_init


def _split_maps(q):
    lane = lax.broadcasted_iota(jnp.int32, q.shape, 1)
    zero = jnp.zeros_like(q)
    return jnp.concatenate([jnp.where(lane < DIFF_QK_DIM, q, zero),
                            jnp.where(lane >= DIFF_QK_DIM, q, zero)], axis=0)


FLASH_ROW_BLOCK = 128
FLASH_KV_CHUNK = 512


def _flash_pair(q2, kc_ref, vc_ref, kl_ref, vl_ref, tk):
    m_rows = q2.shape[0]
    n_lat = kl_ref.shape[0]
    nb = m_rows // FLASH_ROW_BLOCK
    qs = [q2[i * FLASH_ROW_BLOCK:(i + 1) * FLASH_ROW_BLOCK] for i in range(nb)]

    def step(k, v, carry):
        v1 = jnp.concatenate([v, jnp.ones_like(v)], axis=1)
        out = []
        for q, (m, acc) in zip(qs, carry):
            s = _dot_nt(q, k)
            m_new = jnp.maximum(m, jnp.max(s, axis=-1, keepdims=True))
            p = jnp.exp2(s - m_new)
            acc = jnp.exp2(m - m_new) * acc + _dot(p.astype(BF16), v1)
            out.append((m_new, acc))
        return tuple(out)

    carry = tuple((jnp.full((FLASH_ROW_BLOCK, 1), NEG_INF, F32),
                   jnp.zeros((FLASH_ROW_BLOCK, 2 * HEAD_DIM), F32)) for _ in range(nb))
    carry = step(kc_ref[...], vc_ref[...], carry)

    for c in range(n_lat // tk):
        carry = step(kl_ref[c * tk:(c + 1) * tk, :], vl_ref[c * tk:(c + 1) * tk, :], carry)
    return jnp.concatenate([acc[:, :HEAD_DIM] / acc[:, HEAD_DIM:] for _, acc in carry], axis=0)


def _gqa_kernel(q_ref, kc_ref, vc_ref, kl_ref, vl_ref, o_ref, *, tk):
    tq = q_ref.shape[0]
    q = q_ref[...]
    q2 = jnp.concatenate([q[:, :HEAD_DIM], q[:, HEAD_DIM:]], axis=0)
    o = _flash_pair(q2, kc_ref, vc_ref, kl_ref, vl_ref, tk)
    o_ref[:, :HEAD_DIM] = o[:tq].astype(o_ref.dtype)
    o_ref[:, HEAD_DIM:] = o[tq:].astype(o_ref.dtype)


def _diff_kernel(q_ref, kc_ref, vc_ref, kl_ref, vl_ref, lam_ref, subg_ref, o_ref, *, tk, lambda_init):
    tq = q_ref.shape[0]
    o = _flash_pair(_split_maps(q_ref[...]), kc_ref, vc_ref, kl_ref, vl_ref, tk)
    lam = _diff_lambda(lam_ref, lambda_init)
    d = o[:tq] - lam * o[tq:]
    o_ref[...] = (_rms(d) * subg_ref[...] * (1.0 - lambda_init)).astype(o_ref.dtype)


def _gqa_call(p_lat, p_ctx):
    b, n, _ = p_lat.shape
    l = p_ctx.shape[1]
    tq = 512
    tk = FLASH_KV_CHUNK
    hd = HEAD_DIM
    kcol, vcol = COL_GQA_K // hd, COL_GQA_V // hd
    return pl.pallas_call(
        functools.partial(_gqa_kernel, tk=tk),
        grid=(b, 2, n // tq),
        in_specs=[pl.BlockSpec((None, tq, 2 * hd), lambda i, k, t: (i, t, k)),
                  pl.BlockSpec((None, l, hd), lambda i, k, t: (i, 0, kcol + k)),
                  pl.BlockSpec((None, l, hd), lambda i, k, t: (i, 0, vcol + k)),
                  pl.BlockSpec((None, n, hd), lambda i, k, t: (i, 0, kcol + k)),
                  pl.BlockSpec((None, n, hd), lambda i, k, t: (i, 0, vcol + k))],
        out_specs=pl.BlockSpec((None, tq, 2 * hd), lambda i, k, t: (i, t, k)),
        out_shape=jax.ShapeDtypeStruct((b, n, BRANCH_WIDTH), BF16),
        compiler_params=_cparams(("parallel", "parallel", "arbitrary")),
        name="attn_gqa",
    )(p_lat, p_ctx, p_ctx, p_lat, p_lat)


def _diff_call(p_lat, p_ctx, lam_p, subg, lambda_init):
    b, n, _ = p_lat.shape
    l = p_ctx.shape[1]
    tq = 512
    tk = FLASH_KV_CHUNK
    hd = HEAD_DIM
    qcol, kcol, vcol = COL_DIFF_Q // hd, COL_DIFF_K // hd, COL_DIFF_V // hd
    return pl.pallas_call(
        functools.partial(_diff_kernel, tk=tk, lambda_init=lambda_init),
        grid=(b, 4, n // tq),
        in_specs=[pl.BlockSpec((None, tq, hd), lambda i, h, t: (i, t, qcol + h)),
                  pl.BlockSpec((None, l, hd), lambda i, h, t: (i, 0, kcol + h)),
                  pl.BlockSpec((None, l, hd), lambda i, h, t: (i, 0, vcol + h)),
                  pl.BlockSpec((None, n, hd), lambda i, h, t: (i, 0, kcol + h)),
                  pl.BlockSpec((None, n, hd), lambda i, h, t: (i, 0, vcol + h)),
                  pl.BlockSpec((4, DIFF_QK_DIM), lambda i, h, t: (0, 0)),
                  pl.BlockSpec((1, hd), lambda i, h, t: (0, 0))],
        out_specs=pl.BlockSpec((None, tq, hd), lambda i, h, t: (i, t, h)),
        out_shape=jax.ShapeDtypeStruct((b, n, BRANCH_WIDTH), BF16),
        compiler_params=_cparams(("parallel", "parallel", "arbitrary")),
        name="attn_diff",
    )(p_lat, p_ctx, p_ctx, p_lat, p_lat, lam_p, subg.reshape(1, hd))


NA_GROUP_ROWS = 8
NA_BLOCK_ROWS = 4
NA_BLOCK_WIN_ROWS = NA_BLOCK_ROWS + NA_ROWS
NA_WIN_ROWS = 2 * NA_GROUP_ROWS


def _with_ones(v):
    return jnp.concatenate([v, jnp.ones_like(v)], axis=1)


def _band_ctx_attend(s_b, s_c, vb1, vc1, sink=None):
    m = jnp.maximum(jnp.max(s_b, axis=-1, keepdims=True), jnp.max(s_c, axis=-1, keepdims=True))
    if sink is not None:
        m = jnp.maximum(m, sink)
    acc = _dot(jnp.exp2(s_c - m).astype(BF16), vc1) + _dot(jnp.exp2(s_b - m).astype(BF16), vb1)
    den = acc[:, HEAD_DIM:]
    if sink is not None:
        den = den + jnp.exp2(sink - m)
    return acc[:, :HEAD_DIM] / den


NA_DI = 2 * NA_ROWS - 1
NA_TAB_PAD = NA_WIN_ROWS
NA_TAB_LANES = 3072


def _na_bias_tables(rpb):
    nh = rpb.shape[0]
    qc = np.arange(GRID_W)[:, None]
    kc = np.arange(GRID_W)[None, :]
    cstart = np.clip(qc - NA_COLS // 2, 0, GRID_W - NA_COLS)
    ok_col = (kc >= cstart) & (kc < cstart + NA_COLS)
    dj = np.clip(kc - qc + NA_COLS - 1, 0, 2 * NA_COLS - 2)
    sel_j = (dj[..., None] == np.arange(2 * NA_COLS - 1)) & ok_col[..., None]
    by_col = jnp.einsum("hij,qkj->hqik", rpb.astype(F32), jnp.asarray(sel_j, F32),
                        precision=lax.Precision.HIGHEST)
    by_col = jnp.where(ok_col[None, :, None, :], by_col * LOG2E, NEG_INF)
    strip = by_col.reshape(nh, GRID_W, NA_DI * GRID_W)
    total = NA_TAB_LANES + GRID_W
    strip = jnp.pad(strip, ((0, 0), (0, 0), (NA_TAB_PAD * GRID_W, total - (NA_TAB_PAD + NA_DI) * GRID_W)),
                    constant_values=NEG_INF)
    return jnp.stack([strip[:, :, :NA_TAB_LANES], strip[:, :, GRID_W:]], axis=1)


def _na_kernel(q_ref, kc_ref, vc_ref, kl_ref, vl_ref, tab_ref, o_ref, *, rows_n):
    g = pl.program_id(2)
    win = NA_BLOCK_WIN_ROWS * GRID_W
    blk_q = NA_BLOCK_ROWS * GRID_W
    kc = kc_ref[...]
    vc1 = _with_ones(vc_ref[...])
    lane = lax.broadcasted_iota(jnp.int32, (GRID_W, win), 1)
    for rb in range(NA_GROUP_ROWS // NA_BLOCK_ROWS):
        r0 = g * NA_GROUP_ROWS + rb * NA_BLOCK_ROWS
        ws = jnp.clip(r0 - NA_ROWS // 2, 0, rows_n - NA_BLOCK_WIN_ROWS)
        off = pl.multiple_of(ws * GRID_W, NA_BLOCK_ROWS * GRID_W)
        bias = []
        for a in range(NA_BLOCK_ROWS):
            r = r0 + a
            rs = jnp.clip(r - NA_ROWS // 2, 0, rows_n - NA_ROWS)
            blk = ws - r + NA_ROWS - 1 + NA_TAB_PAD
            start = pl.multiple_of((blk >> 1) * (2 * GRID_W), 2 * GRID_W)
            strip = tab_ref[blk & 1, :, pl.ds(start, win)]
            lo = (rs - ws) * GRID_W
            in_rows = jnp.logical_and(lane >= lo, lane < lo + NA_ROWS * GRID_W)
            bias.append(jnp.where(in_rows, strip, NEG_INF))
        q = q_ref[rb * blk_q:(rb + 1) * blk_q, :]
        s_b = _dot_nt(q, kl_ref[pl.ds(off, win), :]) + jnp.concatenate(bias, axis=0)
        o = _band_ctx_attend(s_b, _dot_nt(q, kc), _with_ones(vl_ref[pl.ds(off, win), :]), vc1)
        o_ref[rb * blk_q:(rb + 1) * blk_q, :] = o.astype(o_ref.dtype)


def _na_call(p_lat, p_ctx, bias):
    b, n, _ = p_lat.shape
    l = p_ctx.shape[1]
    hd = HEAD_DIM
    rows_n = n // GRID_W
    n_groups = rows_n // NA_GROUP_ROWS
    tq = NA_GROUP_ROWS * GRID_W
    qcol, kcol, vcol = COL_NA_Q // hd, COL_NA_K // hd, COL_NA_V // hd

    return pl.pallas_call(
        functools.partial(_na_kernel, rows_n=rows_n),
        grid=(b, 4, n_groups),
        in_specs=[pl.BlockSpec((None, tq, hd), lambda i, h, g: (i, g, qcol + h)),
                  pl.BlockSpec((None, l, hd), lambda i, h, g: (i, 0, kcol + h)),
                  pl.BlockSpec((None, l, hd), lambda i, h, g: (i, 0, vcol + h)),
                  pl.BlockSpec((None, n, hd), lambda i, h, g: (i, 0, kcol + h)),
                  pl.BlockSpec((None, n, hd), lambda i, h, g: (i, 0, vcol + h)),
                  pl.BlockSpec((None, 2, GRID_W, NA_TAB_LANES), lambda i, h, g: (h, 0, 0, 0))],
        out_specs=pl.BlockSpec((None, tq, hd), lambda i, h, g: (i, g, h)),
        out_shape=jax.ShapeDtypeStruct((b, n, BRANCH_WIDTH), BF16),
        compiler_params=_cparams(("parallel", "parallel", "arbitrary")),
        name="attn_na",
    )(p_lat, p_ctx, p_ctx, p_lat, p_lat, bias)


SWA_BLOCK = 256


def _swa_kernel(q_ref, kc_ref, vc_ref, kl_ref, vl_ref, sink_ref, o_ref, *, win):
    t = pl.program_id(2)
    tq = q_ref.shape[0]
    n = kl_ref.shape[0]
    kc = kc_ref[...]
    vc1 = _with_ones(vc_ref[...])
    row = lax.broadcasted_iota(jnp.int32, (SWA_BLOCK, win), 0)
    col = lax.broadcasted_iota(jnp.int32, (SWA_BLOCK, win), 1)
    for rb in range(tq // SWA_BLOCK):
        qs = t * tq + rb * SWA_BLOCK
        ws = pl.multiple_of(jnp.clip(qs - SWA_WINDOW, 0, n - win), SWA_WINDOW)
        kb = kl_ref[pl.ds(ws, win), :]
        vb1 = _with_ones(vl_ref[pl.ds(ws, win), :])
        valid = jnp.abs(row + (qs - ws) - col) <= SWA_WINDOW
        rows = slice(rb * SWA_BLOCK, (rb + 1) * SWA_BLOCK)
        for gi in range(2):
            q = q_ref[rows, gi * HEAD_DIM:(gi + 1) * HEAD_DIM]
            s_b = jnp.where(valid, _dot_nt(q, kb), NEG_INF)
            o = _band_ctx_attend(s_b, _dot_nt(q, kc), vb1, vc1, sink=sink_ref[gi:gi + 1, 0:1])
            o_ref[rows, gi * HEAD_DIM:(gi + 1) * HEAD_DIM] = o.astype(o_ref.dtype)


def _swa_call(p_lat, p_ctx, sink_tab):
    b, n, _ = p_lat.shape
    l = p_ctx.shape[1]
    hd = HEAD_DIM
    tq = 512
    win = SWA_BLOCK + 2 * SWA_WINDOW
    qcol, kcol, vcol = COL_SWA_Q // (2 * hd), COL_SWA_K // hd, COL_SWA_V // hd
    return pl.pallas_call(
        functools.partial(_swa_kernel, win=win),
        grid=(b, 2, n // tq),
        in_specs=[pl.BlockSpec((None, tq, 2 * hd), lambda i, k, t: (i, t, qcol + k)),
                  pl.BlockSpec((None, l, hd), lambda i, k, t: (i, 0, kcol + k)),
                  pl.BlockSpec((None, l, hd), lambda i, k, t: (i, 0, vcol + k)),
                  pl.BlockSpec((None, n, hd), lambda i, k, t: (i, 0, kcol + k)),
                  pl.BlockSpec((None, n, hd), lambda i, k, t: (i, 0, vcol + k)),
                  pl.BlockSpec((None, 2, LANES), lambda i, k, t: (k, 0, 0))],
        out_specs=pl.BlockSpec((None, tq, 2 * hd), lambda i, k, t: (i, t, k)),
        out_shape=jax.ShapeDtypeStruct((b, n, BRANCH_WIDTH), BF16),
        compiler_params=_cparams(("parallel", "parallel", "arbitrary")),
        name="attn_swa",
    )(p_lat, p_ctx, p_ctx, p_lat, p_lat, sink_tab)


def _softmax_attend(q, k, v, sink=None):
    s = _dot_nt(q, k)
    m = jnp.max(s, axis=-1, keepdims=True)
    if sink is not None:
        m = jnp.maximum(m, sink)
    e = jnp.exp2(s - m)
    den = jnp.sum(e, axis=-1, keepdims=True)
    if sink is not None:
        den = den + jnp.exp2(sink - m)
    return _dot(e.astype(BF16), v) / den


def _ctx_attn_kernel(p_ref, lam_ref, subg_ref, sink_ref, o_ref, *, lambda_init):
    hd = HEAD_DIM

    def col(c0, h):
        return p_ref[:, c0 + h * hd:c0 + (h + 1) * hd]

    lam = _diff_lambda(lam_ref, lambda_init)
    for h in range(4):
        o = _softmax_attend(col(COL_GQA_Q, h), col(COL_GQA_K, h // 2), col(COL_GQA_V, h // 2))
        o_ref[:, h * hd:(h + 1) * hd] = o.astype(o_ref.dtype)
    for h in range(4):
        q = col(COL_DIFF_Q, h)
        tq = q.shape[0]
        o2 = _softmax_attend(_split_maps(q), col(COL_DIFF_K, h), col(COL_DIFF_V, h))
        d = o2[:tq] - lam * o2[tq:]
        d = _rms(d) * subg_ref[...] * (1.0 - lambda_init)
        o_ref[:, BRANCH_WIDTH + h * hd:BRANCH_WIDTH + (h + 1) * hd] = d.astype(o_ref.dtype)
    for h in range(4):
        o = _softmax_attend(col(COL_NA_Q, h), col(COL_NA_K, h), col(COL_NA_V, h))
        o_ref[:, 2 * BRANCH_WIDTH + h * hd:2 * BRANCH_WIDTH + (h + 1) * hd] = o.astype(o_ref.dtype)
    for h in range(4):
        o = _softmax_attend(col(COL_SWA_Q, h), col(COL_SWA_K, h // 2), col(COL_SWA_V, h // 2),
                            sink=sink_ref[h:h + 1, 0:1])
        o_ref[:, 3 * BRANCH_WIDTH + h * hd:3 * BRANCH_WIDTH + (h + 1) * hd] = o.astype(o_ref.dtype)


def _ctx_attn_call(p_ctx, lam_p, subg, sink_tab, lambda_init):
    b, l, cols = p_ctx.shape
    return pl.pallas_call(
        functools.partial(_ctx_attn_kernel, lambda_init=lambda_init),
        grid=(b,),
        in_specs=[pl.BlockSpec((None, l, cols), lambda i: (i, 0, 0)),
                  pl.BlockSpec((4, DIFF_QK_DIM), lambda i: (0, 0)),
                  pl.BlockSpec((1, HEAD_DIM), lambda i: (0, 0)),
                  pl.BlockSpec((4, LANES), lambda i: (0, 0))],
        out_specs=pl.BlockSpec((None, l, N_BRANCHES * BRANCH_WIDTH), lambda i: (i, 0, 0)),
        out_shape=jax.ShapeDtypeStruct((b, l, N_BRANCHES * BRANCH_WIDTH), BF16),
        compiler_params=_cparams(("parallel",)),
        name="attn_ctx",
    )(p_ctx, lam_p, subg.reshape(1, HEAD_DIM), sink_tab.reshape(4, LANES))


MERGE_CHUNKS = 1


def _merge_kernel(o0, o1, o2, o3, g_ref, wb_ref, wo_ref, x_ref, mod_ref, ng_ref, *rest, with_router):
    if with_router:
        router_ref, xo_ref, h_ref, lg_ref = rest
    else:
        xo_ref, h_ref = rest
    d = x_ref.shape[1]
    dc = d // MERGE_CHUNKS
    o_vals = [o[...] for o in (o0, o1, o2, o3)]
    z = None
    for c in range(MERGE_CHUNKS):
        s = None
        for i, o in enumerate(o_vals):
            y = _dot(o, wb_ref[i, :, c * dc:(c + 1) * dc])
            gate = g_ref[:, i * d + c * dc:i * d + (c + 1) * dc].astype(F32)
            term = jax.nn.sigmoid(gate) * y
            s = term if s is None else s + term
        zc = _dot(s.astype(BF16), wo_ref[c * dc:(c + 1) * dc, :])
        z = zc if z is None else z + zc
    xn = x_ref[...] + mod_ref[2:3, :] * z
    xo_ref[...] = xn
    hn = _norm_mod(xn, ng_ref[...], mod_ref, 3, 4)
    h_ref[...] = hn.astype(h_ref.dtype)
    if with_router:
        lg_ref[...] = _dot_split(hn, router_ref[...])


def _dot_split(a, b):
    a_hi = a.astype(BF16)
    a_lo = (a - a_hi.astype(F32)).astype(BF16)
    b_hi = b.astype(BF16)
    b_lo = (b - b_hi.astype(F32)).astype(BF16)
    n = b.shape[1]
    both = _dot(a_hi, jnp.concatenate([b_hi, b_lo], axis=1))
    return both[:, :n] + both[:, n:] + _dot(a_lo, b_hi)


def _merge_call(o_parts, gates, w_branch, w_out, x, mod, next_g, router=None):
    b, r, d = x.shape
    tm = min(r, 256)
    with_router = router is not None
    whole = dict(pipeline_mode=pl.Buffered(1))

    def o_spec(cb):
        return pl.BlockSpec((None, tm, BRANCH_WIDTH), lambda i, t: (i, t, cb))

    row_spec = pl.BlockSpec((None, tm, d), lambda i, t: (i, t, 0))
    in_specs = [o_spec(cb) for _, cb in o_parts] + [
        pl.BlockSpec((None, tm, N_BRANCHES * d), lambda i, t: (i, t, 0)),
        pl.BlockSpec((N_BRANCHES, BRANCH_WIDTH, d), lambda i, t: (0, 0, 0), **whole),
        pl.BlockSpec((d, d), lambda i, t: (0, 0), **whole),
        row_spec,
        pl.BlockSpec((None, 6, d), lambda i, t: (i, 0, 0)),
        pl.BlockSpec((1, d), lambda i, t: (0, 0))]
    args = [a for a, _ in o_parts] + [gates, w_branch, w_out, x, mod, next_g.reshape(1, d)]
    out_specs = [row_spec, row_spec]
    out_shape = [jax.ShapeDtypeStruct((b, r, d), F32), jax.ShapeDtypeStruct((b, r, d), BF16)]
    if with_router:
        in_specs.append(pl.BlockSpec((d, LANES), lambda i, t: (0, 0), **whole))
        args.append(router)
        out_specs.append(pl.BlockSpec((None, tm, LANES), lambda i, t: (i, t, 0)))
        out_shape.append(jax.ShapeDtypeStruct((b, r, LANES), F32))
    return pl.pallas_call(
        functools.partial(_merge_kernel, with_router=with_router),
        grid=(b, r // tm),
        in_specs=in_specs,
        out_specs=out_specs,
        out_shape=out_shape,
        compiler_params=_cparams(("parallel", "parallel")),
        name="merge",
    )(*args)


def _swiglu_partial(h, wg, wu, wd):
    gate = _dot(h, wg)
    up = _dot(h, wu)
    act = gate * jax.nn.sigmoid(gate) * up
    return _dot(act.astype(BF16), wd)


FFN_ROW_BLOCK = 512


def _ffn_kernel(h_ref, wg_ref, wu_ref, wd_ref, x_ref, mod_ref, ng_ref, nmod_ref, xo_ref, hn_ref,
                acc_ref):
    f = pl.program_id(2)

    @pl.when(f == 0)
    def _():
        acc_ref[...] = jnp.zeros_like(acc_ref)

    for r0 in range(0, h_ref.shape[0], FFN_ROW_BLOCK):
        rows = slice(r0, r0 + FFN_ROW_BLOCK)
        acc_ref[rows, :] += _swiglu_partial(h_ref[rows, :], wg_ref[...], wu_ref[...], wd_ref[...])

    @pl.when(f == pl.num_programs(2) - 1)
    def _():
        xn = x_ref[...] + mod_ref[5:6, :] * acc_ref[...]
        xo_ref[...] = xn
        hn_ref[...] = _norm_mod(xn, ng_ref[...], nmod_ref, 0, 1).astype(hn_ref.dtype)


def _ffn_call(h, wg, wu, wd, x, mod, next_g, next_mod):
    b, r, d = x.shape
    dff = wg.shape[1]
    tf = 512
    tm = min(r, FFN_ROW_BLOCK)
    row_spec = pl.BlockSpec((None, tm, d), lambda i, t, f: (i, t, 0))
    mod_spec = pl.BlockSpec((None, 6, d), lambda i, t, f: (i, 0, 0))
    return pl.pallas_call(
        _ffn_kernel,
        grid=(b, r // tm, dff // tf),
        in_specs=[row_spec,
                  pl.BlockSpec((d, tf), lambda i, t, f: (0, f)),
                  pl.BlockSpec((d, tf), lambda i, t, f: (0, f)),
                  pl.BlockSpec((tf, d), lambda i, t, f: (f, 0)),
                  row_spec, mod_spec,
                  pl.BlockSpec((1, d), lambda i, t, f: (0, 0)),
                  mod_spec],
        out_specs=[row_spec, row_spec],
        out_shape=[jax.ShapeDtypeStruct((b, r, d), F32), jax.ShapeDtypeStruct((b, r, d), BF16)],
        scratch_shapes=[pltpu.VMEM((tm, d), F32)],
        compiler_params=_cparams(("parallel", "parallel", "arbitrary")),
        name="ffn_dense",
    )(h, wg, wu, wd, x, mod, next_g.reshape(1, d), next_mod)


def _route_kernel(lg_ref, idx_ref, w_ref):
    lg = lg_ref[...]
    lane = lax.broadcasted_iota(jnp.int32, lg.shape, 1)
    valid = lane < N_EXPERTS
    mx = jnp.max(jnp.where(valid, lg, -jnp.inf), axis=-1, keepdims=True)
    e = jnp.where(valid, jnp.exp(lg - mx), 0.0)
    p = e / jnp.sum(e, axis=-1, keepdims=True)
    p1 = jnp.max(p, axis=-1, keepdims=True)
    i1 = jnp.min(jnp.where(p == p1, lane, LANES), axis=-1, keepdims=True)
    rest = jnp.where(jnp.logical_or(lane == i1, jnp.logical_not(valid)), -1.0, p)
    p2 = jnp.max(rest, axis=-1, keepdims=True)
    i2 = jnp.min(jnp.where(rest == p2, lane, LANES), axis=-1, keepdims=True)
    tot = p1 + p2
    idx_ref[...] = jnp.where(lane == 0, i1, jnp.where(lane == 1, i2, 0))
    w_ref[...] = jnp.where(lane == 0, p1 / tot, jnp.where(lane == 1, p2 / tot, 0.0))


def _route_call(logits):
    r = logits.shape[0]
    tm = min(r, 1024)
    spec = pl.BlockSpec((tm, LANES), lambda i: (i, 0))
    return pl.pallas_call(
        _route_kernel,
        grid=(r // tm,),
        in_specs=[spec],
        out_specs=[spec, spec],
        out_shape=[jax.ShapeDtypeStruct((r, LANES), jnp.int32), jax.ShapeDtypeStruct((r, LANES), F32)],
        compiler_params=_cparams(("parallel",)),
        name="route_top2",
    )(logits)


MOE_HALF = 512
MOE_TILE = 2 * MOE_HALF
MOE_FF_TILE = 512
MOE_VMEM_LIMIT = 60 * 1024 * 1024


def _moe_kernel(te_ref, nr_ref, x_ref, wg_ref, wu_ref, wd_ref, o_ref, acc_ref):
    i = pl.program_id(0)
    f = pl.program_id(1)

    @pl.when(f == 0)
    def _():
        acc_ref[...] = jnp.zeros_like(acc_ref)

    for half in range(MOE_TILE // MOE_HALF):
        rows = slice(half * MOE_HALF, (half + 1) * MOE_HALF)

        @pl.when(nr_ref[i] > half * MOE_HALF)
        def _():
            acc_ref[rows, :] += _swiglu_partial(x_ref[rows, :], wg_ref[...].astype(BF16),
                                                wu_ref[...].astype(BF16), wd_ref[...].astype(BF16))

    @pl.when(f == pl.num_programs(1) - 1)
    def _():
        o_ref[...] = acc_ref[...].astype(o_ref.dtype)


def _moe_call(xs, wg, wu, wd, tile_expert, tile_rows):
    r, d = xs.shape
    dff = wg.shape[2]
    tf = MOE_FF_TILE
    nf = dff // tf

    def fidx(i, f, nr):
        return jnp.where(nr[i] > 0, f, nf - 1)

    grid_spec = pltpu.PrefetchScalarGridSpec(
        num_scalar_prefetch=2,
        grid=(r // MOE_TILE, nf),
        in_specs=[pl.BlockSpec((MOE_TILE, d), lambda i, f, te, nr: (i, 0)),
                  pl.BlockSpec((None, d, tf), lambda i, f, te, nr: (te[i], 0, fidx(i, f, nr))),
                  pl.BlockSpec((None, d, tf), lambda i, f, te, nr: (te[i], 0, fidx(i, f, nr))),
                  pl.BlockSpec((None, tf, d), lambda i, f, te, nr: (te[i], fidx(i, f, nr), 0))],
        out_specs=pl.BlockSpec((MOE_TILE, d), lambda i, f, te, nr: (i, 0)),
        scratch_shapes=[pltpu.VMEM((MOE_TILE, d), F32)])
    return pl.pallas_call(
        _moe_kernel,
        grid_spec=grid_spec,
        out_shape=jax.ShapeDtypeStruct((r, d), BF16),
        compiler_params=pltpu.CompilerParams(dimension_semantics=("arbitrary", "arbitrary"),
                                             vmem_limit_bytes=MOE_VMEM_LIMIT),
        name="moe_experts",
    )(tile_expert, tile_rows, xs, wg, wu, wd)


def _moe_plan(idx):
    t = idx.shape[0]
    a = t * TOP_K
    flat_e = idx.reshape(a)
    onehot = (flat_e[:, None] == jnp.arange(N_EXPERTS, dtype=jnp.int32)[None, :]).astype(jnp.int32)
    csum = jnp.cumsum(onehot, axis=0)
    rank = jnp.sum((csum - onehot) * onehot, axis=1)
    counts = csum[-1]
    padded = ((counts + MOE_TILE - 1) // MOE_TILE) * MOE_TILE
    ends = jnp.cumsum(padded)
    offs = ends - padded
    dest = offs[flat_e] + rank
    rows = a + N_EXPERTS * MOE_TILE
    src_tok = (jnp.arange(rows, dtype=jnp.int32) % t).at[dest].set(jnp.arange(a, dtype=jnp.int32) // TOP_K)
    tile_start = jnp.arange(rows // MOE_TILE, dtype=jnp.int32) * MOE_TILE
    past = jnp.sum((tile_start[:, None] >= ends[None, :]).astype(jnp.int32), axis=1)
    tile_expert = jnp.minimum(past, N_EXPERTS - 1)
    tile_rows = jnp.where(past < N_EXPERTS,
                          jnp.clip((offs + counts)[tile_expert] - tile_start, 0, MOE_TILE), 0)
    return dest.reshape(t, TOP_K), src_tok, tile_expert, tile_rows.astype(jnp.int32)


def _final_kernel(x_ref, y0_ref, y1_ref, w_ref, mod_ref, g_ref, o_ref):
    y = w_ref[:, 0:1] * y0_ref[...].astype(F32) + w_ref[:, 1:2] * y1_ref[...].astype(F32)
    xn = x_ref[...] + mod_ref[5:6, :] * y
    o_ref[...] = _rms(xn) * g_ref[...]


def _final_call(x, y0, y1, w, mod, g):
    b, r, d = x.shape
    tm = min(r, 512)
    row_spec = pl.BlockSpec((None, tm, d), lambda i, t: (i, t, 0))
    return pl.pallas_call(
        _final_kernel,
        grid=(b, r // tm),
        in_specs=[row_spec, row_spec, row_spec,
                  pl.BlockSpec((None, tm, LANES), lambda i, t: (i, t, 0)),
                  pl.BlockSpec((None, 6, d), lambda i, t: (i, 0, 0)),
                  pl.BlockSpec((1, d), lambda i, t: (0, 0))],
        out_specs=row_spec,
        out_shape=jax.ShapeDtypeStruct((b, r, d), F32),
        compiler_params=_cparams(("parallel", "parallel")),
        name="final_norm",
    )(x, y0, y1, w, mod, g.reshape(1, d))


def _rope_tables(n):
    pos = jnp.arange(n)
    rows = (pos // GRID_W).astype(F32)
    cols = (pos % GRID_W).astype(F32)
    lane = np.arange(LANES)
    out = []
    for hw in (32, 16):
        period = 4 * hw
        u = lane % period
        use_cols = (u // (2 * hw)) == 1
        w = u % (2 * hw)
        freqs = jnp.asarray(ROPE_THETA ** (-(w % hw).astype(np.float32) / hw), F32)
        p = jnp.where(jnp.asarray(use_cols)[None, :], cols[:, None], rows[:, None])
        ang = p * freqs[None, :]
        sign = jnp.asarray(np.where(w < hw, -1.0, 1.0), F32)
        out += [jnp.cos(ang), jnp.sin(ang) * sign[None, :]]
    return out


def kernel(x, c, ctx, c_ctx, attn_norm_g, ffn_norm_g, ada_w, ada_b, w_in, qk_norm_g, diff_lambda,
           diff_subln_g, na_rpb, swa_sink, w_branch, w_out, ffn_w_gate, ffn_w_up, ffn_w_down,
           moe_router, moe_w_gate, moe_w_up, moe_w_down, final_norm_g):
    b, n, d = x.shape
    l = ctx.shape[1]
    depth = w_in.shape[0]
    assert depth == 2, "laid out for one dense layer followed by one routed last layer"
    rows_n = n // GRID_W

    lat_tables = _rope_tables(n)
    ctx_flat = ctx.reshape(1, b * l, d)
    ones = jnp.ones((b * l, LANES), F32)
    zeros = jnp.zeros((b * l, LANES), F32)
    ctx_tables = [ones, zeros, ones, zeros]

    cvec = jnp.zeros((8, d), F32).at[:b].set(c).at[b].set(c_ctx)
    mods = []
    for i in range(depth):
        m = _adaln(cvec, ada_w, ada_b, i).reshape(8, 6, d)
        mods.append((m[:b], m[b:b + 1]))

    def mixers(i, h_lat, h_ctx):
        lambda_init = 0.8 - 0.6 * math.exp(-0.3 * i)
        w_qkv = _qkv_weights_call(w_in, i)
        sink_tab = jnp.broadcast_to((swa_sink[i].astype(F32) * LOG2E)[:, None], (4, LANES))
        p_lat = _qkv_call(h_lat, w_qkv, lat_tables, qk_norm_g[i])
        p_ctx = _qkv_call(h_ctx, w_qkv, ctx_tables, qk_norm_g[i]).reshape(b, l, QKV_COLS)
        o_lat = [_gqa_call(p_lat, p_ctx),
                 _diff_call(p_lat, p_ctx, diff_lambda[i], diff_subln_g[i], lambda_init),
                 _na_call(p_lat, p_ctx, _na_bias_tables(na_rpb[i])),
                 _swa_call(p_lat, p_ctx, sink_tab.reshape(2, 2, LANES))]
        return p_ctx, o_lat, sink_tab, lambda_init

    mod_lat, mod_ctx = mods[0]
    h_lat = _norm_mod_call(x, attn_norm_g[0], mod_lat, 0, 1)
    h_ctx = _norm_mod_call(ctx_flat, attn_norm_g[0], mod_ctx, 0, 1)
    p_ctx, o_lat, sink_tab, lambda_init = mixers(0, h_lat, h_ctx)
    wb = w_branch[0].astype(BF16)
    wo = w_out[0].astype(BF16)
    x_lat, h_lat = _merge_call([(o, 0) for o in o_lat], _gate_proj_call(h_lat, w_in, 0), wb, wo, x,
                               mod_lat, ffn_norm_g[0])
    o_ctx = _ctx_attn_call(p_ctx, diff_lambda[0], diff_subln_g[0], sink_tab, lambda_init)
    o_ctx = o_ctx.reshape(1, b * l, N_BRANCHES * BRANCH_WIDTH)
    x_ctx, h_ctx = _merge_call([(o_ctx, k) for k in range(N_BRANCHES)], _gate_proj_call(h_ctx, w_in, 0),
                               wb, wo, ctx_flat, mod_ctx, ffn_norm_g[0])
    wg = ffn_w_gate[0].astype(BF16)
    wu = ffn_w_up[0].astype(BF16)
    wd = ffn_w_down[0].astype(BF16)
    _, h_ctx = _ffn_call(h_ctx, wg, wu, wd, x_ctx, mod_ctx, attn_norm_g[1], mods[1][1])
    x_lat, h_lat = _ffn_call(h_lat, wg, wu, wd, x_lat, mod_lat, attn_norm_g[1], mods[1][0])

    mod_lat, _ = mods[1]
    _, o_lat, _, _ = mixers(1, h_lat, h_ctx)
    router = jnp.zeros((d, LANES), F32).at[:, :N_EXPERTS].set(moe_router[0])
    x_lat, h_lat, logits = _merge_call(
        [(o, 0) for o in o_lat], _gate_proj_call(h_lat, w_in, 1),
        w_branch[1].astype(BF16), w_out[1].astype(BF16), x_lat, mod_lat, ffn_norm_g[1], router)

    idx_pad, wts_pad = _route_call(logits.reshape(b * n, LANES))
    dest, src_tok, tile_expert, tile_rows = _moe_plan(idx_pad[:, :TOP_K])
    xs = h_lat.reshape(b * n, d).at[src_tok].get(mode="promise_in_bounds")
    ys = _moe_call(xs, moe_w_gate[0], moe_w_up[0], moe_w_down[0], tile_expert, tile_rows)
    y0 = ys.at[dest[:, 0]].get(mode="promise_in_bounds").reshape(b, n, d)
    y1 = ys.at[dest[:, 1]].get(mode="promise_in_bounds").reshape(b, n, d)
    return _final_call(x_lat, y0, y1, wts_pad.reshape(b, n, LANES), mod_lat, final_norm_g)
```

```python
import functools
import math

import numpy as np
import jax
import jax.numpy as jnp
from jax import lax
from jax.experimental import pallas as pl
from jax.experimental.pallas import tpu as pltpu

F32 = jnp.float32
BF16 = jnp.bfloat16

GRID_W = 64
HEAD_DIM = 128
N_BRANCHES = 4
BRANCH_WIDTH = 4 * HEAD_DIM
DIFF_QK_DIM = 64
NA_ROWS = 8
NA_COLS = 16
SWA_WINDOW = 128
N_EXPERTS = 8
TOP_K = 2
NORM_EPS = 1e-6
ROPE_THETA = 10000.0
NEG_INF = -1e30
LOG2E = math.log2(math.e)

QKV_COLS = 5120
COL_GQA_Q, COL_GQA_K, COL_GQA_V = 0, 512, 768
COL_DIFF_Q, COL_DIFF_K, COL_DIFF_V = 1024, 1536, 2048
COL_NA_Q, COL_NA_K, COL_NA_V = 2560, 3072, 3584
COL_SWA_Q, COL_SWA_K, COL_SWA_V = 4096, 4608, 4864

LANES = 128
VMEM_LIMIT = 56 * 1024 * 1024

QS128 = HEAD_DIM ** -0.5 * LOG2E
QS64 = DIFF_QK_DIM ** -0.5 * LOG2E


def _cparams(sem):
    return pltpu.CompilerParams(dimension_semantics=sem, vmem_limit_bytes=VMEM_LIMIT)


def _dot(a, b):
    return jnp.dot(a, b, preferred_element_type=F32)


def _dot_nt(a, b):
    return lax.dot_general(a, b, (((1,), (1,)), ((), ())), preferred_element_type=F32)


def _rms(x):
    return x * lax.rsqrt(jnp.mean(x * x, axis=-1, keepdims=True) + NORM_EPS)


def _norm_mod(x, g, mod_ref, shift_idx, scale_idx):
    y = _rms(x) * g
    return y * (1.0 + mod_ref[scale_idx:scale_idx + 1, :]) + mod_ref[shift_idx:shift_idx + 1, :]


def _adaln_kernel(c_ref, w_ref, b_ref, o_ref):
    c = c_ref[...]
    s = c * jax.nn.sigmoid(c)
    o_ref[...] = jnp.dot(s, w_ref[...], preferred_element_type=F32,
                         precision=lax.Precision.HIGHEST) + b_ref[...]


def _adaln(cvec, w, b, layer):
    rows, d = cvec.shape
    cols = w.shape[2]
    tn = 1024 if cols % 1024 == 0 else cols
    return pl.pallas_call(
        _adaln_kernel,
        grid=(cols // tn,),
        in_specs=[pl.BlockSpec((rows, d), lambda j: (0, 0)),
                  pl.BlockSpec((None, d, tn), lambda j: (layer, 0, j)),
                  pl.BlockSpec((None, 1, tn), lambda j: (layer, 0, j))],
        out_specs=pl.BlockSpec((rows, tn), lambda j: (0, j)),
        out_shape=jax.ShapeDtypeStruct((rows, cols), F32),
        compiler_params=_cparams(("arbitrary",)),
        name="adaln",
    )(cvec, w, b.reshape(b.shape[0], 1, cols))


def _norm_mod_kernel(x_ref, g_ref, mod_ref, h_ref, *, shift_idx, scale_idx):
    h_ref[...] = _norm_mod(x_ref[...], g_ref[...], mod_ref, shift_idx, scale_idx).astype(BF16)


def _norm_mod_call(x, g, mod, shift_idx, scale_idx):
    b, r, d = x.shape
    tm = min(r, 512)
    return pl.pallas_call(
        functools.partial(_norm_mod_kernel, shift_idx=shift_idx, scale_idx=scale_idx),
        grid=(b, r // tm),
        in_specs=[pl.BlockSpec((None, tm, d), lambda i, t: (i, t, 0)),
                  pl.BlockSpec((1, d), lambda i, t: (0, 0)),
                  pl.BlockSpec((None, 6, d), lambda i, t: (i, 0, 0))],
        out_specs=pl.BlockSpec((None, tm, d), lambda i, t: (i, t, 0)),
        out_shape=jax.ShapeDtypeStruct((b, r, d), BF16),
        compiler_params=_cparams(("parallel", "parallel")),
        name="norm_mod",
    )(x, g.reshape(1, d), mod)


QKV_TILE = 4 * LANES
_PLAIN = (None, None, 1.0)
_QKV_TILE_OPS = {
    0: [(0, 32, QS128)] * 4,
    1: [(1, 32, 1.0)] * 2 + [_PLAIN] * 2,
    2: [(None, 16, QS64)] * 4,
    3: [(None, 16, 1.0)] * 4,
    5: [(None, None, QS128)] * 4,
    8: [(None, 32, QS128)] * 4,
    9: [(None, 32, 1.0)] * 2 + [_PLAIN] * 2,
}


def _rope(y, cos, sin, hw):
    lane = lax.broadcasted_iota(jnp.int32, y.shape, 1)
    first = (lane % (2 * hw)) < hw
    partner = jnp.where(first, pltpu.roll(y, LANES - hw, 1), pltpu.roll(y, hw, 1))
    return y * cos + partner * sin


def _qkv_kernel(h_ref, w_ref, cos32_ref, sin32_ref, cos16_ref, sin16_ref, g_ref, o_ref):
    h = h_ref[...]
    for j in range(o_ref.shape[1] // QKV_TILE):
        acc = _dot(h, w_ref[:, j * QKV_TILE:(j + 1) * QKV_TILE])
        for c, (norm_row, hw, scale) in enumerate(_QKV_TILE_OPS.get(j, [_PLAIN] * 4)):
            y = acc[:, c * LANES:(c + 1) * LANES]
            if norm_row is not None:
                y = _rms(y) * g_ref[norm_row:norm_row + 1, :]
            if hw == 32:
                y = _rope(y, cos32_ref[...], sin32_ref[...], 32)
            elif hw == 16:
                y = _rope(y, cos16_ref[...], sin16_ref[...], 16)
            if scale != 1.0:
                y = y * scale
            col = j * QKV_TILE + c * LANES
            o_ref[:, col:col + LANES] = y.astype(o_ref.dtype)


def _qkv_call(h, w_qkv, tables, qk_g):
    b, r, d = h.shape
    cols = w_qkv.shape[1]
    tm = min(r, 512)
    tab_spec = pl.BlockSpec((tm, LANES), lambda i, t: (t, 0))
    return pl.pallas_call(
        _qkv_kernel,
        grid=(b, r // tm),
        in_specs=[pl.BlockSpec((None, tm, d), lambda i, t: (i, t, 0)),
                  pl.BlockSpec((d, cols), lambda i, t: (0, 0), pipeline_mode=pl.Buffered(1)),
                  tab_spec, tab_spec, tab_spec, tab_spec,
                  pl.BlockSpec((2, LANES), lambda i, t: (0, 0))],
        out_specs=pl.BlockSpec((None, tm, cols), lambda i, t: (i, t, 0)),
        out_shape=jax.ShapeDtypeStruct((b, r, cols), BF16),
        compiler_params=_cparams(("parallel", "parallel")),
        name="qkv_proj",
    )(h, w_qkv, *tables, qk_g)


def _matmul_kernel(h_ref, w_ref, o_ref):
    o_ref[...] = _dot(h_ref[...], w_ref[...].astype(BF16)).astype(o_ref.dtype)


def _gate_proj_call(h, w_in, layer):
    b, r, d = h.shape
    tn = 1024
    cols = w_in.shape[2] - QKV_COLS
    col0 = QKV_COLS // tn
    tm = 2048 if r % 2048 == 0 else r
    return pl.pallas_call(
        _matmul_kernel,
        grid=(b, r // tm, cols // tn),
        in_specs=[pl.BlockSpec((None, tm, d), lambda i, t, j: (i, t, 0)),
                  pl.BlockSpec((None, d, tn), lambda i, t, j: (layer, 0, col0 + j))],
        out_specs=pl.BlockSpec((None, tm, tn), lambda i, t, j: (i, t, j)),
        out_shape=jax.ShapeDtypeStruct((b, r, cols), BF16),
        compiler_params=_cparams(("parallel", "parallel", "arbitrary")),
        name="gate_proj",
    )(h, w_in)


def _cast_kernel(w_ref, o_ref):
    o_ref[...] = w_ref[...].astype(o_ref.dtype)


def _qkv_weights_call(w_in, layer):
    d = w_in.shape[1]
    return pl.pallas_call(
        _cast_kernel,
        grid=(QKV_COLS // QKV_TILE,),
        in_specs=[pl.BlockSpec((None, d, QKV_TILE), lambda j: (layer, 0, j))],
        out_specs=pl.BlockSpec((d, QKV_TILE), lambda j: (0, j)),
        out_shape=jax.ShapeDtypeStruct((d, QKV_COLS), BF16),
        compiler_params=_cparams(("parallel",)),
        name="qkv_weights",
    )(w_in)


def _diff_lambda(lam_ref, lambda_init):
    a = jnp.sum(lam_ref[0:1, :] * lam_ref[1:2, :], axis=-1, keepdims=True)
    b = jnp.sum(lam_ref[2:3, :] * lam_ref[3:4, :], axis=-1, keepdims=True)
    return jnp.exp(a) - jnp.exp(b) + lambda_init


def _split_maps(q):
    lane = lax.broadcasted_iota(jnp.int32, q.shape, 1)
    zero = jnp.zeros_like(q)
    return jnp.concatenate([jnp.where(lane < DIFF_QK_DIM, q, zero),
                            jnp.where(lane >= DIFF_QK_DIM, q, zero)], axis=0)


FLASH_ROW_BLOCK = 128
FLASH_KV_CHUNK = 512


def _flash_pair(q2, kc_ref, vc_ref, kl_ref, vl_ref, tk):
    m_rows = q2.shape[0]
    n_lat = kl_ref.shape[0]
    nb = m_rows // FLASH_ROW_BLOCK
    qs = [q2[i * FLASH_ROW_BLOCK:(i + 1) * FLASH_ROW_BLOCK] for i in range(nb)]

    def step(k, v, carry):
        v1 = jnp.concatenate([v, jnp.ones_like(v)], axis=1)
        out = []
        for q, (m, acc) in zip(qs, carry):
            s = _dot_nt(q, k)
            m_new = jnp.maximum(m, jnp.max(s, axis=-1, keepdims=True))
            p = jnp.exp2(s - m_new)
            acc = jnp.exp2(m - m_new) * acc + _dot(p.astype(BF16), v1)
            out.append((m_new, acc))
        return tuple(out)

    carry = tuple((jnp.full((FLASH_ROW_BLOCK, 1), NEG_INF, F32),
                   jnp.zeros((FLASH_ROW_BLOCK, 2 * HEAD_DIM), F32)) for _ in range(nb))
    carry = step(kc_ref[...], vc_ref[...], carry)

    for c in range(n_lat // tk):
        carry = step(kl_ref[c * tk:(c + 1) * tk, :], vl_ref[c * tk:(c + 1) * tk, :], carry)
    return jnp.concatenate([acc[:, :HEAD_DIM] / acc[:, HEAD_DIM:] for _, acc in carry], axis=0)


def _gqa_kernel(q_ref, kc_ref, vc_ref, kl_ref, vl_ref, o_ref, *, tk):
    tq = q_ref.shape[0]
    q = q_ref[...]
    q2 = jnp.concatenate([q[:, :HEAD_DIM], q[:, HEAD_DIM:]], axis=0)
    o = _flash_pair(q2, kc_ref, vc_ref, kl_ref, vl_ref, tk)
    o_ref[:, :HEAD_DIM] = o[:tq].astype(o_ref.dtype)
    o_ref[:, HEAD_DIM:] = o[tq:].astype(o_ref.dtype)


def _diff_kernel(q_ref, kc_ref, vc_ref, kl_ref, vl_ref, lam_ref, subg_ref, o_ref, *, tk, lambda_init):
    tq = q_ref.shape[0]
    o = _flash_pair(_split_maps(q_ref[...]), kc_ref, vc_ref, kl_ref, vl_ref, tk)
    lam = _diff_lambda(lam_ref, lambda_init)
    d = o[:tq] - lam * o[tq:]
    o_ref[...] = (_rms(d) * subg_ref[...] * (1.0 - lambda_init)).astype(o_ref.dtype)


def _gqa_call(p_lat, p_ctx):
    b, n, _ = p_lat.shape
    l = p_ctx.shape[1]
    tq = 512
    tk = FLASH_KV_CHUNK
    hd = HEAD_DIM
    kcol, vcol = COL_GQA_K // hd, COL_GQA_V // hd
    return pl.pallas_call(
        functools.partial(_gqa_kernel, tk=tk),
        grid=(b, 2, n // tq),
        in_specs=[pl.BlockSpec((None, tq, 2 * hd), lambda i, k, t: (i, t, k)),
                  pl.BlockSpec((None, l, hd), lambda i, k, t: (i, 0, kcol + k)),
                  pl.BlockSpec((None, l, hd), lambda i, k, t: (i, 0, vcol + k)),
                  pl.BlockSpec((None, n, hd), lambda i, k, t: (i, 0, kcol + k)),
                  pl.BlockSpec((None, n, hd), lambda i, k, t: (i, 0, vcol + k))],
        out_specs=pl.BlockSpec((None, tq, 2 * hd), lambda i, k, t: (i, t, k)),
        out_shape=jax.ShapeDtypeStruct((b, n, BRANCH_WIDTH), BF16),
        compiler_params=_cparams(("parallel", "parallel", "arbitrary")),
        name="attn_gqa",
    )(p_lat, p_ctx, p_ctx, p_lat, p_lat)


def _diff_call(p_lat, p_ctx, lam_p, subg, lambda_init):
    b, n, _ = p_lat.shape
    l = p_ctx.shape[1]
    tq = 512
    tk = FLASH_KV_CHUNK
    hd = HEAD_DIM
    qcol, kcol, vcol = COL_DIFF_Q // hd, COL_DIFF_K // hd, COL_DIFF_V // hd
    return pl.pallas_call(
        functools.partial(_diff_kernel, tk=tk, lambda_init=lambda_init),
        grid=(b, 4, n // tq),
        in_specs=[pl.BlockSpec((None, tq, hd), lambda i, h, t: (i, t, qcol + h)),
                  pl.BlockSpec((None, l, hd), lambda i, h, t: (i, 0, kcol + h)),
                  pl.BlockSpec((None, l, hd), lambda i, h, t: (i, 0, vcol + h)),
                  pl.BlockSpec((None, n, hd), lambda i, h, t: (i, 0, kcol + h)),
                  pl.BlockSpec((None, n, hd), lambda i, h, t: (i, 0, vcol + h)),
                  pl.BlockSpec((4, DIFF_QK_DIM), lambda i, h, t: (0, 0)),
                  pl.BlockSpec((1, hd), lambda i, h, t: (0, 0))],
        out_specs=pl.BlockSpec((None, tq, hd), lambda i, h, t: (i, t, h)),
        out_shape=jax.ShapeDtypeStruct((b, n, BRANCH_WIDTH), BF16),
        compiler_params=_cparams(("parallel", "parallel", "arbitrary")),
        name="attn_diff",
    )(p_lat, p_ctx, p_ctx, p_lat, p_lat, lam_p, subg.reshape(1, hd))


NA_GROUP_ROWS = 8
NA_BLOCK_ROWS = 4
NA_BLOCK_WIN_ROWS = NA_BLOCK_ROWS + NA_ROWS
NA_WIN_ROWS = 2 * NA_GROUP_ROWS


def _with_ones(v):
    return jnp.concatenate([v, jnp.ones_like(v)], axis=1)


def _band_ctx_attend(s_b, s_c, vb1, vc1, sink=None):
    m = jnp.maximum(jnp.max(s_b, axis=-1, keepdims=True), jnp.max(s_c, axis=-1, keepdims=True))
    if sink is not None:
        m = jnp.maximum(m, sink)
    acc = _dot(jnp.exp2(s_c - m).astype(BF16), vc1) + _dot(jnp.exp2(s_b - m).astype(BF16), vb1)
    den = acc[:, HEAD_DIM:]
    if sink is not None:
        den = den + jnp.exp2(sink - m)
    return acc[:, :HEAD_DIM] / den


NA_DI = 2 * NA_ROWS - 1
NA_TAB_PAD = NA_WIN_ROWS
NA_TAB_LANES = 3072


def _na_bias_tables(rpb):
    nh = rpb.shape[0]
    qc = np.arange(GRID_W)[:, None]
    kc = np.arange(GRID_W)[None, :]
    cstart = np.clip(qc - NA_COLS // 2, 0, GRID_W - NA_COLS)
    ok_col = (kc >= cstart) & (kc < cstart + NA_COLS)
    dj = np.clip(kc - qc + NA_COLS - 1, 0, 2 * NA_COLS - 2)
    sel_j = (dj[..., None] == np.arange(2 * NA_COLS - 1)) & ok_col[..., None]
    by_col = jnp.einsum("hij,qkj->hqik", rpb.astype(F32), jnp.asarray(sel_j, F32),
                        precision=lax.Precision.HIGHEST)
    by_col = jnp.where(ok_col[None, :, None, :], by_col * LOG2E, NEG_INF)
    strip = by_col.reshape(nh, GRID_W, NA_DI * GRID_W)
    total = NA_TAB_LANES + GRID_W
    strip = jnp.pad(strip, ((0, 0), (0, 0), (NA_TAB_PAD * GRID_W, total - (NA_TAB_PAD + NA_DI) * GRID_W)),
                    constant_values=NEG_INF)
    return jnp.stack([strip[:, :, :NA_TAB_LANES], strip[:, :, GRID_W:]], axis=1)


def _na_kernel(q_ref, kc_ref, vc_ref, kl_ref, vl_ref, tab_ref, o_ref, *, rows_n):
    g = pl.program_id(2)
    win = NA_BLOCK_WIN_ROWS * GRID_W
    blk_q = NA_BLOCK_ROWS * GRID_W
    kc = kc_ref[...]
    vc1 = _with_ones(vc_ref[...])
    lane = lax.broadcasted_iota(jnp.int32, (GRID_W, win), 1)
    for rb in range(NA_GROUP_ROWS // NA_BLOCK_ROWS):
        r0 = g * NA_GROUP_ROWS + rb * NA_BLOCK_ROWS
        ws = jnp.clip(r0 - NA_ROWS // 2, 0, rows_n - NA_BLOCK_WIN_ROWS)
        off = pl.multiple_of(ws * GRID_W, NA_BLOCK_ROWS * GRID_W)
        bias = []
        for a in range(NA_BLOCK_ROWS):
            r = r0 + a
            rs = jnp.clip(r - NA_ROWS // 2, 0, rows_n - NA_ROWS)
            blk = ws - r + NA_ROWS - 1 + NA_TAB_PAD
            start = pl.multiple_of((blk >> 1) * (2 * GRID_W), 2 * GRID_W)
            strip = tab_ref[blk & 1, :, pl.ds(start, win)]
            lo = (rs - ws) * GRID_W
            in_rows = jnp.logical_and(lane >= lo, lane < lo + NA_ROWS * GRID_W)
            bias.append(jnp.where(in_rows, strip, NEG_INF))
        q = q_ref[rb * blk_q:(rb + 1) * blk_q, :]
        s_b = _dot_nt(q, kl_ref[pl.ds(off, win), :]) + jnp.concatenate(bias, axis=0)
        o = _band_ctx_attend(s_b, _dot_nt(q, kc), _with_ones(vl_ref[pl.ds(off, win), :]), vc1)
        o_ref[rb * blk_q:(rb + 1) * blk_q, :] = o.astype(o_ref.dtype)


def _na_call(p_lat, p_ctx, bias):
    b, n, _ = p_lat.shape
    l = p_ctx.shape[1]
    hd = HEAD_DIM
    rows_n = n // GRID_W
    n_groups = rows_n // NA_GROUP_ROWS
    tq = NA_GROUP_ROWS * GRID_W
    qcol, kcol, vcol = COL_NA_Q // hd, COL_NA_K // hd, COL_NA_V // hd

    return pl.pallas_call(
        functools.partial(_na_kernel, rows_n=rows_n),
        grid=(b, 4, n_groups),
        in_specs=[pl.BlockSpec((None, tq, hd), lambda i, h, g: (i, g, qcol + h)),
                  pl.BlockSpec((None, l, hd), lambda i, h, g: (i, 0, kcol + h)),
                  pl.BlockSpec((None, l, hd), lambda i, h, g: (i, 0, vcol + h)),
                  pl.BlockSpec((None, n, hd), lambda i, h, g: (i, 0, kcol + h)),
                  pl.BlockSpec((None, n, hd), lambda i, h, g: (i, 0, vcol + h)),
                  pl.BlockSpec((None, 2, GRID_W, NA_TAB_LANES), lambda i, h, g: (h, 0, 0, 0))],
        out_specs=pl.BlockSpec((None, tq, hd), lambda i, h, g: (i, g, h)),
        out_shape=jax.ShapeDtypeStruct((b, n, BRANCH_WIDTH), BF16),
        compiler_params=_cparams(("parallel", "parallel", "arbitrary")),
        name="attn_na",
    )(p_lat, p_ctx, p_ctx, p_lat, p_lat, bias)


SWA_BLOCK = 256


def _swa_kernel(q_ref, kc_ref, vc_ref, kl_ref, vl_ref, sink_ref, o_ref, *, win):
    t = pl.program_id(2)
    tq = q_ref.shape[0]
    n = kl_ref.shape[0]
    kc = kc_ref[...]
    vc1 = _with_ones(vc_ref[...])
    row = lax.broadcasted_iota(jnp.int32, (SWA_BLOCK, win), 0)
    col = lax.broadcasted_iota(jnp.int32, (SWA_BLOCK, win), 1)
    for rb in range(tq // SWA_BLOCK):
        qs = t * tq + rb * SWA_BLOCK
        ws = pl.multiple_of(jnp.clip(qs - SWA_WINDOW, 0, n - win), SWA_WINDOW)
        kb = kl_ref[pl.ds(ws, win), :]
        vb1 = _with_ones(vl_ref[pl.ds(ws, win), :])
        valid = jnp.abs(row + (qs - ws) - col) <= SWA_WINDOW
        rows = slice(rb * SWA_BLOCK, (rb + 1) * SWA_BLOCK)
        for gi in range(2):
            q = q_ref[rows, gi * HEAD_DIM:(gi + 1) * HEAD_DIM]
            s_b = jnp.where(valid, _dot_nt(q, kb), NEG_INF)
            o = _band_ctx_attend(s_b, _dot_nt(q, kc), vb1, vc1, sink=sink_ref[gi:gi + 1, 0:1])
            o_ref[rows, gi * HEAD_DIM:(gi + 1) * HEAD_DIM] = o.astype(o_ref.dtype)


def _swa_call(p_lat, p_ctx, sink_tab):
    b, n, _ = p_lat.shape
    l = p_ctx.shape[1]
    hd = HEAD_DIM
    tq = 512
    win = SWA_BLOCK + 2 * SWA_WINDOW
    qcol, kcol, vcol = COL_SWA_Q // (2 * hd), COL_SWA_K // hd, COL_SWA_V // hd
    return pl.pallas_call(
        functools.partial(_swa_kernel, win=win),
        grid=(b, 2, n // tq),
        in_specs=[pl.BlockSpec((None, tq, 2 * hd), lambda i, k, t: (i, t, qcol + k)),
                  pl.BlockSpec((None, l, hd), lambda i, k, t: (i, 0, kcol + k)),
                  pl.BlockSpec((None, l, hd), lambda i, k, t: (i, 0, vcol + k)),
                  pl.BlockSpec((None, n, hd), lambda i, k, t: (i, 0, kcol + k)),
                  pl.BlockSpec((None, n, hd), lambda i, k, t: (i, 0, vcol + k)),
                  pl.BlockSpec((None, 2, LANES), lambda i, k, t: (k, 0, 0))],
        out_specs=pl.BlockSpec((None, tq, 2 * hd), lambda i, k, t: (i, t, k)),
        out_shape=jax.ShapeDtypeStruct((b, n, BRANCH_WIDTH), BF16),
        compiler_params=_cparams(("parallel", "parallel", "arbitrary")),
        name="attn_swa",
    )(p_lat, p_ctx, p_ctx, p_lat, p_lat, sink_tab)


def _softmax_attend(q, k, v, sink=None):
    s = _dot_nt(q, k)
    m = jnp.max(s, axis=-1, keepdims=True)
    if sink is not None:
        m = jnp.maximum(m, sink)
    e = jnp.exp2(s - m)
    den = jnp.sum(e, axis=-1, keepdims=True)
    if sink is not None:
        den = den + jnp.exp2(sink - m)
    return _dot(e.astype(BF16), v) / den


def _ctx_attn_kernel(p_ref, lam_ref, subg_ref, sink_ref, o_ref, *, lambda_init):
    hd = HEAD_DIM

    def col(c0, h):
        return p_ref[:, c0 + h * hd:c0 + (h + 1) * hd]

    lam = _diff_lambda(lam_ref, lambda_init)
    for h in range(4):
        o = _softmax_attend(col(COL_GQA_Q, h), col(COL_GQA_K, h // 2), col(COL_GQA_V, h // 2))
        o_ref[:, h * hd:(h + 1) * hd] = o.astype(o_ref.dtype)
    for h in range(4):
        q = col(COL_DIFF_Q, h)
        tq = q.shape[0]
        o2 = _softmax_attend(_split_maps(q), col(COL_DIFF_K, h), col(COL_DIFF_V, h))
        d = o2[:tq] - lam * o2[tq:]
        d = _rms(d) * subg_ref[...] * (1.0 - lambda_init)
        o_ref[:, BRANCH_WIDTH + h * hd:BRANCH_WIDTH + (h + 1) * hd] = d.astype(o_ref.dtype)
    for h in range(4):
        o = _softmax_attend(col(COL_NA_Q, h), col(COL_NA_K, h), col(COL_NA_V, h))
        o_ref[:, 2 * BRANCH_WIDTH + h * hd:2 * BRANCH_WIDTH + (h + 1) * hd] = o.astype(o_ref.dtype)
    for h in range(4):
        o = _softmax_attend(col(COL_SWA_Q, h), col(COL_SWA_K, h // 2), col(COL_SWA_V, h // 2),
                            sink=sink_ref[h:h + 1, 0:1])
        o_ref[:, 3 * BRANCH_WIDTH + h * hd:3 * BRANCH_WIDTH + (h + 1) * hd] = o.astype(o_ref.dtype)


def _ctx_attn_call(p_ctx, lam_p, subg, sink_tab, lambda_init):
    b, l, cols = p_ctx.shape
    return pl.pallas_call(
        functools.partial(_ctx_attn_kernel, lambda_init=lambda_init),
        grid=(b,),
        in_specs=[pl.BlockSpec((None, l, cols), lambda i: (i, 0, 0)),
                  pl.BlockSpec((4, DIFF_QK_DIM), lambda i: (0, 0)),
                  pl.BlockSpec((1, HEAD_DIM), lambda i: (0, 0)),
                  pl.BlockSpec((4, LANES), lambda i: (0, 0))],
        out_specs=pl.BlockSpec((None, l, N_BRANCHES * BRANCH_WIDTH), lambda i: (i, 0, 0)),
        out_shape=jax.ShapeDtypeStruct((b, l, N_BRANCHES * BRANCH_WIDTH), BF16),
        compiler_params=_cparams(("parallel",)),
        name="attn_ctx",
    )(p_ctx, lam_p, subg.reshape(1, HEAD_DIM), sink_tab.reshape(4, LANES))


MERGE_CHUNKS = 1


def _merge_kernel(o0, o1, o2, o3, g_ref, wb_ref, wo_ref, x_ref, mod_ref, ng_ref, *rest, with_router):
    if with_router:
        router_ref, xo_ref, h_ref, lg_ref = rest
    else:
        xo_ref, h_ref = rest
    d = x_ref.shape[1]
    dc = d // MERGE_CHUNKS
    o_vals = [o[...] for o in (o0, o1, o2, o3)]
    z = None
    for c in range(MERGE_CHUNKS):
        s = None
        for i, o in enumerate(o_vals):
            y = _dot(o, wb_ref[i, :, c * dc:(c + 1) * dc])
            gate = g_ref[:, i * d + c * dc:i * d + (c + 1) * dc].astype(F32)
            term = jax.nn.sigmoid(gate) * y
            s = term if s is None else s + term
        zc = _dot(s.astype(BF16), wo_ref[c * dc:(c + 1) * dc, :])
        z = zc if z is None else z + zc
    xn = x_ref[...] + mod_ref[2:3, :] * z
    xo_ref[...] = xn
    hn = _norm_mod(xn, ng_ref[...], mod_ref, 3, 4)
    h_ref[...] = hn.astype(h_ref.dtype)
    if with_router:
        lg_ref[...] = _dot_split(hn, router_ref[...])


def _dot_split(a, b):
    a_hi = a.astype(BF16)
    a_lo = (a - a_hi.astype(F32)).astype(BF16)
    b_hi = b.astype(BF16)
    b_lo = (b - b_hi.astype(F32)).astype(BF16)
    n = b.shape[1]
    both = _dot(a_hi, jnp.concatenate([b_hi, b_lo], axis=1))
    return both[:, :n] + both[:, n:] + _dot(a_lo, b_hi)


def _merge_call(o_parts, gates, w_branch, w_out, x, mod, next_g, router=None):
    b, r, d = x.shape
    tm = min(r, 256)
    with_router = router is not None
    whole = dict(pipeline_mode=pl.Buffered(1))

    def o_spec(cb):
        return pl.BlockSpec((None, tm, BRANCH_WIDTH), lambda i, t: (i, t, cb))

    row_spec = pl.BlockSpec((None, tm, d), lambda i, t: (i, t, 0))
    in_specs = [o_spec(cb) for _, cb in o_parts] + [
        pl.BlockSpec((None, tm, N_BRANCHES * d), lambda i, t: (i, t, 0)),
        pl.BlockSpec((N_BRANCHES, BRANCH_WIDTH, d), lambda i, t: (0, 0, 0), **whole),
        pl.BlockSpec((d, d), lambda i, t: (0, 0), **whole),
        row_spec,
        pl.BlockSpec((None, 6, d), lambda i, t: (i, 0, 0)),
        pl.BlockSpec((1, d), lambda i, t: (0, 0))]
    args = [a for a, _ in o_parts] + [gates, w_branch, w_out, x, mod, next_g.reshape(1, d)]
    out_specs = [row_spec, row_spec]
    out_shape = [jax.ShapeDtypeStruct((b, r, d), F32), jax.ShapeDtypeStruct((b, r, d), BF16)]
    if with_router:
        in_specs.append(pl.BlockSpec((d, LANES), lambda i, t: (0, 0), **whole))
        args.append(router)
        out_specs.append(pl.BlockSpec((None, tm, LANES), lambda i, t: (i, t, 0)))
        out_shape.append(jax.ShapeDtypeStruct((b, r, LANES), F32))
    return pl.pallas_call(
        functools.partial(_merge_kernel, with_router=with_router),
        grid=(b, r // tm),
        in_specs=in_specs,
        out_specs=out_specs,
        out_shape=out_shape,
        compiler_params=_cparams(("parallel", "parallel")),
        name="merge",
    )(*args)


def _swiglu_partial(h, wg, wu, wd):
    gate = _dot(h, wg)
    up = _dot(h, wu)
    act = gate * jax.nn.sigmoid(gate) * up
    return _dot(act.astype(BF16), wd)


FFN_ROW_BLOCK = 512
FFN_SHARE = 2


def _ffn_kernel(h_ref, wg_ref, wu_ref, wd_ref, x_ref, mod_ref, ng_ref, nmod_ref, xo_ref, hn_ref,
                acc_ref):
    f = pl.program_id(2)
    j = pl.program_id(3)
    rows = pl.ds(pl.multiple_of(j * FFN_ROW_BLOCK, FFN_ROW_BLOCK), FFN_ROW_BLOCK)

    @pl.when(f == 0)
    def _():
        acc_ref[j] = jnp.zeros(acc_ref.shape[1:], F32)

    acc_ref[j] += _swiglu_partial(h_ref[rows, :], wg_ref[...], wu_ref[...], wd_ref[...])

    @pl.when(f == pl.num_programs(2) - 1)
    def _():
        xn = x_ref[...] + mod_ref[5:6, :] * acc_ref[j]
        xo_ref[...] = xn
        hn_ref[...] = _norm_mod(xn, ng_ref[...], nmod_ref, 0, 1).astype(hn_ref.dtype)


def _ffn_call(h, wg, wu, wd, x, mod, next_g, next_mod):
    b, r, d = x.shape
    dff = wg.shape[1]
    tf = 512
    nf = dff // tf
    tm = FFN_ROW_BLOCK
    group = FFN_SHARE * tm
    assert r % group == 0, (r, group)

    def blk(i, t, f, j):
        return (i, t * FFN_SHARE + jnp.where(f == nf - 1, j, 0), 0)

    row_spec = pl.BlockSpec((None, tm, d), blk)
    mod_spec = pl.BlockSpec((None, 6, d), lambda i, t, f, j: (i, 0, 0))
    return pl.pallas_call(
        _ffn_kernel,
        grid=(b, r // group, nf, FFN_SHARE),
        in_specs=[pl.BlockSpec((None, group, d), lambda i, t, f, j: (i, t, 0)),
                  pl.BlockSpec((d, tf), lambda i, t, f, j: (0, f)),
                  pl.BlockSpec((d, tf), lambda i, t, f, j: (0, f)),
                  pl.BlockSpec((tf, d), lambda i, t, f, j: (f, 0)),
                  row_spec, mod_spec,
                  pl.BlockSpec((1, d), lambda i, t, f, j: (0, 0)),
                  mod_spec],
        out_specs=[row_spec, row_spec],
        out_shape=[jax.ShapeDtypeStruct((b, r, d), F32), jax.ShapeDtypeStruct((b, r, d), BF16)],
        scratch_shapes=[pltpu.VMEM((FFN_SHARE, tm, d), F32)],
        compiler_params=_cparams(("parallel", "parallel", "arbitrary", "arbitrary")),
        name="ffn_dense",
    )(h, wg, wu, wd, x, mod, next_g.reshape(1, d), next_mod)


def _route_kernel(lg_ref, idx_ref, w_ref):
    lg = lg_ref[...]
    lane = lax.broadcasted_iota(jnp.int32, lg.shape, 1)
    valid = lane < N_EXPERTS
    mx = jnp.max(jnp.where(valid, lg, -jnp.inf), axis=-1, keepdims=True)
    e = jnp.where(valid, jnp.exp(lg - mx), 0.0)
    p = e / jnp.sum(e, axis=-1, keepdims=True)
    p1 = jnp.max(p, axis=-1, keepdims=True)
    i1 = jnp.min(jnp.where(p == p1, lane, LANES), axis=-1, keepdims=True)
    rest = jnp.where(jnp.logical_or(lane == i1, jnp.logical_not(valid)), -1.0, p)
    p2 = jnp.max(rest, axis=-1, keepdims=True)
    i2 = jnp.min(jnp.where(rest == p2, lane, LANES), axis=-1, keepdims=True)
    tot = p1 + p2
    idx_ref[...] = jnp.where(lane == 0, i1, jnp.where(lane == 1, i2, 0))
    w_ref[...] = jnp.where(lane == 0, p1 / tot, jnp.where(lane == 1, p2 / tot, 0.0))


def _route_call(logits):
    r = logits.shape[0]
    tm = min(r, 1024)
    spec = pl.BlockSpec((tm, LANES), lambda i: (i, 0))
    return pl.pallas_call(
        _route_kernel,
        grid=(r // tm,),
        in_specs=[spec],
        out_specs=[spec, spec],
        out_shape=[jax.ShapeDtypeStruct((r, LANES), jnp.int32), jax.ShapeDtypeStruct((r, LANES), F32)],
        compiler_params=_cparams(("parallel",)),
        name="route_top2",
    )(logits)


MOE_HALF = 512
MOE_TILE = 2 * MOE_HALF
MOE_FF_TILE = 512
MOE_VMEM_LIMIT = 60 * 1024 * 1024


def _moe_kernel(te_ref, nr_ref, x_ref, wg_ref, wu_ref, wd_ref, o_ref, acc_ref):
    i = pl.program_id(0)
    f = pl.program_id(1)

    @pl.when(f == 0)
    def _():
        acc_ref[...] = jnp.zeros_like(acc_ref)

    for half in range(MOE_TILE // MOE_HALF):
        rows = slice(half * MOE_HALF, (half + 1) * MOE_HALF)

        @pl.when(nr_ref[i] > half * MOE_HALF)
        def _():
            acc_ref[rows, :] += _swiglu_partial(x_ref[rows, :], wg_ref[...].astype(BF16),
                                                wu_ref[...].astype(BF16), wd_ref[...].astype(BF16))

    @pl.when(f == pl.num_programs(1) - 1)
    def _():
        o_ref[...] = acc_ref[...].astype(o_ref.dtype)


def _moe_call(xs, wg, wu, wd, tile_expert, tile_rows):
    r, d = xs.shape
    dff = wg.shape[2]
    tf = MOE_FF_TILE
    nf = dff // tf

    def fidx(i, f, nr):
        return jnp.where(nr[i] > 0, f, nf - 1)

    grid_spec = pltpu.PrefetchScalarGridSpec(
        num_scalar_prefetch=2,
        grid=(r // MOE_TILE, nf),
        in_specs=[pl.BlockSpec((MOE_TILE, d), lambda i, f, te, nr: (i, 0)),
                  pl.BlockSpec((None, d, tf), lambda i, f, te, nr: (te[i], 0, fidx(i, f, nr))),
                  pl.BlockSpec((None, d, tf), lambda i, f, te, nr: (te[i], 0, fidx(i, f, nr))),
                  pl.BlockSpec((None, tf, d), lambda i, f, te, nr: (te[i], fidx(i, f, nr), 0))],
        out_specs=pl.BlockSpec((MOE_TILE, d), lambda i, f, te, nr: (i, 0)),
        scratch_shapes=[pltpu.VMEM((MOE_TILE, d), F32)])
    return pl.pallas_call(
        _moe_kernel,
        grid_spec=grid_spec,
        out_shape=jax.ShapeDtypeStruct((r, d), BF16),
        compiler_params=pltpu.CompilerParams(dimension_semantics=("arbitrary", "arbitrary"),
                                             vmem_limit_bytes=MOE_VMEM_LIMIT),
        name="moe_experts",
    )(tile_expert, tile_rows, xs, wg, wu, wd)


def _moe_plan(idx):
    t = idx.shape[0]
    a = t * TOP_K
    flat_e = idx.reshape(a)
    onehot = (flat_e[:, None] == jnp.arange(N_EXPERTS, dtype=jnp.int32)[None, :]).astype(jnp.int32)
    csum = jnp.cumsum(onehot, axis=0)
    rank = jnp.sum((csum - onehot) * onehot, axis=1)
    counts = csum[-1]
    padded = ((counts + MOE_TILE - 1) // MOE_TILE) * MOE_TILE
    ends = jnp.cumsum(padded)
    offs = ends - padded
    dest = offs[flat_e] + rank
    rows = a + N_EXPERTS * MOE_TILE
    src_tok = (jnp.arange(rows, dtype=jnp.int32) % t).at[dest].set(jnp.arange(a, dtype=jnp.int32) // TOP_K)
    tile_start = jnp.arange(rows // MOE_TILE, dtype=jnp.int32) * MOE_TILE
    past = jnp.sum((tile_start[:, None] >= ends[None, :]).astype(jnp.int32), axis=1)
    tile_expert = jnp.minimum(past, N_EXPERTS - 1)
    tile_rows = jnp.where(past < N_EXPERTS,
                          jnp.clip((offs + counts)[tile_expert] - tile_start, 0, MOE_TILE), 0)
    return dest.reshape(t, TOP_K), src_tok, tile_expert, tile_rows.astype(jnp.int32)


def _final_kernel(x_ref, y0_ref, y1_ref, w_ref, mod_ref, g_ref, o_ref):
    y = w_ref[:, 0:1] * y0_ref[...].astype(F32) + w_ref[:, 1:2] * y1_ref[...].astype(F32)
    xn = x_ref[...] + mod_ref[5:6, :] * y
    o_ref[...] = _rms(xn) * g_ref[...]


def _final_call(x, y0, y1, w, mod, g):
    b, r, d = x.shape
    tm = min(r, 512)
    row_spec = pl.BlockSpec((None, tm, d), lambda i, t: (i, t, 0))
    return pl.pallas_call(
        _final_kernel,
        grid=(b, r // tm),
        in_specs=[row_spec, row_spec, row_spec,
                  pl.BlockSpec((None, tm, LANES), lambda i, t: (i, t, 0)),
                  pl.BlockSpec((None, 6, d), lambda i, t: (i, 0, 0)),
                  pl.BlockSpec((1, d), lambda i, t: (0, 0))],
        out_specs=row_spec,
        out_shape=jax.ShapeDtypeStruct((b, r, d), F32),
        compiler_params=_cparams(("parallel", "parallel")),
        name="final_norm",
    )(x, y0, y1, w, mod, g.reshape(1, d))


def _rope_tables(n):
    pos = jnp.arange(n)
    rows = (pos // GRID_W).astype(F32)
    cols = (pos % GRID_W).astype(F32)
    lane = np.arange(LANES)
    out = []
    for hw in (32, 16):
        period = 4 * hw
        u = lane % period
        use_cols = (u // (2 * hw)) == 1
        w = u % (2 * hw)
        freqs = jnp.asarray(ROPE_THETA ** (-(w % hw).astype(np.float32) / hw), F32)
        p = jnp.where(jnp.asarray(use_cols)[None, :], cols[:, None], rows[:, None])
        ang = p * freqs[None, :]
        sign = jnp.asarray(np.where(w < hw, -1.0, 1.0), F32)
        out += [jnp.cos(ang), jnp.sin(ang) * sign[None, :]]
    return out


def kernel(x, c, ctx, c_ctx, attn_norm_g, ffn_norm_g, ada_w, ada_b, w_in, qk_norm_g, diff_lambda,
           diff_subln_g, na_rpb, swa_sink, w_branch, w_out, ffn_w_gate, ffn_w_up, ffn_w_down,
           moe_router, moe_w_gate, moe_w_up, moe_w_down, final_norm_g):
    b, n, d = x.shape
    l = ctx.shape[1]
    depth = w_in.shape[0]
    assert depth == 2, "laid out for one dense layer followed by one routed last layer"
    rows_n = n // GRID_W

    lat_tables = _rope_tables(n)
    ctx_flat = ctx.reshape(1, b * l, d)
    ones = jnp.ones((b * l, LANES), F32)
    zeros = jnp.zeros((b * l, LANES), F32)
    ctx_tables = [ones, zeros, ones, zeros]

    cvec = jnp.zeros((8, d), F32).at[:b].set(c).at[b].set(c_ctx)
    mods = []
    for i in range(depth):
        m = _adaln(cvec, ada_w, ada_b, i).reshape(8, 6, d)
        mods.append((m[:b], m[b:b + 1]))

    def mixers(i, h_lat, h_ctx):
        lambda_init = 0.8 - 0.6 * math.exp(-0.3 * i)
        w_qkv = _qkv_weights_call(w_in, i)
        sink_tab = jnp.broadcast_to((swa_sink[i].astype(F32) * LOG2E)[:, None], (4, LANES))
        p_lat = _qkv_call(h_lat, w_qkv, lat_tables, qk_norm_g[i])
        p_ctx = _qkv_call(h_ctx, w_qkv, ctx_tables, qk_norm_g[i]).reshape(b, l, QKV_COLS)
        o_lat = [_gqa_call(p_lat, p_ctx),
                 _diff_call(p_lat, p_ctx, diff_lambda[i], diff_subln_g[i], lambda_init),
                 _na_call(p_lat, p_ctx, _na_bias_tables(na_rpb[i])),
                 _swa_call(p_lat, p_ctx, sink_tab.reshape(2, 2, LANES))]
        return p_ctx, o_lat, sink_tab, lambda_init

    mod_lat, mod_ctx = mods[0]
    h_lat = _norm_mod_call(x, attn_norm_g[0], mod_lat, 0, 1)
    h_ctx = _norm_mod_call(ctx_flat, attn_norm_g[0], mod_ctx, 0, 1)
    p_ctx, o_lat, sink_tab, lambda_init = mixers(0, h_lat, h_ctx)
    wb = w_branch[0].astype(BF16)
    wo = w_out[0].astype(BF16)
    x_lat, h_lat = _merge_call([(o, 0) for o in o_lat], _gate_proj_call(h_lat, w_in, 0), wb, wo, x,
                               mod_lat, ffn_norm_g[0])
    o_ctx = _ctx_attn_call(p_ctx, diff_lambda[0], diff_subln_g[0], sink_tab, lambda_init)
    o_ctx = o_ctx.reshape(1, b * l, N_BRANCHES * BRANCH_WIDTH)
    x_ctx, h_ctx = _merge_call([(o_ctx, k) for k in range(N_BRANCHES)], _gate_proj_call(h_ctx, w_in, 0),
                               wb, wo, ctx_flat, mod_ctx, ffn_norm_g[0])
    wg = ffn_w_gate[0].astype(BF16)
    wu = ffn_w_up[0].astype(BF16)
    wd = ffn_w_down[0].astype(BF16)
    _, h_ctx = _ffn_call(h_ctx, wg, wu, wd, x_ctx, mod_ctx, attn_norm_g[1], mods[1][1])
    x_lat, h_lat = _ffn_call(h_lat, wg, wu, wd, x_lat, mod_lat, attn_norm_g[1], mods[1][0])

    mod_lat, _ = mods[1]
    _, o_lat, _, _ = mixers(1, h_lat, h_ctx)
    router = jnp.zeros((d, LANES), F32).at[:, :N_EXPERTS].set(moe_router[0])
    x_lat, h_lat, logits = _merge_call(
        [(o, 0) for o in o_lat], _gate_proj_call(h_lat, w_in, 1),
        w_branch[1].astype(BF16), w_out[1].astype(BF16), x_lat, mod_lat, ffn_norm_g[1], router)

    idx_pad, wts_pad = _route_call(logits.reshape(b * n, LANES))
    dest, src_tok, tile_expert, tile_rows = _moe_plan(idx_pad[:, :TOP_K])
    xs = h_lat.reshape(b * n, d).at[src_tok].get(mode="promise_in_bounds")
    ys = _moe_call(xs, moe_w_gate[0], moe_w_up[0], moe_w_down[0], tile_expert, tile_rows)
    y0 = ys.at[dest[:, 0]].get(mode="promise_in_bounds").reshape(b, n, d)
    y1 = ys.at[dest[:, 1]].get(mode="promise_in_bounds").reshape(b, n, d)
    return _final_call(x_lat, y0, y1, wts_pad.reshape(b, n, LANES), mod_lat, final_norm_g)
```

```python
import functools
import math

import numpy as np
import jax
import jax.numpy as jnp
from jax import lax
from jax.experimental import pallas as pl
from jax.experimental.pallas import tpu as pltpu

F32 = jnp.float32
BF16 = jnp.bfloat16

GRID_W = 64
HEAD_DIM = 128
N_BRANCHES = 4
BRANCH_WIDTH = 4 * HEAD_DIM
DIFF_QK_DIM = 64
NA_ROWS = 8
NA_COLS = 16
SWA_WINDOW = 128
N_EXPERTS = 8
TOP_K = 2
NORM_EPS = 1e-6
ROPE_THETA = 10000.0
NEG_INF = -1e30
LOG2E = math.log2(math.e)

QKV_COLS = 5120
COL_GQA_Q, COL_GQA_K, COL_GQA_V = 0, 512, 768
COL_DIFF_Q, COL_DIFF_K, COL_DIFF_V = 1024, 1536, 2048
COL_NA_Q, COL_NA_K, COL_NA_V = 2560, 3072, 3584
COL_SWA_Q, COL_SWA_K, COL_SWA_V = 4096, 4608, 4864

LANES = 128
VMEM_LIMIT = 56 * 1024 * 1024

QS128 = HEAD_DIM ** -0.5 * LOG2E
QS64 = DIFF_QK_DIM ** -0.5 * LOG2E


def _cparams(sem):
    return pltpu.CompilerParams(dimension_semantics=sem, vmem_limit_bytes=VMEM_LIMIT)


def _dot(a, b):
    return jnp.dot(a, b, preferred_element_type=F32)


def _dot_nt(a, b):
    return lax.dot_general(a, b, (((1,), (1,)), ((), ())), preferred_element_type=F32)


def _rms(x):
    return x * lax.rsqrt(jnp.mean(x * x, axis=-1, keepdims=True) + NORM_EPS)


def _norm_mod(x, g, mod_ref, shift_idx, scale_idx):
    y = _rms(x) * g
    return y * (1.0 + mod_ref[scale_idx:scale_idx + 1, :]) + mod_ref[shift_idx:shift_idx + 1, :]


def _adaln_kernel(c_ref, w_ref, b_ref, o_ref):
    c = c_ref[...]
    s = c * jax.nn.sigmoid(c)
    o_ref[...] = jnp.dot(s, w_ref[...], preferred_element_type=F32,
                         precision=lax.Precision.HIGHEST) + b_ref[...]


def _adaln(cvec, w, b, layer):
    rows, d = cvec.shape
    cols = w.shape[2]
    tn = 1024 if cols % 1024 == 0 else cols
    return pl.pallas_call(
        _adaln_kernel,
        grid=(cols // tn,),
        in_specs=[pl.BlockSpec((rows, d), lambda j: (0, 0)),
                  pl.BlockSpec((None, d, tn), lambda j: (layer, 0, j)),
                  pl.BlockSpec((None, 1, tn), lambda j: (layer, 0, j))],
        out_specs=pl.BlockSpec((rows, tn), lambda j: (0, j)),
        out_shape=jax.ShapeDtypeStruct((rows, cols), F32),
        compiler_params=_cparams(("arbitrary",)),
        name="adaln",
    )(cvec, w, b.reshape(b.shape[0], 1, cols))


def _norm_mod_kernel(x_ref, g_ref, mod_ref, h_ref, *, shift_idx, scale_idx):
    h_ref[...] = _norm_mod(x_ref[...], g_ref[...], mod_ref, shift_idx, scale_idx).astype(BF16)


def _norm_mod_call(x, g, mod, shift_idx, scale_idx):
    b, r, d = x.shape
    tm = min(r, 512)
    return pl.pallas_call(
        functools.partial(_norm_mod_kernel, shift_idx=shift_idx, scale_idx=scale_idx),
        grid=(b, r // tm),
        in_specs=[pl.BlockSpec((None, tm, d), lambda i, t: (i, t, 0)),
                  pl.BlockSpec((1, d), lambda i, t: (0, 0)),
                  pl.BlockSpec((None, 6, d), lambda i, t: (i, 0, 0))],
        out_specs=pl.BlockSpec((None, tm, d), lambda i, t: (i, t, 0)),
        out_shape=jax.ShapeDtypeStruct((b, r, d), BF16),
        compiler_params=_cparams(("parallel", "parallel")),
        name="norm_mod",
    )(x, g.reshape(1, d), mod)


QKV_TILE = 4 * LANES
_PLAIN = (None, None, 1.0)
_QKV_TILE_OPS = {
    0: [(0, 32, QS128)] * 4,
    1: [(1, 32, 1.0)] * 2 + [_PLAIN] * 2,
    2: [(None, 16, QS64)] * 4,
    3: [(None, 16, 1.0)] * 4,
    5: [(None, None, QS128)] * 4,
    8: [(None, 32, QS128)] * 4,
    9: [(None, 32, 1.0)] * 2 + [_PLAIN] * 2,
}


def _rope(y, cos, sin, hw):
    lane = lax.broadcasted_iota(jnp.int32, y.shape, 1)
    first = (lane % (2 * hw)) < hw
    partner = jnp.where(first, pltpu.roll(y, LANES - hw, 1), pltpu.roll(y, hw, 1))
    return y * cos + partner * sin


def _qkv_kernel(h_ref, w_ref, cos32_ref, sin32_ref, cos16_ref, sin16_ref, g_ref, o_ref):
    h = h_ref[...]
    for j in range(o_ref.shape[1] // QKV_TILE):
        acc = _dot(h, w_ref[:, j * QKV_TILE:(j + 1) * QKV_TILE])
        for c, (norm_row, hw, scale) in enumerate(_QKV_TILE_OPS.get(j, [_PLAIN] * 4)):
            y = acc[:, c * LANES:(c + 1) * LANES]
            if norm_row is not None:
                y = _rms(y) * g_ref[norm_row:norm_row + 1, :]
            if hw == 32:
                y = _rope(y, cos32_ref[...], sin32_ref[...], 32)
            elif hw == 16:
                y = _rope(y, cos16_ref[...], sin16_ref[...], 16)
            if scale != 1.0:
                y = y * scale
            col = j * QKV_TILE + c * LANES
            o_ref[:, col:col + LANES] = y.astype(o_ref.dtype)


def _qkv_call(h, w_qkv, tables, qk_g):
    b, r, d = h.shape
    cols = w_qkv.shape[1]
    tm = min(r, 512)
    tab_spec = pl.BlockSpec((tm, LANES), lambda i, t: (t, 0))
    return pl.pallas_call(
        _qkv_kernel,
        grid=(b, r // tm),
        in_specs=[pl.BlockSpec((None, tm, d), lambda i, t: (i, t, 0)),
                  pl.BlockSpec((d, cols), lambda i, t: (0, 0), pipeline_mode=pl.Buffered(1)),
                  tab_spec, tab_spec, tab_spec, tab_spec,
                  pl.BlockSpec((2, LANES), lambda i, t: (0, 0))],
        out_specs=pl.BlockSpec((None, tm, cols), lambda i, t: (i, t, 0)),
        out_shape=jax.ShapeDtypeStruct((b, r, cols), BF16),
        compiler_params=_cparams(("parallel", "parallel")),
        name="qkv_proj",
    )(h, w_qkv, *tables, qk_g)


def _matmul_kernel(h_ref, w_ref, o_ref):
    o_ref[...] = _dot(h_ref[...], w_ref[...].astype(BF16)).astype(o_ref.dtype)


def _gate_proj_call(h, w_in, layer):
    b, r, d = h.shape
    tn = 1024
    cols = w_in.shape[2] - QKV_COLS
    col0 = QKV_COLS // tn
    tm = 2048 if r % 2048 == 0 else r
    return pl.pallas_call(
        _matmul_kernel,
        grid=(b, r // tm, cols // tn),
        in_specs=[pl.BlockSpec((None, tm, d), lambda i, t, j: (i, t, 0)),
                  pl.BlockSpec((None, d, tn), lambda i, t, j: (layer, 0, col0 + j))],
        out_specs=pl.BlockSpec((None, tm, tn), lambda i, t, j: (i, t, j)),
        out_shape=jax.ShapeDtypeStruct((b, r, cols), BF16),
        compiler_params=_cparams(("parallel", "parallel", "arbitrary")),
        name="gate_proj",
    )(h, w_in)


def _cast_kernel(w_ref, o_ref):
    o_ref[...] = w_ref[...].astype(o_ref.dtype)


def _qkv_weights_call(w_in, layer):
    d = w_in.shape[1]
    return pl.pallas_call(
        _cast_kernel,
        grid=(QKV_COLS // QKV_TILE,),
        in_specs=[pl.BlockSpec((None, d, QKV_TILE), lambda j: (layer, 0, j))],
        out_specs=pl.BlockSpec((d, QKV_TILE), lambda j: (0, j)),
        out_shape=jax.ShapeDtypeStruct((d, QKV_COLS), BF16),
        compiler_params=_cparams(("parallel",)),
        name="qkv_weights",
    )(w_in)


def _diff_lambda(lam_ref, lambda_init):
    a = jnp.sum(lam_ref[0:1, :] * lam_ref[1:2, :], axis=-1, keepdims=True)
    b = jnp.sum(lam_ref[2:3, :] * lam_ref[3:4, :], axis=-1, keepdims=True)
    return jnp.exp(a) - jnp.exp(b) + lambda_init


def _split_maps(q):
    lane = lax.broadcasted_iota(jnp.int32, q.shape, 1)
    zero = jnp.zeros_like(q)
    return jnp.concatenate([jnp.where(lane < DIFF_QK_DIM, q, zero),
                            jnp.where(lane >= DIFF_QK_DIM, q, zero)], axis=0)


FLASH_ROW_BLOCK = 128
FLASH_KV_CHUNK = 512
FLASH_Q_TILE = 1024


def _flash_pair(q2, kc_ref, vc_ref, kl_ref, vl_ref, tk):
    m_rows = q2.shape[0]
    n_lat = kl_ref.shape[0]
    nb = m_rows // FLASH_ROW_BLOCK
    qs = [q2[i * FLASH_ROW_BLOCK:(i + 1) * FLASH_ROW_BLOCK] for i in range(nb)]

    def step(k, v, carry):
        v1 = jnp.concatenate([v, jnp.ones_like(v)], axis=1)
        out = []
        for q, (m, acc) in zip(qs, carry):
            s = _dot_nt(q, k)
            m_new = jnp.maximum(m, jnp.max(s, axis=-1, keepdims=True))
            p = jnp.exp2(s - m_new)
            acc = jnp.exp2(m - m_new) * acc + _dot(p.astype(BF16), v1)
            out.append((m_new, acc))
        return tuple(out)

    carry = tuple((jnp.full((FLASH_ROW_BLOCK, 1), NEG_INF, F32),
                   jnp.zeros((FLASH_ROW_BLOCK, 2 * HEAD_DIM), F32)) for _ in range(nb))
    carry = step(kc_ref[...], vc_ref[...], carry)

    for c in range(n_lat // tk):
        carry = step(kl_ref[c * tk:(c + 1) * tk, :], vl_ref[c * tk:(c + 1) * tk, :], carry)
    return jnp.concatenate([acc[:, :HEAD_DIM] / acc[:, HEAD_DIM:] for _, acc in carry], axis=0)


def _gqa_kernel(q_ref, kc_ref, vc_ref, kl_ref, vl_ref, o_ref, *, tk):
    tq = q_ref.shape[0]
    q = q_ref[...]
    q2 = jnp.concatenate([q[:, :HEAD_DIM], q[:, HEAD_DIM:]], axis=0)
    o = _flash_pair(q2, kc_ref, vc_ref, kl_ref, vl_ref, tk)
    o_ref[:, :HEAD_DIM] = o[:tq].astype(o_ref.dtype)
    o_ref[:, HEAD_DIM:] = o[tq:].astype(o_ref.dtype)


def _diff_kernel(q_ref, kc_ref, vc_ref, kl_ref, vl_ref, lam_ref, subg_ref, o_ref, *, tk, lambda_init):
    tq = q_ref.shape[0]
    o = _flash_pair(_split_maps(q_ref[...]), kc_ref, vc_ref, kl_ref, vl_ref, tk)
    lam = _diff_lambda(lam_ref, lambda_init)
    d = o[:tq] - lam * o[tq:]
    o_ref[...] = (_rms(d) * subg_ref[...] * (1.0 - lambda_init)).astype(o_ref.dtype)


def _gqa_call(p_lat, p_ctx):
    b, n, _ = p_lat.shape
    l = p_ctx.shape[1]
    tq = FLASH_Q_TILE
    tk = FLASH_KV_CHUNK
    hd = HEAD_DIM
    kcol, vcol = COL_GQA_K // hd, COL_GQA_V // hd
    return pl.pallas_call(
        functools.partial(_gqa_kernel, tk=tk),
        grid=(b, 2, n // tq),
        in_specs=[pl.BlockSpec((None, tq, 2 * hd), lambda i, k, t: (i, t, k)),
                  pl.BlockSpec((None, l, hd), lambda i, k, t: (i, 0, kcol + k)),
                  pl.BlockSpec((None, l, hd), lambda i, k, t: (i, 0, vcol + k)),
                  pl.BlockSpec((None, n, hd), lambda i, k, t: (i, 0, kcol + k)),
                  pl.BlockSpec((None, n, hd), lambda i, k, t: (i, 0, vcol + k))],
        out_specs=pl.BlockSpec((None, tq, 2 * hd), lambda i, k, t: (i, t, k)),
        out_shape=jax.ShapeDtypeStruct((b, n, BRANCH_WIDTH), BF16),
        compiler_params=_cparams(("parallel", "parallel", "arbitrary")),
        name="attn_gqa",
    )(p_lat, p_ctx, p_ctx, p_lat, p_lat)


def _diff_call(p_lat, p_ctx, lam_p, subg, lambda_init):
    b, n, _ = p_lat.shape
    l = p_ctx.shape[1]
    tq = FLASH_Q_TILE
    tk = FLASH_KV_CHUNK
    hd = HEAD_DIM
    qcol, kcol, vcol = COL_DIFF_Q // hd, COL_DIFF_K // hd, COL_DIFF_V // hd
    return pl.pallas_call(
        functools.partial(_diff_kernel, tk=tk, lambda_init=lambda_init),
        grid=(b, 4, n // tq),
        in_specs=[pl.BlockSpec((None, tq, hd), lambda i, h, t: (i, t, qcol + h)),
                  pl.BlockSpec((None, l, hd), lambda i, h, t: (i, 0, kcol + h)),
                  pl.BlockSpec((None, l, hd), lambda i, h, t: (i, 0, vcol + h)),
                  pl.BlockSpec((None, n, hd), lambda i, h, t: (i, 0, kcol + h)),
                  pl.BlockSpec((None, n, hd), lambda i, h, t: (i, 0, vcol + h)),
                  pl.BlockSpec((4, DIFF_QK_DIM), lambda i, h, t: (0, 0)),
                  pl.BlockSpec((1, hd), lambda i, h, t: (0, 0))],
        out_specs=pl.BlockSpec((None, tq, hd), lambda i, h, t: (i, t, h)),
        out_shape=jax.ShapeDtypeStruct((b, n, BRANCH_WIDTH), BF16),
        compiler_params=_cparams(("parallel", "parallel", "arbitrary")),
        name="attn_diff",
    )(p_lat, p_ctx, p_ctx, p_lat, p_lat, lam_p, subg.reshape(1, hd))


NA_GROUP_ROWS = 8
NA_BLOCK_ROWS = 4
NA_BLOCK_WIN_ROWS = NA_BLOCK_ROWS + NA_ROWS
NA_WIN_ROWS = 2 * NA_GROUP_ROWS


def _with_ones(v):
    return jnp.concatenate([v, jnp.ones_like(v)], axis=1)


def _band_ctx_attend(s_b, s_c, vb1, vc1, sink=None):
    m = jnp.maximum(jnp.max(s_b, axis=-1, keepdims=True), jnp.max(s_c, axis=-1, keepdims=True))
    if sink is not None:
        m = jnp.maximum(m, sink)
    acc = _dot(jnp.exp2(s_c - m).astype(BF16), vc1) + _dot(jnp.exp2(s_b - m).astype(BF16), vb1)
    den = acc[:, HEAD_DIM:]
    if sink is not None:
        den = den + jnp.exp2(sink - m)
    return acc[:, :HEAD_DIM] / den


NA_DI = 2 * NA_ROWS - 1
NA_TAB_PAD = NA_WIN_ROWS
NA_TAB_LANES = 3072


def _na_bias_tables(rpb):
    nh = rpb.shape[0]
    qc = np.arange(GRID_W)[:, None]
    kc = np.arange(GRID_W)[None, :]
    cstart = np.clip(qc - NA_COLS // 2, 0, GRID_W - NA_COLS)
    ok_col = (kc >= cstart) & (kc < cstart + NA_COLS)
    dj = np.clip(kc - qc + NA_COLS - 1, 0, 2 * NA_COLS - 2)
    sel_j = (dj[..., None] == np.arange(2 * NA_COLS - 1)) & ok_col[..., None]
    by_col = jnp.einsum("hij,qkj->hqik", rpb.astype(F32), jnp.asarray(sel_j, F32),
                        precision=lax.Precision.HIGHEST)
    by_col = jnp.where(ok_col[None, :, None, :], by_col * LOG2E, NEG_INF)
    strip = by_col.reshape(nh, GRID_W, NA_DI * GRID_W)
    total = NA_TAB_LANES + GRID_W
    strip = jnp.pad(strip, ((0, 0), (0, 0), (NA_TAB_PAD * GRID_W, total - (NA_TAB_PAD + NA_DI) * GRID_W)),
                    constant_values=NEG_INF)
    return jnp.stack([strip[:, :, :NA_TAB_LANES], strip[:, :, GRID_W:]], axis=1)


def _na_kernel(q_ref, kc_ref, vc_ref, kl_ref, vl_ref, tab_ref, o_ref, *, rows_n):
    g = pl.program_id(2)
    win = NA_BLOCK_WIN_ROWS * GRID_W
    blk_q = NA_BLOCK_ROWS * GRID_W
    kc = kc_ref[...]
    vc1 = _with_ones(vc_ref[...])
    lane = lax.broadcasted_iota(jnp.int32, (GRID_W, win), 1)
    for rb in range(NA_GROUP_ROWS // NA_BLOCK_ROWS):
        r0 = g * NA_GROUP_ROWS + rb * NA_BLOCK_ROWS
        ws = jnp.clip(r0 - NA_ROWS // 2, 0, rows_n - NA_BLOCK_WIN_ROWS)
        off = pl.multiple_of(ws * GRID_W, NA_BLOCK_ROWS * GRID_W)
        bias = []
        for a in range(NA_BLOCK_ROWS):
            r = r0 + a
            rs = jnp.clip(r - NA_ROWS // 2, 0, rows_n - NA_ROWS)
            blk = ws - r + NA_ROWS - 1 + NA_TAB_PAD
            start = pl.multiple_of((blk >> 1) * (2 * GRID_W), 2 * GRID_W)
            strip = tab_ref[blk & 1, :, pl.ds(start, win)]
            lo = (rs - ws) * GRID_W
            in_rows = jnp.logical_and(lane >= lo, lane < lo + NA_ROWS * GRID_W)
            bias.append(jnp.where(in_rows, strip, NEG_INF))
        q = q_ref[rb * blk_q:(rb + 1) * blk_q, :]
        s_b = _dot_nt(q, kl_ref[pl.ds(off, win), :]) + jnp.concatenate(bias, axis=0)
        o = _band_ctx_attend(s_b, _dot_nt(q, kc), _with_ones(vl_ref[pl.ds(off, win), :]), vc1)
        o_ref[rb * blk_q:(rb + 1) * blk_q, :] = o.astype(o_ref.dtype)


def _na_call(p_lat, p_ctx, bias):
    b, n, _ = p_lat.shape
    l = p_ctx.shape[1]
    hd = HEAD_DIM
    rows_n = n // GRID_W
    n_groups = rows_n // NA_GROUP_ROWS
    tq = NA_GROUP_ROWS * GRID_W
    qcol, kcol, vcol = COL_NA_Q // hd, COL_NA_K // hd, COL_NA_V // hd

    return pl.pallas_call(
        functools.partial(_na_kernel, rows_n=rows_n),
        grid=(b, 4, n_groups),
        in_specs=[pl.BlockSpec((None, tq, hd), lambda i, h, g: (i, g, qcol + h)),
                  pl.BlockSpec((None, l, hd), lambda i, h, g: (i, 0, kcol + h)),
                  pl.BlockSpec((None, l, hd), lambda i, h, g: (i, 0, vcol + h)),
                  pl.BlockSpec((None, n, hd), lambda i, h, g: (i, 0, kcol + h)),
                  pl.BlockSpec((None, n, hd), lambda i, h, g: (i, 0, vcol + h)),
                  pl.BlockSpec((None, 2, GRID_W, NA_TAB_LANES), lambda i, h, g: (h, 0, 0, 0))],
        out_specs=pl.BlockSpec((None, tq, hd), lambda i, h, g: (i, g, h)),
        out_shape=jax.ShapeDtypeStruct((b, n, BRANCH_WIDTH), BF16),
        compiler_params=_cparams(("parallel", "parallel", "arbitrary")),
        name="attn_na",
    )(p_lat, p_ctx, p_ctx, p_lat, p_lat, bias)


SWA_BLOCK = 256


def _swa_kernel(q_ref, kc_ref, vc_ref, kl_ref, vl_ref, sink_ref, o_ref, *, win):
    t = pl.program_id(2)
    tq = q_ref.shape[0]
    n = kl_ref.shape[0]
    kc = kc_ref[...]
    vc1 = _with_ones(vc_ref[...])
    row = lax.broadcasted_iota(jnp.int32, (SWA_BLOCK, win), 0)
    col = lax.broadcasted_iota(jnp.int32, (SWA_BLOCK, win), 1)
    for rb in range(tq // SWA_BLOCK):
        qs = t * tq + rb * SWA_BLOCK
        ws = pl.multiple_of(jnp.clip(qs - SWA_WINDOW, 0, n - win), SWA_WINDOW)
        kb = kl_ref[pl.ds(ws, win), :]
        vb1 = _with_ones(vl_ref[pl.ds(ws, win), :])
        valid = jnp.abs(row + (qs - ws) - col) <= SWA_WINDOW
        rows = slice(rb * SWA_BLOCK, (rb + 1) * SWA_BLOCK)
        for gi in range(2):
            q = q_ref[rows, gi * HEAD_DIM:(gi + 1) * HEAD_DIM]
            s_b = jnp.where(valid, _dot_nt(q, kb), NEG_INF)
            o = _band_ctx_attend(s_b, _dot_nt(q, kc), vb1, vc1, sink=sink_ref[gi:gi + 1, 0:1])
            o_ref[rows, gi * HEAD_DIM:(gi + 1) * HEAD_DIM] = o.astype(o_ref.dtype)


def _swa_call(p_lat, p_ctx, sink_tab):
    b, n, _ = p_lat.shape
    l = p_ctx.shape[1]
    hd = HEAD_DIM
    tq = 512
    win = SWA_BLOCK + 2 * SWA_WINDOW
    qcol, kcol, vcol = COL_SWA_Q // (2 * hd), COL_SWA_K // hd, COL_SWA_V // hd
    return pl.pallas_call(
        functools.partial(_swa_kernel, win=win),
        grid=(b, 2, n // tq),
        in_specs=[pl.BlockSpec((None, tq, 2 * hd), lambda i, k, t: (i, t, qcol + k)),
                  pl.BlockSpec((None, l, hd), lambda i, k, t: (i, 0, kcol + k)),
                  pl.BlockSpec((None, l, hd), lambda i, k, t: (i, 0, vcol + k)),
                  pl.BlockSpec((None, n, hd), lambda i, k, t: (i, 0, kcol + k)),
                  pl.BlockSpec((None, n, hd), lambda i, k, t: (i, 0, vcol + k)),
                  pl.BlockSpec((None, 2, LANES), lambda i, k, t: (k, 0, 0))],
        out_specs=pl.BlockSpec((None, tq, 2 * hd), lambda i, k, t: (i, t, k)),
        out_shape=jax.ShapeDtypeStruct((b, n, BRANCH_WIDTH), BF16),
        compiler_params=_cparams(("parallel", "parallel", "arbitrary")),
        name="attn_swa",
    )(p_lat, p_ctx, p_ctx, p_lat, p_lat, sink_tab)


def _softmax_attend(q, k, v, sink=None):
    s = _dot_nt(q, k)
    m = jnp.max(s, axis=-1, keepdims=True)
    if sink is not None:
        m = jnp.maximum(m, sink)
    e = jnp.exp2(s - m)
    den = jnp.sum(e, axis=-1, keepdims=True)
    if sink is not None:
        den = den + jnp.exp2(sink - m)
    return _dot(e.astype(BF16), v) / den


def _ctx_attn_kernel(p_ref, lam_ref, subg_ref, sink_ref, o_ref, *, lambda_init):
    hd = HEAD_DIM

    def col(c0, h):
        return p_ref[:, c0 + h * hd:c0 + (h + 1) * hd]

    lam = _diff_lambda(lam_ref, lambda_init)
    for h in range(4):
        o = _softmax_attend(col(COL_GQA_Q, h), col(COL_GQA_K, h // 2), col(COL_GQA_V, h // 2))
        o_ref[:, h * hd:(h + 1) * hd] = o.astype(o_ref.dtype)
    for h in range(4):
        q = col(COL_DIFF_Q, h)
        tq = q.shape[0]
        o2 = _softmax_attend(_split_maps(q), col(COL_DIFF_K, h), col(COL_DIFF_V, h))
        d = o2[:tq] - lam * o2[tq:]
        d = _rms(d) * subg_ref[...] * (1.0 - lambda_init)
        o_ref[:, BRANCH_WIDTH + h * hd:BRANCH_WIDTH + (h + 1) * hd] = d.astype(o_ref.dtype)
    for h in range(4):
        o = _softmax_attend(col(COL_NA_Q, h), col(COL_NA_K, h), col(COL_NA_V, h))
        o_ref[:, 2 * BRANCH_WIDTH + h * hd:2 * BRANCH_WIDTH + (h + 1) * hd] = o.astype(o_ref.dtype)
    for h in range(4):
        o = _softmax_attend(col(COL_SWA_Q, h), col(COL_SWA_K, h // 2), col(COL_SWA_V, h // 2),
                            sink=sink_ref[h:h + 1, 0:1])
        o_ref[:, 3 * BRANCH_WIDTH + h * hd:3 * BRANCH_WIDTH + (h + 1) * hd] = o.astype(o_ref.dtype)


def _ctx_attn_call(p_ctx, lam_p, subg, sink_tab, lambda_init):
    b, l, cols = p_ctx.shape
    return pl.pallas_call(
        functools.partial(_ctx_attn_kernel, lambda_init=lambda_init),
        grid=(b,),
        in_specs=[pl.BlockSpec((None, l, cols), lambda i: (i, 0, 0)),
                  pl.BlockSpec((4, DIFF_QK_DIM), lambda i: (0, 0)),
                  pl.BlockSpec((1, HEAD_DIM), lambda i: (0, 0)),
                  pl.BlockSpec((4, LANES), lambda i: (0, 0))],
        out_specs=pl.BlockSpec((None, l, N_BRANCHES * BRANCH_WIDTH), lambda i: (i, 0, 0)),
        out_shape=jax.ShapeDtypeStruct((b, l, N_BRANCHES * BRANCH_WIDTH), BF16),
        compiler_params=_cparams(("parallel",)),
        name="attn_ctx",
    )(p_ctx, lam_p, subg.reshape(1, HEAD_DIM), sink_tab.reshape(4, LANES))


MERGE_CHUNKS = 1


def _merge_kernel(o0, o1, o2, o3, g_ref, wb_ref, wo_ref, x_ref, mod_ref, ng_ref, *rest, with_router):
    if with_router:
        router_ref, xo_ref, h_ref, lg_ref = rest
    else:
        xo_ref, h_ref = rest
    d = x_ref.shape[1]
    dc = d // MERGE_CHUNKS
    o_vals = [o[...] for o in (o0, o1, o2, o3)]
    z = None
    for c in range(MERGE_CHUNKS):
        s = None
        for i, o in enumerate(o_vals):
            y = _dot(o, wb_ref[i, :, c * dc:(c + 1) * dc])
            gate = g_ref[:, i * d + c * dc:i * d + (c + 1) * dc].astype(F32)
            term = jax.nn.sigmoid(gate) * y
            s = term if s is None else s + term
        zc = _dot(s.astype(BF16), wo_ref[c * dc:(c + 1) * dc, :])
        z = zc if z is None else z + zc
    xn = x_ref[...] + mod_ref[2:3, :] * z
    xo_ref[...] = xn
    hn = _norm_mod(xn, ng_ref[...], mod_ref, 3, 4)
    h_ref[...] = hn.astype(h_ref.dtype)
    if with_router:
        lg_ref[...] = _dot_split(hn, router_ref[...])


def _dot_split(a, b):
    a_hi = a.astype(BF16)
    a_lo = (a - a_hi.astype(F32)).astype(BF16)
    b_hi = b.astype(BF16)
    b_lo = (b - b_hi.astype(F32)).astype(BF16)
    n = b.shape[1]
    both = _dot(a_hi, jnp.concatenate([b_hi, b_lo], axis=1))
    return both[:, :n] + both[:, n:] + _dot(a_lo, b_hi)


def _merge_call(o_parts, gates, w_branch, w_out, x, mod, next_g, router=None):
    b, r, d = x.shape
    tm = min(r, 256)
    with_router = router is not None
    whole = dict(pipeline_mode=pl.Buffered(1))

    def o_spec(cb):
        return pl.BlockSpec((None, tm, BRANCH_WIDTH), lambda i, t: (i, t, cb))

    row_spec = pl.BlockSpec((None, tm, d), lambda i, t: (i, t, 0))
    in_specs = [o_spec(cb) for _, cb in o_parts] + [
        pl.BlockSpec((None, tm, N_BRANCHES * d), lambda i, t: (i, t, 0)),
        pl.BlockSpec((N_BRANCHES, BRANCH_WIDTH, d), lambda i, t: (0, 0, 0), **whole),
        pl.BlockSpec((d, d), lambda i, t: (0, 0), **whole),
        row_spec,
        pl.BlockSpec((None, 6, d), lambda i, t: (i, 0, 0)),
        pl.BlockSpec((1, d), lambda i, t: (0, 0))]
    args = [a for a, _ in o_parts] + [gates, w_branch, w_out, x, mod, next_g.reshape(1, d)]
    out_specs = [row_spec, row_spec]
    out_shape = [jax.ShapeDtypeStruct((b, r, d), F32), jax.ShapeDtypeStruct((b, r, d), BF16)]
    if with_router:
        in_specs.append(pl.BlockSpec((d, LANES), lambda i, t: (0, 0), **whole))
        args.append(router)
        out_specs.append(pl.BlockSpec((None, tm, LANES), lambda i, t: (i, t, 0)))
        out_shape.append(jax.ShapeDtypeStruct((b, r, LANES), F32))
    return pl.pallas_call(
        functools.partial(_merge_kernel, with_router=with_router),
        grid=(b, r // tm),
        in_specs=in_specs,
        out_specs=out_specs,
        out_shape=out_shape,
        compiler_params=_cparams(("parallel", "parallel")),
        name="merge",
    )(*args)


def _swiglu_partial(h, wg, wu, wd):
    gate = _dot(h, wg)
    up = _dot(h, wu)
    act = gate * jax.nn.sigmoid(gate) * up
    return _dot(act.astype(BF16), wd)


FFN_ROW_BLOCK = 512


def _ffn_kernel(h_ref, wg_ref, wu_ref, wd_ref, x_ref, mod_ref, ng_ref, nmod_ref, xo_ref, hn_ref,
                acc_ref):
    f = pl.program_id(2)

    @pl.when(f == 0)
    def _():
        acc_ref[...] = jnp.zeros_like(acc_ref)

    for r0 in range(0, h_ref.shape[0], FFN_ROW_BLOCK):
        rows = slice(r0, r0 + FFN_ROW_BLOCK)
        acc_ref[rows, :] += _swiglu_partial(h_ref[rows, :], wg_ref[...], wu_ref[...], wd_ref[...])

    @pl.when(f == pl.num_programs(2) - 1)
    def _():
        xn = x_ref[...] + mod_ref[5:6, :] * acc_ref[...]
        xo_ref[...] = xn
        hn_ref[...] = _norm_mod(xn, ng_ref[...], nmod_ref, 0, 1).astype(hn_ref.dtype)


def _ffn_call(h, wg, wu, wd, x, mod, next_g, next_mod):
    b, r, d = x.shape
    dff = wg.shape[1]
    tf = 512
    tm = min(r, FFN_ROW_BLOCK)
    row_spec = pl.BlockSpec((None, tm, d), lambda i, t, f: (i, t, 0))
    mod_spec = pl.BlockSpec((None, 6, d), lambda i, t, f: (i, 0, 0))
    return pl.pallas_call(
        _ffn_kernel,
        grid=(b, r // tm, dff // tf),
        in_specs=[row_spec,
                  pl.BlockSpec((d, tf), lambda i, t, f: (0, f)),
                  pl.BlockSpec((d, tf), lambda i, t, f: (0, f)),
                  pl.BlockSpec((tf, d), lambda i, t, f: (f, 0)),
                  row_spec, mod_spec,
                  pl.BlockSpec((1, d), lambda i, t, f: (0, 0)),
                  mod_spec],
        out_specs=[row_spec, row_spec],
        out_shape=[jax.ShapeDtypeStruct((b, r, d), F32), jax.ShapeDtypeStruct((b, r, d), BF16)],
        scratch_shapes=[pltpu.VMEM((tm, d), F32)],
        compiler_params=_cparams(("parallel", "parallel", "arbitrary")),
        name="ffn_dense",
    )(h, wg, wu, wd, x, mod, next_g.reshape(1, d), next_mod)


def _route_kernel(lg_ref, idx_ref, w_ref):
    lg = lg_ref[...]
    lane = lax.broadcasted_iota(jnp.int32, lg.shape, 1)
    valid = lane < N_EXPERTS
    mx = jnp.max(jnp.where(valid, lg, -jnp.inf), axis=-1, keepdims=True)
    e = jnp.where(valid, jnp.exp(lg - mx), 0.0)
    p = e / jnp.sum(e, axis=-1, keepdims=True)
    p1 = jnp.max(p, axis=-1, keepdims=True)
    i1 = jnp.min(jnp.where(p == p1, lane, LANES), axis=-1, keepdims=True)
    rest = jnp.where(jnp.logical_or(lane == i1, jnp.logical_not(valid)), -1.0, p)
    p2 = jnp.max(rest, axis=-1, keepdims=True)
    i2 = jnp.min(jnp.where(rest == p2, lane, LANES), axis=-1, keepdims=True)
    tot = p1 + p2
    idx_ref[...] = jnp.where(lane == 0, i1, jnp.where(lane == 1, i2, 0))
    w_ref[...] = jnp.where(lane == 0, p1 / tot, jnp.where(lane == 1, p2 / tot, 0.0))


def _route_call(logits):
    r = logits.shape[0]
    tm = min(r, 1024)
    spec = pl.BlockSpec((tm, LANES), lambda i: (i, 0))
    return pl.pallas_call(
        _route_kernel,
        grid=(r // tm,),
        in_specs=[spec],
        out_specs=[spec, spec],
        out_shape=[jax.ShapeDtypeStruct((r, LANES), jnp.int32), jax.ShapeDtypeStruct((r, LANES), F32)],
        compiler_params=_cparams(("parallel",)),
        name="route_top2",
    )(logits)


MOE_HALF = 512
MOE_TILE = 2 * MOE_HALF
MOE_FF_TILE = 512
MOE_VMEM_LIMIT = 60 * 1024 * 1024


def _moe_kernel(te_ref, nr_ref, x_ref, wg_ref, wu_ref, wd_ref, o_ref, acc_ref):
    i = pl.program_id(0)
    f = pl.program_id(1)

    @pl.when(f == 0)
    def _():
        acc_ref[...] = jnp.zeros_like(acc_ref)

    for half in range(MOE_TILE // MOE_HALF):
        rows = slice(half * MOE_HALF, (half + 1) * MOE_HALF)

        @pl.when(nr_ref[i] > half * MOE_HALF)
        def _():
            acc_ref[rows, :] += _swiglu_partial(x_ref[rows, :], wg_ref[...].astype(BF16),
                                                wu_ref[...].astype(BF16), wd_ref[...].astype(BF16))

    @pl.when(f == pl.num_programs(1) - 1)
    def _():
        o_ref[...] = acc_ref[...].astype(o_ref.dtype)


def _moe_call(xs, wg, wu, wd, tile_expert, tile_rows):
    r, d = xs.shape
    dff = wg.shape[2]
    tf = MOE_FF_TILE
    nf = dff // tf

    def fidx(i, f, nr):
        return jnp.where(nr[i] > 0, f, nf - 1)

    grid_spec = pltpu.PrefetchScalarGridSpec(
        num_scalar_prefetch=2,
        grid=(r // MOE_TILE, nf),
        in_specs=[pl.BlockSpec((MOE_TILE, d), lambda i, f, te, nr: (i, 0)),
                  pl.BlockSpec((None, d, tf), lambda i, f, te, nr: (te[i], 0, fidx(i, f, nr))),
                  pl.BlockSpec((None, d, tf), lambda i, f, te, nr: (te[i], 0, fidx(i, f, nr))),
                  pl.BlockSpec((None, tf, d), lambda i, f, te, nr: (te[i], fidx(i, f, nr), 0))],
        out_specs=pl.BlockSpec((MOE_TILE, d), lambda i, f, te, nr: (i, 0)),
        scratch_shapes=[pltpu.VMEM((MOE_TILE, d), F32)])
    return pl.pallas_call(
        _moe_kernel,
        grid_spec=grid_spec,
        out_shape=jax.ShapeDtypeStruct((r, d), BF16),
        compiler_params=pltpu.CompilerParams(dimension_semantics=("arbitrary", "arbitrary"),
                                             vmem_limit_bytes=MOE_VMEM_LIMIT),
        name="moe_experts",
    )(tile_expert, tile_rows, xs, wg, wu, wd)


def _moe_plan(idx):
    t = idx.shape[0]
    a = t * TOP_K
    flat_e = idx.reshape(a)
    onehot = (flat_e[:, None] == jnp.arange(N_EXPERTS, dtype=jnp.int32)[None, :]).astype(jnp.int32)
    csum = jnp.cumsum(onehot, axis=0)
    rank = jnp.sum((csum - onehot) * onehot, axis=1)
    counts = csum[-1]
    padded = ((counts + MOE_TILE - 1) // MOE_TILE) * MOE_TILE
    ends = jnp.cumsum(padded)
    offs = ends - padded
    dest = offs[flat_e] + rank
    rows = a + N_EXPERTS * MOE_TILE
    src_tok = (jnp.arange(rows, dtype=jnp.int32) % t).at[dest].set(jnp.arange(a, dtype=jnp.int32) // TOP_K)
    tile_start = jnp.arange(rows // MOE_TILE, dtype=jnp.int32) * MOE_TILE
    past = jnp.sum((tile_start[:, None] >= ends[None, :]).astype(jnp.int32), axis=1)
    tile_expert = jnp.minimum(past, N_EXPERTS - 1)
    tile_rows = jnp.where(past < N_EXPERTS,
                          jnp.clip((offs + counts)[tile_expert] - tile_start, 0, MOE_TILE), 0)
    return dest.reshape(t, TOP_K), src_tok, tile_expert, tile_rows.astype(jnp.int32)


def _final_kernel(x_ref, y0_ref, y1_ref, w_ref, mod_ref, g_ref, o_ref):
    y = w_ref[:, 0:1] * y0_ref[...].astype(F32) + w_ref[:, 1:2] * y1_ref[...].astype(F32)
    xn = x_ref[...] + mod_ref[5:6, :] * y
    o_ref[...] = _rms(xn) * g_ref[...]


def _final_call(x, y0, y1, w, mod, g):
    b, r, d = x.shape
    tm = min(r, 512)
    row_spec = pl.BlockSpec((None, tm, d), lambda i, t: (i, t, 0))
    return pl.pallas_call(
        _final_kernel,
        grid=(b, r // tm),
        in_specs=[row_spec, row_spec, row_spec,
                  pl.BlockSpec((None, tm, LANES), lambda i, t: (i, t, 0)),
                  pl.BlockSpec((None, 6, d), lambda i, t: (i, 0, 0)),
                  pl.BlockSpec((1, d), lambda i, t: (0, 0))],
        out_specs=row_spec,
        out_shape=jax.ShapeDtypeStruct((b, r, d), F32),
        compiler_params=_cparams(("parallel", "parallel")),
        name="final_norm",
    )(x, y0, y1, w, mod, g.reshape(1, d))


def _rope_tables(n):
    pos = jnp.arange(n)
    rows = (pos // GRID_W).astype(F32)
    cols = (pos % GRID_W).astype(F32)
    lane = np.arange(LANES)
    out = []
    for hw in (32, 16):
        period = 4 * hw
        u = lane % period
        use_cols = (u // (2 * hw)) == 1
        w = u % (2 * hw)
        freqs = jnp.asarray(ROPE_THETA ** (-(w % hw).astype(np.float32) / hw), F32)
        p = jnp.where(jnp.asarray(use_cols)[None, :], cols[:, None], rows[:, None])
        ang = p * freqs[None, :]
        sign = jnp.asarray(np.where(w < hw, -1.0, 1.0), F32)
        out += [jnp.cos(ang), jnp.sin(ang) * sign[None, :]]
    return out


def kernel(x, c, ctx, c_ctx, attn_norm_g, ffn_norm_g, ada_w, ada_b, w_in, qk_norm_g, diff_lambda,
           diff_subln_g, na_rpb, swa_sink, w_branch, w_out, ffn_w_gate, ffn_w_up, ffn_w_down,
           moe_router, moe_w_gate, moe_w_up, moe_w_down, final_norm_g):
    b, n, d = x.shape
    l = ctx.shape[1]
    depth = w_in.shape[0]
    assert depth == 2, "laid out for one dense layer followed by one routed last layer"
    rows_n = n // GRID_W

    lat_tables = _rope_tables(n)
    ctx_flat = ctx.reshape(1, b * l, d)
    ones = jnp.ones((b * l, LANES), F32)
    zeros = jnp.zeros((b * l, LANES), F32)
    ctx_tables = [ones, zeros, ones, zeros]

    cvec = jnp.zeros((8, d), F32).at[:b].set(c).at[b].set(c_ctx)
    mods = []
    for i in range(depth):
        m = _adaln(cvec, ada_w, ada_b, i).reshape(8, 6, d)
        mods.append((m[:b], m[b:b + 1]))

    def mixers(i, h_lat, h_ctx):
        lambda_init = 0.8 - 0.6 * math.exp(-0.3 * i)
        w_qkv = _qkv_weights_call(w_in, i)
        sink_tab = jnp.broadcast_to((swa_sink[i].astype(F32) * LOG2E)[:, None], (4, LANES))
        p_lat = _qkv_call(h_lat, w_qkv, lat_tables, qk_norm_g[i])
        p_ctx = _qkv_call(h_ctx, w_qkv, ctx_tables, qk_norm_g[i]).reshape(b, l, QKV_COLS)
        o_lat = [_gqa_call(p_lat, p_ctx),
                 _diff_call(p_lat, p_ctx, diff_lambda[i], diff_subln_g[i], lambda_init),
                 _na_call(p_lat, p_ctx, _na_bias_tables(na_rpb[i])),
                 _swa_call(p_lat, p_ctx, sink_tab.reshape(2, 2, LANES))]
        return p_ctx, o_lat, sink_tab, lambda_init

    mod_lat, mod_ctx = mods[0]
    h_lat = _norm_mod_call(x, attn_norm_g[0], mod_lat, 0, 1)
    h_ctx = _norm_mod_call(ctx_flat, attn_norm_g[0], mod_ctx, 0, 1)
    p_ctx, o_lat, sink_tab, lambda_init = mixers(0, h_lat, h_ctx)
    wb = w_branch[0].astype(BF16)
    wo = w_out[0].astype(BF16)
    x_lat, h_lat = _merge_call([(o, 0) for o in o_lat], _gate_proj_call(h_lat, w_in, 0), wb, wo, x,
                               mod_lat, ffn_norm_g[0])
    o_ctx = _ctx_attn_call(p_ctx, diff_lambda[0], diff_subln_g[0], sink_tab, lambda_init)
    o_ctx = o_ctx.reshape(1, b * l, N_BRANCHES * BRANCH_WIDTH)
    x_ctx, h_ctx = _merge_call([(o_ctx, k) for k in range(N_BRANCHES)], _gate_proj_call(h_ctx, w_in, 0),
                               wb, wo, ctx_flat, mod_ctx, ffn_norm_g[0])
    wg = ffn_w_gate[0].astype(BF16)
    wu = ffn_w_up[0].astype(BF16)
    wd = ffn_w_down[0].astype(BF16)
    _, h_ctx = _ffn_call(h_ctx, wg, wu, wd, x_ctx, mod_ctx, attn_norm_g[1], mods[1][1])
    x_lat, h_lat = _ffn_call(h_lat, wg, wu, wd, x_lat, mod_lat, attn_norm_g[1], mods[1][0])

    mod_lat, _ = mods[1]
    _, o_lat, _, _ = mixers(1, h_lat, h_ctx)
    router = jnp.zeros((d, LANES), F32).at[:, :N_EXPERTS].set(moe_router[0])
    x_lat, h_lat, logits = _merge_call(
        [(o, 0) for o in o_lat], _gate_proj_call(h_lat, w_in, 1),
        w_branch[1].astype(BF16), w_out[1].astype(BF16), x_lat, mod_lat, ffn_norm_g[1], router)

    idx_pad, wts_pad = _route_call(logits.reshape(b * n, LANES))
    dest, src_tok, tile_expert, tile_rows = _moe_plan(idx_pad[:, :TOP_K])
    xs = h_lat.reshape(b * n, d).at[src_tok].get(mode="promise_in_bounds")
    ys = _moe_call(xs, moe_w_gate[0], moe_w_up[0], moe_w_down[0], tile_expert, tile_rows)
    y0 = ys.at[dest[:, 0]].get(mode="promise_in_bounds").reshape(b, n, d)
    y1 = ys.at[dest[:, 1]].get(mode="promise_in_bounds").reshape(b, n, d)
    return _final_call(x_lat, y0, y1, wts_pad.reshape(b, n, LANES), mod_lat, final_norm_g)
```

```python
import functools
import math

import numpy as np
import jax
import jax.numpy as jnp
from jax import lax
from jax.experimental import pallas as pl
from jax.experimental.pallas import tpu as pltpu

F32 = jnp.float32
BF16 = jnp.bfloat16

GRID_W = 64
HEAD_DIM = 128
N_BRANCHES = 4
BRANCH_WIDTH = 4 * HEAD_DIM
DIFF_QK_DIM = 64
NA_ROWS = 8
NA_COLS = 16
SWA_WINDOW = 128
N_EXPERTS = 8
TOP_K = 2
NORM_EPS = 1e-6
ROPE_THETA = 10000.0
NEG_INF = -1e30
LOG2E = math.log2(math.e)

QKV_COLS = 5120
COL_GQA_Q, COL_GQA_K, COL_GQA_V = 0, 512, 768
COL_DIFF_Q, COL_DIFF_K, COL_DIFF_V = 1024, 1536, 2048
COL_NA_Q, COL_NA_K, COL_NA_V = 2560, 3072, 3584
COL_SWA_Q, COL_SWA_K, COL_SWA_V = 4096, 4608, 4864

LANES = 128
VMEM_LIMIT = 56 * 1024 * 1024

QS128 = HEAD_DIM ** -0.5 * LOG2E
QS64 = DIFF_QK_DIM ** -0.5 * LOG2E


def _cparams(sem):
    return pltpu.CompilerParams(dimension_semantics=sem, vmem_limit_bytes=VMEM_LIMIT)


def _dot(a, b):
    return jnp.dot(a, b, preferred_element_type=F32)


def _dot_nt(a, b):
    return lax.dot_general(a, b, (((1,), (1,)), ((), ())), preferred_element_type=F32)


def _rms(x):
    return x * lax.rsqrt(jnp.mean(x * x, axis=-1, keepdims=True) + NORM_EPS)


def _norm_mod(x, g, mod_ref, shift_idx, scale_idx):
    y = _rms(x) * g
    return y * (1.0 + mod_ref[scale_idx:scale_idx + 1, :]) + mod_ref[shift_idx:shift_idx + 1, :]


def _adaln_kernel(c_ref, w_ref, b_ref, o_ref):
    c = c_ref[...]
    s = c * jax.nn.sigmoid(c)
    o_ref[...] = jnp.dot(s, w_ref[...], preferred_element_type=F32,
                         precision=lax.Precision.HIGHEST) + b_ref[...]


def _adaln(cvec, w, b, layer):
    rows, d = cvec.shape
    cols = w.shape[2]
    tn = 1024 if cols % 1024 == 0 else cols
    return pl.pallas_call(
        _adaln_kernel,
        grid=(cols // tn,),
        in_specs=[pl.BlockSpec((rows, d), lambda j: (0, 0)),
                  pl.BlockSpec((None, d, tn), lambda j: (layer, 0, j)),
                  pl.BlockSpec((None, 1, tn), lambda j: (layer, 0, j))],
        out_specs=pl.BlockSpec((rows, tn), lambda j: (0, j)),
        out_shape=jax.ShapeDtypeStruct((rows, cols), F32),
        compiler_params=_cparams(("arbitrary",)),
        name="adaln",
    )(cvec, w, b.reshape(b.shape[0], 1, cols))


def _norm_mod_kernel(x_ref, g_ref, mod_ref, h_ref, *, shift_idx, scale_idx):
    h_ref[...] = _norm_mod(x_ref[...], g_ref[...], mod_ref, shift_idx, scale_idx).astype(BF16)


def _norm_mod_call(x, g, mod, shift_idx, scale_idx):
    b, r, d = x.shape
    tm = min(r, 512)
    return pl.pallas_call(
        functools.partial(_norm_mod_kernel, shift_idx=shift_idx, scale_idx=scale_idx),
        grid=(b, r // tm),
        in_specs=[pl.BlockSpec((None, tm, d), lambda i, t: (i, t, 0)),
                  pl.BlockSpec((1, d), lambda i, t: (0, 0)),
                  pl.BlockSpec((None, 6, d), lambda i, t: (i, 0, 0))],
        out_specs=pl.BlockSpec((None, tm, d), lambda i, t: (i, t, 0)),
        out_shape=jax.ShapeDtypeStruct((b, r, d), BF16),
        compiler_params=_cparams(("parallel", "parallel")),
        name="norm_mod",
    )(x, g.reshape(1, d), mod)


QKV_TILE = 4 * LANES
_PLAIN = (None, None, 1.0)
_QKV_TILE_OPS = {
    0: [(0, 32, QS128)] * 4,
    1: [(1, 32, 1.0)] * 2 + [_PLAIN] * 2,
    2: [(None, 16, QS64)] * 4,
    3: [(None, 16, 1.0)] * 4,
    5: [(None, None, QS128)] * 4,
    8: [(None, 32, QS128)] * 4,
    9: [(None, 32, 1.0)] * 2 + [_PLAIN] * 2,
}


def _rope(y, cos, sin, hw):
    lane = lax.broadcasted_iota(jnp.int32, y.shape, 1)
    first = (lane % (2 * hw)) < hw
    partner = jnp.where(first, pltpu.roll(y, LANES - hw, 1), pltpu.roll(y, hw, 1))
    return y * cos + partner * sin


def _qkv_kernel(h_ref, w_ref, cos32_ref, sin32_ref, cos16_ref, sin16_ref, g_ref, o_ref):
    h = h_ref[...]
    for j in range(o_ref.shape[1] // QKV_TILE):
        acc = _dot(h, w_ref[:, j * QKV_TILE:(j + 1) * QKV_TILE])
        for c, (norm_row, hw, scale) in enumerate(_QKV_TILE_OPS.get(j, [_PLAIN] * 4)):
            y = acc[:, c * LANES:(c + 1) * LANES]
            if norm_row is not None:
                y = _rms(y) * g_ref[norm_row:norm_row + 1, :]
            if hw == 32:
                y = _rope(y, cos32_ref[...], sin32_ref[...], 32)
            elif hw == 16:
                y = _rope(y, cos16_ref[...], sin16_ref[...], 16)
            if scale != 1.0:
                y = y * scale
            col = j * QKV_TILE + c * LANES
            o_ref[:, col:col + LANES] = y.astype(o_ref.dtype)


def _qkv_call(h, w_qkv, tables, qk_g):
    b, r, d = h.shape
    cols = w_qkv.shape[1]
    tm = min(r, 512)
    tab_spec = pl.BlockSpec((tm, LANES), lambda i, t: (t, 0))
    return pl.pallas_call(
        _qkv_kernel,
        grid=(b, r // tm),
        in_specs=[pl.BlockSpec((None, tm, d), lambda i, t: (i, t, 0)),
                  pl.BlockSpec((d, cols), lambda i, t: (0, 0), pipeline_mode=pl.Buffered(1)),
                  tab_spec, tab_spec, tab_spec, tab_spec,
                  pl.BlockSpec((2, LANES), lambda i, t: (0, 0))],
        out_specs=pl.BlockSpec((None, tm, cols), lambda i, t: (i, t, 0)),
        out_shape=jax.ShapeDtypeStruct((b, r, cols), BF16),
        compiler_params=_cparams(("parallel", "parallel")),
        name="qkv_proj",
    )(h, w_qkv, *tables, qk_g)


def _matmul_kernel(h_ref, w_ref, o_ref):
    o_ref[...] = _dot(h_ref[...], w_ref[...].astype(BF16)).astype(o_ref.dtype)


def _gate_proj_call(h, w_in, layer):
    b, r, d = h.shape
    tn = 1024
    cols = w_in.shape[2] - QKV_COLS
    col0 = QKV_COLS // tn
    tm = 2048 if r % 2048 == 0 else r
    return pl.pallas_call(
        _matmul_kernel,
        grid=(b, r // tm, cols // tn),
        in_specs=[pl.BlockSpec((None, tm, d), lambda i, t, j: (i, t, 0)),
                  pl.BlockSpec((None, d, tn), lambda i, t, j: (layer, 0, col0 + j))],
        out_specs=pl.BlockSpec((None, tm, tn), lambda i, t, j: (i, t, j)),
        out_shape=jax.ShapeDtypeStruct((b, r, cols), BF16),
        compiler_params=_cparams(("parallel", "parallel", "arbitrary")),
        name="gate_proj",
    )(h, w_in)


def _cast_kernel(w_ref, o_ref):
    o_ref[...] = w_ref[...].astype(o_ref.dtype)


def _qkv_weights_call(w_in, layer):
    d = w_in.shape[1]
    return pl.pallas_call(
        _cast_kernel,
        grid=(QKV_COLS // QKV_TILE,),
        in_specs=[pl.BlockSpec((None, d, QKV_TILE), lambda j: (layer, 0, j))],
        out_specs=pl.BlockSpec((d, QKV_TILE), lambda j: (0, j)),
        out_shape=jax.ShapeDtypeStruct((d, QKV_COLS), BF16),
        compiler_params=_cparams(("parallel",)),
        name="qkv_weights",
    )(w_in)


def _diff_lambda(lam_ref, lambda_init):
    a = jnp.sum(lam_ref[0:1, :] * lam_ref[1:2, :], axis=-1, keepdims=True)
    b = jnp.sum(lam_ref[2:3, :] * lam_ref[3:4, :], axis=-1, keepdims=True)
    return jnp.exp(a) - jnp.exp(b) + lambda_init


def _split_maps(q):
    lane = lax.broadcasted_iota(jnp.int32, q.shape, 1)
    zero = jnp.zeros_like(q)
    return jnp.concatenate([jnp.where(lane < DIFF_QK_DIM, q, zero),
                            jnp.where(lane >= DIFF_QK_DIM, q, zero)], axis=0)


FLASH_ROW_BLOCK = 128
FLASH_KV_CHUNK = 512
FLASH_Q_TILE = 1024


def _flash_pair(q2, kc_ref, vc_ref, kl_ref, vl_ref, tk):
    m_rows = q2.shape[0]
    n_lat = kl_ref.shape[0]
    nb = m_rows // FLASH_ROW_BLOCK
    qs = [q2[i * FLASH_ROW_BLOCK:(i + 1) * FLASH_ROW_BLOCK] for i in range(nb)]

    def step(k, v, carry):
        v1 = jnp.concatenate([v, jnp.ones_like(v)], axis=1)
        out = []
        for q, (m, acc) in zip(qs, carry):
            s = _dot_nt(q, k)
            m_new = jnp.maximum(m, jnp.max(s, axis=-1, keepdims=True))
            p = jnp.exp2(s - m_new)
            acc = jnp.exp2(m - m_new) * acc + _dot(p.astype(BF16), v1)
            out.append((m_new, acc))
        return tuple(out)

    carry = tuple((jnp.full((FLASH_ROW_BLOCK, 1), NEG_INF, F32),
                   jnp.zeros((FLASH_ROW_BLOCK, 2 * HEAD_DIM), F32)) for _ in range(nb))
    carry = step(kc_ref[...], vc_ref[...], carry)

    for c in range(n_lat // tk):
        carry = step(kl_ref[c * tk:(c + 1) * tk, :], vl_ref[c * tk:(c + 1) * tk, :], carry)
    return jnp.concatenate([acc[:, :HEAD_DIM] / acc[:, HEAD_DIM:] for _, acc in carry], axis=0)


def _gqa_kernel(q_ref, kc_ref, vc_ref, kl_ref, vl_ref, o_ref, *, tk):
    tq = q_ref.shape[0]
    q = q_ref[...]
    q2 = jnp.concatenate([q[:, :HEAD_DIM], q[:, HEAD_DIM:]], axis=0)
    o = _flash_pair(q2, kc_ref, vc_ref, kl_ref, vl_ref, tk)
    o_ref[:, :HEAD_DIM] = o[:tq].astype(o_ref.dtype)
    o_ref[:, HEAD_DIM:] = o[tq:].astype(o_ref.dtype)


def _diff_kernel(q_ref, kc_ref, vc_ref, kl_ref, vl_ref, lam_ref, subg_ref, o_ref, *, tk, lambda_init):
    tq = q_ref.shape[0]
    o = _flash_pair(_split_maps(q_ref[...]), kc_ref, vc_ref, kl_ref, vl_ref, tk)
    lam = _diff_lambda(lam_ref, lambda_init)
    d = o[:tq] - lam * o[tq:]
    o_ref[...] = (_rms(d) * subg_ref[...] * (1.0 - lambda_init)).astype(o_ref.dtype)


def _gqa_call(p_lat, p_ctx):
    b, n, _ = p_lat.shape
    l = p_ctx.shape[1]
    tq = FLASH_Q_TILE
    tk = FLASH_KV_CHUNK
    hd = HEAD_DIM
    kcol, vcol = COL_GQA_K // hd, COL_GQA_V // hd
    return pl.pallas_call(
        functools.partial(_gqa_kernel, tk=tk),
        grid=(b, 2, n // tq),
        in_specs=[pl.BlockSpec((None, tq, 2 * hd), lambda i, k, t: (i, t, k)),
                  pl.BlockSpec((None, l, hd), lambda i, k, t: (i, 0, kcol + k)),
                  pl.BlockSpec((None, l, hd), lambda i, k, t: (i, 0, vcol + k)),
                  pl.BlockSpec((None, n, hd), lambda i, k, t: (i, 0, kcol + k)),
                  pl.BlockSpec((None, n, hd), lambda i, k, t: (i, 0, vcol + k))],
        out_specs=pl.BlockSpec((None, tq, 2 * hd), lambda i, k, t: (i, t, k)),
        out_shape=jax.ShapeDtypeStruct((b, n, BRANCH_WIDTH), BF16),
        compiler_params=_cparams(("parallel", "parallel", "arbitrary")),
        name="attn_gqa",
    )(p_lat, p_ctx, p_ctx, p_lat, p_lat)


def _diff_call(p_lat, p_ctx, lam_p, subg, lambda_init):
    b, n, _ = p_lat.shape
    l = p_ctx.shape[1]
    tq = FLASH_Q_TILE
    tk = FLASH_KV_CHUNK
    hd = HEAD_DIM
    qcol, kcol, vcol = COL_DIFF_Q // hd, COL_DIFF_K // hd, COL_DIFF_V // hd
    return pl.pallas_call(
        functools.partial(_diff_kernel, tk=tk, lambda_init=lambda_init),
        grid=(b, 4, n // tq),
        in_specs=[pl.BlockSpec((None, tq, hd), lambda i, h, t: (i, t, qcol + h)),
                  pl.BlockSpec((None, l, hd), lambda i, h, t: (i, 0, kcol + h)),
                  pl.BlockSpec((None, l, hd), lambda i, h, t: (i, 0, vcol + h)),
                  pl.BlockSpec((None, n, hd), lambda i, h, t: (i, 0, kcol + h)),
                  pl.BlockSpec((None, n, hd), lambda i, h, t: (i, 0, vcol + h)),
                  pl.BlockSpec((4, DIFF_QK_DIM), lambda i, h, t: (0, 0)),
                  pl.BlockSpec((1, hd), lambda i, h, t: (0, 0))],
        out_specs=pl.BlockSpec((None, tq, hd), lambda i, h, t: (i, t, h)),
        out_shape=jax.ShapeDtypeStruct((b, n, BRANCH_WIDTH), BF16),
        compiler_params=_cparams(("parallel", "parallel", "arbitrary")),
        name="attn_diff",
    )(p_lat, p_ctx, p_ctx, p_lat, p_lat, lam_p, subg.reshape(1, hd))


NA_GROUP_ROWS = 16
NA_BLOCK_ROWS = 4
NA_BLOCK_WIN_ROWS = NA_BLOCK_ROWS + NA_ROWS


def _with_ones(v):
    return jnp.concatenate([v, jnp.ones_like(v)], axis=1)


def _band_ctx_attend(s_b, s_c, vb1, vc1, sink=None):
    m = jnp.maximum(jnp.max(s_b, axis=-1, keepdims=True), jnp.max(s_c, axis=-1, keepdims=True))
    if sink is not None:
        m = jnp.maximum(m, sink)
    acc = _dot(jnp.exp2(s_c - m).astype(BF16), vc1) + _dot(jnp.exp2(s_b - m).astype(BF16), vb1)
    den = acc[:, HEAD_DIM:]
    if sink is not None:
        den = den + jnp.exp2(sink - m)
    return acc[:, :HEAD_DIM] / den


NA_DI = 2 * NA_ROWS - 1
NA_TAB_PAD = 2 * NA_ROWS
NA_TAB_LANES = 3072


def _na_bias_tables(rpb):
    nh = rpb.shape[0]
    qc = np.arange(GRID_W)[:, None]
    kc = np.arange(GRID_W)[None, :]
    cstart = np.clip(qc - NA_COLS // 2, 0, GRID_W - NA_COLS)
    ok_col = (kc >= cstart) & (kc < cstart + NA_COLS)
    dj = np.clip(kc - qc + NA_COLS - 1, 0, 2 * NA_COLS - 2)
    sel_j = (dj[..., None] == np.arange(2 * NA_COLS - 1)) & ok_col[..., None]
    by_col = jnp.einsum("hij,qkj->hqik", rpb.astype(F32), jnp.asarray(sel_j, F32),
                        precision=lax.Precision.HIGHEST)
    by_col = jnp.where(ok_col[None, :, None, :], by_col * LOG2E, NEG_INF)
    strip = by_col.reshape(nh, GRID_W, NA_DI * GRID_W)
    total = NA_TAB_LANES + GRID_W
    strip = jnp.pad(strip, ((0, 0), (0, 0), (NA_TAB_PAD * GRID_W, total - (NA_TAB_PAD + NA_DI) * GRID_W)),
                    constant_values=NEG_INF)
    return jnp.stack([strip[:, :, :NA_TAB_LANES], strip[:, :, GRID_W:]], axis=1)


def _na_kernel(q_ref, kc_ref, vc_ref, kl_ref, vl_ref, tab_ref, o_ref, *, rows_n):
    g = pl.program_id(2)
    win = NA_BLOCK_WIN_ROWS * GRID_W
    blk_q = NA_BLOCK_ROWS * GRID_W
    kc = kc_ref[...]
    vc1 = _with_ones(vc_ref[...])
    lane = lax.broadcasted_iota(jnp.int32, (GRID_W, win), 1)
    for rb in range(NA_GROUP_ROWS // NA_BLOCK_ROWS):
        r0 = g * NA_GROUP_ROWS + rb * NA_BLOCK_ROWS
        ws = jnp.clip(r0 - NA_ROWS // 2, 0, rows_n - NA_BLOCK_WIN_ROWS)
        off = pl.multiple_of(ws * GRID_W, NA_BLOCK_ROWS * GRID_W)
        bias = []
        for a in range(NA_BLOCK_ROWS):
            r = r0 + a
            rs = jnp.clip(r - NA_ROWS // 2, 0, rows_n - NA_ROWS)
            blk = ws - r + NA_ROWS - 1 + NA_TAB_PAD
            start = pl.multiple_of((blk >> 1) * (2 * GRID_W), 2 * GRID_W)
            strip = tab_ref[blk & 1, :, pl.ds(start, win)]
            lo = (rs - ws) * GRID_W
            in_rows = jnp.logical_and(lane >= lo, lane < lo + NA_ROWS * GRID_W)
            bias.append(jnp.where(in_rows, strip, NEG_INF))
        q = q_ref[rb * blk_q:(rb + 1) * blk_q, :]
        s_b = _dot_nt(q, kl_ref[pl.ds(off, win), :]) + jnp.concatenate(bias, axis=0)
        o = _band_ctx_attend(s_b, _dot_nt(q, kc), _with_ones(vl_ref[pl.ds(off, win), :]), vc1)
        o_ref[rb * blk_q:(rb + 1) * blk_q, :] = o.astype(o_ref.dtype)


def _na_call(p_lat, p_ctx, bias):
    b, n, _ = p_lat.shape
    l = p_ctx.shape[1]
    hd = HEAD_DIM
    rows_n = n // GRID_W
    n_groups = rows_n // NA_GROUP_ROWS
    tq = NA_GROUP_ROWS * GRID_W
    qcol, kcol, vcol = COL_NA_Q // hd, COL_NA_K // hd, COL_NA_V // hd

    return pl.pallas_call(
        functools.partial(_na_kernel, rows_n=rows_n),
        grid=(b, 4, n_groups),
        in_specs=[pl.BlockSpec((None, tq, hd), lambda i, h, g: (i, g, qcol + h)),
                  pl.BlockSpec((None, l, hd), lambda i, h, g: (i, 0, kcol + h)),
                  pl.BlockSpec((None, l, hd), lambda i, h, g: (i, 0, vcol + h)),
                  pl.BlockSpec((None, n, hd), lambda i, h, g: (i, 0, kcol + h)),
                  pl.BlockSpec((None, n, hd), lambda i, h, g: (i, 0, vcol + h)),
                  pl.BlockSpec((None, 2, GRID_W, NA_TAB_LANES), lambda i, h, g: (h, 0, 0, 0))],
        out_specs=pl.BlockSpec((None, tq, hd), lambda i, h, g: (i, g, h)),
        out_shape=jax.ShapeDtypeStruct((b, n, BRANCH_WIDTH), BF16),
        compiler_params=_cparams(("parallel", "parallel", "arbitrary")),
        name="attn_na",
    )(p_lat, p_ctx, p_ctx, p_lat, p_lat, bias)


SWA_BLOCK = 256


def _swa_kernel(q_ref, kc_ref, vc_ref, kl_ref, vl_ref, sink_ref, o_ref, *, win):
    t = pl.program_id(2)
    tq = q_ref.shape[0]
    n = kl_ref.shape[0]
    kc = kc_ref[...]
    vc1 = _with_ones(vc_ref[...])
    row = lax.broadcasted_iota(jnp.int32, (SWA_BLOCK, win), 0)
    col = lax.broadcasted_iota(jnp.int32, (SWA_BLOCK, win), 1)
    for rb in range(tq // SWA_BLOCK):
        qs = t * tq + rb * SWA_BLOCK
        ws = pl.multiple_of(jnp.clip(qs - SWA_WINDOW, 0, n - win), SWA_WINDOW)
        kb = kl_ref[pl.ds(ws, win), :]
        vb1 = _with_ones(vl_ref[pl.ds(ws, win), :])
        valid = jnp.abs(row + (qs - ws) - col) <= SWA_WINDOW
        rows = slice(rb * SWA_BLOCK, (rb + 1) * SWA_BLOCK)
        for gi in range(2):
            q = q_ref[rows, gi * HEAD_DIM:(gi + 1) * HEAD_DIM]
            s_b = jnp.where(valid, _dot_nt(q, kb), NEG_INF)
            o = _band_ctx_attend(s_b, _dot_nt(q, kc), vb1, vc1, sink=sink_ref[gi:gi + 1, 0:1])
            o_ref[rows, gi * HEAD_DIM:(gi + 1) * HEAD_DIM] = o.astype(o_ref.dtype)


def _swa_call(p_lat, p_ctx, sink_tab):
    b, n, _ = p_lat.shape
    l = p_ctx.shape[1]
    hd = HEAD_DIM
    tq = 4 * SWA_BLOCK
    win = SWA_BLOCK + 2 * SWA_WINDOW
    qcol, kcol, vcol = COL_SWA_Q // (2 * hd), COL_SWA_K // hd, COL_SWA_V // hd
    return pl.pallas_call(
        functools.partial(_swa_kernel, win=win),
        grid=(b, 2, n // tq),
        in_specs=[pl.BlockSpec((None, tq, 2 * hd), lambda i, k, t: (i, t, qcol + k)),
                  pl.BlockSpec((None, l, hd), lambda i, k, t: (i, 0, kcol + k)),
                  pl.BlockSpec((None, l, hd), lambda i, k, t: (i, 0, vcol + k)),
                  pl.BlockSpec((None, n, hd), lambda i, k, t: (i, 0, kcol + k)),
                  pl.BlockSpec((None, n, hd), lambda i, k, t: (i, 0, vcol + k)),
                  pl.BlockSpec((None, 2, LANES), lambda i, k, t: (k, 0, 0))],
        out_specs=pl.BlockSpec((None, tq, 2 * hd), lambda i, k, t: (i, t, k)),
        out_shape=jax.ShapeDtypeStruct((b, n, BRANCH_WIDTH), BF16),
        compiler_params=_cparams(("parallel", "parallel", "arbitrary")),
        name="attn_swa",
    )(p_lat, p_ctx, p_ctx, p_lat, p_lat, sink_tab)


def _softmax_attend(q, k, v, sink=None):
    s = _dot_nt(q, k)
    m = jnp.max(s, axis=-1, keepdims=True)
    if sink is not None:
        m = jnp.maximum(m, sink)
    e = jnp.exp2(s - m)
    den = jnp.sum(e, axis=-1, keepdims=True)
    if sink is not None:
        den = den + jnp.exp2(sink - m)
    return _dot(e.astype(BF16), v) / den


def _ctx_attn_kernel(p_ref, lam_ref, subg_ref, sink_ref, o_ref, *, lambda_init):
    hd = HEAD_DIM

    def col(c0, h):
        return p_ref[:, c0 + h * hd:c0 + (h + 1) * hd]

    lam = _diff_lambda(lam_ref, lambda_init)
    for h in range(4):
        o = _softmax_attend(col(COL_GQA_Q, h), col(COL_GQA_K, h // 2), col(COL_GQA_V, h // 2))
        o_ref[:, h * hd:(h + 1) * hd] = o.astype(o_ref.dtype)
    for h in range(4):
        q = col(COL_DIFF_Q, h)
        tq = q.shape[0]
        o2 = _softmax_attend(_split_maps(q), col(COL_DIFF_K, h), col(COL_DIFF_V, h))
        d = o2[:tq] - lam * o2[tq:]
        d = _rms(d) * subg_ref[...] * (1.0 - lambda_init)
        o_ref[:, BRANCH_WIDTH + h * hd:BRANCH_WIDTH + (h + 1) * hd] = d.astype(o_ref.dtype)
    for h in range(4):
        o = _softmax_attend(col(COL_NA_Q, h), col(COL_NA_K, h), col(COL_NA_V, h))
        o_ref[:, 2 * BRANCH_WIDTH + h * hd:2 * BRANCH_WIDTH + (h + 1) * hd] = o.astype(o_ref.dtype)
    for h in range(4):
        o = _softmax_attend(col(COL_SWA_Q, h), col(COL_SWA_K, h // 2), col(COL_SWA_V, h // 2),
                            sink=sink_ref[h:h + 1, 0:1])
        o_ref[:, 3 * BRANCH_WIDTH + h * hd:3 * BRANCH_WIDTH + (h + 1) * hd] = o.astype(o_ref.dtype)


def _ctx_attn_call(p_ctx, lam_p, subg, sink_tab, lambda_init):
    b, l, cols = p_ctx.shape
    return pl.pallas_call(
        functools.partial(_ctx_attn_kernel, lambda_init=lambda_init),
        grid=(b,),
        in_specs=[pl.BlockSpec((None, l, cols), lambda i: (i, 0, 0)),
                  pl.BlockSpec((4, DIFF_QK_DIM), lambda i: (0, 0)),
                  pl.BlockSpec((1, HEAD_DIM), lambda i: (0, 0)),
                  pl.BlockSpec((4, LANES), lambda i: (0, 0))],
        out_specs=pl.BlockSpec((None, l, N_BRANCHES * BRANCH_WIDTH), lambda i: (i, 0, 0)),
        out_shape=jax.ShapeDtypeStruct((b, l, N_BRANCHES * BRANCH_WIDTH), BF16),
        compiler_params=_cparams(("parallel",)),
        name="attn_ctx",
    )(p_ctx, lam_p, subg.reshape(1, HEAD_DIM), sink_tab.reshape(4, LANES))


MERGE_CHUNKS = 1


def _merge_kernel(o0, o1, o2, o3, g_ref, wb_ref, wo_ref, x_ref, mod_ref, ng_ref, *rest, with_router):
    if with_router:
        router_ref, xo_ref, h_ref, lg_ref = rest
    else:
        xo_ref, h_ref = rest
    d = x_ref.shape[1]
    dc = d // MERGE_CHUNKS
    o_vals = [o[...] for o in (o0, o1, o2, o3)]
    z = None
    for c in range(MERGE_CHUNKS):
        s = None
        for i, o in enumerate(o_vals):
            y = _dot(o, wb_ref[i, :, c * dc:(c + 1) * dc])
            gate = g_ref[:, i * d + c * dc:i * d + (c + 1) * dc].astype(F32)
            term = jax.nn.sigmoid(gate) * y
            s = term if s is None else s + term
        zc = _dot(s.astype(BF16), wo_ref[c * dc:(c + 1) * dc, :])
        z = zc if z is None else z + zc
    xn = x_ref[...] + mod_ref[2:3, :] * z
    xo_ref[...] = xn
    hn = _norm_mod(xn, ng_ref[...], mod_ref, 3, 4)
    h_ref[...] = hn.astype(h_ref.dtype)
    if with_router:
        lg_ref[...] = _dot_split(hn, router_ref[...])


def _dot_split(a, b):
    a_hi = a.astype(BF16)
    a_lo = (a - a_hi.astype(F32)).astype(BF16)
    b_hi = b.astype(BF16)
    b_lo = (b - b_hi.astype(F32)).astype(BF16)
    n = b.shape[1]
    both = _dot(a_hi, jnp.concatenate([b_hi, b_lo], axis=1))
    return both[:, :n] + both[:, n:] + _dot(a_lo, b_hi)


def _merge_call(o_parts, gates, w_branch, w_out, x, mod, next_g, router=None):
    b, r, d = x.shape
    tm = min(r, 256)
    with_router = router is not None
    whole = dict(pipeline_mode=pl.Buffered(1))

    def o_spec(cb):
        return pl.BlockSpec((None, tm, BRANCH_WIDTH), lambda i, t: (i, t, cb))

    row_spec = pl.BlockSpec((None, tm, d), lambda i, t: (i, t, 0))
    in_specs = [o_spec(cb) for _, cb in o_parts] + [
        pl.BlockSpec((None, tm, N_BRANCHES * d), lambda i, t: (i, t, 0)),
        pl.BlockSpec((N_BRANCHES, BRANCH_WIDTH, d), lambda i, t: (0, 0, 0), **whole),
        pl.BlockSpec((d, d), lambda i, t: (0, 0), **whole),
        row_spec,
        pl.BlockSpec((None, 6, d), lambda i, t: (i, 0, 0)),
        pl.BlockSpec((1, d), lambda i, t: (0, 0))]
    args = [a for a, _ in o_parts] + [gates, w_branch, w_out, x, mod, next_g.reshape(1, d)]
    out_specs = [row_spec, row_spec]
    out_shape = [jax.ShapeDtypeStruct((b, r, d), F32), jax.ShapeDtypeStruct((b, r, d), BF16)]
    if with_router:
        in_specs.append(pl.BlockSpec((d, LANES), lambda i, t: (0, 0), **whole))
        args.append(router)
        out_specs.append(pl.BlockSpec((None, tm, LANES), lambda i, t: (i, t, 0)))
        out_shape.append(jax.ShapeDtypeStruct((b, r, LANES), F32))
    return pl.pallas_call(
        functools.partial(_merge_kernel, with_router=with_router),
        grid=(b, r // tm),
        in_specs=in_specs,
        out_specs=out_specs,
        out_shape=out_shape,
        compiler_params=_cparams(("parallel", "parallel")),
        name="merge",
    )(*args)


def _swiglu_partial(h, wg, wu, wd):
    gate = _dot(h, wg)
    up = _dot(h, wu)
    act = gate * jax.nn.sigmoid(gate) * up
    return _dot(act.astype(BF16), wd)


FFN_ROW_BLOCK = 512


def _ffn_kernel(h_ref, wg_ref, wu_ref, wd_ref, x_ref, mod_ref, ng_ref, nmod_ref, xo_ref, hn_ref,
                acc_ref):
    f = pl.program_id(2)

    @pl.when(f == 0)
    def _():
        acc_ref[...] = jnp.zeros_like(acc_ref)

    for r0 in range(0, h_ref.shape[0], FFN_ROW_BLOCK):
        rows = slice(r0, r0 + FFN_ROW_BLOCK)
        acc_ref[rows, :] += _swiglu_partial(h_ref[rows, :], wg_ref[...], wu_ref[...], wd_ref[...])

    @pl.when(f == pl.num_programs(2) - 1)
    def _():
        xn = x_ref[...] + mod_ref[5:6, :] * acc_ref[...]
        xo_ref[...] = xn
        hn_ref[...] = _norm_mod(xn, ng_ref[...], nmod_ref, 0, 1).astype(hn_ref.dtype)


def _ffn_call(h, wg, wu, wd, x, mod, next_g, next_mod):
    b, r, d = x.shape
    dff = wg.shape[1]
    tf = 512
    tm = min(r, FFN_ROW_BLOCK)
    row_spec = pl.BlockSpec((None, tm, d), lambda i, t, f: (i, t, 0))
    mod_spec = pl.BlockSpec((None, 6, d), lambda i, t, f: (i, 0, 0))
    return pl.pallas_call(
        _ffn_kernel,
        grid=(b, r // tm, dff // tf),
        in_specs=[row_spec,
                  pl.BlockSpec((d, tf), lambda i, t, f: (0, f)),
                  pl.BlockSpec((d, tf), lambda i, t, f: (0, f)),
                  pl.BlockSpec((tf, d), lambda i, t, f: (f, 0)),
                  row_spec, mod_spec,
                  pl.BlockSpec((1, d), lambda i, t, f: (0, 0)),
                  mod_spec],
        out_specs=[row_spec, row_spec],
        out_shape=[jax.ShapeDtypeStruct((b, r, d), F32), jax.ShapeDtypeStruct((b, r, d), BF16)],
        scratch_shapes=[pltpu.VMEM((tm, d), F32)],
        compiler_params=_cparams(("parallel", "parallel", "arbitrary")),
        name="ffn_dense",
    )(h, wg, wu, wd, x, mod, next_g.reshape(1, d), next_mod)


def _route_kernel(lg_ref, idx_ref, w_ref):
    lg = lg_ref[...]
    lane = lax.broadcasted_iota(jnp.int32, lg.shape, 1)
    valid = lane < N_EXPERTS
    mx = jnp.max(jnp.where(valid, lg, -jnp.inf), axis=-1, keepdims=True)
    e = jnp.where(valid, jnp.exp(lg - mx), 0.0)
    p = e / jnp.sum(e, axis=-1, keepdims=True)
    p1 = jnp.max(p, axis=-1, keepdims=True)
    i1 = jnp.min(jnp.where(p == p1, lane, LANES), axis=-1, keepdims=True)
    rest = jnp.where(jnp.logical_or(lane == i1, jnp.logical_not(valid)), -1.0, p)
    p2 = jnp.max(rest, axis=-1, keepdims=True)
    i2 = jnp.min(jnp.where(rest == p2, lane, LANES), axis=-1, keepdims=True)
    tot = p1 + p2
    idx_ref[...] = jnp.where(lane == 0, i1, jnp.where(lane == 1, i2, 0))
    w_ref[...] = jnp.where(lane == 0, p1 / tot, jnp.where(lane == 1, p2 / tot, 0.0))


def _route_call(logits):
    r = logits.shape[0]
    tm = min(r, 1024)
    spec = pl.BlockSpec((tm, LANES), lambda i: (i, 0))
    return pl.pallas_call(
        _route_kernel,
        grid=(r // tm,),
        in_specs=[spec],
        out_specs=[spec, spec],
        out_shape=[jax.ShapeDtypeStruct((r, LANES), jnp.int32), jax.ShapeDtypeStruct((r, LANES), F32)],
        compiler_params=_cparams(("parallel",)),
        name="route_top2",
    )(logits)


MOE_HALF = 512
MOE_TILE = 2 * MOE_HALF
MOE_FF_TILE = 512
MOE_VMEM_LIMIT = 60 * 1024 * 1024


def _moe_kernel(te_ref, nr_ref, x_ref, wg_ref, wu_ref, wd_ref, o_ref, acc_ref):
    i = pl.program_id(0)
    f = pl.program_id(1)

    @pl.when(f == 0)
    def _():
        acc_ref[...] = jnp.zeros_like(acc_ref)

    def accumulate(n_rows):
        wg = wg_ref[...].astype(BF16)
        wu = wu_ref[...].astype(BF16)
        wd = wd_ref[...].astype(BF16)
        for r0 in range(0, n_rows, MOE_HALF):
            rows = slice(r0, r0 + MOE_HALF)
            acc_ref[rows, :] += _swiglu_partial(x_ref[rows, :], wg, wu, wd)

    n_valid = nr_ref[i]
    pl.when(n_valid > MOE_HALF)(functools.partial(accumulate, MOE_TILE))
    pl.when(jnp.logical_and(n_valid > 0, n_valid <= MOE_HALF))(functools.partial(accumulate, MOE_HALF))

    @pl.when(f == pl.num_programs(1) - 1)
    def _():
        o_ref[...] = acc_ref[...].astype(o_ref.dtype)


def _moe_call(xs, wg, wu, wd, tile_expert, tile_rows):
    r, d = xs.shape
    dff = wg.shape[2]
    tf = MOE_FF_TILE
    nf = dff // tf

    def fidx(i, f, nr):
        return jnp.where(nr[i] > 0, f, nf - 1)

    grid_spec = pltpu.PrefetchScalarGridSpec(
        num_scalar_prefetch=2,
        grid=(r // MOE_TILE, nf),
        in_specs=[pl.BlockSpec((MOE_TILE, d), lambda i, f, te, nr: (i, 0)),
                  pl.BlockSpec((None, d, tf), lambda i, f, te, nr: (te[i], 0, fidx(i, f, nr))),
                  pl.BlockSpec((None, d, tf), lambda i, f, te, nr: (te[i], 0, fidx(i, f, nr))),
                  pl.BlockSpec((None, tf, d), lambda i, f, te, nr: (te[i], fidx(i, f, nr), 0))],
        out_specs=pl.BlockSpec((MOE_TILE, d), lambda i, f, te, nr: (i, 0)),
        scratch_shapes=[pltpu.VMEM((MOE_TILE, d), F32)])
    return pl.pallas_call(
        _moe_kernel,
        grid_spec=grid_spec,
        out_shape=jax.ShapeDtypeStruct((r, d), BF16),
        compiler_params=pltpu.CompilerParams(dimension_semantics=("arbitrary", "arbitrary"),
                                             vmem_limit_bytes=MOE_VMEM_LIMIT),
        name="moe_experts",
    )(tile_expert, tile_rows, xs, wg, wu, wd)


def _moe_plan(idx):
    t = idx.shape[0]
    a = t * TOP_K
    flat_e = idx.reshape(a)
    onehot = (flat_e[:, None] == jnp.arange(N_EXPERTS, dtype=jnp.int32)[None, :]).astype(jnp.int32)
    csum = jnp.cumsum(onehot, axis=0)
    rank = jnp.sum((csum - onehot) * onehot, axis=1)
    counts = csum[-1]
    padded = ((counts + MOE_TILE - 1) // MOE_TILE) * MOE_TILE
    ends = jnp.cumsum(padded)
    offs = ends - padded
    dest = offs[flat_e] + rank
    rows = a + N_EXPERTS * MOE_TILE
    src_tok = (jnp.arange(rows, dtype=jnp.int32) % t).at[dest].set(jnp.arange(a, dtype=jnp.int32) // TOP_K)
    tile_start = jnp.arange(rows // MOE_TILE, dtype=jnp.int32) * MOE_TILE
    past = jnp.sum((tile_start[:, None] >= ends[None, :]).astype(jnp.int32), axis=1)
    tile_expert = jnp.minimum(past, N_EXPERTS - 1)
    tile_rows = jnp.where(past < N_EXPERTS,
                          jnp.clip((offs + counts)[tile_expert] - tile_start, 0, MOE_TILE), 0)
    return dest.reshape(t, TOP_K), src_tok, tile_expert, tile_rows.astype(jnp.int32)


def _final_kernel(x_ref, y0_ref, y1_ref, w_ref, mod_ref, g_ref, o_ref):
    y = w_ref[:, 0:1] * y0_ref[...].astype(F32) + w_ref[:, 1:2] * y1_ref[...].astype(F32)
    xn = x_ref[...] + mod_ref[5:6, :] * y
    o_ref[...] = _rms(xn) * g_ref[...]


def _final_call(x, y0, y1, w, mod, g):
    b, r, d = x.shape
    tm = min(r, 512)
    row_spec = pl.BlockSpec((None, tm, d), lambda i, t: (i, t, 0))
    return pl.pallas_call(
        _final_kernel,
        grid=(b, r // tm),
        in_specs=[row_spec, row_spec, row_spec,
                  pl.BlockSpec((None, tm, LANES), lambda i, t: (i, t, 0)),
                  pl.BlockSpec((None, 6, d), lambda i, t: (i, 0, 0)),
                  pl.BlockSpec((1, d), lambda i, t: (0, 0))],
        out_specs=row_spec,
        out_shape=jax.ShapeDtypeStruct((b, r, d), F32),
        compiler_params=_cparams(("parallel", "parallel")),
        name="final_norm",
    )(x, y0, y1, w, mod, g.reshape(1, d))


def _rope_tables(n):
    pos = jnp.arange(n)
    rows = (pos // GRID_W).astype(F32)
    cols = (pos % GRID_W).astype(F32)
    lane = np.arange(LANES)
    out = []
    for hw in (32, 16):
        period = 4 * hw
        u = lane % period
        use_cols = (u // (2 * hw)) == 1
        w = u % (2 * hw)
        freqs = jnp.asarray(ROPE_THETA ** (-(w % hw).astype(np.float32) / hw), F32)
        p = jnp.where(jnp.asarray(use_cols)[None, :], cols[:, None], rows[:, None])
        ang = p * freqs[None, :]
        sign = jnp.asarray(np.where(w < hw, -1.0, 1.0), F32)
        out += [jnp.cos(ang), jnp.sin(ang) * sign[None, :]]
    return out


def kernel(x, c, ctx, c_ctx, attn_norm_g, ffn_norm_g, ada_w, ada_b, w_in, qk_norm_g, diff_lambda,
           diff_subln_g, na_rpb, swa_sink, w_branch, w_out, ffn_w_gate, ffn_w_up, ffn_w_down,
           moe_router, moe_w_gate, moe_w_up, moe_w_down, final_norm_g):
    b, n, d = x.shape
    l = ctx.shape[1]
    depth = w_in.shape[0]
    assert depth == 2, "laid out for one dense layer followed by one routed last layer"
    rows_n = n // GRID_W

    lat_tables = _rope_tables(n)
    ctx_flat = ctx.reshape(1, b * l, d)
    ones = jnp.ones((b * l, LANES), F32)
    zeros = jnp.zeros((b * l, LANES), F32)
    ctx_tables = [ones, zeros, ones, zeros]

    cvec = jnp.zeros((8, d), F32).at[:b].set(c).at[b].set(c_ctx)
    mods = []
    for i in range(depth):
        m = _adaln(cvec, ada_w, ada_b, i).reshape(8, 6, d)
        mods.append((m[:b], m[b:b + 1]))

    def mixers(i, h_lat, h_ctx):
        lambda_init = 0.8 - 0.6 * math.exp(-0.3 * i)
        w_qkv = _qkv_weights_call(w_in, i)
        sink_tab = jnp.broadcast_to((swa_sink[i].astype(F32) * LOG2E)[:, None], (4, LANES))
        p_lat = _qkv_call(h_lat, w_qkv, lat_tables, qk_norm_g[i])
        p_ctx = _qkv_call(h_ctx, w_qkv, ctx_tables, qk_norm_g[i]).reshape(b, l, QKV_COLS)
        o_lat = [_gqa_call(p_lat, p_ctx),
                 _diff_call(p_lat, p_ctx, diff_lambda[i], diff_subln_g[i], lambda_init),
                 _na_call(p_lat, p_ctx, _na_bias_tables(na_rpb[i])),
                 _swa_call(p_lat, p_ctx, sink_tab.reshape(2, 2, LANES))]
        return p_ctx, o_lat, sink_tab, lambda_init

    mod_lat, mod_ctx = mods[0]
    h_lat = _norm_mod_call(x, attn_norm_g[0], mod_lat, 0, 1)
    h_ctx = _norm_mod_call(ctx_flat, attn_norm_g[0], mod_ctx, 0, 1)
    p_ctx, o_lat, sink_tab, lambda_init = mixers(0, h_lat, h_ctx)
    wb = w_branch[0].astype(BF16)
    wo = w_out[0].astype(BF16)
    x_lat, h_lat = _merge_call([(o, 0) for o in o_lat], _gate_proj_call(h_lat, w_in, 0), wb, wo, x,
                               mod_lat, ffn_norm_g[0])
    o_ctx = _ctx_attn_call(p_ctx, diff_lambda[0], diff_subln_g[0], sink_tab, lambda_init)
    o_ctx = o_ctx.reshape(1, b * l, N_BRANCHES * BRANCH_WIDTH)
    x_ctx, h_ctx = _merge_call([(o_ctx, k) for k in range(N_BRANCHES)], _gate_proj_call(h_ctx, w_in, 0),
                               wb, wo, ctx_flat, mod_ctx, ffn_norm_g[0])
    wg = ffn_w_gate[0].astype(BF16)
    wu = ffn_w_up[0].astype(BF16)
    wd = ffn_w_down[0].astype(BF16)
    _, h_ctx = _ffn_call(h_ctx, wg, wu, wd, x_ctx, mod_ctx, attn_norm_g[1], mods[1][1])
    x_lat, h_lat = _ffn_call(h_lat, wg, wu, wd, x_lat, mod_lat, attn_norm_g[1], mods[1][0])

    mod_lat, _ = mods[1]
    _, o_lat, _, _ = mixers(1, h_lat, h_ctx)
    router = jnp.zeros((d, LANES), F32).at[:, :N_EXPERTS].set(moe_router[0])
    x_lat, h_lat, logits = _merge_call(
        [(o, 0) for o in o_lat], _gate_proj_call(h_lat, w_in, 1),
        w_branch[1].astype(BF16), w_out[1].astype(BF16), x_lat, mod_lat, ffn_norm_g[1], router)

    idx_pad, wts_pad = _route_call(logits.reshape(b * n, LANES))
    dest, src_tok, tile_expert, tile_rows = _moe_plan(idx_pad[:, :TOP_K])
    xs = h_lat.reshape(b * n, d).at[src_tok].get(mode="promise_in_bounds")
    ys = _moe_call(xs, moe_w_gate[0], moe_w_up[0], moe_w_down[0], tile_expert, tile_rows)
    y0 = ys.at[dest[:, 0]].get(mode="promise_in_bounds").reshape(b, n, d)
    y1 = ys.at[dest[:, 1]].get(mode="promise_in_bounds").reshape(b, n, d)
    return _final_call(x_lat, y0, y1, wts_pad.reshape(b, n, LANES), mod_lat, final_norm_g)
```

```python
import functools
import math

import numpy as np
import jax
import jax.numpy as jnp
from jax import lax
from jax.experimental import pallas as pl
from jax.experimental.pallas import tpu as pltpu

F32 = jnp.float32
BF16 = jnp.bfloat16

GRID_W = 64
HEAD_DIM = 128
N_BRANCHES = 4
BRANCH_WIDTH = 4 * HEAD_DIM
DIFF_QK_DIM = 64
NA_ROWS = 8
NA_COLS = 16
SWA_WINDOW = 128
N_EXPERTS = 8
TOP_K = 2
NORM_EPS = 1e-6
ROPE_THETA = 10000.0
NEG_INF = -1e30
LOG2E = math.log2(math.e)

QKV_COLS = 5120
COL_GQA_Q, COL_GQA_K, COL_GQA_V = 0, 512, 768
COL_DIFF_Q, COL_DIFF_K, COL_DIFF_V = 1024, 1536, 2048
COL_NA_Q, COL_NA_K, COL_NA_V = 2560, 3072, 3584
COL_SWA_Q, COL_SWA_K, COL_SWA_V = 4096, 4608, 4864

LANES = 128
VMEM_LIMIT = 56 * 1024 * 1024

QS128 = HEAD_DIM ** -0.5 * LOG2E
QS64 = DIFF_QK_DIM ** -0.5 * LOG2E


def _cparams(sem):
    return pltpu.CompilerParams(dimension_semantics=sem, vmem_limit_bytes=VMEM_LIMIT)


def _dot(a, b):
    return jnp.dot(a, b, preferred_element_type=F32)


def _dot_nt(a, b):
    return lax.dot_general(a, b, (((1,), (1,)), ((), ())), preferred_element_type=F32)


def _rms(x):
    return x * lax.rsqrt(jnp.mean(x * x, axis=-1, keepdims=True) + NORM_EPS)


def _norm_mod(x, g, mod_ref, shift_idx, scale_idx):
    y = _rms(x) * g
    return y * (1.0 + mod_ref[scale_idx:scale_idx + 1, :]) + mod_ref[shift_idx:shift_idx + 1, :]


def _adaln_kernel(c_ref, w_ref, b_ref, o_ref):
    c = c_ref[...]
    s = c * jax.nn.sigmoid(c)
    o_ref[...] = jnp.dot(s, w_ref[...], preferred_element_type=F32,
                         precision=lax.Precision.HIGHEST) + b_ref[...]


def _adaln(cvec, w, b, layer):
    rows, d = cvec.shape
    cols = w.shape[2]
    tn = 1024 if cols % 1024 == 0 else cols
    return pl.pallas_call(
        _adaln_kernel,
        grid=(cols // tn,),
        in_specs=[pl.BlockSpec((rows, d), lambda j: (0, 0)),
                  pl.BlockSpec((None, d, tn), lambda j: (layer, 0, j)),
                  pl.BlockSpec((None, 1, tn), lambda j: (layer, 0, j))],
        out_specs=pl.BlockSpec((rows, tn), lambda j: (0, j)),
        out_shape=jax.ShapeDtypeStruct((rows, cols), F32),
        compiler_params=_cparams(("arbitrary",)),
        name="adaln",
    )(cvec, w, b.reshape(b.shape[0], 1, cols))


def _norm_mod_kernel(x_ref, g_ref, mod_ref, h_ref, *, shift_idx, scale_idx):
    h_ref[...] = _norm_mod(x_ref[...], g_ref[...], mod_ref, shift_idx, scale_idx).astype(BF16)


def _norm_mod_call(x, g, mod, shift_idx, scale_idx):
    b, r, d = x.shape
    tm = min(r, 512)
    return pl.pallas_call(
        functools.partial(_norm_mod_kernel, shift_idx=shift_idx, scale_idx=scale_idx),
        grid=(b, r // tm),
        in_specs=[pl.BlockSpec((None, tm, d), lambda i, t: (i, t, 0)),
                  pl.BlockSpec((1, d), lambda i, t: (0, 0)),
                  pl.BlockSpec((None, 6, d), lambda i, t: (i, 0, 0))],
        out_specs=pl.BlockSpec((None, tm, d), lambda i, t: (i, t, 0)),
        out_shape=jax.ShapeDtypeStruct((b, r, d), BF16),
        compiler_params=_cparams(("parallel", "parallel")),
        name="norm_mod",
    )(x, g.reshape(1, d), mod)


QKV_TILE = 4 * LANES
_PLAIN = (None, None, 1.0)
_QKV_TILE_OPS = {
    0: [(0, 32, QS128)] * 4,
    1: [(1, 32, 1.0)] * 2 + [_PLAIN] * 2,
    2: [(None, 16, QS64)] * 4,
    3: [(None, 16, 1.0)] * 4,
    5: [(None, None, QS128)] * 4,
    8: [(None, 32, QS128)] * 4,
    9: [(None, 32, 1.0)] * 2 + [_PLAIN] * 2,
}


def _rope(y, cos, sin, hw):
    lane = lax.broadcasted_iota(jnp.int32, y.shape, 1)
    first = (lane % (2 * hw)) < hw
    partner = jnp.where(first, pltpu.roll(y, LANES - hw, 1), pltpu.roll(y, hw, 1))
    return y * cos + partner * sin


def _qkv_kernel(h_ref, w_ref, cos32_ref, sin32_ref, cos16_ref, sin16_ref, g_ref, o_ref):
    h = h_ref[...]
    for j in range(o_ref.shape[1] // QKV_TILE):
        acc = _dot(h, w_ref[:, j * QKV_TILE:(j + 1) * QKV_TILE])
        for c, (norm_row, hw, scale) in enumerate(_QKV_TILE_OPS.get(j, [_PLAIN] * 4)):
            y = acc[:, c * LANES:(c + 1) * LANES]
            if norm_row is not None:
                y = _rms(y) * g_ref[norm_row:norm_row + 1, :]
            if hw == 32:
                y = _rope(y, cos32_ref[...], sin32_ref[...], 32)
            elif hw == 16:
                y = _rope(y, cos16_ref[...], sin16_ref[...], 16)
            if scale != 1.0:
                y = y * scale
            col = j * QKV_TILE + c * LANES
            o_ref[:, col:col + LANES] = y.astype(o_ref.dtype)


def _qkv_call(h, w_qkv, tables, qk_g):
    b, r, d = h.shape
    cols = w_qkv.shape[1]
    tm = min(r, 512)
    tab_spec = pl.BlockSpec((tm, LANES), lambda i, t: (t, 0))
    return pl.pallas_call(
        _qkv_kernel,
        grid=(b, r // tm),
        in_specs=[pl.BlockSpec((None, tm, d), lambda i, t: (i, t, 0)),
                  pl.BlockSpec((d, cols), lambda i, t: (0, 0), pipeline_mode=pl.Buffered(1)),
                  tab_spec, tab_spec, tab_spec, tab_spec,
                  pl.BlockSpec((2, LANES), lambda i, t: (0, 0))],
        out_specs=pl.BlockSpec((None, tm, cols), lambda i, t: (i, t, 0)),
        out_shape=jax.ShapeDtypeStruct((b, r, cols), BF16),
        compiler_params=_cparams(("parallel", "parallel")),
        name="qkv_proj",
    )(h, w_qkv, *tables, qk_g)


def _matmul_kernel(h_ref, w_ref, o_ref):
    o_ref[...] = _dot(h_ref[...], w_ref[...].astype(BF16)).astype(o_ref.dtype)


def _gate_proj_call(h, w_in, layer):
    b, r, d = h.shape
    tn = 1024
    cols = w_in.shape[2] - QKV_COLS
    col0 = QKV_COLS // tn
    tm = 2048 if r % 2048 == 0 else r
    return pl.pallas_call(
        _matmul_kernel,
        grid=(b, r // tm, cols // tn),
        in_specs=[pl.BlockSpec((None, tm, d), lambda i, t, j: (i, t, 0)),
                  pl.BlockSpec((None, d, tn), lambda i, t, j: (layer, 0, col0 + j))],
        out_specs=pl.BlockSpec((None, tm, tn), lambda i, t, j: (i, t, j)),
        out_shape=jax.ShapeDtypeStruct((b, r, cols), BF16),
        compiler_params=_cparams(("parallel", "parallel", "arbitrary")),
        name="gate_proj",
    )(h, w_in)


def _cast_kernel(w_ref, o_ref):
    o_ref[...] = w_ref[...].astype(o_ref.dtype)


def _qkv_weights_call(w_in, layer):
    d = w_in.shape[1]
    return pl.pallas_call(
        _cast_kernel,
        grid=(QKV_COLS // QKV_TILE,),
        in_specs=[pl.BlockSpec((None, d, QKV_TILE), lambda j: (layer, 0, j))],
        out_specs=pl.BlockSpec((d, QKV_TILE), lambda j: (0, j)),
        out_shape=jax.ShapeDtypeStruct((d, QKV_COLS), BF16),
        compiler_params=_cparams(("parallel",)),
        name="qkv_weights",
    )(w_in)


def _diff_lambda(lam_ref, lambda_init):
    a = jnp.sum(lam_ref[0:1, :] * lam_ref[1:2, :], axis=-1, keepdims=True)
    b = jnp.sum(lam_ref[2:3, :] * lam_ref[3:4, :], axis=-1, keepdims=True)
    return jnp.exp(a) - jnp.exp(b) + lambda_init


def _split_maps(q):
    lane = lax.broadcasted_iota(jnp.int32, q.shape, 1)
    zero = jnp.zeros_like(q)
    return jnp.concatenate([jnp.where(lane < DIFF_QK_DIM, q, zero),
                            jnp.where(lane >= DIFF_QK_DIM, q, zero)], axis=0)


FLASH_ROW_BLOCK = 128
FLASH_KV_CHUNK = 512
FLASH_Q_TILE = 1024


def _flash_pair(q2, kc_ref, vc_ref, kl_ref, vl_ref, tk):
    m_rows = q2.shape[0]
    n_lat = kl_ref.shape[0]
    nb = m_rows // FLASH_ROW_BLOCK
    qs = [q2[i * FLASH_ROW_BLOCK:(i + 1) * FLASH_ROW_BLOCK] for i in range(nb)]

    def step(k, v, carry):
        v1 = jnp.concatenate([v, jnp.ones_like(v)], axis=1)
        out = []
        for q, (m, acc) in zip(qs, carry):
            s = _dot_nt(q, k)
            m_new = jnp.maximum(m, jnp.max(s, axis=-1, keepdims=True))
            p = jnp.exp2(s - m_new)
            acc = jnp.exp2(m - m_new) * acc + _dot(p.astype(BF16), v1)
            out.append((m_new, acc))
        return tuple(out)

    carry = tuple((jnp.full((FLASH_ROW_BLOCK, 1), NEG_INF, F32),
                   jnp.zeros((FLASH_ROW_BLOCK, 2 * HEAD_DIM), F32)) for _ in range(nb))
    carry = step(kc_ref[...], vc_ref[...], carry)

    for c in range(n_lat // tk):
        carry = step(kl_ref[c * tk:(c + 1) * tk, :], vl_ref[c * tk:(c + 1) * tk, :], carry)
    return jnp.concatenate([acc[:, :HEAD_DIM] / acc[:, HEAD_DIM:] for _, acc in carry], axis=0)


def _gqa_kernel(q_ref, kc_ref, vc_ref, kl_ref, vl_ref, o_ref, *, tk):
    tq = q_ref.shape[0]
    q = q_ref[...]
    q2 = jnp.concatenate([q[:, :HEAD_DIM], q[:, HEAD_DIM:]], axis=0)
    o = _flash_pair(q2, kc_ref, vc_ref, kl_ref, vl_ref, tk)
    o_ref[:, :HEAD_DIM] = o[:tq].astype(o_ref.dtype)
    o_ref[:, HEAD_DIM:] = o[tq:].astype(o_ref.dtype)


def _diff_kernel(q_ref, kc_ref, vc_ref, kl_ref, vl_ref, lam_ref, subg_ref, o_ref, *, tk, lambda_init):
    tq = q_ref.shape[0]
    o = _flash_pair(_split_maps(q_ref[...]), kc_ref, vc_ref, kl_ref, vl_ref, tk)
    lam = _diff_lambda(lam_ref, lambda_init)
    d = o[:tq] - lam * o[tq:]
    o_ref[...] = (_rms(d) * subg_ref[...] * (1.0 - lambda_init)).astype(o_ref.dtype)


def _gqa_call(p_lat, p_ctx):
    b, n, _ = p_lat.shape
    l = p_ctx.shape[1]
    tq = FLASH_Q_TILE
    tk = FLASH_KV_CHUNK
    hd = HEAD_DIM
    kcol, vcol = COL_GQA_K // hd, COL_GQA_V // hd
    return pl.pallas_call(
        functools.partial(_gqa_kernel, tk=tk),
        grid=(b, 2, n // tq),
        in_specs=[pl.BlockSpec((None, tq, 2 * hd), lambda i, k, t: (i, t, k)),
                  pl.BlockSpec((None, l, hd), lambda i, k, t: (i, 0, kcol + k)),
                  pl.BlockSpec((None, l, hd), lambda i, k, t: (i, 0, vcol + k)),
                  pl.BlockSpec((None, n, hd), lambda i, k, t: (i, 0, kcol + k)),
                  pl.BlockSpec((None, n, hd), lambda i, k, t: (i, 0, vcol + k))],
        out_specs=pl.BlockSpec((None, tq, 2 * hd), lambda i, k, t: (i, t, k)),
        out_shape=jax.ShapeDtypeStruct((b, n, BRANCH_WIDTH), BF16),
        compiler_params=_cparams(("parallel", "parallel", "arbitrary")),
        name="attn_gqa",
    )(p_lat, p_ctx, p_ctx, p_lat, p_lat)


def _diff_call(p_lat, p_ctx, lam_p, subg, lambda_init):
    b, n, _ = p_lat.shape
    l = p_ctx.shape[1]
    tq = FLASH_Q_TILE
    tk = FLASH_KV_CHUNK
    hd = HEAD_DIM
    qcol, kcol, vcol = COL_DIFF_Q // hd, COL_DIFF_K // hd, COL_DIFF_V // hd
    return pl.pallas_call(
        functools.partial(_diff_kernel, tk=tk, lambda_init=lambda_init),
        grid=(b, 4, n // tq),
        in_specs=[pl.BlockSpec((None, tq, hd), lambda i, h, t: (i, t, qcol + h)),
                  pl.BlockSpec((None, l, hd), lambda i, h, t: (i, 0, kcol + h)),
                  pl.BlockSpec((None, l, hd), lambda i, h, t: (i, 0, vcol + h)),
                  pl.BlockSpec((None, n, hd), lambda i, h, t: (i, 0, kcol + h)),
                  pl.BlockSpec((None, n, hd), lambda i, h, t: (i, 0, vcol + h)),
                  pl.BlockSpec((4, DIFF_QK_DIM), lambda i, h, t: (0, 0)),
                  pl.BlockSpec((1, hd), lambda i, h, t: (0, 0))],
        out_specs=pl.BlockSpec((None, tq, hd), lambda i, h, t: (i, t, h)),
        out_shape=jax.ShapeDtypeStruct((b, n, BRANCH_WIDTH), BF16),
        compiler_params=_cparams(("parallel", "parallel", "arbitrary")),
        name="attn_diff",
    )(p_lat, p_ctx, p_ctx, p_lat, p_lat, lam_p, subg.reshape(1, hd))


NA_GROUP_ROWS = 32
NA_BLOCK_ROWS = 4
NA_BLOCK_WIN_ROWS = NA_BLOCK_ROWS + NA_ROWS


def _with_ones(v):
    return jnp.concatenate([v, jnp.ones_like(v)], axis=1)


def _band_ctx_attend(s_b, s_c, vb1, vc1, sink=None):
    m = jnp.maximum(jnp.max(s_b, axis=-1, keepdims=True), jnp.max(s_c, axis=-1, keepdims=True))
    if sink is not None:
        m = jnp.maximum(m, sink)
    acc = _dot(jnp.exp2(s_c - m).astype(BF16), vc1) + _dot(jnp.exp2(s_b - m).astype(BF16), vb1)
    den = acc[:, HEAD_DIM:]
    if sink is not None:
        den = den + jnp.exp2(sink - m)
    return acc[:, :HEAD_DIM] / den


NA_DI = 2 * NA_ROWS - 1
NA_TAB_PAD = 2 * NA_ROWS
NA_TAB_LANES = 3072


def _na_bias_tables(rpb):
    nh = rpb.shape[0]
    qc = np.arange(GRID_W)[:, None]
    kc = np.arange(GRID_W)[None, :]
    cstart = np.clip(qc - NA_COLS // 2, 0, GRID_W - NA_COLS)
    ok_col = (kc >= cstart) & (kc < cstart + NA_COLS)
    dj = np.clip(kc - qc + NA_COLS - 1, 0, 2 * NA_COLS - 2)
    sel_j = (dj[..., None] == np.arange(2 * NA_COLS - 1)) & ok_col[..., None]
    by_col = jnp.einsum("hij,qkj->hqik", rpb.astype(F32), jnp.asarray(sel_j, F32),
                        precision=lax.Precision.HIGHEST)
    by_col = jnp.where(ok_col[None, :, None, :], by_col * LOG2E, NEG_INF)
    strip = by_col.reshape(nh, GRID_W, NA_DI * GRID_W)
    total = NA_TAB_LANES + GRID_W
    strip = jnp.pad(strip, ((0, 0), (0, 0), (NA_TAB_PAD * GRID_W, total - (NA_TAB_PAD + NA_DI) * GRID_W)),
                    constant_values=NEG_INF)
    return jnp.stack([strip[:, :, :NA_TAB_LANES], strip[:, :, GRID_W:]], axis=1)


def _na_kernel(q_ref, kc_ref, vc_ref, kl_ref, vl_ref, tab_ref, o_ref, *, rows_n):
    g = pl.program_id(2)
    win = NA_BLOCK_WIN_ROWS * GRID_W
    blk_q = NA_BLOCK_ROWS * GRID_W
    kc = kc_ref[...]
    vc1 = _with_ones(vc_ref[...])
    lane = lax.broadcasted_iota(jnp.int32, (GRID_W, win), 1)
    for rb in range(NA_GROUP_ROWS // NA_BLOCK_ROWS):
        r0 = g * NA_GROUP_ROWS + rb * NA_BLOCK_ROWS
        ws = jnp.clip(r0 - NA_ROWS // 2, 0, rows_n - NA_BLOCK_WIN_ROWS)
        off = pl.multiple_of(ws * GRID_W, NA_BLOCK_ROWS * GRID_W)
        bias = []
        for a in range(NA_BLOCK_ROWS):
            r = r0 + a
            rs = jnp.clip(r - NA_ROWS // 2, 0, rows_n - NA_ROWS)
            blk = ws - r + NA_ROWS - 1 + NA_TAB_PAD
            start = pl.multiple_of((blk >> 1) * (2 * GRID_W), 2 * GRID_W)
            strip = tab_ref[blk & 1, :, pl.ds(start, win)]
            lo = (rs - ws) * GRID_W
            in_rows = jnp.logical_and(lane >= lo, lane < lo + NA_ROWS * GRID_W)
            bias.append(jnp.where(in_rows, strip, NEG_INF))
        q = q_ref[rb * blk_q:(rb + 1) * blk_q, :]
        s_b = _dot_nt(q, kl_ref[pl.ds(off, win), :]) + jnp.concatenate(bias, axis=0)
        o = _band_ctx_attend(s_b, _dot_nt(q, kc), _with_ones(vl_ref[pl.ds(off, win), :]), vc1)
        o_ref[rb * blk_q:(rb + 1) * blk_q, :] = o.astype(o_ref.dtype)


def _na_call(p_lat, p_ctx, bias):
    b, n, _ = p_lat.shape
    l = p_ctx.shape[1]
    hd = HEAD_DIM
    rows_n = n // GRID_W
    n_groups = rows_n // NA_GROUP_ROWS
    tq = NA_GROUP_ROWS * GRID_W
    qcol, kcol, vcol = COL_NA_Q // hd, COL_NA_K // hd, COL_NA_V // hd

    return pl.pallas_call(
        functools.partial(_na_kernel, rows_n=rows_n),
        grid=(b, 4, n_groups),
        in_specs=[pl.BlockSpec((None, tq, hd), lambda i, h, g: (i, g, qcol + h)),
                  pl.BlockSpec((None, l, hd), lambda i, h, g: (i, 0, kcol + h)),
                  pl.BlockSpec((None, l, hd), lambda i, h, g: (i, 0, vcol + h)),
                  pl.BlockSpec((None, n, hd), lambda i, h, g: (i, 0, kcol + h)),
                  pl.BlockSpec((None, n, hd), lambda i, h, g: (i, 0, vcol + h)),
                  pl.BlockSpec((None, 2, GRID_W, NA_TAB_LANES), lambda i, h, g: (h, 0, 0, 0))],
        out_specs=pl.BlockSpec((None, tq, hd), lambda i, h, g: (i, g, h)),
        out_shape=jax.ShapeDtypeStruct((b, n, BRANCH_WIDTH), BF16),
        compiler_params=_cparams(("parallel", "parallel", "arbitrary")),
        name="attn_na",
    )(p_lat, p_ctx, p_ctx, p_lat, p_lat, bias)


SWA_BLOCK = 256


def _swa_kernel(q_ref, kc_ref, vc_ref, kl_ref, vl_ref, sink_ref, o_ref, *, win):
    t = pl.program_id(2)
    tq = q_ref.shape[0]
    n = kl_ref.shape[0]
    kc = kc_ref[...]
    vc1 = _with_ones(vc_ref[...])
    row = lax.broadcasted_iota(jnp.int32, (SWA_BLOCK, win), 0)
    col = lax.broadcasted_iota(jnp.int32, (SWA_BLOCK, win), 1)
    for rb in range(tq // SWA_BLOCK):
        qs = t * tq + rb * SWA_BLOCK
        ws = pl.multiple_of(jnp.clip(qs - SWA_WINDOW, 0, n - win), SWA_WINDOW)
        kb = kl_ref[pl.ds(ws, win), :]
        vb1 = _with_ones(vl_ref[pl.ds(ws, win), :])
        valid = jnp.abs(row + (qs - ws) - col) <= SWA_WINDOW
        rows = slice(rb * SWA_BLOCK, (rb + 1) * SWA_BLOCK)
        for gi in range(2):
            q = q_ref[rows, gi * HEAD_DIM:(gi + 1) * HEAD_DIM]
            s_b = jnp.where(valid, _dot_nt(q, kb), NEG_INF)
            o = _band_ctx_attend(s_b, _dot_nt(q, kc), vb1, vc1, sink=sink_ref[gi:gi + 1, 0:1])
            o_ref[rows, gi * HEAD_DIM:(gi + 1) * HEAD_DIM] = o.astype(o_ref.dtype)


def _swa_call(p_lat, p_ctx, sink_tab):
    b, n, _ = p_lat.shape
    l = p_ctx.shape[1]
    hd = HEAD_DIM
    tq = 8 * SWA_BLOCK
    win = SWA_BLOCK + 2 * SWA_WINDOW
    qcol, kcol, vcol = COL_SWA_Q // (2 * hd), COL_SWA_K // hd, COL_SWA_V // hd
    return pl.pallas_call(
        functools.partial(_swa_kernel, win=win),
        grid=(b, 2, n // tq),
        in_specs=[pl.BlockSpec((None, tq, 2 * hd), lambda i, k, t: (i, t, qcol + k)),
                  pl.BlockSpec((None, l, hd), lambda i, k, t: (i, 0, kcol + k)),
                  pl.BlockSpec((None, l, hd), lambda i, k, t: (i, 0, vcol + k)),
                  pl.BlockSpec((None, n, hd), lambda i, k, t: (i, 0, kcol + k)),
                  pl.BlockSpec((None, n, hd), lambda i, k, t: (i, 0, vcol + k)),
                  pl.BlockSpec((None, 2, LANES), lambda i, k, t: (k, 0, 0))],
        out_specs=pl.BlockSpec((None, tq, 2 * hd), lambda i, k, t: (i, t, k)),
        out_shape=jax.ShapeDtypeStruct((b, n, BRANCH_WIDTH), BF16),
        compiler_params=_cparams(("parallel", "parallel", "arbitrary")),
        name="attn_swa",
    )(p_lat, p_ctx, p_ctx, p_lat, p_lat, sink_tab)


def _softmax_attend(q, k, v, sink=None):
    s = _dot_nt(q, k)
    m = jnp.max(s, axis=-1, keepdims=True)
    if sink is not None:
        m = jnp.maximum(m, sink)
    e = jnp.exp2(s - m)
    den = jnp.sum(e, axis=-1, keepdims=True)
    if sink is not None:
        den = den + jnp.exp2(sink - m)
    return _dot(e.astype(BF16), v) / den


def _ctx_attn_kernel(p_ref, lam_ref, subg_ref, sink_ref, o_ref, *, lambda_init):
    hd = HEAD_DIM

    def col(c0, h):
        return p_ref[:, c0 + h * hd:c0 + (h + 1) * hd]

    lam = _diff_lambda(lam_ref, lambda_init)
    for h in range(4):
        o = _softmax_attend(col(COL_GQA_Q, h), col(COL_GQA_K, h // 2), col(COL_GQA_V, h // 2))
        o_ref[:, h * hd:(h + 1) * hd] = o.astype(o_ref.dtype)
    for h in range(4):
        q = col(COL_DIFF_Q, h)
        tq = q.shape[0]
        o2 = _softmax_attend(_split_maps(q), col(COL_DIFF_K, h), col(COL_DIFF_V, h))
        d = o2[:tq] - lam * o2[tq:]
        d = _rms(d) * subg_ref[...] * (1.0 - lambda_init)
        o_ref[:, BRANCH_WIDTH + h * hd:BRANCH_WIDTH + (h + 1) * hd] = d.astype(o_ref.dtype)
    for h in range(4):
        o = _softmax_attend(col(COL_NA_Q, h), col(COL_NA_K, h), col(COL_NA_V, h))
        o_ref[:, 2 * BRANCH_WIDTH + h * hd:2 * BRANCH_WIDTH + (h + 1) * hd] = o.astype(o_ref.dtype)
    for h in range(4):
        o = _softmax_attend(col(COL_SWA_Q, h), col(COL_SWA_K, h // 2), col(COL_SWA_V, h // 2),
                            sink=sink_ref[h:h + 1, 0:1])
        o_ref[:, 3 * BRANCH_WIDTH + h * hd:3 * BRANCH_WIDTH + (h + 1) * hd] = o.astype(o_ref.dtype)


def _ctx_attn_call(p_ctx, lam_p, subg, sink_tab, lambda_init):
    b, l, cols = p_ctx.shape
    return pl.pallas_call(
        functools.partial(_ctx_attn_kernel, lambda_init=lambda_init),
        grid=(b,),
        in_specs=[pl.BlockSpec((None, l, cols), lambda i: (i, 0, 0)),
                  pl.BlockSpec((4, DIFF_QK_DIM), lambda i: (0, 0)),
                  pl.BlockSpec((1, HEAD_DIM), lambda i: (0, 0)),
                  pl.BlockSpec((4, LANES), lambda i: (0, 0))],
        out_specs=pl.BlockSpec((None, l, N_BRANCHES * BRANCH_WIDTH), lambda i: (i, 0, 0)),
        out_shape=jax.ShapeDtypeStruct((b, l, N_BRANCHES * BRANCH_WIDTH), BF16),
        compiler_params=_cparams(("parallel",)),
        name="attn_ctx",
    )(p_ctx, lam_p, subg.reshape(1, HEAD_DIM), sink_tab.reshape(4, LANES))


MERGE_CHUNKS = 1


def _merge_kernel(o0, o1, o2, o3, g_ref, wb_ref, wo_ref, x_ref, mod_ref, ng_ref, *rest, with_router):
    if with_router:
        router_ref, xo_ref, h_ref, lg_ref = rest
    else:
        xo_ref, h_ref = rest
    d = x_ref.shape[1]
    dc = d // MERGE_CHUNKS
    o_vals = [o[...] for o in (o0, o1, o2, o3)]
    z = None
    for c in range(MERGE_CHUNKS):
        s = None
        for i, o in enumerate(o_vals):
            y = _dot(o, wb_ref[i, :, c * dc:(c + 1) * dc])
            gate = g_ref[:, i * d + c * dc:i * d + (c + 1) * dc].astype(F32)
            term = jax.nn.sigmoid(gate) * y
            s = term if s is None else s + term
        zc = _dot(s.astype(BF16), wo_ref[c * dc:(c + 1) * dc, :])
        z = zc if z is None else z + zc
    xn = x_ref[...] + mod_ref[2:3, :] * z
    xo_ref[...] = xn
    hn = _norm_mod(xn, ng_ref[...], mod_ref, 3, 4)
    h_ref[...] = hn.astype(h_ref.dtype)
    if with_router:
        lg_ref[...] = _dot_split(hn, router_ref[...])


def _dot_split(a, b):
    a_hi = a.astype(BF16)
    a_lo = (a - a_hi.astype(F32)).astype(BF16)
    b_hi = b.astype(BF16)
    b_lo = (b - b_hi.astype(F32)).astype(BF16)
    n = b.shape[1]
    both = _dot(a_hi, jnp.concatenate([b_hi, b_lo], axis=1))
    return both[:, :n] + both[:, n:] + _dot(a_lo, b_hi)


def _merge_call(o_parts, gates, w_branch, w_out, x, mod, next_g, router=None):
    b, r, d = x.shape
    tm = min(r, 256)
    with_router = router is not None
    whole = dict(pipeline_mode=pl.Buffered(1))

    def o_spec(cb):
        return pl.BlockSpec((None, tm, BRANCH_WIDTH), lambda i, t: (i, t, cb))

    row_spec = pl.BlockSpec((None, tm, d), lambda i, t: (i, t, 0))
    in_specs = [o_spec(cb) for _, cb in o_parts] + [
        pl.BlockSpec((None, tm, N_BRANCHES * d), lambda i, t: (i, t, 0)),
        pl.BlockSpec((N_BRANCHES, BRANCH_WIDTH, d), lambda i, t: (0, 0, 0), **whole),
        pl.BlockSpec((d, d), lambda i, t: (0, 0), **whole),
        row_spec,
        pl.BlockSpec((None, 6, d), lambda i, t: (i, 0, 0)),
        pl.BlockSpec((1, d), lambda i, t: (0, 0))]
    args = [a for a, _ in o_parts] + [gates, w_branch, w_out, x, mod, next_g.reshape(1, d)]
    out_specs = [row_spec, row_spec]
    out_shape = [jax.ShapeDtypeStruct((b, r, d), F32), jax.ShapeDtypeStruct((b, r, d), BF16)]
    if with_router:
        in_specs.append(pl.BlockSpec((d, LANES), lambda i, t: (0, 0), **whole))
        args.append(router)
        out_specs.append(pl.BlockSpec((None, tm, LANES), lambda i, t: (i, t, 0)))
        out_shape.append(jax.ShapeDtypeStruct((b, r, LANES), F32))
    return pl.pallas_call(
        functools.partial(_merge_kernel, with_router=with_router),
        grid=(b, r // tm),
        in_specs=in_specs,
        out_specs=out_specs,
        out_shape=out_shape,
        compiler_params=_cparams(("parallel", "parallel")),
        name="merge",
    )(*args)


def _swiglu_partial(h, wg, wu, wd):
    gate = _dot(h, wg)
    up = _dot(h, wu)
    act = gate * jax.nn.sigmoid(gate) * up
    return _dot(act.astype(BF16), wd)


FFN_ROW_BLOCK = 512


def _ffn_kernel(h_ref, wg_ref, wu_ref, wd_ref, x_ref, mod_ref, ng_ref, nmod_ref, xo_ref, hn_ref,
                acc_ref):
    f = pl.program_id(2)

    @pl.when(f == 0)
    def _():
        acc_ref[...] = jnp.zeros_like(acc_ref)

    for r0 in range(0, h_ref.shape[0], FFN_ROW_BLOCK):
        rows = slice(r0, r0 + FFN_ROW_BLOCK)
        acc_ref[rows, :] += _swiglu_partial(h_ref[rows, :], wg_ref[...], wu_ref[...], wd_ref[...])

    @pl.when(f == pl.num_programs(2) - 1)
    def _():
        xn = x_ref[...] + mod_ref[5:6, :] * acc_ref[...]
        xo_ref[...] = xn
        hn_ref[...] = _norm_mod(xn, ng_ref[...], nmod_ref, 0, 1).astype(hn_ref.dtype)


def _ffn_call(h, wg, wu, wd, x, mod, next_g, next_mod):
    b, r, d = x.shape
    dff = wg.shape[1]
    tf = 512
    tm = min(r, FFN_ROW_BLOCK)
    row_spec = pl.BlockSpec((None, tm, d), lambda i, t, f: (i, t, 0))
    mod_spec = pl.BlockSpec((None, 6, d), lambda i, t, f: (i, 0, 0))
    return pl.pallas_call(
        _ffn_kernel,
        grid=(b, r // tm, dff // tf),
        in_specs=[row_spec,
                  pl.BlockSpec((d, tf), lambda i, t, f: (0, f)),
                  pl.BlockSpec((d, tf), lambda i, t, f: (0, f)),
                  pl.BlockSpec((tf, d), lambda i, t, f: (f, 0)),
                  row_spec, mod_spec,
                  pl.BlockSpec((1, d), lambda i, t, f: (0, 0)),
                  mod_spec],
        out_specs=[row_spec, row_spec],
        out_shape=[jax.ShapeDtypeStruct((b, r, d), F32), jax.ShapeDtypeStruct((b, r, d), BF16)],
        scratch_shapes=[pltpu.VMEM((tm, d), F32)],
        compiler_params=_cparams(("parallel", "parallel", "arbitrary")),
        name="ffn_dense",
    )(h, wg, wu, wd, x, mod, next_g.reshape(1, d), next_mod)


def _route_kernel(lg_ref, idx_ref, w_ref):
    lg = lg_ref[...]
    lane = lax.broadcasted_iota(jnp.int32, lg.shape, 1)
    valid = lane < N_EXPERTS
    mx = jnp.max(jnp.where(valid, lg, -jnp.inf), axis=-1, keepdims=True)
    e = jnp.where(valid, jnp.exp(lg - mx), 0.0)
    p = e / jnp.sum(e, axis=-1, keepdims=True)
    p1 = jnp.max(p, axis=-1, keepdims=True)
    i1 = jnp.min(jnp.where(p == p1, lane, LANES), axis=-1, keepdims=True)
    rest = jnp.where(jnp.logical_or(lane == i1, jnp.logical_not(valid)), -1.0, p)
    p2 = jnp.max(rest, axis=-1, keepdims=True)
    i2 = jnp.min(jnp.where(rest == p2, lane, LANES), axis=-1, keepdims=True)
    tot = p1 + p2
    idx_ref[...] = jnp.where(lane == 0, i1, jnp.where(lane == 1, i2, 0))
    w_ref[...] = jnp.where(lane == 0, p1 / tot, jnp.where(lane == 1, p2 / tot, 0.0))


def _route_call(logits):
    r = logits.shape[0]
    tm = min(r, 1024)
    spec = pl.BlockSpec((tm, LANES), lambda i: (i, 0))
    return pl.pallas_call(
        _route_kernel,
        grid=(r // tm,),
        in_specs=[spec],
        out_specs=[spec, spec],
        out_shape=[jax.ShapeDtypeStruct((r, LANES), jnp.int32), jax.ShapeDtypeStruct((r, LANES), F32)],
        compiler_params=_cparams(("parallel",)),
        name="route_top2",
    )(logits)


MOE_HALF = 512
MOE_TILE = 2 * MOE_HALF
MOE_FF_TILE = 512
MOE_VMEM_LIMIT = 60 * 1024 * 1024


def _moe_kernel(te_ref, nr_ref, x_ref, wg_ref, wu_ref, wd_ref, o_ref, acc_ref):
    i = pl.program_id(0)
    f = pl.program_id(1)

    @pl.when(f == 0)
    def _():
        acc_ref[...] = jnp.zeros_like(acc_ref)

    def accumulate(n_rows):
        wg = wg_ref[...].astype(BF16)
        wu = wu_ref[...].astype(BF16)
        wd = wd_ref[...].astype(BF16)
        for r0 in range(0, n_rows, MOE_HALF):
            rows = slice(r0, r0 + MOE_HALF)
            acc_ref[rows, :] += _swiglu_partial(x_ref[rows, :], wg, wu, wd)

    n_valid = nr_ref[i]
    pl.when(n_valid > MOE_HALF)(functools.partial(accumulate, MOE_TILE))
    pl.when(jnp.logical_and(n_valid > 0, n_valid <= MOE_HALF))(functools.partial(accumulate, MOE_HALF))

    @pl.when(f == pl.num_programs(1) - 1)
    def _():
        o_ref[...] = acc_ref[...].astype(o_ref.dtype)


def _moe_call(xs, wg, wu, wd, tile_expert, tile_rows):
    r, d = xs.shape
    dff = wg.shape[2]
    tf = MOE_FF_TILE
    nf = dff // tf

    def fidx(i, f, nr):
        return jnp.where(nr[i] > 0, f, nf - 1)

    grid_spec = pltpu.PrefetchScalarGridSpec(
        num_scalar_prefetch=2,
        grid=(r // MOE_TILE, nf),
        in_specs=[pl.BlockSpec((MOE_TILE, d), lambda i, f, te, nr: (i, 0)),
                  pl.BlockSpec((None, d, tf), lambda i, f, te, nr: (te[i], 0, fidx(i, f, nr))),
                  pl.BlockSpec((None, d, tf), lambda i, f, te, nr: (te[i], 0, fidx(i, f, nr))),
                  pl.BlockSpec((None, tf, d), lambda i, f, te, nr: (te[i], fidx(i, f, nr), 0))],
        out_specs=pl.BlockSpec((MOE_TILE, d), lambda i, f, te, nr: (i, 0)),
        scratch_shapes=[pltpu.VMEM((MOE_TILE, d), F32)])
    return pl.pallas_call(
        _moe_kernel,
        grid_spec=grid_spec,
        out_shape=jax.ShapeDtypeStruct((r, d), BF16),
        compiler_params=pltpu.CompilerParams(dimension_semantics=("arbitrary", "arbitrary"),
                                             vmem_limit_bytes=MOE_VMEM_LIMIT),
        name="moe_experts",
    )(tile_expert, tile_rows, xs, wg, wu, wd)


def _moe_plan(idx):
    t = idx.shape[0]
    a = t * TOP_K
    flat_e = idx.reshape(a)
    onehot = (flat_e[:, None] == jnp.arange(N_EXPERTS, dtype=jnp.int32)[None, :]).astype(jnp.int32)
    csum = jnp.cumsum(onehot, axis=0)
    rank = jnp.sum((csum - onehot) * onehot, axis=1)
    counts = csum[-1]
    padded = ((counts + MOE_TILE - 1) // MOE_TILE) * MOE_TILE
    ends = jnp.cumsum(padded)
    offs = ends - padded
    dest = offs[flat_e] + rank
    rows = a + N_EXPERTS * MOE_TILE
    src_tok = (jnp.arange(rows, dtype=jnp.int32) % t).at[dest].set(jnp.arange(a, dtype=jnp.int32) // TOP_K)
    tile_start = jnp.arange(rows // MOE_TILE, dtype=jnp.int32) * MOE_TILE
    past = jnp.sum((tile_start[:, None] >= ends[None, :]).astype(jnp.int32), axis=1)
    tile_expert = jnp.minimum(past, N_EXPERTS - 1)
    tile_rows = jnp.where(past < N_EXPERTS,
                          jnp.clip((offs + counts)[tile_expert] - tile_start, 0, MOE_TILE), 0)
    return dest.reshape(t, TOP_K), src_tok, tile_expert, tile_rows.astype(jnp.int32)


def _final_kernel(x_ref, y0_ref, y1_ref, w_ref, mod_ref, g_ref, o_ref):
    y = w_ref[:, 0:1] * y0_ref[...].astype(F32) + w_ref[:, 1:2] * y1_ref[...].astype(F32)
    xn = x_ref[...] + mod_ref[5:6, :] * y
    o_ref[...] = _rms(xn) * g_ref[...]


def _final_call(x, y0, y1, w, mod, g):
    b, r, d = x.shape
    tm = min(r, 512)
    row_spec = pl.BlockSpec((None, tm, d), lambda i, t: (i, t, 0))
    return pl.pallas_call(
        _final_kernel,
        grid=(b, r // tm),
        in_specs=[row_spec, row_spec, row_spec,
                  pl.BlockSpec((None, tm, LANES), lambda i, t: (i, t, 0)),
                  pl.BlockSpec((None, 6, d), lambda i, t: (i, 0, 0)),
                  pl.BlockSpec((1, d), lambda i, t: (0, 0))],
        out_specs=row_spec,
        out_shape=jax.ShapeDtypeStruct((b, r, d), F32),
        compiler_params=_cparams(("parallel", "parallel")),
        name="final_norm",
    )(x, y0, y1, w, mod, g.reshape(1, d))


def _rope_tables(n):
    pos = jnp.arange(n)
    rows = (pos // GRID_W).astype(F32)
    cols = (pos % GRID_W).astype(F32)
    lane = np.arange(LANES)
    out = []
    for hw in (32, 16):
        period = 4 * hw
        u = lane % period
        use_cols = (u // (2 * hw)) == 1
        w = u % (2 * hw)
        freqs = jnp.asarray(ROPE_THETA ** (-(w % hw).astype(np.float32) / hw), F32)
        p = jnp.where(jnp.asarray(use_cols)[None, :], cols[:, None], rows[:, None])
        ang = p * freqs[None, :]
        sign = jnp.asarray(np.where(w < hw, -1.0, 1.0), F32)
        out += [jnp.cos(ang), jnp.sin(ang) * sign[None, :]]
    return out


def kernel(x, c, ctx, c_ctx, attn_norm_g, ffn_norm_g, ada_w, ada_b, w_in, qk_norm_g, diff_lambda,
           diff_subln_g, na_rpb, swa_sink, w_branch, w_out, ffn_w_gate, ffn_w_up, ffn_w_down,
           moe_router, moe_w_gate, moe_w_up, moe_w_down, final_norm_g):
    b, n, d = x.shape
    l = ctx.shape[1]
    depth = w_in.shape[0]
    assert depth == 2, "laid out for one dense layer followed by one routed last layer"
    rows_n = n // GRID_W

    lat_tables = _rope_tables(n)
    ctx_flat = ctx.reshape(1, b * l, d)
    ones = jnp.ones((b * l, LANES), F32)
    zeros = jnp.zeros((b * l, LANES), F32)
    ctx_tables = [ones, zeros, ones, zeros]

    cvec = jnp.zeros((8, d), F32).at[:b].set(c).at[b].set(c_ctx)
    mods = []
    for i in range(depth):
        m = _adaln(cvec, ada_w, ada_b, i).reshape(8, 6, d)
        mods.append((m[:b], m[b:b + 1]))

    def mixers(i, h_lat, h_ctx):
        lambda_init = 0.8 - 0.6 * math.exp(-0.3 * i)
        w_qkv = _qkv_weights_call(w_in, i)
        sink_tab = jnp.broadcast_to((swa_sink[i].astype(F32) * LOG2E)[:, None], (4, LANES))
        p_lat = _qkv_call(h_lat, w_qkv, lat_tables, qk_norm_g[i])
        p_ctx = _qkv_call(h_ctx, w_qkv, ctx_tables, qk_norm_g[i]).reshape(b, l, QKV_COLS)
        o_lat = [_gqa_call(p_lat, p_ctx),
                 _diff_call(p_lat, p_ctx, diff_lambda[i], diff_subln_g[i], lambda_init),
                 _na_call(p_lat, p_ctx, _na_bias_tables(na_rpb[i])),
                 _swa_call(p_lat, p_ctx, sink_tab.reshape(2, 2, LANES))]
        return p_ctx, o_lat, sink_tab, lambda_init

    mod_lat, mod_ctx = mods[0]
    h_lat = _norm_mod_call(x, attn_norm_g[0], mod_lat, 0, 1)
    h_ctx = _norm_mod_call(ctx_flat, attn_norm_g[0], mod_ctx, 0, 1)
    p_ctx, o_lat, sink_tab, lambda_init = mixers(0, h_lat, h_ctx)
    wb = w_branch[0].astype(BF16)
    wo = w_out[0].astype(BF16)
    x_lat, h_lat = _merge_call([(o, 0) for o in o_lat], _gate_proj_call(h_lat, w_in, 0), wb, wo, x,
                               mod_lat, ffn_norm_g[0])
    o_ctx = _ctx_attn_call(p_ctx, diff_lambda[0], diff_subln_g[0], sink_tab, lambda_init)
    o_ctx = o_ctx.reshape(1, b * l, N_BRANCHES * BRANCH_WIDTH)
    x_ctx, h_ctx = _merge_call([(o_ctx, k) for k in range(N_BRANCHES)], _gate_proj_call(h_ctx, w_in, 0),
                               wb, wo, ctx_flat, mod_ctx, ffn_norm_g[0])
    wg = ffn_w_gate[0].astype(BF16)
    wu = ffn_w_up[0].astype(BF16)
    wd = ffn_w_down[0].astype(BF16)
    _, h_ctx = _ffn_call(h_ctx, wg, wu, wd, x_ctx, mod_ctx, attn_norm_g[1], mods[1][1])
    x_lat, h_lat = _ffn_call(h_lat, wg, wu, wd, x_lat, mod_lat, attn_norm_g[1], mods[1][0])

    mod_lat, _ = mods[1]
    _, o_lat, _, _ = mixers(1, h_lat, h_ctx)
    router = jnp.zeros((d, LANES), F32).at[:, :N_EXPERTS].set(moe_router[0])
    x_lat, h_lat, logits = _merge_call(
        [(o, 0) for o in o_lat], _gate_proj_call(h_lat, w_in, 1),
        w_branch[1].astype(BF16), w_out[1].astype(BF16), x_lat, mod_lat, ffn_norm_g[1], router)

    idx_pad, wts_pad = _route_call(logits.reshape(b * n, LANES))
    dest, src_tok, tile_expert, tile_rows = _moe_plan(idx_pad[:, :TOP_K])
    xs = h_lat.reshape(b * n, d).at[src_tok].get(mode="promise_in_bounds")
    ys = _moe_call(xs, moe_w_gate[0], moe_w_up[0], moe_w_down[0], tile_expert, tile_rows)
    y0 = ys.at[dest[:, 0]].get(mode="promise_in_bounds").reshape(b, n, d)
    y1 = ys.at[dest[:, 1]].get(mode="promise_in_bounds").reshape(b, n, d)
    return _final_call(x_lat, y0, y1, wts_pad.reshape(b, n, LANES), mod_lat, final_norm_g)
```

```python
import functools
import math

import numpy as np
import jax
import jax.numpy as jnp
from jax import lax
from jax.experimental import pallas as pl
from jax.experimental.pallas import tpu as pltpu

F32 = jnp.float32
BF16 = jnp.bfloat16

GRID_W = 64
HEAD_DIM = 128
N_BRANCHES = 4
BRANCH_WIDTH = 4 * HEAD_DIM
DIFF_QK_DIM = 64
NA_ROWS = 8
NA_COLS = 16
SWA_WINDOW = 128
N_EXPERTS = 8
TOP_K = 2
NORM_EPS = 1e-6
ROPE_THETA = 10000.0
NEG_INF = -1e30
LOG2E = math.log2(math.e)

QKV_COLS = 5120
COL_GQA_Q, COL_GQA_K, COL_GQA_V = 0, 512, 768
COL_DIFF_Q, COL_DIFF_K, COL_DIFF_V = 1024, 1536, 2048
COL_NA_Q, COL_NA_K, COL_NA_V = 2560, 3072, 3584
COL_SWA_Q, COL_SWA_K, COL_SWA_V = 4096, 4608, 4864

LANES = 128
VMEM_LIMIT = 56 * 1024 * 1024

QS128 = HEAD_DIM ** -0.5 * LOG2E
QS64 = DIFF_QK_DIM ** -0.5 * LOG2E


def _cparams(sem):
    return pltpu.CompilerParams(dimension_semantics=sem, vmem_limit_bytes=VMEM_LIMIT)


def _dot(a, b):
    return jnp.dot(a, b, preferred_element_type=F32)


def _dot_nt(a, b):
    return lax.dot_general(a, b, (((1,), (1,)), ((), ())), preferred_element_type=F32)


def _rms(x):
    return x * lax.rsqrt(jnp.mean(x * x, axis=-1, keepdims=True) + NORM_EPS)


def _norm_mod(x, g, mod_ref, shift_idx, scale_idx):
    y = _rms(x) * g
    return y * (1.0 + mod_ref[scale_idx:scale_idx + 1, :]) + mod_ref[shift_idx:shift_idx + 1, :]


def _adaln_kernel(c_ref, w_ref, b_ref, o_ref):
    c = c_ref[...]
    s = c * jax.nn.sigmoid(c)
    o_ref[...] = jnp.dot(s, w_ref[...], preferred_element_type=F32,
                         precision=lax.Precision.HIGHEST) + b_ref[...]


def _adaln(cvec, w, b, layer):
    rows, d = cvec.shape
    cols = w.shape[2]
    tn = 1024 if cols % 1024 == 0 else cols
    return pl.pallas_call(
        _adaln_kernel,
        grid=(cols // tn,),
        in_specs=[pl.BlockSpec((rows, d), lambda j: (0, 0)),
                  pl.BlockSpec((None, d, tn), lambda j: (layer, 0, j)),
                  pl.BlockSpec((None, 1, tn), lambda j: (layer, 0, j))],
        out_specs=pl.BlockSpec((rows, tn), lambda j: (0, j)),
        out_shape=jax.ShapeDtypeStruct((rows, cols), F32),
        compiler_params=_cparams(("arbitrary",)),
        name="adaln",
    )(cvec, w, b.reshape(b.shape[0], 1, cols))


def _norm_mod_kernel(x_ref, g_ref, mod_ref, h_ref, *, shift_idx, scale_idx):
    h_ref[...] = _norm_mod(x_ref[...], g_ref[...], mod_ref, shift_idx, scale_idx).astype(BF16)


def _norm_mod_call(x, g, mod, shift_idx, scale_idx):
    b, r, d = x.shape
    tm = min(r, 512)
    return pl.pallas_call(
        functools.partial(_norm_mod_kernel, shift_idx=shift_idx, scale_idx=scale_idx),
        grid=(b, r // tm),
        in_specs=[pl.BlockSpec((None, tm, d), lambda i, t: (i, t, 0)),
                  pl.BlockSpec((1, d), lambda i, t: (0, 0)),
                  pl.BlockSpec((None, 6, d), lambda i, t: (i, 0, 0))],
        out_specs=pl.BlockSpec((None, tm, d), lambda i, t: (i, t, 0)),
        out_shape=jax.ShapeDtypeStruct((b, r, d), BF16),
        compiler_params=_cparams(("parallel", "parallel")),
        name="norm_mod",
    )(x, g.reshape(1, d), mod)


QKV_TILE = 4 * LANES
_PLAIN = (None, None, 1.0)
_QKV_TILE_OPS = {
    0: [(0, 32, QS128)] * 4,
    1: [(1, 32, 1.0)] * 2 + [_PLAIN] * 2,
    2: [(None, 16, QS64)] * 4,
    3: [(None, 16, 1.0)] * 4,
    5: [(None, None, QS128)] * 4,
    8: [(None, 32, QS128)] * 4,
    9: [(None, 32, 1.0)] * 2 + [_PLAIN] * 2,
}


def _rope(y, cos, sin, hw):
    lane = lax.broadcasted_iota(jnp.int32, y.shape, 1)
    first = (lane % (2 * hw)) < hw
    partner = jnp.where(first, pltpu.roll(y, LANES - hw, 1), pltpu.roll(y, hw, 1))
    return y * cos + partner * sin


def _qkv_kernel(h_ref, w_ref, cos32_ref, sin32_ref, cos16_ref, sin16_ref, g_ref, o_ref):
    h = h_ref[...]
    for j in range(o_ref.shape[1] // QKV_TILE):
        acc = _dot(h, w_ref[:, j * QKV_TILE:(j + 1) * QKV_TILE])
        for c, (norm_row, hw, scale) in enumerate(_QKV_TILE_OPS.get(j, [_PLAIN] * 4)):
            y = acc[:, c * LANES:(c + 1) * LANES]
            if norm_row is not None:
                y = _rms(y) * g_ref[norm_row:norm_row + 1, :]
            if hw == 32:
                y = _rope(y, cos32_ref[...], sin32_ref[...], 32)
            elif hw == 16:
                y = _rope(y, cos16_ref[...], sin16_ref[...], 16)
            if scale != 1.0:
                y = y * scale
            col = j * QKV_TILE + c * LANES
            o_ref[:, col:col + LANES] = y.astype(o_ref.dtype)


def _qkv_call(h, w_qkv, tables, qk_g):
    b, r, d = h.shape
    cols = w_qkv.shape[1]
    tm = min(r, 512)
    tab_spec = pl.BlockSpec((tm, LANES), lambda i, t: (t, 0))
    return pl.pallas_call(
        _qkv_kernel,
        grid=(b, r // tm),
        in_specs=[pl.BlockSpec((None, tm, d), lambda i, t: (i, t, 0)),
                  pl.BlockSpec((d, cols), lambda i, t: (0, 0), pipeline_mode=pl.Buffered(1)),
                  tab_spec, tab_spec, tab_spec, tab_spec,
                  pl.BlockSpec((2, LANES), lambda i, t: (0, 0))],
        out_specs=pl.BlockSpec((None, tm, cols), lambda i, t: (i, t, 0)),
        out_shape=jax.ShapeDtypeStruct((b, r, cols), BF16),
        compiler_params=_cparams(("parallel", "parallel")),
        name="qkv_proj",
    )(h, w_qkv, *tables, qk_g)


def _matmul_kernel(h_ref, w_ref, o_ref):
    o_ref[...] = _dot(h_ref[...], w_ref[...].astype(BF16)).astype(o_ref.dtype)


def _gate_proj_call(h, w_in, layer):
    b, r, d = h.shape
    tn = 1024
    cols = w_in.shape[2] - QKV_COLS
    col0 = QKV_COLS // tn
    tm = 2048 if r % 2048 == 0 else r
    return pl.pallas_call(
        _matmul_kernel,
        grid=(b, r // tm, cols // tn),
        in_specs=[pl.BlockSpec((None, tm, d), lambda i, t, j: (i, t, 0)),
                  pl.BlockSpec((None, d, tn), lambda i, t, j: (layer, 0, col0 + j))],
        out_specs=pl.BlockSpec((None, tm, tn), lambda i, t, j: (i, t, j)),
        out_shape=jax.ShapeDtypeStruct((b, r, cols), BF16),
        compiler_params=_cparams(("parallel", "parallel", "arbitrary")),
        name="gate_proj",
    )(h, w_in)


def _cast_kernel(w_ref, o_ref):
    o_ref[...] = w_ref[...].astype(o_ref.dtype)


def _qkv_weights_call(w_in, layer):
    d = w_in.shape[1]
    return pl.pallas_call(
        _cast_kernel,
        grid=(QKV_COLS // QKV_TILE,),
        in_specs=[pl.BlockSpec((None, d, QKV_TILE), lambda j: (layer, 0, j))],
        out_specs=pl.BlockSpec((d, QKV_TILE), lambda j: (0, j)),
        out_shape=jax.ShapeDtypeStruct((d, QKV_COLS), BF16),
        compiler_params=_cparams(("parallel",)),
        name="qkv_weights",
    )(w_in)


def _diff_lambda(lam_ref, lambda_init):
    a = jnp.sum(lam_ref[0:1, :] * lam_ref[1:2, :], axis=-1, keepdims=True)
    b = jnp.sum(lam_ref[2:3, :] * lam_ref[3:4, :], axis=-1, keepdims=True)
    return jnp.exp(a) - jnp.exp(b) + lambda_init


def _split_maps(q):
    lane = lax.broadcasted_iota(jnp.int32, q.shape, 1)
    zero = jnp.zeros_like(q)
    return jnp.concatenate([jnp.where(lane < DIFF_QK_DIM, q, zero),
                            jnp.where(lane >= DIFF_QK_DIM, q, zero)], axis=0)


FLASH_ROW_BLOCK = 128
FLASH_KV_CHUNK = 512
FLASH_Q_TILE = 1024


def _flash_pair(q2, kc_ref, vc_ref, kl_ref, vl_ref, tk):
    m_rows = q2.shape[0]
    n_lat = kl_ref.shape[0]
    nb = m_rows // FLASH_ROW_BLOCK
    qs = [q2[i * FLASH_ROW_BLOCK:(i + 1) * FLASH_ROW_BLOCK] for i in range(nb)]

    def step(k, v, carry):
        v1 = jnp.concatenate([v, jnp.ones_like(v)], axis=1)
        out = []
        for q, (m, acc) in zip(qs, carry):
            s = _dot_nt(q, k)
            m_new = jnp.maximum(m, jnp.max(s, axis=-1, keepdims=True))
            p = jnp.exp2(s - m_new)
            acc = jnp.exp2(m - m_new) * acc + _dot(p.astype(BF16), v1)
            out.append((m_new, acc))
        return tuple(out)

    carry = tuple((jnp.full((FLASH_ROW_BLOCK, 1), NEG_INF, F32),
                   jnp.zeros((FLASH_ROW_BLOCK, 2 * HEAD_DIM), F32)) for _ in range(nb))
    carry = step(kc_ref[...], vc_ref[...], carry)

    for c in range(n_lat // tk):
        carry = step(kl_ref[c * tk:(c + 1) * tk, :], vl_ref[c * tk:(c + 1) * tk, :], carry)
    return jnp.concatenate([acc[:, :HEAD_DIM] / acc[:, HEAD_DIM:] for _, acc in carry], axis=0)


def _gqa_kernel(q_ref, kc_ref, vc_ref, kl_ref, vl_ref, o_ref, *, tk):
    tq = q_ref.shape[0]
    q = q_ref[...]
    q2 = jnp.concatenate([q[:, :HEAD_DIM], q[:, HEAD_DIM:]], axis=0)
    o = _flash_pair(q2, kc_ref, vc_ref, kl_ref, vl_ref, tk)
    o_ref[:, :HEAD_DIM] = o[:tq].astype(o_ref.dtype)
    o_ref[:, HEAD_DIM:] = o[tq:].astype(o_ref.dtype)


def _diff_kernel(q_ref, kc_ref, vc_ref, kl_ref, vl_ref, lam_ref, subg_ref, o_ref, *, tk, lambda_init):
    tq = q_ref.shape[0]
    o = _flash_pair(_split_maps(q_ref[...]), kc_ref, vc_ref, kl_ref, vl_ref, tk)
    lam = _diff_lambda(lam_ref, lambda_init)
    d = o[:tq] - lam * o[tq:]
    o_ref[...] = (_rms(d) * subg_ref[...] * (1.0 - lambda_init)).astype(o_ref.dtype)


def _gqa_call(p_lat, p_ctx):
    b, n, _ = p_lat.shape
    l = p_ctx.shape[1]
    tq = FLASH_Q_TILE
    tk = FLASH_KV_CHUNK
    hd = HEAD_DIM
    kcol, vcol = COL_GQA_K // hd, COL_GQA_V // hd
    return pl.pallas_call(
        functools.partial(_gqa_kernel, tk=tk),
        grid=(b, 2, n // tq),
        in_specs=[pl.BlockSpec((None, tq, 2 * hd), lambda i, k, t: (i, t, k)),
                  pl.BlockSpec((None, l, hd), lambda i, k, t: (i, 0, kcol + k)),
                  pl.BlockSpec((None, l, hd), lambda i, k, t: (i, 0, vcol + k)),
                  pl.BlockSpec((None, n, hd), lambda i, k, t: (i, 0, kcol + k)),
                  pl.BlockSpec((None, n, hd), lambda i, k, t: (i, 0, vcol + k))],
        out_specs=pl.BlockSpec((None, tq, 2 * hd), lambda i, k, t: (i, t, k)),
        out_shape=jax.ShapeDtypeStruct((b, n, BRANCH_WIDTH), BF16),
        compiler_params=_cparams(("parallel", "parallel", "arbitrary")),
        name="attn_gqa",
    )(p_lat, p_ctx, p_ctx, p_lat, p_lat)


def _diff_call(p_lat, p_ctx, lam_p, subg, lambda_init):
    b, n, _ = p_lat.shape
    l = p_ctx.shape[1]
    tq = FLASH_Q_TILE
    tk = FLASH_KV_CHUNK
    hd = HEAD_DIM
    qcol, kcol, vcol = COL_DIFF_Q // hd, COL_DIFF_K // hd, COL_DIFF_V // hd
    return pl.pallas_call(
        functools.partial(_diff_kernel, tk=tk, lambda_init=lambda_init),
        grid=(b, 4, n // tq),
        in_specs=[pl.BlockSpec((None, tq, hd), lambda i, h, t: (i, t, qcol + h)),
                  pl.BlockSpec((None, l, hd), lambda i, h, t: (i, 0, kcol + h)),
                  pl.BlockSpec((None, l, hd), lambda i, h, t: (i, 0, vcol + h)),
                  pl.BlockSpec((None, n, hd), lambda i, h, t: (i, 0, kcol + h)),
                  pl.BlockSpec((None, n, hd), lambda i, h, t: (i, 0, vcol + h)),
                  pl.BlockSpec((4, DIFF_QK_DIM), lambda i, h, t: (0, 0)),
                  pl.BlockSpec((1, hd), lambda i, h, t: (0, 0))],
        out_specs=pl.BlockSpec((None, tq, hd), lambda i, h, t: (i, t, h)),
        out_shape=jax.ShapeDtypeStruct((b, n, BRANCH_WIDTH), BF16),
        compiler_params=_cparams(("parallel", "parallel", "arbitrary")),
        name="attn_diff",
    )(p_lat, p_ctx, p_ctx, p_lat, p_lat, lam_p, subg.reshape(1, hd))


NA_GROUP_ROWS = 32
NA_BLOCK_ROWS = 4
NA_BLOCK_WIN_ROWS = NA_BLOCK_ROWS + NA_ROWS


def _with_ones(v):
    return jnp.concatenate([v, jnp.ones_like(v)], axis=1)


def _band_ctx_attend(s_b, s_c, vb1, vc1, sink=None):
    m = jnp.maximum(jnp.max(s_b, axis=-1, keepdims=True), jnp.max(s_c, axis=-1, keepdims=True))
    if sink is not None:
        m = jnp.maximum(m, sink)
    acc = _dot(jnp.exp2(s_c - m).astype(BF16), vc1) + _dot(jnp.exp2(s_b - m).astype(BF16), vb1)
    den = acc[:, HEAD_DIM:]
    if sink is not None:
        den = den + jnp.exp2(sink - m)
    return acc[:, :HEAD_DIM] / den


NA_DI = 2 * NA_ROWS - 1
NA_TAB_PAD = 2 * NA_ROWS
NA_TAB_LANES = 3072


def _na_bias_tables(rpb):
    nh = rpb.shape[0]
    qc = np.arange(GRID_W)[:, None]
    kc = np.arange(GRID_W)[None, :]
    cstart = np.clip(qc - NA_COLS // 2, 0, GRID_W - NA_COLS)
    ok_col = (kc >= cstart) & (kc < cstart + NA_COLS)
    dj = np.clip(kc - qc + NA_COLS - 1, 0, 2 * NA_COLS - 2)
    sel_j = (dj[..., None] == np.arange(2 * NA_COLS - 1)) & ok_col[..., None]
    by_col = jnp.einsum("hij,qkj->hqik", rpb.astype(F32), jnp.asarray(sel_j, F32),
                        precision=lax.Precision.HIGHEST)
    by_col = jnp.where(ok_col[None, :, None, :], by_col * LOG2E, NEG_INF)
    strip = by_col.reshape(nh, GRID_W, NA_DI * GRID_W)
    total = NA_TAB_LANES + GRID_W
    strip = jnp.pad(strip, ((0, 0), (0, 0), (NA_TAB_PAD * GRID_W, total - (NA_TAB_PAD + NA_DI) * GRID_W)),
                    constant_values=NEG_INF)
    return jnp.stack([strip[:, :, :NA_TAB_LANES], strip[:, :, GRID_W:]], axis=1)


def _na_kernel(q_ref, kc_ref, vc_ref, kl_ref, vl_ref, tab_ref, o_ref, *, rows_n):
    g = pl.program_id(2)
    win = NA_BLOCK_WIN_ROWS * GRID_W
    blk_q = NA_BLOCK_ROWS * GRID_W
    kc = kc_ref[...]
    vc1 = _with_ones(vc_ref[...])
    lane = lax.broadcasted_iota(jnp.int32, (GRID_W, win), 1)
    for rb in range(NA_GROUP_ROWS // NA_BLOCK_ROWS):
        r0 = g * NA_GROUP_ROWS + rb * NA_BLOCK_ROWS
        ws = jnp.clip(r0 - NA_ROWS // 2, 0, rows_n - NA_BLOCK_WIN_ROWS)
        off = pl.multiple_of(ws * GRID_W, NA_BLOCK_ROWS * GRID_W)
        bias = []
        for a in range(NA_BLOCK_ROWS):
            r = r0 + a
            rs = jnp.clip(r - NA_ROWS // 2, 0, rows_n - NA_ROWS)
            blk = ws - r + NA_ROWS - 1 + NA_TAB_PAD
            start = pl.multiple_of((blk >> 1) * (2 * GRID_W), 2 * GRID_W)
            strip = tab_ref[blk & 1, :, pl.ds(start, win)]
            lo = (rs - ws) * GRID_W
            in_rows = jnp.logical_and(lane >= lo, lane < lo + NA_ROWS * GRID_W)
            bias.append(jnp.where(in_rows, strip, NEG_INF))
        q = q_ref[rb * blk_q:(rb + 1) * blk_q, :]
        s_b = _dot_nt(q, kl_ref[pl.ds(off, win), :]) + jnp.concatenate(bias, axis=0)
        o = _band_ctx_attend(s_b, _dot_nt(q, kc), _with_ones(vl_ref[pl.ds(off, win), :]), vc1)
        o_ref[rb * blk_q:(rb + 1) * blk_q, :] = o.astype(o_ref.dtype)


def _na_call(p_lat, p_ctx, bias):
    b, n, _ = p_lat.shape
    l = p_ctx.shape[1]
    hd = HEAD_DIM
    rows_n = n // GRID_W
    n_groups = rows_n // NA_GROUP_ROWS
    tq = NA_GROUP_ROWS * GRID_W
    qcol, kcol, vcol = COL_NA_Q // hd, COL_NA_K // hd, COL_NA_V // hd

    return pl.pallas_call(
        functools.partial(_na_kernel, rows_n=rows_n),
        grid=(b, 4, n_groups),
        in_specs=[pl.BlockSpec((None, tq, hd), lambda i, h, g: (i, g, qcol + h)),
                  pl.BlockSpec((None, l, hd), lambda i, h, g: (i, 0, kcol + h)),
                  pl.BlockSpec((None, l, hd), lambda i, h, g: (i, 0, vcol + h)),
                  pl.BlockSpec((None, n, hd), lambda i, h, g: (i, 0, kcol + h)),
                  pl.BlockSpec((None, n, hd), lambda i, h, g: (i, 0, vcol + h)),
                  pl.BlockSpec((None, 2, GRID_W, NA_TAB_LANES), lambda i, h, g: (h, 0, 0, 0))],
        out_specs=pl.BlockSpec((None, tq, hd), lambda i, h, g: (i, g, h)),
        out_shape=jax.ShapeDtypeStruct((b, n, BRANCH_WIDTH), BF16),
        compiler_params=_cparams(("parallel", "parallel", "arbitrary")),
        name="attn_na",
    )(p_lat, p_ctx, p_ctx, p_lat, p_lat, bias)


SWA_BLOCK = 256


def _swa_kernel(q_ref, kc_ref, vc_ref, kl_ref, vl_ref, sink_ref, o_ref, *, win):
    t = pl.program_id(2)
    tq = q_ref.shape[0]
    n = kl_ref.shape[0]
    kc = kc_ref[...]
    vc1 = _with_ones(vc_ref[...])
    row = lax.broadcasted_iota(jnp.int32, (SWA_BLOCK, win), 0)
    col = lax.broadcasted_iota(jnp.int32, (SWA_BLOCK, win), 1)
    for rb in range(tq // SWA_BLOCK):
        qs = t * tq + rb * SWA_BLOCK
        ws = pl.multiple_of(jnp.clip(qs - SWA_WINDOW, 0, n - win), SWA_WINDOW)
        kb = kl_ref[pl.ds(ws, win), :]
        vb1 = _with_ones(vl_ref[pl.ds(ws, win), :])
        valid = jnp.abs(row + (qs - ws) - col) <= SWA_WINDOW
        rows = slice(rb * SWA_BLOCK, (rb + 1) * SWA_BLOCK)
        for gi in range(2):
            q = q_ref[rows, gi * HEAD_DIM:(gi + 1) * HEAD_DIM]
            s_b = jnp.where(valid, _dot_nt(q, kb), NEG_INF)
            o = _band_ctx_attend(s_b, _dot_nt(q, kc), vb1, vc1, sink=sink_ref[gi:gi + 1, 0:1])
            o_ref[rows, gi * HEAD_DIM:(gi + 1) * HEAD_DIM] = o.astype(o_ref.dtype)


def _swa_call(p_lat, p_ctx, sink_tab):
    b, n, _ = p_lat.shape
    l = p_ctx.shape[1]
    hd = HEAD_DIM
    tq = 8 * SWA_BLOCK
    win = SWA_BLOCK + 2 * SWA_WINDOW
    qcol, kcol, vcol = COL_SWA_Q // (2 * hd), COL_SWA_K // hd, COL_SWA_V // hd
    return pl.pallas_call(
        functools.partial(_swa_kernel, win=win),
        grid=(b, 2, n // tq),
        in_specs=[pl.BlockSpec((None, tq, 2 * hd), lambda i, k, t: (i, t, qcol + k)),
                  pl.BlockSpec((None, l, hd), lambda i, k, t: (i, 0, kcol + k)),
                  pl.BlockSpec((None, l, hd), lambda i, k, t: (i, 0, vcol + k)),
                  pl.BlockSpec((None, n, hd), lambda i, k, t: (i, 0, kcol + k)),
                  pl.BlockSpec((None, n, hd), lambda i, k, t: (i, 0, vcol + k)),
                  pl.BlockSpec((None, 2, LANES), lambda i, k, t: (k, 0, 0))],
        out_specs=pl.BlockSpec((None, tq, 2 * hd), lambda i, k, t: (i, t, k)),
        out_shape=jax.ShapeDtypeStruct((b, n, BRANCH_WIDTH), BF16),
        compiler_params=_cparams(("parallel", "parallel", "arbitrary")),
        name="attn_swa",
    )(p_lat, p_ctx, p_ctx, p_lat, p_lat, sink_tab)


def _softmax_attend(q, k, v, sink=None):
    s = _dot_nt(q, k)
    m = jnp.max(s, axis=-1, keepdims=True)
    if sink is not None:
        m = jnp.maximum(m, sink)
    e = jnp.exp2(s - m)
    den = jnp.sum(e, axis=-1, keepdims=True)
    if sink is not None:
        den = den + jnp.exp2(sink - m)
    return _dot(e.astype(BF16), v) / den


def _ctx_attn_kernel(p_ref, lam_ref, subg_ref, sink_ref, o_ref, *, lambda_init):
    hd = HEAD_DIM

    def col(c0, h):
        return p_ref[:, c0 + h * hd:c0 + (h + 1) * hd]

    lam = _diff_lambda(lam_ref, lambda_init)
    for h in range(4):
        o = _softmax_attend(col(COL_GQA_Q, h), col(COL_GQA_K, h // 2), col(COL_GQA_V, h // 2))
        o_ref[:, h * hd:(h + 1) * hd] = o.astype(o_ref.dtype)
    for h in range(4):
        q = col(COL_DIFF_Q, h)
        tq = q.shape[0]
        o2 = _softmax_attend(_split_maps(q), col(COL_DIFF_K, h), col(COL_DIFF_V, h))
        d = o2[:tq] - lam * o2[tq:]
        d = _rms(d) * subg_ref[...] * (1.0 - lambda_init)
        o_ref[:, BRANCH_WIDTH + h * hd:BRANCH_WIDTH + (h + 1) * hd] = d.astype(o_ref.dtype)
    for h in range(4):
        o = _softmax_attend(col(COL_NA_Q, h), col(COL_NA_K, h), col(COL_NA_V, h))
        o_ref[:, 2 * BRANCH_WIDTH + h * hd:2 * BRANCH_WIDTH + (h + 1) * hd] = o.astype(o_ref.dtype)
    for h in range(4):
        o = _softmax_attend(col(COL_SWA_Q, h), col(COL_SWA_K, h // 2), col(COL_SWA_V, h // 2),
                            sink=sink_ref[h:h + 1, 0:1])
        o_ref[:, 3 * BRANCH_WIDTH + h * hd:3 * BRANCH_WIDTH + (h + 1) * hd] = o.astype(o_ref.dtype)


def _ctx_attn_call(p_ctx, lam_p, subg, sink_tab, lambda_init):
    b, l, cols = p_ctx.shape
    return pl.pallas_call(
        functools.partial(_ctx_attn_kernel, lambda_init=lambda_init),
        grid=(b,),
        in_specs=[pl.BlockSpec((None, l, cols), lambda i: (i, 0, 0)),
                  pl.BlockSpec((4, DIFF_QK_DIM), lambda i: (0, 0)),
                  pl.BlockSpec((1, HEAD_DIM), lambda i: (0, 0)),
                  pl.BlockSpec((4, LANES), lambda i: (0, 0))],
        out_specs=pl.BlockSpec((None, l, N_BRANCHES * BRANCH_WIDTH), lambda i: (i, 0, 0)),
        out_shape=jax.ShapeDtypeStruct((b, l, N_BRANCHES * BRANCH_WIDTH), BF16),
        compiler_params=_cparams(("parallel",)),
        name="attn_ctx",
    )(p_ctx, lam_p, subg.reshape(1, HEAD_DIM), sink_tab.reshape(4, LANES))


MERGE_CHUNKS = 1


def _merge_kernel(o0, o1, o2, o3, g_ref, wb_ref, wo_ref, x_ref, mod_ref, ng_ref, *rest, with_router):
    if with_router:
        router_ref, xo_ref, h_ref, lg_ref = rest
    else:
        xo_ref, h_ref = rest
    d = x_ref.shape[1]
    dc = d // MERGE_CHUNKS
    o_vals = [o[...] for o in (o0, o1, o2, o3)]
    z = None
    for c in range(MERGE_CHUNKS):
        s = None
        for i, o in enumerate(o_vals):
            y = _dot(o, wb_ref[i, :, c * dc:(c + 1) * dc])
            gate = g_ref[:, i * d + c * dc:i * d + (c + 1) * dc].astype(F32)
            term = jax.nn.sigmoid(gate) * y
            s = term if s is None else s + term
        zc = _dot(s.astype(BF16), wo_ref[c * dc:(c + 1) * dc, :])
        z = zc if z is None else z + zc
    xn = x_ref[...] + mod_ref[2:3, :] * z
    xo_ref[...] = xn
    hn = _norm_mod(xn, ng_ref[...], mod_ref, 3, 4)
    h_ref[...] = hn.astype(h_ref.dtype)
    if with_router:
        lg_ref[...] = _dot_split(hn, router_ref[...])


def _dot_split(a, b):
    a_hi = a.astype(BF16)
    a_lo = (a - a_hi.astype(F32)).astype(BF16)
    b_hi = b.astype(BF16)
    b_lo = (b - b_hi.astype(F32)).astype(BF16)
    n = b.shape[1]
    both = _dot(a_hi, jnp.concatenate([b_hi, b_lo], axis=1))
    return both[:, :n] + both[:, n:] + _dot(a_lo, b_hi)


def _merge_call(o_parts, gates, w_branch, w_out, x, mod, next_g, router=None):
    b, r, d = x.shape
    tm = min(r, 256)
    with_router = router is not None
    whole = dict(pipeline_mode=pl.Buffered(1))

    def o_spec(cb):
        return pl.BlockSpec((None, tm, BRANCH_WIDTH), lambda i, t: (i, t, cb))

    row_spec = pl.BlockSpec((None, tm, d), lambda i, t: (i, t, 0))
    in_specs = [o_spec(cb) for _, cb in o_parts] + [
        pl.BlockSpec((None, tm, N_BRANCHES * d), lambda i, t: (i, t, 0)),
        pl.BlockSpec((N_BRANCHES, BRANCH_WIDTH, d), lambda i, t: (0, 0, 0), **whole),
        pl.BlockSpec((d, d), lambda i, t: (0, 0), **whole),
        row_spec,
        pl.BlockSpec((None, 6, d), lambda i, t: (i, 0, 0)),
        pl.BlockSpec((1, d), lambda i, t: (0, 0))]
    args = [a for a, _ in o_parts] + [gates, w_branch, w_out, x, mod, next_g.reshape(1, d)]
    out_specs = [row_spec, row_spec]
    out_shape = [jax.ShapeDtypeStruct((b, r, d), F32), jax.ShapeDtypeStruct((b, r, d), BF16)]
    if with_router:
        in_specs.append(pl.BlockSpec((d, LANES), lambda i, t: (0, 0), **whole))
        args.append(router)
        out_specs.append(pl.BlockSpec((None, tm, LANES), lambda i, t: (i, t, 0)))
        out_shape.append(jax.ShapeDtypeStruct((b, r, LANES), F32))
    return pl.pallas_call(
        functools.partial(_merge_kernel, with_router=with_router),
        grid=(b, r // tm),
        in_specs=in_specs,
        out_specs=out_specs,
        out_shape=out_shape,
        compiler_params=_cparams(("parallel", "parallel")),
        name="merge",
    )(*args)


def _swiglu_partial(h, wg, wu, wd):
    gate = _dot(h, wg)
    up = _dot(h, wu)
    act = gate * jax.nn.sigmoid(gate) * up
    return _dot(act.astype(BF16), wd)


FFN_ROW_BLOCK = 512


def _ffn_kernel(h_ref, wgu_ref, wd_ref, x_ref, mod_ref, ng_ref, nmod_ref, xo_ref, hn_ref, acc_ref):
    f = pl.program_id(2)
    tf = wd_ref.shape[0]

    @pl.when(f == 0)
    def _():
        acc_ref[...] = jnp.zeros_like(acc_ref)

    for r0 in range(0, h_ref.shape[0], FFN_ROW_BLOCK):
        rows = slice(r0, r0 + FFN_ROW_BLOCK)
        gu = _dot(h_ref[rows, :], wgu_ref[...])
        gate, up = gu[:, :tf], gu[:, tf:]
        act = gate * jax.nn.sigmoid(gate) * up
        acc_ref[rows, :] += _dot(act.astype(BF16), wd_ref[...])

    @pl.when(f == pl.num_programs(2) - 1)
    def _():
        xn = x_ref[...] + mod_ref[5:6, :] * acc_ref[...]
        xo_ref[...] = xn
        hn_ref[...] = _norm_mod(xn, ng_ref[...], nmod_ref, 0, 1).astype(hn_ref.dtype)


FFN_FF_TILE = 512


def _ffn_gate_up_weights(wg, wu):
    d, dff = wg.shape
    nf = dff // FFN_FF_TILE
    both = jnp.concatenate([wg.astype(BF16).reshape(d, nf, FFN_FF_TILE),
                            wu.astype(BF16).reshape(d, nf, FFN_FF_TILE)], axis=2)
    return both.reshape(d, 2 * dff)


def _ffn_call(h, wgu, wd, x, mod, next_g, next_mod):
    b, r, d = x.shape
    dff = wd.shape[0]
    tf = FFN_FF_TILE
    tm = min(r, FFN_ROW_BLOCK)
    row_spec = pl.BlockSpec((None, tm, d), lambda i, t, f: (i, t, 0))
    mod_spec = pl.BlockSpec((None, 6, d), lambda i, t, f: (i, 0, 0))
    return pl.pallas_call(
        _ffn_kernel,
        grid=(b, r // tm, dff // tf),
        in_specs=[row_spec,
                  pl.BlockSpec((d, 2 * tf), lambda i, t, f: (0, f)),
                  pl.BlockSpec((tf, d), lambda i, t, f: (f, 0)),
                  row_spec, mod_spec,
                  pl.BlockSpec((1, d), lambda i, t, f: (0, 0)),
                  mod_spec],
        out_specs=[row_spec, row_spec],
        out_shape=[jax.ShapeDtypeStruct((b, r, d), F32), jax.ShapeDtypeStruct((b, r, d), BF16)],
        scratch_shapes=[pltpu.VMEM((tm, d), F32)],
        compiler_params=_cparams(("parallel", "parallel", "arbitrary")),
        name="ffn_dense",
    )(h, wgu, wd, x, mod, next_g.reshape(1, d), next_mod)


def _route_kernel(lg_ref, idx_ref, w_ref):
    lg = lg_ref[...]
    lane = lax.broadcasted_iota(jnp.int32, lg.shape, 1)
    valid = lane < N_EXPERTS
    mx = jnp.max(jnp.where(valid, lg, -jnp.inf), axis=-1, keepdims=True)
    e = jnp.where(valid, jnp.exp(lg - mx), 0.0)
    p = e / jnp.sum(e, axis=-1, keepdims=True)
    p1 = jnp.max(p, axis=-1, keepdims=True)
    i1 = jnp.min(jnp.where(p == p1, lane, LANES), axis=-1, keepdims=True)
    rest = jnp.where(jnp.logical_or(lane == i1, jnp.logical_not(valid)), -1.0, p)
    p2 = jnp.max(rest, axis=-1, keepdims=True)
    i2 = jnp.min(jnp.where(rest == p2, lane, LANES), axis=-1, keepdims=True)
    tot = p1 + p2
    idx_ref[...] = jnp.where(lane == 0, i1, jnp.where(lane == 1, i2, 0))
    w_ref[...] = jnp.where(lane == 0, p1 / tot, jnp.where(lane == 1, p2 / tot, 0.0))


def _route_call(logits):
    r = logits.shape[0]
    tm = min(r, 1024)
    spec = pl.BlockSpec((tm, LANES), lambda i: (i, 0))
    return pl.pallas_call(
        _route_kernel,
        grid=(r // tm,),
        in_specs=[spec],
        out_specs=[spec, spec],
        out_shape=[jax.ShapeDtypeStruct((r, LANES), jnp.int32), jax.ShapeDtypeStruct((r, LANES), F32)],
        compiler_params=_cparams(("parallel",)),
        name="route_top2",
    )(logits)


MOE_HALF = 512
MOE_TILE = 2 * MOE_HALF
MOE_FF_TILE = 512
MOE_VMEM_LIMIT = 60 * 1024 * 1024


def _moe_kernel(te_ref, nr_ref, x_ref, wg_ref, wu_ref, wd_ref, o_ref, acc_ref):
    i = pl.program_id(0)
    f = pl.program_id(1)

    @pl.when(f == 0)
    def _():
        acc_ref[...] = jnp.zeros_like(acc_ref)

    def accumulate(n_rows):
        wg = wg_ref[...].astype(BF16)
        wu = wu_ref[...].astype(BF16)
        wd = wd_ref[...].astype(BF16)
        for r0 in range(0, n_rows, MOE_HALF):
            rows = slice(r0, r0 + MOE_HALF)
            acc_ref[rows, :] += _swiglu_partial(x_ref[rows, :], wg, wu, wd)

    n_valid = nr_ref[i]
    pl.when(n_valid > MOE_HALF)(functools.partial(accumulate, MOE_TILE))
    pl.when(jnp.logical_and(n_valid > 0, n_valid <= MOE_HALF))(functools.partial(accumulate, MOE_HALF))

    @pl.when(f == pl.num_programs(1) - 1)
    def _():
        o_ref[...] = acc_ref[...].astype(o_ref.dtype)


def _moe_call(xs, wg, wu, wd, tile_expert, tile_rows):
    r, d = xs.shape
    dff = wg.shape[2]
    tf = MOE_FF_TILE
    nf = dff // tf

    def fidx(i, f, nr):
        return jnp.where(nr[i] > 0, f, nf - 1)

    grid_spec = pltpu.PrefetchScalarGridSpec(
        num_scalar_prefetch=2,
        grid=(r // MOE_TILE, nf),
        in_specs=[pl.BlockSpec((MOE_TILE, d), lambda i, f, te, nr: (i, 0)),
                  pl.BlockSpec((None, d, tf), lambda i, f, te, nr: (te[i], 0, fidx(i, f, nr))),
                  pl.BlockSpec((None, d, tf), lambda i, f, te, nr: (te[i], 0, fidx(i, f, nr))),
                  pl.BlockSpec((None, tf, d), lambda i, f, te, nr: (te[i], fidx(i, f, nr), 0))],
        out_specs=pl.BlockSpec((MOE_TILE, d), lambda i, f, te, nr: (i, 0)),
        scratch_shapes=[pltpu.VMEM((MOE_TILE, d), F32)])
    return pl.pallas_call(
        _moe_kernel,
        grid_spec=grid_spec,
        out_shape=jax.ShapeDtypeStruct((r, d), BF16),
        compiler_params=pltpu.CompilerParams(dimension_semantics=("arbitrary", "arbitrary"),
                                             vmem_limit_bytes=MOE_VMEM_LIMIT),
        name="moe_experts",
    )(tile_expert, tile_rows, xs, wg, wu, wd)


def _moe_plan(idx):
    t = idx.shape[0]
    a = t * TOP_K
    flat_e = idx.reshape(a)
    onehot = (flat_e[:, None] == jnp.arange(N_EXPERTS, dtype=jnp.int32)[None, :]).astype(jnp.int32)
    csum = jnp.cumsum(onehot, axis=0)
    rank = jnp.sum((csum - onehot) * onehot, axis=1)
    counts = csum[-1]
    padded = ((counts + MOE_TILE - 1) // MOE_TILE) * MOE_TILE
    ends = jnp.cumsum(padded)
    offs = ends - padded
    dest = offs[flat_e] + rank
    rows = a + N_EXPERTS * MOE_TILE
    src_tok = (jnp.arange(rows, dtype=jnp.int32) % t).at[dest].set(jnp.arange(a, dtype=jnp.int32) // TOP_K)
    tile_start = jnp.arange(rows // MOE_TILE, dtype=jnp.int32) * MOE_TILE
    past = jnp.sum((tile_start[:, None] >= ends[None, :]).astype(jnp.int32), axis=1)
    tile_expert = jnp.minimum(past, N_EXPERTS - 1)
    tile_rows = jnp.where(past < N_EXPERTS,
                          jnp.clip((offs + counts)[tile_expert] - tile_start, 0, MOE_TILE), 0)
    return dest.reshape(t, TOP_K), src_tok, tile_expert, tile_rows.astype(jnp.int32)


def _final_kernel(x_ref, y0_ref, y1_ref, w_ref, mod_ref, g_ref, o_ref):
    y = w_ref[:, 0:1] * y0_ref[...].astype(F32) + w_ref[:, 1:2] * y1_ref[...].astype(F32)
    xn = x_ref[...] + mod_ref[5:6, :] * y
    o_ref[...] = _rms(xn) * g_ref[...]


def _final_call(x, y0, y1, w, mod, g):
    b, r, d = x.shape
    tm = min(r, 512)
    row_spec = pl.BlockSpec((None, tm, d), lambda i, t: (i, t, 0))
    return pl.pallas_call(
        _final_kernel,
        grid=(b, r // tm),
        in_specs=[row_spec, row_spec, row_spec,
                  pl.BlockSpec((None, tm, LANES), lambda i, t: (i, t, 0)),
                  pl.BlockSpec((None, 6, d), lambda i, t: (i, 0, 0)),
                  pl.BlockSpec((1, d), lambda i, t: (0, 0))],
        out_specs=row_spec,
        out_shape=jax.ShapeDtypeStruct((b, r, d), F32),
        compiler_params=_cparams(("parallel", "parallel")),
        name="final_norm",
    )(x, y0, y1, w, mod, g.reshape(1, d))


def _rope_tables(n):
    pos = jnp.arange(n)
    rows = (pos // GRID_W).astype(F32)
    cols = (pos % GRID_W).astype(F32)
    lane = np.arange(LANES)
    out = []
    for hw in (32, 16):
        period = 4 * hw
        u = lane % period
        use_cols = (u // (2 * hw)) == 1
        w = u % (2 * hw)
        freqs = jnp.asarray(ROPE_THETA ** (-(w % hw).astype(np.float32) / hw), F32)
        p = jnp.where(jnp.asarray(use_cols)[None, :], cols[:, None], rows[:, None])
        ang = p * freqs[None, :]
        sign = jnp.asarray(np.where(w < hw, -1.0, 1.0), F32)
        out += [jnp.cos(ang), jnp.sin(ang) * sign[None, :]]
    return out


def kernel(x, c, ctx, c_ctx, attn_norm_g, ffn_norm_g, ada_w, ada_b, w_in, qk_norm_g, diff_lambda,
           diff_subln_g, na_rpb, swa_sink, w_branch, w_out, ffn_w_gate, ffn_w_up, ffn_w_down,
           moe_router, moe_w_gate, moe_w_up, moe_w_down, final_norm_g):
    b, n, d = x.shape
    l = ctx.shape[1]
    depth = w_in.shape[0]
    assert depth == 2, "laid out for one dense layer followed by one routed last layer"
    rows_n = n // GRID_W

    lat_tables = _rope_tables(n)
    ctx_flat = ctx.reshape(1, b * l, d)
    ones = jnp.ones((b * l, LANES), F32)
    zeros = jnp.zeros((b * l, LANES), F32)
    ctx_tables = [ones, zeros, ones, zeros]

    cvec = jnp.zeros((8, d), F32).at[:b].set(c).at[b].set(c_ctx)
    mods = []
    for i in range(depth):
        m = _adaln(cvec, ada_w, ada_b, i).reshape(8, 6, d)
        mods.append((m[:b], m[b:b + 1]))

    def mixers(i, h_lat, h_ctx):
        lambda_init = 0.8 - 0.6 * math.exp(-0.3 * i)
        w_qkv = _qkv_weights_call(w_in, i)
        sink_tab = jnp.broadcast_to((swa_sink[i].astype(F32) * LOG2E)[:, None], (4, LANES))
        p_lat = _qkv_call(h_lat, w_qkv, lat_tables, qk_norm_g[i])
        p_ctx = _qkv_call(h_ctx, w_qkv, ctx_tables, qk_norm_g[i]).reshape(b, l, QKV_COLS)
        o_lat = [_gqa_call(p_lat, p_ctx),
                 _diff_call(p_lat, p_ctx, diff_lambda[i], diff_subln_g[i], lambda_init),
                 _na_call(p_lat, p_ctx, _na_bias_tables(na_rpb[i])),
                 _swa_call(p_lat, p_ctx, sink_tab.reshape(2, 2, LANES))]
        return p_ctx, o_lat, sink_tab, lambda_init

    mod_lat, mod_ctx = mods[0]
    h_lat = _norm_mod_call(x, attn_norm_g[0], mod_lat, 0, 1)
    h_ctx = _norm_mod_call(ctx_flat, attn_norm_g[0], mod_ctx, 0, 1)
    p_ctx, o_lat, sink_tab, lambda_init = mixers(0, h_lat, h_ctx)
    wb = w_branch[0].astype(BF16)
    wo = w_out[0].astype(BF16)
    x_lat, h_lat = _merge_call([(o, 0) for o in o_lat], _gate_proj_call(h_lat, w_in, 0), wb, wo, x,
                               mod_lat, ffn_norm_g[0])
    o_ctx = _ctx_attn_call(p_ctx, diff_lambda[0], diff_subln_g[0], sink_tab, lambda_init)
    o_ctx = o_ctx.reshape(1, b * l, N_BRANCHES * BRANCH_WIDTH)
    x_ctx, h_ctx = _merge_call([(o_ctx, k) for k in range(N_BRANCHES)], _gate_proj_call(h_ctx, w_in, 0),
                               wb, wo, ctx_flat, mod_ctx, ffn_norm_g[0])
    wgu = _ffn_gate_up_weights(ffn_w_gate[0], ffn_w_up[0])
    wd = ffn_w_down[0].astype(BF16)
    _, h_ctx = _ffn_call(h_ctx, wgu, wd, x_ctx, mod_ctx, attn_norm_g[1], mods[1][1])
    x_lat, h_lat = _ffn_call(h_lat, wgu, wd, x_lat, mod_lat, attn_norm_g[1], mods[1][0])

    mod_lat, _ = mods[1]
    _, o_lat, _, _ = mixers(1, h_lat, h_ctx)
    router = jnp.zeros((d, LANES), F32).at[:, :N_EXPERTS].set(moe_router[0])
    x_lat, h_lat, logits = _merge_call(
        [(o, 0) for o in o_lat], _gate_proj_call(h_lat, w_in, 1),
        w_branch[1].astype(BF16), w_out[1].astype(BF16), x_lat, mod_lat, ffn_norm_g[1], router)

    idx_pad, wts_pad = _route_call(logits.reshape(b * n, LANES))
    dest, src_tok, tile_expert, tile_rows = _moe_plan(idx_pad[:, :TOP_K])
    xs = h_lat.reshape(b * n, d).at[src_tok].get(mode="promise_in_bounds")
    ys = _moe_call(xs, moe_w_gate[0], moe_w_up[0], moe_w_down[0], tile_expert, tile_rows)
    y0 = ys.at[dest[:, 0]].get(mode="promise_in_bounds").reshape(b, n, d)
    y1 = ys.at[dest[:, 1]].get(mode="promise_in_bounds").reshape(b, n, d)
    return _final_call(x_lat, y0, y1, wts_pad.reshape(b, n, LANES), mod_lat, final_norm_g)
```
